```python
import math, functools
import jax, jax.numpy as jnp
from jax import lax
import numpy as np

D_MODEL = 1024
BATCH = 1
SEQ = 16384
DEPTH = 1
DEC_BATCH = 32
DEC_SEQ = 1
PAST_LEN = 16384
PAGE_SIZE = 128

NSA_HEADS = 16
NSA_KV_HEADS = 4
NSA_GROUP = NSA_HEADS // NSA_KV_HEADS
NSA_HEAD_DIM = 64
NSA_BLOCK = 64
NSA_TOPK = 16
NSA_WINDOW = 512
NSA_QBLOCK = 128
HG_HEADS = 8
HG_DK = 128
HG_DV = 128
HG_CHUNK = 64
PK_HEADS = 8
PK_DIM = 256
N_KEYS = 128
N_EXPERTS = N_KEYS * N_KEYS
PK_TOPK = 16
PEER_BLOCK = 128

RMS_EPS = 1e-6
NEG_INF = -1e30

NSA_Q_WIDTH = NSA_HEADS * NSA_HEAD_DIM
NSA_KV_WIDTH = NSA_KV_HEADS * NSA_HEAD_DIM
HG_WIDTH = HG_HEADS * HG_DK
IN_SPLITS = (NSA_Q_WIDTH, 6 * NSA_KV_WIDTH, 3 * NSA_HEADS, HG_WIDTH, HG_WIDTH, HG_WIDTH, HG_WIDTH, D_MODEL, D_MODEL)
IN_WIDTH = sum(IN_SPLITS)

kernel_name = "nsa_hgrn2_peer_hybrid_step"


def _split_points():
    pts, acc = [], 0
    for w in IN_SPLITS[:-1]:
        acc += w
        pts.append(acc)
    return pts


def rmsnorm(x, w):
    xf = x.astype(jnp.float32)
    y = xf * lax.rsqrt(jnp.mean(xf * xf, axis=-1, keepdims=True) + RMS_EPS)
    return (y * w.astype(jnp.float32)).astype(x.dtype)


def alibi_slopes():
    h = jnp.arange(NSA_HEADS, dtype=jnp.float32)
    return (2.0 ** (-8.0 * (h + 1.0) / NSA_HEADS)).reshape(NSA_KV_HEADS, NSA_GROUP)


def masked_softmax(s, mask):
    p = jax.nn.softmax(jnp.where(mask, s.astype(jnp.float32), NEG_INF), axis=-1)
    return jnp.where(mask, p, 0.0)


def compress_blocks(kv_blocks, w_cmp):
    return jnp.einsum('bnlckd,cl->bnckd', kv_blocks, w_cmp.astype(kv_blocks.dtype))


def nsa_core(q, t_pos, kvc, gather_sel, kvw, s_pos, gates, slopes):
    f32 = jnp.float32
    nc = kvc.shape[1]
    tf = t_pos.astype(f32)
    sl = slopes[None, None, :, :, None]
    c_end = (jnp.arange(nc) + 1) * NSA_BLOCK - 1
    s_c = jnp.einsum('bqkgd,bnkd->bqkgn', q, kvc[:, :, 0]).astype(f32)
    s_c = s_c - sl * (tf[:, None] - c_end.astype(f32)[None, :])[None, :, None, None, :]
    valid_c = (c_end[None, :] <= t_pos[:, None])[None, :, None, None, :]
    p_c = masked_softmax(s_c, valid_c)
    o_c = jnp.einsum('bqkgn,bnkd->bqkgd', p_c.astype(q.dtype), kvc[:, :, 1])
    blk = jnp.arange(nc)[None, :]
    cur = (t_pos // NSA_BLOCK)[:, None]
    forced = ((blk == 0) | (blk == cur) | (blk == cur - 1))[None, :, None, :]
    future = (blk > cur)[None, :, None, :]
    imp = p_c.sum(axis=3)
    score = jnp.where(future, -1.0, jnp.where(forced, NSA_GROUP + 1.0, imp))
    _, idx = lax.top_k(score, min(NSA_TOPK, nc))
    ks, vs = gather_sel(idx)
    kpos = idx[..., None] * NSA_BLOCK + jnp.arange(NSA_BLOCK)
    dist = t_pos[None, :, None, None, None] - kpos
    s_s = jnp.einsum('bqkgd,bqknld->bqkgnl', q, ks).astype(f32)
    s_s = s_s - sl[..., None] * dist[:, :, :, None].astype(f32)
    mask_s = (dist >= 0)[:, :, :, None]
    s_s = s_s.reshape(s_s.shape[:4] + (-1,))
    mask_s = mask_s.reshape(mask_s.shape[:4] + (-1,))
    p_s = masked_softmax(s_s, mask_s)
    vs = vs.reshape(vs.shape[:3] + (-1, NSA_HEAD_DIM))
    o_s = jnp.einsum('bqkgm,bqkmd->bqkgd', p_s.astype(q.dtype), vs)
    dw = t_pos[:, None] - s_pos[None, :]
    s_w = jnp.einsum('bqkgd,bwkd->bqkgw', q, kvw[:, :, 0]).astype(f32)
    s_w = s_w - sl * dw.astype(f32)[None, :, None, None, :]
    mask_w = ((dw >= 0) & (dw <= NSA_WINDOW) & (s_pos[None, :] >= 0))[None, :, None, None, :]
    p_w = masked_softmax(s_w, mask_w)
    o_w = jnp.einsum('bqkgw,bwkd->bqkgd', p_w.astype(q.dtype), kvw[:, :, 1])
    return gates[..., 0:1] * o_c + gates[..., 1:2] * o_s + gates[..., 2:3] * o_w


def nsa_prompt(q, kv_cmp, kv_sel, kv_win, gates, w_cmp, slopes):
    b, s = q.shape[0], q.shape[1]
    nb = s // NSA_BLOCK
    blk_shape = (b, nb, NSA_BLOCK, 2, NSA_KV_HEADS, NSA_HEAD_DIM)
    kvc = compress_blocks(kv_cmp.reshape(blk_shape), w_cmp)
    sel_blocks = kv_sel.reshape(blk_shape)
    kvw_pad = jnp.pad(kv_win, ((0, 0), (NSA_WINDOW, 0), (0, 0), (0, 0), (0, 0)))
    bi = jnp.arange(b)[:, None, None, None]
    kh = jnp.arange(NSA_KV_HEADS)[None, None, :, None]

    def gather_sel(idx):
        g = sel_blocks[bi, idx, :, :, kh]
        return g[..., 0, :], g[..., 1, :]

    def one_block(i):
        q0 = i * NSA_QBLOCK
        qb = lax.dynamic_slice_in_dim(q, q0, NSA_QBLOCK, axis=1)
        gb = lax.dynamic_slice_in_dim(gates, q0, NSA_QBLOCK, axis=1)
        kvw = lax.dynamic_slice_in_dim(kvw_pad, q0, NSA_WINDOW + NSA_QBLOCK, axis=1)
        t_pos = q0 + jnp.arange(NSA_QBLOCK)
        s_pos = q0 - NSA_WINDOW + jnp.arange(NSA_WINDOW + NSA_QBLOCK)
        return nsa_core(qb, t_pos, kvc, gather_sel, kvw, s_pos, gb, slopes)

    out = lax.map(one_block, jnp.arange(s // NSA_QBLOCK))
    out = jnp.moveaxis(out, 0, 1).reshape(b, s, NSA_KV_HEADS, NSA_GROUP, NSA_HEAD_DIM)
    return out, kvw_pad[:, -NSA_WINDOW:]


def nsa_sample(q, kv_cmp, kv_sel, kv_win, gates, cache_cmp, cache_sel, cache_win, page_table, w_cmp, slopes):
    db, t = q.shape[0], q.shape[1]
    L = NSA_BLOCK
    npb = PAST_LEN // L
    nnb = -(-t // L)
    pad = nnb * L - t
    bpp = PAGE_SIZE // L
    pos = PAST_LEN + jnp.arange(t)
    padw = ((0, 0), (0, pad), (0, 0), (0, 0), (0, 0))
    past_cmp = cache_cmp[page_table].reshape(db, npb, L, 2, NSA_KV_HEADS, NSA_HEAD_DIM)
    new_cmp = jnp.pad(kv_cmp, padw).reshape(db, nnb, L, 2, NSA_KV_HEADS, NSA_HEAD_DIM)
    kvc = compress_blocks(jnp.concatenate([past_cmp, new_cmp.astype(past_cmp.dtype)], axis=1), w_cmp)
    new_sel = jnp.pad(kv_sel, padw).reshape(db, nnb, L, 2, NSA_KV_HEADS, NSA_HEAD_DIM).astype(cache_sel.dtype)
    bi = jnp.arange(db)[:, None, None, None]
    kh = jnp.arange(NSA_KV_HEADS)[None, None, :, None]

    def gather_sel(idx):
        pidx = jnp.clip(idx, 0, npb - 1)
        page = page_table[bi, pidx // bpp]
        rows = ((pidx % bpp) * L)[..., None] + jnp.arange(L)
        past = cache_sel[page[..., None], rows, :, kh[..., None]]
        nidx = jnp.clip(idx - npb, 0, nnb - 1)
        new = new_sel[bi, nidx, :, :, kh]
        g = jnp.where((idx >= npb)[..., None, None, None], new, past)
        return g[..., 0, :], g[..., 1, :]

    kvw = jnp.concatenate([cache_win, kv_win.astype(cache_win.dtype)], axis=1)
    s_pos = PAST_LEN - NSA_WINDOW + jnp.arange(NSA_WINDOW + t)
    o = nsa_core(q, pos, kvc, gather_sel, kvw, s_pos, gates, slopes)
    return o, kvw[:, t:]


def hgrn2_scan(q, logf, k, v, s0):
    b, t, h = q.shape[0], q.shape[1], q.shape[2]
    c = min(HG_CHUNK, t)
    nch = -(-t // c)
    pad = nch * c - t

    def prep(a):
        a = jnp.pad(a, ((0, 0), (0, pad), (0, 0), (0, 0)))
        return a.reshape(b, nch, c, h, a.shape[-1]).transpose(1, 0, 3, 2, 4)

    tri = (jnp.arange(c)[:, None] >= jnp.arange(c)[None, :])[:, :, None]

    def step(S, inp):
        qc, lf, kc, vc = inp
        cb = jnp.cumsum(lf, axis=2)
        o_inter = jnp.einsum('bhtk,bhkv->bhtv', qc * jnp.exp(cb), S)
        dec = jnp.exp(jnp.where(tri, cb[:, :, :, None, :] - cb[:, :, None, :, :], -jnp.inf))
        att = jnp.einsum('bhtk,bhtsk,bhsk->bhts', qc, dec, kc)
        o = o_inter + jnp.einsum('bhts,bhsv->bhtv', att, vc)
        last = cb[:, :, -1, :]
        S = jnp.exp(last)[..., None] * S + jnp.einsum('bhsk,bhsv->bhkv', kc * jnp.exp(last[:, :, None, :] - cb), vc)
        return S, o

    S, o = lax.scan(step, s0, (prep(q), prep(logf), prep(k), prep(v)))
    o = o.transpose(1, 0, 3, 2, 4).reshape(b, nch * c, h, HG_DV)[:, :t]
    return o, S


def hgrn2_branch(hq, hf, hi, hg, lb, w_hnorm, s0):
    f32 = jnp.float32
    b, t = hq.shape[0], hq.shape[1]
    lbr = lb.reshape(HG_HEADS, HG_DK)
    zf = hf.reshape(b, t, HG_HEADS, HG_DK).astype(f32)
    logf = jnp.log(lbr + (1.0 - lbr) * jax.nn.sigmoid(zf))
    k = (1.0 - lbr) * jax.nn.sigmoid(-zf)
    q = hq.reshape(b, t, HG_HEADS, HG_DK).astype(f32)
    v = hi.reshape(b, t, HG_HEADS, HG_DV).astype(f32)
    o, S = hgrn2_scan(q, logf, k, v, s0.astype(f32))
    o = rmsnorm(o, w_hnorm) * jax.nn.silu(hg.reshape(b, t, HG_HEADS, HG_DV).astype(f32))
    return o.reshape(b, t, HG_WIDTH).astype(hq.dtype), S


def peer(h, w_q, sub_keys, u_tab, v_tab):
    b, t, d = h.shape
    n = b * t
    blk = PEER_BLOCK if n % PEER_BLOCK == 0 else n

    def one(hb):
        q = (hb @ w_q).reshape(blk, PK_HEADS, 2, PK_DIM // 2)
        s = jnp.einsum('thcd,hcnd->thcn', q, sub_keys).astype(jnp.float32)
        s1, i1 = lax.top_k(s[:, :, 0], PK_TOPK)
        s2, i2 = lax.top_k(s[:, :, 1], PK_TOPK)
        cand = (s1[..., :, None] + s2[..., None, :]).reshape(blk, PK_HEADS, PK_TOPK * PK_TOPK)
        cidx = (i1[..., :, None] * N_KEYS + i2[..., None, :]).reshape(blk, PK_HEADS, PK_TOPK * PK_TOPK)
        sc, sel = lax.top_k(cand, PK_TOPK)
        eidx = jnp.take_along_axis(cidx, sel, axis=-1)
        g = jax.nn.softmax(sc, axis=-1)
        a = jax.nn.gelu(jnp.einsum('td,thkd->thk', hb, u_tab[eidx]).astype(jnp.float32))
        return jnp.einsum('thk,thkd->td', (g * a).astype(hb.dtype), v_tab[eidx])

    y = lax.map(one, h.reshape(n // blk, blk, d))
    return y.reshape(b, t, d)


def layer_forward(x, nsa_fn, hg_s0, lb, w_norm_mix, w_in, w_proj_nsa, w_proj_hgrn, w_hgrn_norm,
                  w_out, w_norm_ffn, w_peer_q, peer_sub_keys, peer_u, peer_v):
    b, t, _ = x.shape
    h = rmsnorm(x, w_norm_mix)
    z = h @ w_in
    q_a, kv_a, g_a, q_b, f_b, i_b, g_b, m_a, m_b = jnp.split(z, _split_points(), axis=-1)
    q_a = q_a.reshape(b, t, NSA_KV_HEADS, NSA_GROUP, NSA_HEAD_DIM) * (NSA_HEAD_DIM ** -0.5)
    kv = kv_a.reshape(b, t, 3, 2, NSA_KV_HEADS, NSA_HEAD_DIM)
    gates = jax.nn.sigmoid(g_a.reshape(b, t, NSA_KV_HEADS, NSA_GROUP, 3))
    o_a, new_win = nsa_fn(q_a, kv[:, :, 0], kv[:, :, 1], kv[:, :, 2], gates)
    o_b, new_hg = hgrn2_branch(q_b, f_b, i_b, g_b, lb, w_hgrn_norm, hg_s0)
    mix = (jax.nn.sigmoid(m_a) * (o_a.reshape(b, t, NSA_Q_WIDTH) @ w_proj_nsa)
           + jax.nn.sigmoid(m_b) * (o_b @ w_proj_hgrn))
    x = x + mix @ w_out
    x = x + peer(rmsnorm(x, w_norm_ffn), w_peer_q, peer_sub_keys, peer_u, peer_v)
    return x, (kv[:, :, 0], kv[:, :, 1], new_win, new_hg.astype(x.dtype))


def setup_inputs(seed: int = 0) -> dict:
    key = jax.random.key(seed)
    ks = jax.random.split(key, 24)
    f32 = jnp.float32
    n_pages = PAST_LEN // PAGE_SIZE
    n_phys = (5 * DEC_BATCH * n_pages + 3) // 4
    nrm = lambda k, shape, sc: jax.random.normal(k, shape, f32) * sc
    kvshape = (NSA_KV_HEADS, NSA_HEAD_DIM)
    page_table = jax.random.permutation(ks[0], n_phys)[:DEC_BATCH * n_pages].reshape(DEC_BATCH, n_pages).astype(jnp.int32)
    lb_logits = jnp.linspace(-1.0, 1.0, DEPTH + 1, dtype=f32)[:, None] + nrm(ks[1], (DEPTH + 1, HG_WIDTH), 0.1)
    return {
        "x_prompt": nrm(ks[2], (BATCH, SEQ, D_MODEL), 1.0),
        "x_sample": nrm(ks[3], (DEC_BATCH, DEC_SEQ, D_MODEL), 1.0),
        "cache_cmp_kv": nrm(ks[4], (DEPTH, n_phys, PAGE_SIZE, 2) + kvshape, 1.0),
        "cache_sel_kv": nrm(ks[5], (DEPTH, n_phys, PAGE_SIZE, 2) + kvshape, 1.0),
        "cache_win_kv": nrm(ks[6], (DEPTH, DEC_BATCH, NSA_WINDOW, 2) + kvshape, 1.0),
        "state_hgrn": nrm(ks[7], (DEPTH, DEC_BATCH, HG_HEADS, HG_DK, HG_DV), 0.5),
        "page_table": page_table,
        "w_norm_mix": 1.0 + nrm(ks[8], (DEPTH, D_MODEL), 0.02),
        "w_in": nrm(ks[9], (DEPTH, D_MODEL, IN_WIDTH), D_MODEL ** -0.5),
        "w_cmp": (1.0 + nrm(ks[10], (DEPTH, 2, NSA_BLOCK), 0.1)) / NSA_BLOCK,
        "w_proj_nsa": nrm(ks[11], (DEPTH, NSA_Q_WIDTH, D_MODEL), NSA_Q_WIDTH ** -0.5),
        "w_proj_hgrn": nrm(ks[12], (DEPTH, HG_WIDTH, D_MODEL), HG_WIDTH ** -0.5),
        "w_hgrn_norm": 1.0 + nrm(ks[13], (DEPTH, HG_DV), 0.02),
        "hgrn_lb_logits": lb_logits,
        "w_out": nrm(ks[14], (DEPTH, D_MODEL, D_MODEL), D_MODEL ** -0.5),
        "w_norm_ffn": 1.0 + nrm(ks[15], (DEPTH, D_MODEL), 0.02),
        "w_peer_q": nrm(ks[16], (DEPTH, D_MODEL, PK_HEADS * PK_DIM), D_MODEL ** -0.5),
        "peer_sub_keys": nrm(ks[17], (DEPTH, PK_HEADS, 2, N_KEYS, PK_DIM // 2), (PK_DIM // 2) ** -0.5),
        "peer_u": nrm(ks[18], (DEPTH, N_EXPERTS, D_MODEL), D_MODEL ** -0.5),
        "peer_v": nrm(ks[19], (DEPTH, N_EXPERTS, D_MODEL), 0.3),
        "w_norm_final": 1.0 + nrm(ks[20], (D_MODEL,), 0.02),
    }


def reference(x_prompt, x_sample, cache_cmp_kv, cache_sel_kv, cache_win_kv, state_hgrn, page_table,
              w_norm_mix, w_in, w_cmp, w_proj_nsa, w_proj_hgrn, w_hgrn_norm, hgrn_lb_logits, w_out,
              w_norm_ffn, w_peer_q, peer_sub_keys, peer_u, peer_v, w_norm_final):
    slopes = alibi_slopes()
    lbs = jnp.cumsum(jax.nn.softmax(hgrn_lb_logits.astype(jnp.float32), axis=0), axis=0)
    xp, xs = x_prompt, x_sample
    st_p, st_s = [], []
    for l in range(DEPTH):
        shared = (w_norm_mix[l], w_in[l], w_proj_nsa[l], w_proj_hgrn[l], w_hgrn_norm[l], w_out[l],
                  w_norm_ffn[l], w_peer_q[l], peer_sub_keys[l], peer_u[l], peer_v[l])
        nsa_p = functools.partial(nsa_prompt, w_cmp=w_cmp[l], slopes=slopes)
        nsa_s = functools.partial(nsa_sample, cache_cmp=cache_cmp_kv[l], cache_sel=cache_sel_kv[l],
                                  cache_win=cache_win_kv[l], page_table=page_table, w_cmp=w_cmp[l], slopes=slopes)
        hp0 = jnp.zeros((xp.shape[0], HG_HEADS, HG_DK, HG_DV), jnp.float32)
        xp, sp = layer_forward(xp, nsa_p, hp0, lbs[l], *shared)
        xs, ss = layer_forward(xs, nsa_s, state_hgrn[l], lbs[l], *shared)
        st_p.append(sp)
        st_s.append(ss)
    y_prompt = rmsnorm(xp, w_norm_final)
    y_sample = rmsnorm(xs, w_norm_final)
    p_cmp = jnp.stack([s[0] for s in st_p])
    p_sel = jnp.stack([s[1] for s in st_p])
    p_win = jnp.stack([s[2] for s in st_p])
    p_hg = jnp.stack([s[3] for s in st_p])
    s_cmp = jnp.stack([s[0] for s in st_s])
    s_sel = jnp.stack([s[1] for s in st_s])
    s_win = jnp.stack([s[2] for s in st_s])
    s_hg = jnp.stack([s[3] for s in st_s])
    return (y_prompt, y_sample, p_cmp, p_sel, p_win, p_hg, s_cmp, s_sel, s_win, s_hg)
```

```python
import functools

import jax
import jax.numpy as jnp
from jax import lax
from jax.experimental import pallas as pl
from jax.experimental.pallas import tpu as pltpu

F32 = jnp.float32
BF16 = jnp.bfloat16

NSA_HEADS = 16
NSA_KV_HEADS = 4
NSA_GROUP = NSA_HEADS // NSA_KV_HEADS
NSA_HEAD_DIM = 64
NSA_BLOCK = 64
NSA_TOPK = 16
NSA_WINDOW = 512
NSA_QBLOCK = 128
HG_HEADS = 8
HG_DK = 128
HG_DV = 128
HG_CHUNK = 64
HG_SUB = 16
PK_HEADS = 8
PK_DIM = 256
PK_TOPK = 16
RMS_EPS = 1e-6
NEG_INF = -1e30
LOWEST = -3e38

NSA_Q_WIDTH = NSA_HEADS * NSA_HEAD_DIM
NSA_KV_WIDTH = NSA_KV_HEADS * NSA_HEAD_DIM
HG_WIDTH = HG_HEADS * HG_DK

LANES = 128
VMEM_LIMIT_BYTES = 56 * 1024 * 1024

NSA_COLS = NSA_GROUP * NSA_QBLOCK
NSA_KTILE = 512
NSA_CDIM = 128
NSA_WKEYS = NSA_WINDOW + NSA_QBLOCK


def _cparams(sem):
    return pltpu.CompilerParams(dimension_semantics=sem, vmem_limit_bytes=VMEM_LIMIT_BYTES)


def _rms(x, w):
    return x * lax.rsqrt(jnp.mean(x * x, axis=-1, keepdims=True) + RMS_EPS) * w


def _proj_kernel(x_ref, wn_ref, w_ref, *out_refs, widths, transposed, chunk):
    hb = _rms(x_ref[...], wn_ref[...]).astype(BF16)
    off = 0
    for o_ref, wd, tr in zip(out_refs, widths, transposed):
        for c0 in range(0, wd, chunk):
            cw = min(chunk, wd - c0)
            r = jnp.dot(hb, w_ref[:, off + c0:off + c0 + cw], preferred_element_type=F32)
            if tr:
                o_ref[c0:c0 + cw, :] = r.T
            else:
                o_ref[:, c0:c0 + cw] = r
        off += wd


def _proj(x, wn, w, widths, transposed, tm):
    t, d = x.shape
    n = w.shape[1]
    assert sum(widths) == n and t % tm == 0
    out_shape, out_specs = [], []
    for wd, tr in zip(widths, transposed):
        if tr:
            out_shape.append(jax.ShapeDtypeStruct((wd, t), F32))
            out_specs.append(pl.BlockSpec((wd, tm), lambda i: (0, i)))
        else:
            out_shape.append(jax.ShapeDtypeStruct((t, wd), F32))
            out_specs.append(pl.BlockSpec((tm, wd), lambda i: (i, 0)))
    return pl.pallas_call(
        functools.partial(_proj_kernel, widths=tuple(widths), transposed=tuple(transposed), chunk=512),
        grid=(t // tm,),
        in_specs=[pl.BlockSpec((tm, d), lambda i: (i, 0)),
                  pl.BlockSpec((1, d), lambda i: (0, 0)),
                  pl.BlockSpec((d, n), lambda i: (0, 0))],
        out_specs=out_specs, out_shape=out_shape,
        compiler_params=_cparams(("parallel",)), name="rms_proj",
    )(x, wn, w)


def _compress_kernel(kv_ref, w_ref, o_ref, *, nb):
    x = kv_ref[...]
    width = x.shape[-1]
    x3 = x.reshape(nb, NSA_BLOCK, width) * w_ref[...][None]
    o_ref[...] = jnp.sum(x3, axis=1)


def _compress(kv, wfull, nb):
    t, width = kv.shape
    rows = nb * NSA_BLOCK
    assert t % rows == 0
    return pl.pallas_call(
        functools.partial(_compress_kernel, nb=nb),
        grid=(t // rows,),
        in_specs=[pl.BlockSpec((rows, width), lambda i: (i, 0)),
                  pl.BlockSpec((NSA_BLOCK, width), lambda i: (0, 0))],
        out_specs=pl.BlockSpec((nb, width), lambda i: (i, 0)),
        out_shape=jax.ShapeDtypeStruct((t // NSA_BLOCK, width), F32),
        compiler_params=_cparams(("parallel",)), name="nsa_compress",
    )(kv, wfull)


def _topk_select_bias(score, k):
    n = score.shape[0]
    rows = lax.broadcasted_iota(jnp.int32, score.shape, 0).astype(F32)
    bias = jnp.full(score.shape, NEG_INF, F32)
    for _ in range(k):
        mx = jnp.max(score, axis=0, keepdims=True)
        idx = jnp.min(jnp.where(score == mx, rows, float(n)), axis=0, keepdims=True)
        hit = rows == idx
        bias = jnp.where(hit, 0.0, bias)
        score = jnp.where(hit, LOWEST, score)
    return bias


def _nsa_prompt_kernel(slope_ref, qT_ref, gT_ref, kc_ref, vcT_ref, ksel_ref, vselT_ref,
                       kw0, kw1, kw2, kw3, kw4, vw0, vw1, vw2, vw3, vw4, bw_ref,
                       out_ref, qs_ref, selb_ref, *, nblk):
    i = pl.program_id(1)
    t0 = i * NSA_QBLOCK
    dh, qb, ncol, tk = NSA_HEAD_DIM, NSA_QBLOCK, NSA_COLS, NSA_KTILE
    slope = slope_ref[...]
    col = lax.broadcasted_iota(jnp.int32, (1, ncol), 1)
    tpos = t0 + (col & (qb - 1))
    tposf = tpos.astype(F32)
    q4 = qT_ref[...] * (dh ** -0.5)
    qT = jnp.concatenate([q4[g * dh:(g + 1) * dh, :] for g in range(NSA_GROUP)], axis=1)
    xrow = lax.broadcasted_iota(jnp.int32, (NSA_CDIM - dh, ncol), 0)

    qc = jnp.concatenate([qT, jnp.zeros((NSA_CDIM - dh, ncol), F32)], axis=0).astype(BF16)
    sc = jnp.dot(kc_ref[...], qc, preferred_element_type=F32)
    c_end = lax.broadcasted_iota(jnp.int32, (nblk, 1), 0) * NSA_BLOCK + (NSA_BLOCK - 1)
    valid = c_end <= tpos
    s = jnp.where(valid, sc - slope * (tposf - c_end.astype(F32)), NEG_INF)
    e = jnp.exp(s - jnp.max(s, axis=0, keepdims=True))
    p = jnp.where(valid, e / jnp.sum(e, axis=0, keepdims=True), 0.0)
    ocT = jnp.dot(vcT_ref[...], p.astype(BF16), preferred_element_type=F32)

    imp = p[:, 0:qb]
    for g in range(1, NSA_GROUP):
        imp = imp + p[:, g * qb:(g + 1) * qb]
    blk = lax.broadcasted_iota(jnp.int32, (nblk, qb), 0)
    cur = tpos[:, 0:qb] >> (NSA_BLOCK.bit_length() - 1)
    forced = (blk == 0) | (blk == cur) | (blk == cur - 1)
    score = jnp.where(blk > cur, -1.0, jnp.where(forced, NSA_GROUP + 1.0, imp))
    selb = _topk_select_bias(score, min(NSA_TOPK, nblk))
    selb_ref[...] = jnp.concatenate([selb] * NSA_GROUP, axis=1)

    s1 = slope.astype(BF16).astype(F32)
    r1 = slope - s1
    s2 = r1.astype(BF16).astype(F32)
    s3 = (r1 - s2).astype(BF16).astype(F32)
    half = float(tk // 2)
    ext = jnp.zeros((NSA_CDIM - dh, ncol), F32)
    for r, v in enumerate((s1, s2, s3, s1 * half, s2 * half, s3 * half)):
        ext = jnp.where(xrow == 8 + r, v, ext)
    qs_ref[0:dh, :] = qT
    qs_ref[dh:, :] = ext

    def kv_step(j, carry, causal):
        m, l, acc = carry
        b0 = pl.multiple_of(j * (tk // NSA_BLOCK), 8)
        qs_ref[dh:dh + 8, :] = selb_ref[pl.ds(b0, tk // NSA_BLOCK), :]
        qp = qs_ref[...].astype(BF16)
        k0 = pl.multiple_of(j * tk, tk)
        sj = jnp.dot(ksel_ref[pl.ds(k0, tk), :], qp, preferred_element_type=F32)
        if causal:
            kpos = k0 + lax.broadcasted_iota(jnp.int32, (tk, 1), 0)
            sj = jnp.where(kpos > tpos, NEG_INF, sj)
        cj = slope * (k0 - tpos).astype(F32)
        m_new = jnp.maximum(m, jnp.max(sj, axis=0, keepdims=True) + cj)
        pj = jnp.exp(sj - (m_new - cj))
        alpha = jnp.exp(m - m_new)
        l = alpha * l + jnp.sum(pj, axis=0, keepdims=True)
        acc = alpha * acc + jnp.dot(vselT_ref[j], pj.astype(BF16), preferred_element_type=F32)
        return m_new, l, acc

    jl = t0 // tk
    init = (jnp.full((1, ncol), NEG_INF, F32), jnp.zeros((1, ncol), F32), jnp.zeros((dh, ncol), F32))
    carry = lax.fori_loop(0, jl, lambda j, c: kv_step(j, c, False), init)
    _, l_s, acc_s = kv_step(jl, carry, True)
    osT = acc_s / l_s

    qw = jnp.concatenate([qT, jnp.where(xrow == 0, NEG_INF, 0.0)], axis=0).astype(BF16)
    kw = jnp.concatenate([kw0[...], kw1[...], kw2[...], kw3[...], kw4[...]], axis=0)
    sw = jnp.dot(kw, qw, preferred_element_type=F32) + bw_ref[...]
    ew = jnp.exp(sw - jnp.max(sw, axis=0, keepdims=True))
    vw = jnp.concatenate([vw0[...], vw1[...], vw2[...], vw3[...], vw4[...]], axis=1)
    owT = jnp.dot(vw, ew.astype(BF16), preferred_element_type=F32) / jnp.sum(ew, axis=0, keepdims=True)

    sg = jax.nn.sigmoid(gT_ref[...])

    def gate(c):
        return jnp.concatenate([sg[c * NSA_GROUP + g:c * NSA_GROUP + g + 1, :] for g in range(NSA_GROUP)], axis=1)

    oT = gate(0) * ocT + gate(1) * osT + gate(2) * owT
    o4 = jnp.concatenate([oT[:, g * qb:(g + 1) * qb] for g in range(NSA_GROUP)], axis=0)
    out_ref[...] = o4.T


def _alibi_slopes():
    h = jnp.arange(NSA_HEADS, dtype=F32)
    return (2.0 ** (-8.0 * (h + 1.0) / NSA_HEADS)).reshape(NSA_KV_HEADS, NSA_GROUP)


def _nsa_prompt(qT, gT, kvc, kv_sel, kv_win):
    s_len = qT.shape[1]
    dh, qb, tk, kvh = NSA_HEAD_DIM, NSA_QBLOCK, NSA_KTILE, NSA_KV_HEADS
    assert s_len % tk == 0
    nblk = s_len // NSA_BLOCK
    slopes = _alibi_slopes()
    slope_cols = jnp.repeat(slopes, qb, axis=1).reshape(kvh, 1, NSA_COLS)

    def heads_major(a):
        return a.reshape(a.shape[0], kvh, dh).transpose(1, 0, 2)

    def pad_lanes(a):
        return jnp.pad(a, ((0, 0), (0, 0), (0, NSA_CDIM - a.shape[-1])))

    kc = pad_lanes(heads_major(kvc[:, :NSA_KV_WIDTH])).astype(BF16)
    vcT = heads_major(kvc[:, NSA_KV_WIDTH:]).transpose(0, 2, 1).astype(BF16)
    r = jnp.arange(tk)
    onehot = (r[:, None] // NSA_BLOCK == jnp.arange(tk // NSA_BLOCK)[None, :]).astype(F32)
    lo = (r % (tk // 2)).astype(F32)[:, None]
    hi = (r // (tk // 2)).astype(F32)[:, None]
    kext = jnp.concatenate([onehot, lo, lo, lo, hi, hi, hi], axis=1)
    kext = jnp.tile(kext, (s_len // tk, 1))
    ksel = pad_lanes(jnp.concatenate(
        [heads_major(kv_sel[:, :NSA_KV_WIDTH]), jnp.broadcast_to(kext[None], (kvh,) + kext.shape)], axis=-1)).astype(BF16)
    vselT = heads_major(kv_sel[:, NSA_KV_WIDTH:]).reshape(kvh, s_len // tk, tk, dh).transpose(0, 1, 3, 2).astype(BF16)
    kwin = heads_major(kv_win[:, :NSA_KV_WIDTH])
    kwin = jnp.concatenate([kwin, jnp.zeros((kvh, s_len, 1), F32)], axis=-1)
    padk = jnp.zeros((kvh, NSA_WINDOW, dh + 1), F32).at[:, :, dh].set(1.0)
    kwin = pad_lanes(jnp.concatenate([padk, kwin], axis=1)).astype(BF16)
    vwinT = jnp.pad(heads_major(kv_win[:, NSA_KV_WIDTH:]), ((0, 0), (NSA_WINDOW, 0), (0, 0))).transpose(0, 2, 1).astype(BF16)
    rr = jnp.arange(NSA_WKEYS)[:, None]
    cc = jnp.arange(NSA_COLS)[None, :]
    dw = (cc % qb) + NSA_WINDOW - rr
    bw = jnp.where((dw >= 0) & (dw <= NSA_WINDOW), -slope_cols * dw.astype(F32)[None], NEG_INF)

    nq = s_len // qb
    nwb = NSA_WKEYS // qb
    kw_specs = [pl.BlockSpec((None, qb, NSA_CDIM), functools.partial(lambda k, i, j: (k, i + j, 0), j=j)) for j in range(nwb)]
    vw_specs = [pl.BlockSpec((None, dh, qb), functools.partial(lambda k, i, j: (k, 0, i + j), j=j)) for j in range(nwb)]
    assert nwb == 5
    return pl.pallas_call(
        functools.partial(_nsa_prompt_kernel, nblk=nblk),
        grid=(kvh, nq),
        in_specs=[pl.BlockSpec((None, 1, NSA_COLS), lambda k, i: (k, 0, 0)),
                  pl.BlockSpec((NSA_GROUP * dh, qb), lambda k, i: (k, i)),
                  pl.BlockSpec((None, 16, qb), lambda k, i: (k, 0, i)),
                  pl.BlockSpec((None, nblk, NSA_CDIM), lambda k, i: (k, 0, 0)),
                  pl.BlockSpec((None, dh, nblk), lambda k, i: (k, 0, 0)),
                  pl.BlockSpec((None, s_len, NSA_CDIM), lambda k, i: (k, 0, 0)),
                  pl.BlockSpec((None, s_len // tk, dh, tk), lambda k, i: (k, 0, 0, 0))]
                 + kw_specs + vw_specs
                 + [pl.BlockSpec((None, NSA_WKEYS, NSA_COLS), lambda k, i: (k, 0, 0))],
        out_specs=pl.BlockSpec((qb, NSA_GROUP * dh), lambda k, i: (i, k)),
        out_shape=jax.ShapeDtypeStruct((s_len, NSA_Q_WIDTH), F32),
        scratch_shapes=[pltpu.VMEM((NSA_CDIM, NSA_COLS), F32), pltpu.VMEM((nblk, NSA_COLS), F32)],
        compiler_params=_cparams(("arbitrary", "arbitrary")), name="nsa_prompt",
    )(slope_cols, qT, gT.reshape(kvh, 16, s_len), kc, vcT, ksel, vselT,
      *([kwin] * nwb), *([vwinT] * nwb), bw)


def _hgrn_chunk(qc, zf, vc, lb, st):
    c, sub = HG_CHUNK, HG_SUB
    logf = jnp.log(lb + (1.0 - lb) * jax.nn.sigmoid(zf))
    kc = (1.0 - lb) * jax.nn.sigmoid(-zf)
    tri = (lax.broadcasted_iota(jnp.int32, (c, c), 0) >= lax.broadcasted_iota(jnp.int32, (c, c), 1)).astype(F32)
    cb = jnp.dot(tri, logf, preferred_element_type=F32, precision=lax.Precision.HIGHEST)
    o = lax.dot_general((qc * jnp.exp(cb)).astype(BF16), st.astype(BF16), (((1,), (1,)), ((), ())),
                        preferred_element_type=F32)
    t3 = lax.broadcasted_iota(jnp.int32, (sub, sub, 1), 0) >= lax.broadcasted_iota(jnp.int32, (sub, sub, 1), 1)
    outs = []
    for a in range(c // sub):
        ra = slice(a * sub, (a + 1) * sub)
        cba, qa, ka, va = cb[ra], qc[ra], kc[ra], vc[ra]
        d3 = cba[:, None, :] - cba[None, :, :]
        x3 = jnp.where(t3, jnp.exp(d3), 0.0) * qa[:, None, :] * ka[None, :, :]
        att3 = jnp.sum(x3, axis=2, keepdims=True)
        oa = o[ra] + jnp.sum(att3 * va[None, :, :], axis=1)
        if a > 0:
            ref = cb[a * sub - 1:a * sub, :]
            qd = (qa * jnp.exp(cba - ref)).astype(BF16)
            kd = (kc[:a * sub] * jnp.exp(ref - cb[:a * sub])).astype(BF16)
            att = lax.dot_general(qd, kd, (((1,), (1,)), ((), ())), preferred_element_type=F32)
            oa = oa + jnp.dot(att.astype(BF16), vc[:a * sub].astype(BF16), preferred_element_type=F32)
        outs.append(oa)
    o = jnp.concatenate(outs, axis=0)
    last = cb[c - 1:c, :]
    kdec = (kc * jnp.exp(last - cb)).astype(BF16)
    st = st * jnp.exp(last) + lax.dot_general(vc.astype(BF16), kdec, (((0,), (0,)), ((), ())),
                                              preferred_element_type=F32)
    return o, st


def _hgrn_kernel(lb_ref, wn_ref, q_ref, f_ref, v_ref, g_ref, s0_ref, o_ref, sfin_ref, st_ref, *, nsub):
    c = pl.program_id(1)

    @pl.when(c == 0)
    def _():
        st_ref[...] = s0_ref[...].T

    lb = lb_ref[...]
    wn = wn_ref[...]

    def body(u, st):
        rows = pl.ds(pl.multiple_of(u * HG_CHUNK, HG_CHUNK), HG_CHUNK)
        o, st = _hgrn_chunk(q_ref[rows, :], f_ref[rows, :], v_ref[rows, :], lb, st)
        g = g_ref[rows, :]
        o_ref[rows, :] = _rms(o, wn) * (g * jax.nn.sigmoid(g))
        return st

    st = lax.fori_loop(0, nsub, body, st_ref[...])
    st_ref[...] = st

    @pl.when(c == pl.num_programs(1) - 1)
    def _():
        sfin_ref[...] = st.T


def _hgrn_prompt(hq, hf, hi, hg, lb, wn, s0, tb):
    t = hq.shape[0]
    assert t % tb == 0 and tb % HG_CHUNK == 0
    tok = pl.BlockSpec((tb, HG_DK), lambda h, c: (c, h))
    return pl.pallas_call(
        functools.partial(_hgrn_kernel, nsub=tb // HG_CHUNK),
        grid=(HG_HEADS, t // tb),
        in_specs=[pl.BlockSpec((1, HG_DK), lambda h, c: (0, h)),
                  pl.BlockSpec((1, HG_DV), lambda h, c: (0, 0)),
                  tok, tok, tok, tok,
                  pl.BlockSpec((None, HG_DK, HG_DV), lambda h, c: (h, 0, 0))],
        out_specs=[pl.BlockSpec((tb, HG_DV), lambda h, c: (c, h)),
                   pl.BlockSpec((None, HG_DK, HG_DV), lambda h, c: (h, 0, 0))],
        out_shape=[jax.ShapeDtypeStruct((t, HG_HEADS * HG_DV), F32),
                   jax.ShapeDtypeStruct((HG_HEADS, HG_DK, HG_DV), F32)],
        scratch_shapes=[pltpu.VMEM((HG_DV, HG_DK), F32)],
        compiler_params=_cparams(("arbitrary", "arbitrary")), name="hgrn_prompt",
    )(lb, wn, hq, hf, hi, hg, s0)


def _hgrn_step_kernel(lb_ref, wn_ref, q_ref, f_ref, v_ref, g_ref, s0_ref, o_ref, s_ref):
    wn = wn_ref[...]
    for h in range(HG_HEADS):
        rk = slice(h * HG_DK, (h + 1) * HG_DK)
        lb = lb_ref[rk, :]
        zf = f_ref[rk, :]
        f = lb + (1.0 - lb) * jax.nn.sigmoid(zf)
        kk = (1.0 - lb) * jax.nn.sigmoid(-zf)
        vrow = v_ref[:, h * HG_DV:(h + 1) * HG_DV]
        s_new = f * s0_ref[h] + kk * vrow
        s_ref[h] = s_new
        o = jnp.sum(s_new * q_ref[rk, :], axis=0, keepdims=True)
        g = g_ref[:, h * HG_DV:(h + 1) * HG_DV]
        o_ref[:, h * HG_DV:(h + 1) * HG_DV] = _rms(o, wn) * (g * jax.nn.sigmoid(g))


def _hgrn_step(hq_col, hf_col, hi, hg, lb_col, wn, s0):
    b = hi.shape[0]
    col = pl.BlockSpec((None, HG_WIDTH, 1), lambda i: (i, 0, 0))
    row = pl.BlockSpec((None, 1, HG_WIDTH), lambda i: (i, 0, 0))
    st = pl.BlockSpec((None, HG_HEADS, HG_DK, HG_DV), lambda i: (i, 0, 0, 0))
    return pl.pallas_call(
        _hgrn_step_kernel, grid=(b,),
        in_specs=[pl.BlockSpec((HG_WIDTH, 1), lambda i: (0, 0)), pl.BlockSpec((1, HG_DV), lambda i: (0, 0)),
                  col, col, row, row, st],
        out_specs=[row, st],
        out_shape=[jax.ShapeDtypeStruct((b, 1, HG_WIDTH), F32), jax.ShapeDtypeStruct(s0.shape, F32)],
        compiler_params=_cparams(("parallel",)), name="hgrn_step",
    )(lb_col, wn, hq_col, hf_col, hi, hg, s0)


def _merge_kernel(x_ref, oa_ref, ob_ref, ma_ref, mb_ref, pa_ref, pb_ref, wo_ref, y_ref):
    ya = jnp.dot(oa_ref[...].astype(BF16), pa_ref[...], preferred_element_type=F32)
    yb = jnp.dot(ob_ref[...].astype(BF16), pb_ref[...], preferred_element_type=F32)
    mix = jax.nn.sigmoid(ma_ref[...]) * ya + jax.nn.sigmoid(mb_ref[...]) * yb
    y_ref[...] = x_ref[...] + jnp.dot(mix.astype(BF16), wo_ref[...], preferred_element_type=F32)


def _merge(x, oa, ob, ma, mb, pa, pb, wo, tm):
    t, d = x.shape
    tok = pl.BlockSpec((tm, d), lambda i: (i, 0))
    wsp = pl.BlockSpec((d, d), lambda i: (0, 0))
    return pl.pallas_call(
        _merge_kernel, grid=(t // tm,),
        in_specs=[tok, tok, tok, tok, tok, wsp, wsp, wsp],
        out_specs=tok, out_shape=jax.ShapeDtypeStruct((t, d), F32),
        compiler_params=_cparams(("parallel",)), name="branch_merge",
    )(x, oa, ob, ma, mb, pa, pb, wo)


def _topk_rows(s, k):
    n = s.shape[0]
    rows = lax.broadcasted_iota(jnp.int32, s.shape, 0).astype(F32)
    rank = jnp.full(s.shape, float(n), F32)
    tops = []
    for r in range(k):
        mx = jnp.max(s, axis=0, keepdims=True)
        idx = jnp.min(jnp.where(s == mx, rows, float(n)), axis=0, keepdims=True)
        hit = rows == idx
        rank = jnp.where(hit, float(r), rank)
        s = jnp.where(hit, LOWEST, s)
        tops.append(mx)
    return jnp.concatenate(tops, axis=0), rank


def _peer_kernel(x_ref, wn_ref, wq_ref, sk_ref, u_ref, vT_ref, wf_ref, y_ref,
                 hnT_ref, acc_ref, n_ref, a1_ref, r2_ref, e2_ref, *, ib, nkeys, final_norm):
    i = pl.program_id(1)
    kt = PK_TOPK
    hd = PK_DIM // 2

    @pl.when(i == 0)
    def _():
        hnT = _rms(x_ref[...], wn_ref[...]).T.astype(BF16)
        hnT_ref[...] = hnT
        acc_ref[...] = jnp.zeros(acc_ref.shape, F32)
        tt = hnT.shape[1]
        for h in range(PK_HEADS):
            ss, tops, ranks = [], [], []
            for c in range(2):
                r0 = (h * 2 + c) * hd
                qhc = jnp.dot(wq_ref[r0:r0 + hd, :], hnT, preferred_element_type=F32)
                s = jnp.dot(sk_ref[h * 2 + c], qhc.astype(BF16), preferred_element_type=F32)
                top, rank = _topk_rows(s, kt)
                ss.append(s)
                tops.append(top)
                ranks.append(rank)
            rep1 = jnp.concatenate([jnp.broadcast_to(tops[0][a:a + 1, :], (kt, tt)) for a in range(kt)], axis=0)
            til2 = jnp.concatenate([tops[1]] * kt, axis=0)
            cand = rep1 + til2
            _, crank = _topk_rows(cand, kt)
            selc = (crank < float(kt)).astype(F32)
            n_a = jnp.sum(selc.reshape(kt, kt, tt), axis=1)
            e1t = jnp.exp(tops[0] - tops[0][0:1, :])
            e2t = jnp.exp(tops[1] - tops[1][0:1, :])
            rep_e1 = jnp.concatenate([jnp.broadcast_to(e1t[a:a + 1, :], (kt, tt)) for a in range(kt)], axis=0)
            z = jnp.sum(selc * rep_e1 * jnp.concatenate([e2t] * kt, axis=0), axis=0, keepdims=True)
            nfull = jnp.zeros((nkeys, tt), F32)
            for a in range(kt):
                nfull = jnp.where(ranks[0] == float(a), n_a[a:a + 1, :], nfull)
            n_ref[h] = nfull
            a1_ref[h] = jnp.exp(ss[0] - tops[0][0:1, :]) / z
            r2_ref[h] = ranks[1]
            e2_ref[h] = jnp.exp(ss[1] - tops[1][0:1, :])

    hnT = hnT_ref[...]
    for ii in range(ib):
        ig = i * ib + ii
        aT = jnp.dot(u_ref[ii * nkeys:(ii + 1) * nkeys, :], hnT, preferred_element_type=F32)
        gsum = jnp.zeros(aT.shape, F32)
        for h in range(PK_HEADS):
            nrow = n_ref[h, pl.ds(ig, 1), :]
            arow = a1_ref[h, pl.ds(ig, 1), :]
            gsum = gsum + jnp.where(r2_ref[h] < nrow, arow * e2_ref[h], 0.0)
        wt = (jax.nn.gelu(aT) * gsum).astype(BF16)
        acc_ref[...] += jnp.dot(vT_ref[ii], wt, preferred_element_type=F32)

    @pl.when(i == pl.num_programs(1) - 1)
    def _():
        y = x_ref[...] + acc_ref[...].T
        if final_norm:
            y = _rms(y, wf_ref[...])
        y_ref[...] = y


def _peer(x, wn, wqT, sk, u, vT, wf, tt, ib, final_norm):
    t, d = x.shape
    nkeys = sk.shape[1]
    assert t % tt == 0 and nkeys % ib == 0
    stat = pltpu.VMEM((PK_HEADS, nkeys, tt), F32)
    return pl.pallas_call(
        functools.partial(_peer_kernel, ib=ib, nkeys=nkeys, final_norm=final_norm),
        grid=(t // tt, nkeys // ib),
        in_specs=[pl.BlockSpec((tt, d), lambda a, i: (a, 0)),
                  pl.BlockSpec((1, d), lambda a, i: (0, 0)),
                  pl.BlockSpec(wqT.shape, lambda a, i: (0, 0)),
                  pl.BlockSpec(sk.shape, lambda a, i: (0, 0, 0)),
                  pl.BlockSpec((ib * nkeys, d), lambda a, i: (i, 0)),
                  pl.BlockSpec((ib, d, nkeys), lambda a, i: (i, 0, 0)),
                  pl.BlockSpec((1, d), lambda a, i: (0, 0))],
        out_specs=pl.BlockSpec((tt, d), lambda a, i: (a, 0)),
        out_shape=jax.ShapeDtypeStruct((t, d), F32),
        scratch_shapes=[pltpu.VMEM((d, tt), BF16), pltpu.VMEM((d, tt), F32), stat, stat, stat, stat],
        compiler_params=_cparams(("arbitrary", "arbitrary")), name="peer_dense",
    )(x, wn, wqT, sk, u, vT, wf)


def _masked_softmax(s, mask):
    p = jax.nn.softmax(jnp.where(mask, s.astype(F32), NEG_INF), axis=-1)
    return jnp.where(mask, p, 0.0)


def _nsa_sample(q, kv_cmp, kv_sel, kv_win, gates, cache_cmp, cache_sel, cache_win, page_table, w_cmp, slopes):
    db, t = q.shape[0], q.shape[1]
    L = NSA_BLOCK
    page_size = cache_cmp.shape[1]
    past_len = page_table.shape[1] * page_size
    npb = past_len // L
    nnb = -(-t // L)
    pad = nnb * L - t
    bpp = page_size // L
    pos = past_len + jnp.arange(t)
    padw = ((0, 0), (0, pad), (0, 0), (0, 0), (0, 0))
    past_cmp = cache_cmp[page_table].reshape(db, npb, L, 2, NSA_KV_HEADS, NSA_HEAD_DIM)
    new_cmp = jnp.pad(kv_cmp, padw).reshape(db, nnb, L, 2, NSA_KV_HEADS, NSA_HEAD_DIM)
    kvc = jnp.einsum('bnlckd,cl->bnckd', jnp.concatenate([past_cmp, new_cmp], axis=1), w_cmp)
    new_sel = jnp.pad(kv_sel, padw).reshape(db, nnb, L, 2, NSA_KV_HEADS, NSA_HEAD_DIM)
    bi = jnp.arange(db)[:, None, None, None]
    kh = jnp.arange(NSA_KV_HEADS)[None, None, :, None]
    nc = kvc.shape[1]
    tf = pos.astype(F32)
    sl = slopes[None, None, :, :, None]
    c_end = (jnp.arange(nc) + 1) * L - 1
    s_c = jnp.einsum('bqkgd,bnkd->bqkgn', q, kvc[:, :, 0]).astype(F32)
    s_c = s_c - sl * (tf[:, None] - c_end.astype(F32)[None, :])[None, :, None, None, :]
    valid_c = (c_end[None, :] <= pos[:, None])[None, :, None, None, :]
    p_c = _masked_softmax(s_c, valid_c)
    o_c = jnp.einsum('bqkgn,bnkd->bqkgd', p_c, kvc[:, :, 1])
    blk = jnp.arange(nc)[None, :]
    cur = (pos // L)[:, None]
    forced = ((blk == 0) | (blk == cur) | (blk == cur - 1))[None, :, None, :]
    future = (blk > cur)[None, :, None, :]
    imp = p_c.sum(axis=3)
    score = jnp.where(future, -1.0, jnp.where(forced, NSA_GROUP + 1.0, imp))
    _, idx = lax.top_k(score, min(NSA_TOPK, nc))
    pidx = jnp.clip(idx, 0, npb - 1)
    page = page_table[bi, pidx // bpp]
    rows = ((pidx % bpp) * L)[..., None] + jnp.arange(L)
    past = cache_sel[page[..., None], rows, :, kh[..., None]]
    nidx = jnp.clip(idx - npb, 0, nnb - 1)
    new = new_sel[bi, nidx, :, :, kh]
    gsel = jnp.where((idx >= npb)[..., None, None, None], new, past)
    ks, vs = gsel[..., 0, :], gsel[..., 1, :]
    kpos = idx[..., None] * L + jnp.arange(L)
    dist = pos[None, :, None, None, None] - kpos
    s_s = jnp.einsum('bqkgd,bqknld->bqkgnl', q, ks).astype(F32)
    s_s = s_s - sl[..., None] * dist[:, :, :, None].astype(F32)
    mask_s = (dist >= 0)[:, :, :, None]
    s_s = s_s.reshape(s_s.shape[:4] + (-1,))
    mask_s = jnp.broadcast_to(mask_s, mask_s.shape[:3] + (NSA_GROUP,) + mask_s.shape[4:]).reshape(s_s.shape)
    p_s = _masked_softmax(s_s, mask_s)
    vs = vs.reshape(vs.shape[:3] + (-1, NSA_HEAD_DIM))
    o_s = jnp.einsum('bqkgm,bqkmd->bqkgd', p_s, vs)
    kvw = jnp.concatenate([cache_win, kv_win], axis=1)
    s_pos = past_len - NSA_WINDOW + jnp.arange(NSA_WINDOW + t)
    dw = pos[:, None] - s_pos[None, :]
    s_w = jnp.einsum('bqkgd,bwkd->bqkgw', q, kvw[:, :, 0]).astype(F32)
    s_w = s_w - sl * dw.astype(F32)[None, :, None, None, :]
    mask_w = ((dw >= 0) & (dw <= NSA_WINDOW) & (s_pos[None, :] >= 0))[None, :, None, None, :]
    p_w = _masked_softmax(s_w, mask_w)
    o_w = jnp.einsum('bqkgw,bwkd->bqkgd', p_w, kvw[:, :, 1])
    o = gates[..., 0:1] * o_c + gates[..., 1:2] * o_s + gates[..., 2:3] * o_w
    return o, kvw[:, t:]


def _split_w_in(w_in):
    pts, acc = [], 0
    for w in (NSA_Q_WIDTH, 6 * NSA_KV_WIDTH, 3 * NSA_HEADS, HG_WIDTH, HG_WIDTH, HG_WIDTH, HG_WIDTH, w_in.shape[0]):
        acc += w
        pts.append(acc)
    return jnp.split(w_in, pts, axis=1)


def _prep_layer(w_in, w_proj_nsa, w_proj_hgrn, w_out, w_peer_q, peer_sub_keys, peer_u, peer_v):
    d = w_in.shape[0]
    wq, wkv, wg, wbq, wbf, wbi, wbg, wma, wmb = _split_w_in(w_in)
    wg = wg.reshape(d, NSA_KV_HEADS, NSA_GROUP, 3).transpose(0, 1, 3, 2).reshape(d, NSA_KV_HEADS, 3 * NSA_GROUP)
    wg = jnp.pad(wg, ((0, 0), (0, 0), (0, 16 - 3 * NSA_GROUP))).reshape(d, NSA_KV_HEADS * 16)
    wg = jnp.pad(wg, ((0, 0), (0, LANES - NSA_KV_HEADS * 16)))
    nkeys = peer_sub_keys.shape[2]
    return dict(
        w_qg=jnp.concatenate([wq, wg], axis=1).astype(BF16),
        w_kv=wkv.astype(BF16),
        w_hg=jnp.concatenate([wbq, wbf, wbi, wbg], axis=1).astype(BF16),
        w_m=jnp.concatenate([wma, wmb], axis=1).astype(BF16),
        pa=w_proj_nsa.astype(BF16), pb=w_proj_hgrn.astype(BF16), wo=w_out.astype(BF16),
        wqT=w_peer_q.T.astype(BF16),
        sk=peer_sub_keys.reshape(PK_HEADS * 2, nkeys, PK_DIM // 2).astype(BF16),
        u=peer_u.astype(BF16),
        vT=peer_v.reshape(nkeys, nkeys, d).transpose(0, 2, 1).astype(BF16),
    )


def _tile(t, pref):
    return pref if t % pref == 0 else t


def _peer_tokens(x, wn, wf, prm, final_norm):
    t = x.shape[0]
    tp = -(-t // LANES) * LANES
    xp = jnp.pad(x, ((0, tp - t), (0, 0)))
    tt = 512 if tp % 512 == 0 else LANES
    nkeys = prm["sk"].shape[1]
    y = _peer(xp, wn, prm["wqT"], prm["sk"], prm["u"], prm["vT"], wf, tt, 4 if nkeys % 4 == 0 else 1, final_norm)
    return y[:t]


def _layer_prompt(x, lb, prm, w_norm_mix, w_cmp, w_hgrn_norm, w_norm_ffn, w_norm_final, final_norm):
    s_len, d = x.shape
    tm = _tile(s_len, 256)
    kvw = NSA_KV_WIDTH
    kv_cmp, kv_sel, kv_win = _proj(x, w_norm_mix, prm["w_kv"], [2 * kvw] * 3, [False] * 3, tm)
    qT, gT = _proj(x, w_norm_mix, prm["w_qg"], [NSA_Q_WIDTH, LANES], [True, True], tm)
    hq, hf, hi, hg = _proj(x, w_norm_mix, prm["w_hg"], [HG_WIDTH] * 4, [False] * 4, tm)
    ma, mb = _proj(x, w_norm_mix, prm["w_m"], [d, d], [False, False], tm)
    wfull = jnp.concatenate([jnp.broadcast_to(w_cmp[0][:, None], (NSA_BLOCK, kvw)),
                             jnp.broadcast_to(w_cmp[1][:, None], (NSA_BLOCK, kvw))], axis=1)
    nblk = s_len // NSA_BLOCK
    kvc = _compress(kv_cmp, wfull, 8 if nblk % 8 == 0 else nblk)
    o_a = _nsa_prompt(qT, gT[:NSA_KV_HEADS * 16], kvc, kv_sel, kv_win)
    s0 = jnp.zeros((HG_HEADS, HG_DK, HG_DV), F32)
    o_b, s_fin = _hgrn_prompt(hq, hf, hi, hg, lb, w_hgrn_norm, s0, _tile(s_len, 512))
    x1 = _merge(x, o_a, o_b, ma, mb, prm["pa"], prm["pb"], prm["wo"], tm)
    x2 = _peer_tokens(x1, w_norm_ffn, w_norm_final, prm, final_norm)
    shp = (s_len, 2, NSA_KV_HEADS, NSA_HEAD_DIM)
    return x2, kv_cmp.reshape(shp), kv_sel.reshape(shp), kv_win.reshape(shp)[-NSA_WINDOW:], s_fin


def _layer_sample(x, lb, prm, w_norm_mix, w_cmp, w_hgrn_norm, w_norm_ffn, w_norm_final, final_norm,
                  cache_cmp, cache_sel, cache_win, s0, page_table):
    b, d = x.shape
    kvw = NSA_KV_WIDTH
    w_q = prm["w_qg"][:, :NSA_Q_WIDTH]
    (q,) = _proj(x, w_norm_mix, w_q, [NSA_Q_WIDTH], [False], b)
    kv_cmp, kv_sel, kv_win = _proj(x, w_norm_mix, prm["w_kv"], [2 * kvw] * 3, [False] * 3, b)
    (gTt,) = _proj(x, w_norm_mix, prm["w_qg"][:, NSA_Q_WIDTH:], [LANES], [False], b)
    hq, hf, hi, hg = _proj(x, w_norm_mix, prm["w_hg"], [HG_WIDTH] * 4, [False] * 4, b)
    ma, mb = _proj(x, w_norm_mix, prm["w_m"], [d, d], [False, False], b)
    shp = (b, 1, 2, NSA_KV_HEADS, NSA_HEAD_DIM)
    q5 = q.reshape(b, 1, NSA_KV_HEADS, NSA_GROUP, NSA_HEAD_DIM) * (NSA_HEAD_DIM ** -0.5)
    gates = jax.nn.sigmoid(gTt[:, :NSA_KV_HEADS * 16].reshape(b, 1, NSA_KV_HEADS, 16)[..., :3 * NSA_GROUP]
                           .reshape(b, 1, NSA_KV_HEADS, 3, NSA_GROUP).transpose(0, 1, 2, 4, 3))
    o_a, new_win = _nsa_sample(q5, kv_cmp.reshape(shp), kv_sel.reshape(shp), kv_win.reshape(shp), gates,
                               cache_cmp, cache_sel, cache_win, page_table, w_cmp, _alibi_slopes())
    o_b, s_new = _hgrn_step(hq.reshape(b, HG_WIDTH, 1), hf.reshape(b, HG_WIDTH, 1), hi.reshape(b, 1, HG_WIDTH),
                            hg.reshape(b, 1, HG_WIDTH), lb.reshape(HG_WIDTH, 1), w_hgrn_norm, s0)
    x1 = _merge(x, o_a.reshape(b, NSA_Q_WIDTH), o_b.reshape(b, HG_WIDTH), ma, mb, prm["pa"], prm["pb"], prm["wo"], b)
    x2 = _peer_tokens(x1, w_norm_ffn, w_norm_final, prm, final_norm)
    return x2, kv_cmp.reshape(shp), kv_sel.reshape(shp), new_win, s_new


def kernel(x_prompt, x_sample, cache_cmp_kv, cache_sel_kv, cache_win_kv, state_hgrn, page_table,
           w_norm_mix, w_in, w_cmp, w_proj_nsa, w_proj_hgrn, w_hgrn_norm, hgrn_lb_logits, w_out,
           w_norm_ffn, w_peer_q, peer_sub_keys, peer_u, peer_v, w_norm_final):
    depth = w_in.shape[0]
    bsz, s_len, d = x_prompt.shape
    db, dt, _ = x_sample.shape
    assert dt == 1
    lbs = jnp.cumsum(jax.nn.softmax(hgrn_lb_logits.astype(F32), axis=0), axis=0)
    wfin = w_norm_final.reshape(1, d)
    xp = [x_prompt[b] for b in range(bsz)]
    xs = x_sample.reshape(db, d)
    st_p, st_s = [], []
    for l in range(depth):
        last = l == depth - 1
        prm = _prep_layer(w_in[l], w_proj_nsa[l], w_proj_hgrn[l], w_out[l], w_peer_q[l], peer_sub_keys[l],
                          peer_u[l], peer_v[l])
        shared = (lbs[l].reshape(1, HG_WIDTH), prm, w_norm_mix[l].reshape(1, d), w_cmp[l],
                  w_hgrn_norm[l].reshape(1, HG_DV), w_norm_ffn[l].reshape(1, d), wfin, last)
        outs = [_layer_prompt(xp[b], *shared) for b in range(bsz)]
        xp = [o[0] for o in outs]
        st_p.append(tuple(jnp.stack([o[k] for o in outs]) for k in range(1, 5)))
        xs, *ss = _layer_sample(xs, *shared, cache_cmp_kv[l], cache_sel_kv[l], cache_win_kv[l], state_hgrn[l],
                                page_table)
        st_s.append(tuple(ss))
    y_prompt = jnp.stack(xp)
    y_sample = xs.reshape(db, dt, d)
    return (y_prompt, y_sample,
            jnp.stack([s[0] for s in st_p]), jnp.stack([s[1] for s in st_p]),
            jnp.stack([s[2] for s in st_p]), jnp.stack([s[3] for s in st_p]),
            jnp.stack([s[0] for s in st_s]), jnp.stack([s[1] for s in st_s]),
            jnp.stack([s[2] for s in st_s]), jnp.stack([s[3] for s in st_s]))
```

```python
import functools

import jax
import jax.numpy as jnp
from jax import lax
from jax.experimental import pallas as pl
from jax.experimental.pallas import tpu as pltpu

F32 = jnp.float32
BF16 = jnp.bfloat16

NSA_HEADS = 16
NSA_KV_HEADS = 4
NSA_GROUP = NSA_HEADS // NSA_KV_HEADS
NSA_HEAD_DIM = 64
NSA_BLOCK = 64
NSA_TOPK = 16
NSA_WINDOW = 512
NSA_QBLOCK = 128
HG_HEADS = 8
HG_DK = 128
HG_DV = 128
HG_CHUNK = 64
HG_SUB = 16
HG_HPB = 4
PK_HEADS = 8
PK_DIM = 256
PK_TOPK = 16
RMS_EPS = 1e-6
NEG_INF = -1e30
LOWEST = -3e38

NSA_Q_WIDTH = NSA_HEADS * NSA_HEAD_DIM
NSA_KV_WIDTH = NSA_KV_HEADS * NSA_HEAD_DIM
HG_WIDTH = HG_HEADS * HG_DK

LANES = 128
VMEM_LIMIT_BYTES = 56 * 1024 * 1024

NSA_COLS = NSA_GROUP * NSA_QBLOCK
NSA_KTILE = 512
NSA_CDIM = 128
NSA_WKEYS = NSA_WINDOW + NSA_QBLOCK
PEER_TT = 512
PEER_IB = 8


def _cparams(sem):
    return pltpu.CompilerParams(dimension_semantics=sem, vmem_limit_bytes=VMEM_LIMIT_BYTES)


def _rms(x, w):
    return x * lax.rsqrt(jnp.mean(x * x, axis=-1, keepdims=True) + RMS_EPS) * w


def _proj_kernel(x_ref, wn_ref, w_ref, *out_refs, widths, transposed, chunk):
    hb = _rms(x_ref[...], wn_ref[...]).astype(BF16)
    off = 0
    for o_ref, wd, tr in zip(out_refs, widths, transposed):
        for c0 in range(0, wd, chunk):
            cw = min(chunk, wd - c0)
            r = jnp.dot(hb, w_ref[:, off + c0:off + c0 + cw], preferred_element_type=F32)
            if tr:
                o_ref[c0:c0 + cw, :] = r.T
            else:
                o_ref[:, c0:c0 + cw] = r
        off += wd


def _proj(x, wn, w, widths, transposed, tm):
    t, d = x.shape
    n = w.shape[1]
    assert sum(widths) == n and t % tm == 0
    out_shape, out_specs = [], []
    for wd, tr in zip(widths, transposed):
        if tr:
            out_shape.append(jax.ShapeDtypeStruct((wd, t), F32))
            out_specs.append(pl.BlockSpec((wd, tm), lambda i: (0, i)))
        else:
            out_shape.append(jax.ShapeDtypeStruct((t, wd), F32))
            out_specs.append(pl.BlockSpec((tm, wd), lambda i: (i, 0)))
    return pl.pallas_call(
        functools.partial(_proj_kernel, widths=tuple(widths), transposed=tuple(transposed), chunk=512),
        grid=(t // tm,),
        in_specs=[pl.BlockSpec((tm, d), lambda i: (i, 0)),
                  pl.BlockSpec((1, d), lambda i: (0, 0)),
                  pl.BlockSpec((d, n), lambda i: (0, 0))],
        out_specs=out_specs, out_shape=out_shape,
        compiler_params=_cparams(("parallel",)), name="rms_proj",
    )(x, wn, w)


def _compress_kernel(kv_ref, w_ref, o_ref, *, nb):
    x = kv_ref[...]
    width = x.shape[-1]
    x3 = x.reshape(nb, NSA_BLOCK, width) * w_ref[...][None]
    o_ref[...] = jnp.sum(x3, axis=1)


def _compress(kv, wfull, nb):
    t, width = kv.shape
    rows = nb * NSA_BLOCK
    assert t % rows == 0
    return pl.pallas_call(
        functools.partial(_compress_kernel, nb=nb),
        grid=(t // rows,),
        in_specs=[pl.BlockSpec((rows, width), lambda i: (i, 0)),
                  pl.BlockSpec((NSA_BLOCK, width), lambda i: (0, 0))],
        out_specs=pl.BlockSpec((nb, width), lambda i: (i, 0)),
        out_shape=jax.ShapeDtypeStruct((t // NSA_BLOCK, width), F32),
        compiler_params=_cparams(("parallel",)), name="nsa_compress",
    )(kv, wfull)


def _topk_select_bias(score, k):
    n = score.shape[0]
    rows = lax.broadcasted_iota(jnp.int32, score.shape, 0).astype(F32)
    bias = jnp.full(score.shape, NEG_INF, F32)
    for _ in range(k):
        mx = jnp.max(score, axis=0, keepdims=True)
        idx = jnp.min(jnp.where(score == mx, rows, float(n)), axis=0, keepdims=True)
        hit = rows == idx
        bias = jnp.where(hit, 0.0, bias)
        score = jnp.where(hit, LOWEST, score)
    return bias


def _nsa_prompt_kernel(slope_ref, qT_ref, gT_ref, kc_ref, vcT_ref, ksel_ref, vselT_ref,
                       kd_ref, vdT_ref, kw0, kw1, kw2, kw3, kw4, vw0, vw1, vw2, vw3, vw4, bw_ref,
                       out_ref, qs_ref, selb_ref, sa_ref, sb_ref, pa_ref, pb_ref, *, nblk):
    i = pl.program_id(1)
    t0 = i * NSA_QBLOCK
    dh, qb, ncol, tk = NSA_HEAD_DIM, NSA_QBLOCK, NSA_COLS, NSA_KTILE
    slope = slope_ref[...]
    col = lax.broadcasted_iota(jnp.int32, (1, ncol), 1)
    tpos = t0 + (col & (qb - 1))
    tposf = tpos.astype(F32)
    q4 = qT_ref[...] * (dh ** -0.5)
    qT = jnp.concatenate([q4[g * dh:(g + 1) * dh, :] for g in range(NSA_GROUP)], axis=1)
    xrow = lax.broadcasted_iota(jnp.int32, (NSA_CDIM - dh, ncol), 0)

    qc = jnp.concatenate([qT, jnp.zeros((NSA_CDIM - dh, ncol), F32)], axis=0).astype(BF16)
    sc = jnp.dot(kc_ref[...], qc, preferred_element_type=F32)
    c_end = lax.broadcasted_iota(jnp.int32, (nblk, 1), 0) * NSA_BLOCK + (NSA_BLOCK - 1)
    valid = c_end <= tpos
    s = jnp.where(valid, sc - slope * (tposf - c_end.astype(F32)), NEG_INF)
    e = jnp.exp(s - jnp.max(s, axis=0, keepdims=True))
    p = jnp.where(valid, e / jnp.sum(e, axis=0, keepdims=True), 0.0)
    ocT = jnp.dot(vcT_ref[...], p.astype(BF16), preferred_element_type=F32)

    qw = jnp.concatenate([qT, jnp.where(xrow == 0, NEG_INF, 0.0)], axis=0).astype(BF16)
    kw = jnp.concatenate([kw0[...], kw1[...], kw2[...], kw3[...], kw4[...]], axis=0)
    sw = jnp.dot(kw, qw, preferred_element_type=F32) + bw_ref[...]
    ew = jnp.exp(sw - jnp.max(sw, axis=0, keepdims=True))
    vw = jnp.concatenate([vw0[...], vw1[...], vw2[...], vw3[...], vw4[...]], axis=1)
    owT = jnp.dot(vw, ew.astype(BF16), preferred_element_type=F32) / jnp.sum(ew, axis=0, keepdims=True)

    nbt = tk // NSA_BLOCK
    jl = t0 // tk
    s1 = slope.astype(BF16).astype(F32)
    r1 = slope - s1
    s2 = r1.astype(BF16).astype(F32)
    s3 = (r1 - s2).astype(BF16).astype(F32)
    half = float(tk // 2)
    ext = jnp.zeros((NSA_CDIM - dh, ncol), F32)
    for r, v in enumerate((s1, s2, s3, s1 * half, s2 * half, s3 * half)):
        ext = jnp.where(xrow == nbt + r, v, ext)

    qd = jnp.concatenate([qT, ext], axis=0).astype(BF16)
    sd = jnp.dot(kd_ref[...], qd, preferred_element_type=F32)
    kposd = t0 + lax.broadcasted_iota(jnp.int32, (qb, 1), 0)
    sd = jnp.where(kposd > tpos, NEG_INF, sd)
    mx_d = jnp.max(sd, axis=0, keepdims=True)
    pd = jnp.exp(sd - mx_d)
    m_d = mx_d + slope * (jl * tk - tpos).astype(F32)
    l_d = jnp.sum(pd, axis=0, keepdims=True)
    acc_d = jnp.dot(vdT_ref[...], pd.astype(BF16), preferred_element_type=F32)

    imp = p[:, 0:qb]
    for g in range(1, NSA_GROUP):
        imp = imp + p[:, g * qb:(g + 1) * qb]
    blk = lax.broadcasted_iota(jnp.int32, (nblk, qb), 0)
    cur = tpos[:, 0:qb] >> (NSA_BLOCK.bit_length() - 1)
    forced = (blk == 0) | (blk == cur) | (blk == cur - 1)
    score = jnp.where(blk > cur, -1.0, jnp.where(forced, NSA_GROUP + 1.0, imp))
    selb = _topk_select_bias(score, min(NSA_TOPK, nblk))
    blkc = lax.broadcasted_iota(jnp.int32, (nblk, ncol), 0)
    selb_ref[0:nblk, :] = jnp.where(blkc >= t0 // NSA_BLOCK, NEG_INF, jnp.concatenate([selb] * NSA_GROUP, axis=1))
    selb_ref[nblk:nblk + nbt, :] = jnp.full((nbt, ncol), NEG_INF, F32)

    qs_ref[0:dh, :] = qT
    qs_ref[dh:, :] = ext
    n1 = jl + 1
    npairs = (n1 + 1) // 2

    def qk_into(t, s_ref):
        tc = jnp.minimum(t, jl)
        b0 = pl.multiple_of(jnp.where(t < n1, tc * nbt, nblk), 8)
        qs_ref[dh:dh + nbt, :] = selb_ref[pl.ds(b0, nbt), :]
        k0 = pl.multiple_of(tc * tk, tk)
        s_ref[...] = jnp.dot(ksel_ref[pl.ds(k0, tk), :], qs_ref[...].astype(BF16), preferred_element_type=F32)

    def soft(t, s_ref, p_ref, m, l):
        cj = slope * (jnp.minimum(t, jl) * tk - tpos).astype(F32)
        sj = s_ref[...]
        m_new = jnp.maximum(m, jnp.max(sj, axis=0, keepdims=True) + cj)
        pj = jnp.exp(sj - (m_new - cj))
        alpha = jnp.exp(m - m_new)
        p_ref[...] = pj.astype(BF16)
        return m_new, alpha * l + jnp.sum(pj, axis=0, keepdims=True), alpha

    def pv(t, p_ref):
        return jnp.dot(vselT_ref[jnp.clip(t, 0, jl)], p_ref[...], preferred_element_type=F32)

    def pair(i, carry):
        m, l, accp = carry
        ta = 2 * i
        pvb = pv(ta - 1, pb_ref)
        qk_into(ta + 1, sb_ref)
        m, l, alpha = soft(ta, sa_ref, pa_ref, m, l)
        accp = alpha * (accp + pvb)
        pva = pv(ta, pa_ref)
        qk_into(ta + 2, sa_ref)
        m, l, alpha = soft(ta + 1, sb_ref, pb_ref, m, l)
        accp = alpha * (accp + pva)
        return m, l, accp

    pb_ref[...] = jnp.zeros(pb_ref.shape, BF16)
    qk_into(0, sa_ref)
    init = (jnp.full((1, ncol), NEG_INF, F32), jnp.zeros((1, ncol), F32), jnp.zeros((dh, ncol), F32))
    m_s, l_s, accp = lax.fori_loop(0, npairs, pair, init)
    acc_s = accp + pv(2 * npairs - 1, pb_ref)

    m_f = jnp.maximum(m_s, m_d)
    a_s = jnp.exp(m_s - m_f)
    a_d = jnp.exp(m_d - m_f)
    osT = (a_s * acc_s + a_d * acc_d) / (a_s * l_s + a_d * l_d)

    sg = jax.nn.sigmoid(gT_ref[...])

    def gate(c):
        return jnp.concatenate([sg[c * NSA_GROUP + g:c * NSA_GROUP + g + 1, :] for g in range(NSA_GROUP)], axis=1)

    oT = gate(0) * ocT + gate(1) * osT + gate(2) * owT
    o4 = jnp.concatenate([oT[:, g * qb:(g + 1) * qb] for g in range(NSA_GROUP)], axis=0)
    out_ref[...] = o4.T


def _alibi_slopes():
    h = jnp.arange(NSA_HEADS, dtype=F32)
    return (2.0 ** (-8.0 * (h + 1.0) / NSA_HEADS)).reshape(NSA_KV_HEADS, NSA_GROUP)


def _nsa_prompt(qT, gT, kvc, kv_sel, kv_win):
    s_len = qT.shape[1]
    dh, qb, tk, kvh = NSA_HEAD_DIM, NSA_QBLOCK, NSA_KTILE, NSA_KV_HEADS
    assert s_len % tk == 0
    nblk = s_len // NSA_BLOCK
    slopes = _alibi_slopes()
    slope_cols = jnp.repeat(slopes, qb, axis=1).reshape(kvh, 1, NSA_COLS)

    def heads_major(a):
        return a.reshape(a.shape[0], kvh, dh).transpose(1, 0, 2)

    def pad_lanes(a):
        return jnp.pad(a, ((0, 0), (0, 0), (0, NSA_CDIM - a.shape[-1])))

    kc = pad_lanes(heads_major(kvc[:, :NSA_KV_WIDTH])).astype(BF16)
    vcT = heads_major(kvc[:, NSA_KV_WIDTH:]).transpose(0, 2, 1).astype(BF16)
    r = jnp.arange(tk)
    onehot = (r[:, None] // NSA_BLOCK == jnp.arange(tk // NSA_BLOCK)[None, :]).astype(F32)
    lo = (r % (tk // 2)).astype(F32)[:, None]
    hi = (r // (tk // 2)).astype(F32)[:, None]
    kext = jnp.concatenate([onehot, lo, lo, lo, hi, hi, hi], axis=1)
    kext = jnp.tile(kext, (s_len // tk, 1))
    ksel = pad_lanes(jnp.concatenate(
        [heads_major(kv_sel[:, :NSA_KV_WIDTH]), jnp.broadcast_to(kext[None], (kvh,) + kext.shape)], axis=-1)).astype(BF16)
    vsel = heads_major(kv_sel[:, NSA_KV_WIDTH:]).astype(BF16)
    vselT = vsel.reshape(kvh, s_len // tk, tk, dh).transpose(0, 1, 3, 2)
    vselT_flat = vsel.transpose(0, 2, 1)
    kwin = heads_major(kv_win[:, :NSA_KV_WIDTH])
    kwin = jnp.concatenate([kwin, jnp.zeros((kvh, s_len, 1), F32)], axis=-1)
    padk = jnp.zeros((kvh, NSA_WINDOW, dh + 1), F32).at[:, :, dh].set(1.0)
    kwin = pad_lanes(jnp.concatenate([padk, kwin], axis=1)).astype(BF16)
    vwinT = jnp.pad(heads_major(kv_win[:, NSA_KV_WIDTH:]), ((0, 0), (NSA_WINDOW, 0), (0, 0))).transpose(0, 2, 1).astype(BF16)
    rr = jnp.arange(NSA_WKEYS)[:, None]
    cc = jnp.arange(NSA_COLS)[None, :]
    dw = (cc % qb) + NSA_WINDOW - rr
    bw = jnp.where((dw >= 0) & (dw <= NSA_WINDOW), -slope_cols * dw.astype(F32)[None], NEG_INF)

    nq = s_len // qb
    nwb = NSA_WKEYS // qb
    kw_specs = [pl.BlockSpec((None, qb, NSA_CDIM), functools.partial(lambda k, i, j: (k, i + j, 0), j=j)) for j in range(nwb)]
    vw_specs = [pl.BlockSpec((None, dh, qb), functools.partial(lambda k, i, j: (k, 0, i + j), j=j)) for j in range(nwb)]
    assert nwb == 5
    return pl.pallas_call(
        functools.partial(_nsa_prompt_kernel, nblk=nblk),
        grid=(kvh, nq),
        in_specs=[pl.BlockSpec((None, 1, NSA_COLS), lambda k, i: (k, 0, 0)),
                  pl.BlockSpec((NSA_GROUP * dh, qb), lambda k, i: (k, i)),
                  pl.BlockSpec((None, 16, qb), lambda k, i: (k, 0, i)),
                  pl.BlockSpec((None, nblk, NSA_CDIM), lambda k, i: (k, 0, 0)),
                  pl.BlockSpec((None, dh, nblk), lambda k, i: (k, 0, 0)),
                  pl.BlockSpec((None, s_len, NSA_CDIM), lambda k, i: (k, 0, 0)),
                  pl.BlockSpec((None, s_len // tk, dh, tk), lambda k, i: (k, 0, 0, 0)),
                  pl.BlockSpec((None, qb, NSA_CDIM), lambda k, i: (k, i, 0)),
                  pl.BlockSpec((None, dh, qb), lambda k, i: (k, 0, i))]
                 + kw_specs + vw_specs
                 + [pl.BlockSpec((None, NSA_WKEYS, NSA_COLS), lambda k, i: (k, 0, 0))],
        out_specs=pl.BlockSpec((qb, NSA_GROUP * dh), lambda k, i: (i, k)),
        out_shape=jax.ShapeDtypeStruct((s_len, NSA_Q_WIDTH), F32),
        scratch_shapes=[pltpu.VMEM((NSA_CDIM, NSA_COLS), F32),
                        pltpu.VMEM((nblk + tk // NSA_BLOCK, NSA_COLS), F32),
                        pltpu.VMEM((tk, NSA_COLS), F32), pltpu.VMEM((tk, NSA_COLS), F32),
                        pltpu.VMEM((tk, NSA_COLS), BF16), pltpu.VMEM((tk, NSA_COLS), BF16)],
        compiler_params=_cparams(("arbitrary", "arbitrary")), name="nsa_prompt",
    )(slope_cols, qT, gT.reshape(kvh, 16, s_len), kc, vcT, ksel, vselT, ksel, vselT_flat,
      *([kwin] * nwb), *([vwinT] * nwb), bw)


def _hgrn_chunk(qc, zf, vc, lb, st):
    c, sub = HG_CHUNK, HG_SUB
    logf = jnp.log(lb + (1.0 - lb) * jax.nn.sigmoid(zf))
    kc = (1.0 - lb) * jax.nn.sigmoid(-zf)
    tri = (lax.broadcasted_iota(jnp.int32, (c, c), 0) >= lax.broadcasted_iota(jnp.int32, (c, c), 1)).astype(F32)
    cb = jnp.dot(tri, logf, preferred_element_type=F32, precision=lax.Precision.HIGHEST)
    o = lax.dot_general((qc * jnp.exp(cb)).astype(BF16), st.astype(BF16), (((1,), (1,)), ((), ())),
                        preferred_element_type=F32)
    t3 = lax.broadcasted_iota(jnp.int32, (sub, sub, 1), 0) >= lax.broadcasted_iota(jnp.int32, (sub, sub, 1), 1)
    outs = []
    for a in range(c // sub):
        ra = slice(a * sub, (a + 1) * sub)
        cba, qa, ka, va = cb[ra], qc[ra], kc[ra], vc[ra]
        d3 = cba[:, None, :] - cba[None, :, :]
        x3 = jnp.where(t3, jnp.exp(d3), 0.0) * qa[:, None, :] * ka[None, :, :]
        att3 = jnp.sum(x3, axis=2, keepdims=True)
        oa = o[ra] + jnp.sum(att3 * va[None, :, :], axis=1)
        if a > 0:
            ref = cb[a * sub - 1:a * sub, :]
            qd = (qa * jnp.exp(cba - ref)).astype(BF16)
            kd = (kc[:a * sub] * jnp.exp(ref - cb[:a * sub])).astype(BF16)
            att = lax.dot_general(qd, kd, (((1,), (1,)), ((), ())), preferred_element_type=F32)
            oa = oa + jnp.dot(att.astype(BF16), vc[:a * sub].astype(BF16), preferred_element_type=F32)
        outs.append(oa)
    o = jnp.concatenate(outs, axis=0)
    last = cb[c - 1:c, :]
    kdec = (kc * jnp.exp(last - cb)).astype(BF16)
    st = st * jnp.exp(last) + lax.dot_general(vc.astype(BF16), kdec, (((0,), (0,)), ((), ())),
                                              preferred_element_type=F32)
    return o, st


def _hgrn_kernel(lb_ref, wn_ref, q_ref, f_ref, v_ref, g_ref, s0_ref, o_ref, sfin_ref, st_ref, *, nsub):
    c = pl.program_id(1)
    hpb = HG_HPB

    @pl.when(c == 0)
    def _():
        for j in range(hpb):
            st_ref[j] = s0_ref[j].T

    wn = wn_ref[...]

    def body(u, sts):
        rows = pl.ds(pl.multiple_of(u * HG_CHUNK, HG_CHUNK), HG_CHUNK)
        out = []
        for j in range(hpb):
            cols = slice(j * HG_DK, (j + 1) * HG_DK)
            o, st = _hgrn_chunk(q_ref[rows, cols], f_ref[rows, cols], v_ref[rows, cols], lb_ref[:, cols], sts[j])
            g = g_ref[rows, cols]
            o_ref[rows, cols] = _rms(o, wn) * (g * jax.nn.sigmoid(g))
            out.append(st)
        return tuple(out)

    sts = lax.fori_loop(0, nsub, body, tuple(st_ref[j] for j in range(hpb)))
    for j in range(hpb):
        st_ref[j] = sts[j]

    @pl.when(c == pl.num_programs(1) - 1)
    def _():
        for j in range(hpb):
            sfin_ref[j] = sts[j].T


def _hgrn_prompt(hq, hf, hi, hg, lb, wn, s0, tb):
    t = hq.shape[0]
    hpb = HG_HPB
    assert t % tb == 0 and tb % HG_CHUNK == 0 and HG_HEADS % hpb == 0 and HG_DK == HG_DV
    tok = pl.BlockSpec((tb, hpb * HG_DK), lambda h, c: (c, h))
    stt = pl.BlockSpec((hpb, HG_DK, HG_DV), lambda h, c: (h, 0, 0))
    return pl.pallas_call(
        functools.partial(_hgrn_kernel, nsub=tb // HG_CHUNK),
        grid=(HG_HEADS // hpb, t // tb),
        in_specs=[pl.BlockSpec((1, hpb * HG_DK), lambda h, c: (0, h)),
                  pl.BlockSpec((1, HG_DV), lambda h, c: (0, 0)),
                  tok, tok, tok, tok, stt],
        out_specs=[tok, stt],
        out_shape=[jax.ShapeDtypeStruct((t, HG_HEADS * HG_DV), F32),
                   jax.ShapeDtypeStruct((HG_HEADS, HG_DK, HG_DV), F32)],
        scratch_shapes=[pltpu.VMEM((hpb, HG_DV, HG_DK), F32)],
        compiler_params=_cparams(("arbitrary", "arbitrary")), name="hgrn_prompt",
    )(lb, wn, hq, hf, hi, hg, s0)


def _hgrn_step_kernel(lb_ref, wn_ref, q_ref, f_ref, v_ref, g_ref, s0_ref, o_ref, s_ref):
    wn = wn_ref[...]
    for h in range(HG_HEADS):
        rk = slice(h * HG_DK, (h + 1) * HG_DK)
        lb = lb_ref[rk, :]
        zf = f_ref[rk, :]
        f = lb + (1.0 - lb) * jax.nn.sigmoid(zf)
        kk = (1.0 - lb) * jax.nn.sigmoid(-zf)
        vrow = v_ref[:, h * HG_DV:(h + 1) * HG_DV]
        s_new = f * s0_ref[h] + kk * vrow
        s_ref[h] = s_new
        o = jnp.sum(s_new * q_ref[rk, :], axis=0, keepdims=True)
        g = g_ref[:, h * HG_DV:(h + 1) * HG_DV]
        o_ref[:, h * HG_DV:(h + 1) * HG_DV] = _rms(o, wn) * (g * jax.nn.sigmoid(g))


def _hgrn_step(hq_col, hf_col, hi, hg, lb_col, wn, s0):
    b = hi.shape[0]
    col = pl.BlockSpec((None, HG_WIDTH, 1), lambda i: (i, 0, 0))
    row = pl.BlockSpec((None, 1, HG_WIDTH), lambda i: (i, 0, 0))
    st = pl.BlockSpec((None, HG_HEADS, HG_DK, HG_DV), lambda i: (i, 0, 0, 0))
    return pl.pallas_call(
        _hgrn_step_kernel, grid=(b,),
        in_specs=[pl.BlockSpec((HG_WIDTH, 1), lambda i: (0, 0)), pl.BlockSpec((1, HG_DV), lambda i: (0, 0)),
                  col, col, row, row, st],
        out_specs=[row, st],
        out_shape=[jax.ShapeDtypeStruct((b, 1, HG_WIDTH), F32), jax.ShapeDtypeStruct(s0.shape, F32)],
        compiler_params=_cparams(("parallel",)), name="hgrn_step",
    )(lb_col, wn, hq_col, hf_col, hi, hg, s0)


def _merge_kernel(x_ref, oa_ref, ob_ref, ma_ref, mb_ref, pa_ref, pb_ref, wo_ref, y_ref):
    ya = jnp.dot(oa_ref[...].astype(BF16), pa_ref[...], preferred_element_type=F32)
    yb = jnp.dot(ob_ref[...].astype(BF16), pb_ref[...], preferred_element_type=F32)
    mix = jax.nn.sigmoid(ma_ref[...]) * ya + jax.nn.sigmoid(mb_ref[...]) * yb
    y_ref[...] = x_ref[...] + jnp.dot(mix.astype(BF16), wo_ref[...], preferred_element_type=F32)


def _merge(x, oa, ob, ma, mb, pa, pb, wo, tm):
    t, d = x.shape
    tok = pl.BlockSpec((tm, d), lambda i: (i, 0))
    wsp = pl.BlockSpec((d, d), lambda i: (0, 0))
    return pl.pallas_call(
        _merge_kernel, grid=(t // tm,),
        in_specs=[tok, tok, tok, tok, tok, wsp, wsp, wsp],
        out_specs=tok, out_shape=jax.ShapeDtypeStruct((t, d), F32),
        compiler_params=_cparams(("parallel",)), name="branch_merge",
    )(x, oa, ob, ma, mb, pa, pb, wo)


def _topk_rows(s, k):
    n = s.shape[0]
    rows = lax.broadcasted_iota(jnp.int32, s.shape, 0).astype(F32)
    rank = jnp.full(s.shape, float(n), F32)
    tops = []
    for r in range(k):
        mx = jnp.max(s, axis=0, keepdims=True)
        idx = jnp.min(jnp.where(s == mx, rows, float(n)), axis=0, keepdims=True)
        hit = rows == idx
        rank = jnp.where(hit, float(r), rank)
        s = jnp.where(hit, LOWEST, s)
        tops.append(mx)
    return jnp.concatenate(tops, axis=0), rank


_STAIR_GROUPS = ((0, 16), (1, 8), (2, 8), (3, 8))
_STAIR_QUAD = (4, 5, 6, 7)
_STAIR_TAIL = 8


def _stair_rows(t1, t2, op):
    tt = t1.shape[1]
    parts = [op(jnp.broadcast_to(t1[a:a + 1, :], (nb, tt)), t2[0:nb, :]) for a, nb in _STAIR_GROUPS]
    r16 = lax.broadcasted_iota(jnp.int32, (16, tt), 0)
    v1 = jnp.broadcast_to(t1[_STAIR_QUAD[3]:_STAIR_QUAD[3] + 1, :], (16, tt))
    v2 = jnp.broadcast_to(t2[3:4, :], (16, tt))
    for q in (2, 1, 0):
        v1 = jnp.where(r16 < 4 * (q + 1), jnp.broadcast_to(t1[_STAIR_QUAD[q]:_STAIR_QUAD[q] + 1, :], (16, tt)), v1)
        v2 = jnp.where((r16 & 3) == q, jnp.broadcast_to(t2[q:q + 1, :], (16, tt)), v2)
    parts.append(op(v1, v2))
    parts.append(op(t1[_STAIR_TAIL:, :], jnp.broadcast_to(t2[0:1, :], (PK_TOPK - _STAIR_TAIL, tt))))
    return jnp.concatenate(parts, axis=0)


def _stair_row_counts(selc):
    out, r0 = [], 0
    for _, nb in _STAIR_GROUPS:
        out.append(jnp.sum(selc[r0:r0 + nb, :], axis=0, keepdims=True))
        r0 += nb
    quad = selc[r0:r0 + 16, :]
    r16 = lax.broadcasted_iota(jnp.int32, quad.shape, 0)
    for q in range(4):
        out.append(jnp.sum(jnp.where((r16 >> 2) == q, quad, 0.0), axis=0, keepdims=True))
    r0 += 16
    for a in range(PK_TOPK - _STAIR_TAIL):
        out.append(selc[r0 + a:r0 + a + 1, :])
    return out


def _peer_kernel(x_ref, wn_ref, wq_ref, sk_ref, u_ref, vT_ref, wf_ref, y_ref,
                 hnT_ref, acc_ref, wcat_ref, n_ref, a1_ref, r2_ref, e2_ref, *, ib, nkeys, final_norm):
    i = pl.program_id(1)
    kt = PK_TOPK
    hd = PK_DIM // 2

    @pl.when(i == 0)
    def _():
        hnT = _rms(x_ref[...], wn_ref[...]).T.astype(BF16)
        hnT_ref[...] = hnT
        acc_ref[...] = jnp.zeros(acc_ref.shape, F32)
        wcat_ref[...] = jnp.zeros(wcat_ref.shape, BF16)
        tt = hnT.shape[1]
        for h in range(PK_HEADS):
            ss, tops, ranks = [], [], []
            for c in range(2):
                r0 = (h * 2 + c) * hd
                qhc = jnp.dot(wq_ref[r0:r0 + hd, :], hnT, preferred_element_type=F32)
                s = jnp.dot(sk_ref[h * 2 + c], qhc.astype(BF16), preferred_element_type=F32)
                top, rank = _topk_rows(s, kt)
                ss.append(s)
                tops.append(top)
                ranks.append(rank)
            cand = _stair_rows(tops[0], tops[1], jnp.add)
            _, crank = _topk_rows(cand, kt)
            selc = (crank < float(kt)).astype(F32)
            n_a = _stair_row_counts(selc)
            e1t = jnp.exp(tops[0] - tops[0][0:1, :])
            e2t = jnp.exp(tops[1] - tops[1][0:1, :])
            z = jnp.sum(selc * _stair_rows(e1t, e2t, jnp.multiply), axis=0, keepdims=True)
            nfull = jnp.zeros((nkeys, tt), F32)
            for a in range(kt):
                nfull = jnp.where(ranks[0] == float(a), n_a[a], nfull)
            n_ref[h] = nfull
            a1_ref[h] = jnp.exp(ss[0] - tops[0][0:1, :]) / z
            r2_ref[h] = ranks[1]
            e2_ref[h] = jnp.exp(ss[1] - tops[1][0:1, :])

    nsteps = pl.num_programs(1) - 1

    @pl.when(i < nsteps)
    def _():
        acc_ref[...] += jnp.dot(vT_ref[...], wcat_ref[...], preferred_element_type=F32)
        hnT = hnT_ref[...]
        for ii in range(ib):
            ig = i * ib + ii
            aT = jnp.dot(u_ref[ii * nkeys:(ii + 1) * nkeys, :], hnT, preferred_element_type=F32)
            gsum = jnp.zeros(aT.shape, F32)
            for h in range(PK_HEADS):
                nrow = n_ref[h, pl.ds(ig, 1), :]
                arow = a1_ref[h, pl.ds(ig, 1), :]
                gsum = gsum + jnp.where(r2_ref[h] < nrow, arow * e2_ref[h], 0.0)
            wcat_ref[ii * nkeys:(ii + 1) * nkeys, :] = (jax.nn.gelu(aT) * gsum).astype(BF16)

    @pl.when(i == nsteps)
    def _():
        acc = acc_ref[...] + jnp.dot(vT_ref[...], wcat_ref[...], preferred_element_type=F32)
        y = x_ref[...] + acc.T
        if final_norm:
            y = _rms(y, wf_ref[...])
        y_ref[...] = y


def _peer(x, wn, wqT, sk, u, vT, wf, tt, ib, final_norm):
    t, d = x.shape
    nkeys = sk.shape[1]
    assert t % tt == 0 and nkeys % ib == 0
    nsteps = nkeys // ib
    stat = pltpu.VMEM((PK_HEADS, nkeys, tt), F32)
    return pl.pallas_call(
        functools.partial(_peer_kernel, ib=ib, nkeys=nkeys, final_norm=final_norm),
        grid=(t // tt, nsteps + 1),
        in_specs=[pl.BlockSpec((tt, d), lambda a, i: (a, 0)),
                  pl.BlockSpec((1, d), lambda a, i: (0, 0)),
                  pl.BlockSpec(wqT.shape, lambda a, i: (0, 0)),
                  pl.BlockSpec(sk.shape, lambda a, i: (0, 0, 0)),
                  pl.BlockSpec((ib * nkeys, d), lambda a, i: (jnp.minimum(i, nsteps - 1), 0)),
                  pl.BlockSpec((None, d, ib * nkeys), lambda a, i: (jnp.maximum(i - 1, 0), 0, 0)),
                  pl.BlockSpec((1, d), lambda a, i: (0, 0))],
        out_specs=pl.BlockSpec((tt, d), lambda a, i: (a, 0)),
        out_shape=jax.ShapeDtypeStruct((t, d), F32),
        scratch_shapes=[pltpu.VMEM((d, tt), BF16), pltpu.VMEM((d, tt), F32), pltpu.VMEM((ib * nkeys, tt), BF16),
                        stat, stat, stat, stat],
        compiler_params=_cparams(("arbitrary", "arbitrary")), name="peer_dense",
    )(x, wn, wqT, sk, u, vT, wf)


def _compress_pages_kernel(pt_ref, *refs, pg, bpp):
    w = refs[pg][...]
    o_ref = refs[pg + 1]
    for j in range(pg):
        x = refs[j][...]
        x3 = x.reshape(bpp, NSA_BLOCK, x.shape[-1]) * w[None]
        o_ref[j * bpp:(j + 1) * bpp, :] = jnp.sum(x3, axis=1)


def _compress_pages(cache, page_table, wfull, pg):
    _, page, width = cache.shape
    db, npages = page_table.shape
    bpp = page // NSA_BLOCK
    assert npages % pg == 0
    page_specs = [pl.BlockSpec((None, page, width),
                               functools.partial(lambda b, g, pt, j: (pt[b, g * pg + j], 0, 0), j=j)) for j in range(pg)]
    return pl.pallas_call(
        functools.partial(_compress_pages_kernel, pg=pg, bpp=bpp),
        grid_spec=pltpu.PrefetchScalarGridSpec(
            num_scalar_prefetch=1, grid=(db, npages // pg),
            in_specs=page_specs + [pl.BlockSpec((NSA_BLOCK, width), lambda b, g, pt: (0, 0))],
            out_specs=pl.BlockSpec((None, pg * bpp, width), lambda b, g, pt: (b, g, 0))),
        out_shape=jax.ShapeDtypeStruct((db, npages * bpp, width), F32),
        compiler_params=_cparams(("arbitrary", "arbitrary")), name="nsa_compress_pages",
    )(page_table, *([cache] * pg), wfull)


def _nsa_decode_head_kernel(slope_ref, qbd_ref, g_ref, kvc_ref, wnew_ref, cwin_ref, part_ref, selb_ref,
                            *, past_len, k_past):
    kvw = NSA_KV_WIDTH
    qbd = qbd_ref[...]
    qb = qbd.astype(BF16)
    slope = slope_ref[...]
    nh = qbd.shape[0]
    npb = kvc_ref.shape[0]
    nt = (((1,), (1,)), ((), ()))
    kvc = kvc_ref[...]
    sc = lax.dot_general(qb, kvc[:, :kvw].astype(BF16), nt, preferred_element_type=F32)
    c_end = lax.broadcasted_iota(jnp.int32, (1, npb), 1) * NSA_BLOCK + (NSA_BLOCK - 1)
    s = sc - slope * (past_len - c_end).astype(F32)
    e = jnp.exp(s - jnp.max(s, axis=1, keepdims=True))
    p = e / jnp.sum(e, axis=1, keepdims=True)
    oc = jnp.dot(p.astype(BF16), kvc[:, kvw:].astype(BF16), preferred_element_type=F32)
    rows = []
    for k in range(NSA_KV_HEADS):
        r = p[k * NSA_GROUP:k * NSA_GROUP + 1, :]
        for g in range(1, NSA_GROUP):
            r = r + p[k * NSA_GROUP + g:k * NSA_GROUP + g + 1, :]
        rows.append(jnp.broadcast_to(r, (NSA_GROUP, npb)))
    imp = jnp.concatenate(rows, axis=0)
    blk = lax.broadcasted_iota(jnp.int32, (nh, npb), 1)
    score = jnp.where((blk == 0) | (blk == npb - 1), NSA_GROUP + 1.0, imp)
    lanes = blk.astype(F32)
    selb = jnp.full((nh, npb), NEG_INF, F32)
    for _ in range(k_past):
        mx = jnp.max(score, axis=1, keepdims=True)
        idx = jnp.min(jnp.where(score == mx, lanes, float(npb)), axis=1, keepdims=True)
        hit = lanes == idx
        selb = jnp.where(hit, 0.0, selb)
        score = jnp.where(hit, LOWEST, score)
    selb_ref[...] = selb
    cw = cwin_ref[...]
    nw = cw.shape[0]
    sw = lax.dot_general(qb, cw[:, :kvw].astype(BF16), nt, preferred_element_type=F32)
    dw = nw - lax.broadcasted_iota(jnp.int32, (1, nw), 1)
    sw = sw - slope * dw.astype(F32)
    wnew = wnew_ref[...]
    s_n = jnp.sum(qbd * wnew[:, :kvw], axis=1, keepdims=True)
    m = jnp.maximum(jnp.max(sw, axis=1, keepdims=True), s_n)
    ew = jnp.exp(sw - m)
    en = jnp.exp(s_n - m)
    ow = (jnp.dot(ew.astype(BF16), cw[:, kvw:].astype(BF16), preferred_element_type=F32) + en * wnew[:, kvw:]) \
        / (jnp.sum(ew, axis=1, keepdims=True) + en)
    sg = jax.nn.sigmoid(g_ref[...])
    part_ref[...] = sg[:, 0:1] * oc + sg[:, 2:3] * ow


def _nsa_decode_sel_kernel(pt_ref, slope_ref, qbd_ref, g_ref, selb_ref, snew_ref, part_ref, *refs,
                           pg, past_len):
    pages, o_ref, m_ref, l_ref, acc_ref = refs[:pg], refs[pg], refs[pg + 1], refs[pg + 2], refs[pg + 3]
    kvw = NSA_KV_WIDTH
    g = pl.program_id(1)

    @pl.when(g == 0)
    def _():
        m_ref[...] = jnp.full(m_ref.shape, NEG_INF, F32)
        l_ref[...] = jnp.zeros(l_ref.shape, F32)
        acc_ref[...] = jnp.zeros(acc_ref.shape, F32)

    qbd = qbd_ref[...]
    slope = slope_ref[...]
    kv = jnp.concatenate([r[...] for r in pages], axis=0)
    nk = kv.shape[0]
    nb = nk // NSA_BLOCK
    s = lax.dot_general(qbd.astype(BF16), kv[:, :kvw].astype(BF16), (((1,), (1,)), ((), ())),
                        preferred_element_type=F32)
    kidx = lax.broadcasted_iota(jnp.int32, (nb, nk), 1)
    expand = ((kidx >> (NSA_BLOCK.bit_length() - 1)) == lax.broadcasted_iota(jnp.int32, (nb, nk), 0)).astype(BF16)
    bias = jnp.dot(selb_ref[...].astype(BF16), expand, preferred_element_type=F32)
    kpos = g * nk + lax.broadcasted_iota(jnp.int32, (1, nk), 1)
    s = s - slope * (past_len - kpos).astype(F32) + bias
    m_old = m_ref[...]
    m_new = jnp.maximum(m_old, jnp.max(s, axis=1, keepdims=True))
    p = jnp.exp(s - m_new)
    alpha = jnp.exp(m_old - m_new)
    l_ref[...] = alpha * l_ref[...] + jnp.sum(p, axis=1, keepdims=True)
    acc_ref[...] = alpha * acc_ref[...] + jnp.dot(p.astype(BF16), kv[:, kvw:].astype(BF16), preferred_element_type=F32)
    m_ref[...] = m_new

    @pl.when(g == pl.num_programs(1) - 1)
    def _():
        snew = snew_ref[...]
        s_n = jnp.sum(qbd * snew[:, :kvw], axis=1, keepdims=True)
        m2 = jnp.maximum(m_ref[...], s_n)
        a = jnp.exp(m_ref[...] - m2)
        en = jnp.exp(s_n - m2)
        o_s = (a * acc_ref[...] + en * snew[:, kvw:]) / (a * l_ref[...] + en)
        tot = part_ref[...] + jax.nn.sigmoid(g_ref[...])[:, 1:2] * o_s
        for k in range(NSA_KV_HEADS):
            o_ref[k * NSA_GROUP:(k + 1) * NSA_GROUP, :] = \
                tot[k * NSA_GROUP:(k + 1) * NSA_GROUP, k * NSA_HEAD_DIM:(k + 1) * NSA_HEAD_DIM]


def _nsa_sample(q, kv_cmp, kv_sel, kv_win, glog, cache_cmp, cache_sel, cache_win, page_table, w_cmp):
    db = q.shape[0]
    kvh, grp, dh, kvw = NSA_KV_HEADS, NSA_GROUP, NSA_HEAD_DIM, NSA_KV_WIDTH
    n_phys, page = cache_cmp.shape[0], cache_cmp.shape[1]
    npages = page_table.shape[1]
    past_len = npages * page
    nwin = cache_win.shape[1]
    assert past_len >= nwin and past_len % NSA_BLOCK == 0
    npb = past_len // NSA_BLOCK
    pg = 8 if npages % 8 == 0 else 1
    width = 2 * kvw
    wfull = jnp.concatenate([jnp.broadcast_to(w_cmp[0][:, None], (NSA_BLOCK, kvw)),
                             jnp.broadcast_to(w_cmp[1][:, None], (NSA_BLOCK, kvw))], axis=1)
    kvc = _compress_pages(cache_cmp.reshape(n_phys, page, width), page_table, wfull, pg)
    q4 = q.reshape(db, kvh, grp, dh) * (dh ** -0.5)
    qbd = (q4[:, :, :, None, :] * jnp.eye(kvh, dtype=F32)[None, :, None, :, None]).reshape(db, kvh * grp, kvw)
    slope = _alibi_slopes().reshape(kvh * grp, 1)
    nh = kvh * grp
    k_past = min(NSA_TOPK, npb + 1) - 1
    full2 = lambda shape: pl.BlockSpec(shape, lambda b: (0,) * len(shape))
    per_b = lambda shape: pl.BlockSpec((None,) + shape, lambda b: (b,) + (0,) * len(shape))
    part, selb = pl.pallas_call(
        functools.partial(_nsa_decode_head_kernel, past_len=past_len, k_past=k_past),
        grid=(db,),
        in_specs=[full2((nh, 1)), per_b((nh, kvw)), per_b((nh, 3)), per_b((npb, width)), per_b((1, width)),
                  per_b((nwin, width))],
        out_specs=[per_b((nh, kvw)), per_b((nh, npb))],
        out_shape=[jax.ShapeDtypeStruct((db, nh, kvw), F32), jax.ShapeDtypeStruct((db, nh, npb), F32)],
        compiler_params=_cparams(("parallel",)), name="nsa_decode_head",
    )(slope, qbd, glog, kvc, kv_win.reshape(db, 1, width), cache_win.reshape(db, nwin, width))
    bpg = pg * page // NSA_BLOCK
    ngrp = npages // pg
    selb4 = selb.reshape(db, nh, ngrp, bpg).transpose(0, 2, 1, 3)
    page_specs = [pl.BlockSpec((None, page, width),
                               functools.partial(lambda b, g, pt, j: (pt[b, g * pg + j], 0, 0), j=j)) for j in range(pg)]
    bsp = lambda shape: pl.BlockSpec((None,) + shape, lambda b, g, pt: (b,) + (0,) * len(shape))
    o = pl.pallas_call(
        functools.partial(_nsa_decode_sel_kernel, pg=pg, past_len=past_len),
        grid_spec=pltpu.PrefetchScalarGridSpec(
            num_scalar_prefetch=1, grid=(db, ngrp),
            in_specs=[pl.BlockSpec((nh, 1), lambda b, g, pt: (0, 0)), bsp((nh, kvw)), bsp((nh, 3)),
                      pl.BlockSpec((None, None, nh, bpg), lambda b, g, pt: (b, g, 0, 0)),
                      bsp((1, width)), bsp((nh, kvw))] + page_specs,
            out_specs=bsp((nh, dh)),
            scratch_shapes=[pltpu.VMEM((nh, 1), F32), pltpu.VMEM((nh, 1), F32), pltpu.VMEM((nh, kvw), F32)]),
        out_shape=jax.ShapeDtypeStruct((db, nh, dh), F32),
        compiler_params=_cparams(("arbitrary", "arbitrary")), name="nsa_decode_sel",
    )(page_table, slope, qbd, glog, selb4, kv_sel.reshape(db, 1, width), part,
      *([cache_sel.reshape(n_phys, page, width)] * pg))
    new_win = jnp.concatenate([cache_win, kv_win.reshape((db, 1) + cache_win.shape[2:])], axis=1)[:, 1:]
    return o.reshape(db, nh * dh), new_win


def _split_w_in(w_in):
    pts, acc = [], 0
    for w in (NSA_Q_WIDTH, 6 * NSA_KV_WIDTH, 3 * NSA_HEADS, HG_WIDTH, HG_WIDTH, HG_WIDTH, HG_WIDTH, w_in.shape[0]):
        acc += w
        pts.append(acc)
    return jnp.split(w_in, pts, axis=1)


def _prep_layer(w_in, w_proj_nsa, w_proj_hgrn, w_out, w_peer_q, peer_sub_keys, peer_u, peer_v):
    d = w_in.shape[0]
    wq, wkv, wg, wbq, wbf, wbi, wbg, wma, wmb = _split_w_in(w_in)
    wg = wg.reshape(d, NSA_KV_HEADS, NSA_GROUP, 3).transpose(0, 1, 3, 2).reshape(d, NSA_KV_HEADS, 3 * NSA_GROUP)
    wg = jnp.pad(wg, ((0, 0), (0, 0), (0, 16 - 3 * NSA_GROUP))).reshape(d, NSA_KV_HEADS * 16)
    wg = jnp.pad(wg, ((0, 0), (0, LANES - NSA_KV_HEADS * 16)))
    nkeys = peer_sub_keys.shape[2]
    return dict(
        w_qg=jnp.concatenate([wq, wg], axis=1).astype(BF16),
        w_kv=wkv.astype(BF16),
        w_hg=jnp.concatenate([wbq, wbf, wbi, wbg], axis=1).astype(BF16),
        w_m=jnp.concatenate([wma, wmb], axis=1).astype(BF16),
        pa=w_proj_nsa.astype(BF16), pb=w_proj_hgrn.astype(BF16), wo=w_out.astype(BF16),
        wqT=w_peer_q.T.astype(BF16),
        sk=peer_sub_keys.reshape(PK_HEADS * 2, nkeys, PK_DIM // 2).astype(BF16),
        u=peer_u.astype(BF16),
        vT=peer_v.astype(BF16).reshape(nkeys // PEER_IB, PEER_IB * nkeys, d).transpose(0, 2, 1),
    )


def _tile(t, pref):
    return pref if t % pref == 0 else t


def _peer_tokens(x, wn, wf, prm, final_norm):
    t = x.shape[0]
    tp = -(-t // LANES) * LANES
    xp = jnp.pad(x, ((0, tp - t), (0, 0)))
    tt = PEER_TT if tp % PEER_TT == 0 else LANES
    y = _peer(xp, wn, prm["wqT"], prm["sk"], prm["u"], prm["vT"], wf, tt, PEER_IB, final_norm)
    return y[:t]


def _layer_prompt(x, lb, prm, w_norm_mix, w_cmp, w_hgrn_norm, w_norm_ffn, w_norm_final, final_norm):
    s_len, d = x.shape
    tm = _tile(s_len, 256)
    kvw = NSA_KV_WIDTH
    kv_cmp, kv_sel, kv_win = _proj(x, w_norm_mix, prm["w_kv"], [2 * kvw] * 3, [False] * 3, tm)
    qT, gT = _proj(x, w_norm_mix, prm["w_qg"], [NSA_Q_WIDTH, LANES], [True, True], tm)
    hq, hf, hi, hg = _proj(x, w_norm_mix, prm["w_hg"], [HG_WIDTH] * 4, [False] * 4, tm)
    ma, mb = _proj(x, w_norm_mix, prm["w_m"], [d, d], [False, False], tm)
    wfull = jnp.concatenate([jnp.broadcast_to(w_cmp[0][:, None], (NSA_BLOCK, kvw)),
                             jnp.broadcast_to(w_cmp[1][:, None], (NSA_BLOCK, kvw))], axis=1)
    nblk = s_len // NSA_BLOCK
    kvc = _compress(kv_cmp, wfull, 8 if nblk % 8 == 0 else nblk)
    o_a = _nsa_prompt(qT, gT[:NSA_KV_HEADS * 16], kvc, kv_sel, kv_win)
    s0 = jnp.zeros((HG_HEADS, HG_DK, HG_DV), F32)
    o_b, s_fin = _hgrn_prompt(hq, hf, hi, hg, lb, w_hgrn_norm, s0, _tile(s_len, 512))
    x1 = _merge(x, o_a, o_b, ma, mb, prm["pa"], prm["pb"], prm["wo"], tm)
    x2 = _peer_tokens(x1, w_norm_ffn, w_norm_final, prm, final_norm)
    shp = (s_len, 2, NSA_KV_HEADS, NSA_HEAD_DIM)
    return x2, kv_cmp.reshape(shp), kv_sel.reshape(shp), kv_win.reshape(shp)[-NSA_WINDOW:], s_fin


def _layer_sample(x, lb, prm, w_norm_mix, w_cmp, w_hgrn_norm, w_norm_ffn, w_norm_final, final_norm,
                  cache_cmp, cache_sel, cache_win, s0, page_table):
    b, d = x.shape
    kvw = NSA_KV_WIDTH
    w_q = prm["w_qg"][:, :NSA_Q_WIDTH]
    (q,) = _proj(x, w_norm_mix, w_q, [NSA_Q_WIDTH], [False], b)
    kv_cmp, kv_sel, kv_win = _proj(x, w_norm_mix, prm["w_kv"], [2 * kvw] * 3, [False] * 3, b)
    (gTt,) = _proj(x, w_norm_mix, prm["w_qg"][:, NSA_Q_WIDTH:], [LANES], [False], b)
    hq, hf, hi, hg = _proj(x, w_norm_mix, prm["w_hg"], [HG_WIDTH] * 4, [False] * 4, b)
    ma, mb = _proj(x, w_norm_mix, prm["w_m"], [d, d], [False, False], b)
    shp = (b, 1, 2, NSA_KV_HEADS, NSA_HEAD_DIM)
    glog = (gTt[:, :NSA_KV_HEADS * 16].reshape(b, NSA_KV_HEADS, 16)[..., :3 * NSA_GROUP]
            .reshape(b, NSA_KV_HEADS, 3, NSA_GROUP).transpose(0, 1, 3, 2).reshape(b, NSA_HEADS, 3))
    o_a, new_win = _nsa_sample(q, kv_cmp, kv_sel, kv_win, glog, cache_cmp, cache_sel, cache_win, page_table, w_cmp)
    o_b, s_new = _hgrn_step(hq.reshape(b, HG_WIDTH, 1), hf.reshape(b, HG_WIDTH, 1), hi.reshape(b, 1, HG_WIDTH),
                            hg.reshape(b, 1, HG_WIDTH), lb.reshape(HG_WIDTH, 1), w_hgrn_norm, s0)
    x1 = _merge(x, o_a.reshape(b, NSA_Q_WIDTH), o_b.reshape(b, HG_WIDTH), ma, mb, prm["pa"], prm["pb"], prm["wo"], b)
    x2 = _peer_tokens(x1, w_norm_ffn, w_norm_final, prm, final_norm)
    return x2, kv_cmp.reshape(shp), kv_sel.reshape(shp), new_win, s_new


def kernel(x_prompt, x_sample, cache_cmp_kv, cache_sel_kv, cache_win_kv, state_hgrn, page_table,
           w_norm_mix, w_in, w_cmp, w_proj_nsa, w_proj_hgrn, w_hgrn_norm, hgrn_lb_logits, w_out,
           w_norm_ffn, w_peer_q, peer_sub_keys, peer_u, peer_v, w_norm_final):
    depth = w_in.shape[0]
    bsz, s_len, d = x_prompt.shape
    db, dt, _ = x_sample.shape
    assert dt == 1
    lbs = jnp.cumsum(jax.nn.softmax(hgrn_lb_logits.astype(F32), axis=0), axis=0)
    wfin = w_norm_final.reshape(1, d)
    xp = [x_prompt[b] for b in range(bsz)]
    xs = x_sample.reshape(db, d)
    st_p, st_s = [], []
    for l in range(depth):
        last = l == depth - 1
        prm = _prep_layer(w_in[l], w_proj_nsa[l], w_proj_hgrn[l], w_out[l], w_peer_q[l], peer_sub_keys[l],
                          peer_u[l], peer_v[l])
        shared = (lbs[l].reshape(1, HG_WIDTH), prm, w_norm_mix[l].reshape(1, d), w_cmp[l],
                  w_hgrn_norm[l].reshape(1, HG_DV), w_norm_ffn[l].reshape(1, d), wfin, last)
        outs = [_layer_prompt(xp[b], *shared) for b in range(bsz)]
        xp = [o[0] for o in outs]
        st_p.append(tuple(jnp.stack([o[k] for o in outs]) for k in range(1, 5)))
        xs, *ss = _layer_sample(xs, *shared, cache_cmp_kv[l], cache_sel_kv[l], cache_win_kv[l], state_hgrn[l],
                                page_table)
        st_s.append(tuple(ss))
    y_prompt = jnp.stack(xp)
    y_sample = xs.reshape(db, dt, d)
    return (y_prompt, y_sample,
            jnp.stack([s[0] for s in st_p]), jnp.stack([s[1] for s in st_p]),
            jnp.stack([s[2] for s in st_p]), jnp.stack([s[3] for s in st_p]),
            jnp.stack([s[0] for s in st_s]), jnp.stack([s[1] for s in st_s]),
            jnp.stack([s[2] for s in st_s]), jnp.stack([s[3] for s in st_s]))
```

```python
import functools

import jax
import jax.numpy as jnp
from jax import lax
from jax.experimental import pallas as pl
from jax.experimental.pallas import tpu as pltpu

F32 = jnp.float32
BF16 = jnp.bfloat16

NSA_HEADS = 16
NSA_KV_HEADS = 4
NSA_GROUP = NSA_HEADS // NSA_KV_HEADS
NSA_HEAD_DIM = 64
NSA_BLOCK = 64
NSA_TOPK = 16
NSA_WINDOW = 512
NSA_QBLOCK = 128
HG_HEADS = 8
HG_DK = 128
HG_DV = 128
HG_CHUNK = 64
HG_SUB = 16
HG_HPB = 4
PK_HEADS = 8
PK_DIM = 256
PK_TOPK = 16
RMS_EPS = 1e-6
NEG_INF = -1e30
LOWEST = -3e38

NSA_Q_WIDTH = NSA_HEADS * NSA_HEAD_DIM
NSA_KV_WIDTH = NSA_KV_HEADS * NSA_HEAD_DIM
HG_WIDTH = HG_HEADS * HG_DK

LANES = 128
VMEM_LIMIT_BYTES = 56 * 1024 * 1024

NSA_COLS = NSA_GROUP * NSA_QBLOCK
NSA_KTILE = 512
NSA_CDIM = 128
NSA_WKEYS = NSA_WINDOW + NSA_QBLOCK
PEER_TT = 512
PEER_IB = 8


def _cparams(sem):
    return pltpu.CompilerParams(dimension_semantics=sem, vmem_limit_bytes=VMEM_LIMIT_BYTES)


def _rms(x, w):
    return x * lax.rsqrt(jnp.mean(x * x, axis=-1, keepdims=True) + RMS_EPS) * w


def _proj_kernel(x_ref, wn_ref, w_ref, *out_refs, widths, transposed, chunk):
    hb = _rms(x_ref[...], wn_ref[...]).astype(BF16)
    off = 0
    for o_ref, wd, tr in zip(out_refs, widths, transposed):
        for c0 in range(0, wd, chunk):
            cw = min(chunk, wd - c0)
            r = jnp.dot(hb, w_ref[:, off + c0:off + c0 + cw], preferred_element_type=F32)
            if tr:
                o_ref[c0:c0 + cw, :] = r.T
            else:
                o_ref[:, c0:c0 + cw] = r
        off += wd


def _proj(x, wn, w, widths, transposed, tm):
    t, d = x.shape
    n = w.shape[1]
    assert sum(widths) == n and t % tm == 0
    out_shape, out_specs = [], []
    for wd, tr in zip(widths, transposed):
        if tr:
            out_shape.append(jax.ShapeDtypeStruct((wd, t), F32))
            out_specs.append(pl.BlockSpec((wd, tm), lambda i: (0, i)))
        else:
            out_shape.append(jax.ShapeDtypeStruct((t, wd), F32))
            out_specs.append(pl.BlockSpec((tm, wd), lambda i: (i, 0)))
    return pl.pallas_call(
        functools.partial(_proj_kernel, widths=tuple(widths), transposed=tuple(transposed), chunk=512),
        grid=(t // tm,),
        in_specs=[pl.BlockSpec((tm, d), lambda i: (i, 0)),
                  pl.BlockSpec((1, d), lambda i: (0, 0)),
                  pl.BlockSpec((d, n), lambda i: (0, 0))],
        out_specs=out_specs, out_shape=out_shape,
        compiler_params=_cparams(("parallel",)), name="rms_proj",
    )(x, wn, w)


def _compress_kernel(kv_ref, w_ref, o_ref, *, nb):
    x = kv_ref[...]
    width = x.shape[-1]
    x3 = x.reshape(nb, NSA_BLOCK, width) * w_ref[...][None]
    o_ref[...] = jnp.sum(x3, axis=1)


def _compress(kv, wfull, nb):
    t, width = kv.shape
    rows = nb * NSA_BLOCK
    assert t % rows == 0
    return pl.pallas_call(
        functools.partial(_compress_kernel, nb=nb),
        grid=(t // rows,),
        in_specs=[pl.BlockSpec((rows, width), lambda i: (i, 0)),
                  pl.BlockSpec((NSA_BLOCK, width), lambda i: (0, 0))],
        out_specs=pl.BlockSpec((nb, width), lambda i: (i, 0)),
        out_shape=jax.ShapeDtypeStruct((t // NSA_BLOCK, width), F32),
        compiler_params=_cparams(("parallel",)), name="nsa_compress",
    )(kv, wfull)


def _topk_select_bias(score, k):
    n = score.shape[0]
    rows = lax.broadcasted_iota(jnp.int32, score.shape, 0).astype(F32)
    bias = jnp.full(score.shape, NEG_INF, F32)
    for _ in range(k):
        mx = jnp.max(score, axis=0, keepdims=True)
        idx = jnp.min(jnp.where(score == mx, rows, float(n)), axis=0, keepdims=True)
        hit = rows == idx
        bias = jnp.where(hit, 0.0, bias)
        score = jnp.where(hit, LOWEST, score)
    return bias


def _nsa_prompt_kernel(slope_ref, qT_ref, gT_ref, kc_ref, vcT_ref, ksel_ref, vselT_ref,
                       kd_ref, vdT_ref, kw0, kw1, kw2, kw3, kw4, vw0, vw1, vw2, vw3, vw4, bw_ref,
                       out_ref, qs_ref, selb_ref, sa_ref, sb_ref, pa_ref, pb_ref, *, nblk):
    i = pl.program_id(1)
    t0 = i * NSA_QBLOCK
    dh, qb, ncol, tk = NSA_HEAD_DIM, NSA_QBLOCK, NSA_COLS, NSA_KTILE
    slope = slope_ref[...]
    col = lax.broadcasted_iota(jnp.int32, (1, ncol), 1)
    tpos = t0 + (col & (qb - 1))
    tposf = tpos.astype(F32)
    q4 = qT_ref[...] * (dh ** -0.5)
    qT = jnp.concatenate([q4[g * dh:(g + 1) * dh, :] for g in range(NSA_GROUP)], axis=1)
    xrow = lax.broadcasted_iota(jnp.int32, (NSA_CDIM - dh, ncol), 0)

    qc = jnp.concatenate([qT, jnp.zeros((NSA_CDIM - dh, ncol), F32)], axis=0).astype(BF16)
    sc = jnp.dot(kc_ref[...], qc, preferred_element_type=F32)
    c_end = lax.broadcasted_iota(jnp.int32, (nblk, 1), 0) * NSA_BLOCK + (NSA_BLOCK - 1)
    valid = c_end <= tpos
    s = jnp.where(valid, sc - slope * (tposf - c_end.astype(F32)), NEG_INF)
    e = jnp.exp(s - jnp.max(s, axis=0, keepdims=True))
    p = jnp.where(valid, e / jnp.sum(e, axis=0, keepdims=True), 0.0)
    ocT = jnp.dot(vcT_ref[...], p.astype(BF16), preferred_element_type=F32)

    qw = jnp.concatenate([qT, jnp.where(xrow == 0, NEG_INF, 0.0)], axis=0).astype(BF16)
    kw = jnp.concatenate([kw0[...], kw1[...], kw2[...], kw3[...], kw4[...]], axis=0)
    sw = jnp.dot(kw, qw, preferred_element_type=F32) + bw_ref[...]
    ew = jnp.exp(sw - jnp.max(sw, axis=0, keepdims=True))
    vw = jnp.concatenate([vw0[...], vw1[...], vw2[...], vw3[...], vw4[...]], axis=1)
    owT = jnp.dot(vw, ew.astype(BF16), preferred_element_type=F32) / jnp.sum(ew, axis=0, keepdims=True)

    nbt = tk // NSA_BLOCK
    jl = t0 // tk
    s1 = slope.astype(BF16).astype(F32)
    r1 = slope - s1
    s2 = r1.astype(BF16).astype(F32)
    s3 = (r1 - s2).astype(BF16).astype(F32)
    half = float(tk // 2)
    ext = jnp.zeros((NSA_CDIM - dh, ncol), F32)
    for r, v in enumerate((s1, s2, s3, s1 * half, s2 * half, s3 * half)):
        ext = jnp.where(xrow == nbt + r, v, ext)

    qd = jnp.concatenate([qT, ext], axis=0).astype(BF16)
    sd = jnp.dot(kd_ref[...], qd, preferred_element_type=F32)
    kposd = t0 + lax.broadcasted_iota(jnp.int32, (qb, 1), 0)
    sd = jnp.where(kposd > tpos, NEG_INF, sd)
    mx_d = jnp.max(sd, axis=0, keepdims=True)
    pd = jnp.exp(sd - mx_d)
    m_d = mx_d + slope * (jl * tk - tpos).astype(F32)
    l_d = jnp.sum(pd, axis=0, keepdims=True)
    acc_d = jnp.dot(vdT_ref[...], pd.astype(BF16), preferred_element_type=F32)

    imp = p[:, 0:qb]
    for g in range(1, NSA_GROUP):
        imp = imp + p[:, g * qb:(g + 1) * qb]
    blk = lax.broadcasted_iota(jnp.int32, (nblk, qb), 0)
    cur = tpos[:, 0:qb] >> (NSA_BLOCK.bit_length() - 1)
    forced = (blk == 0) | (blk == cur) | (blk == cur - 1)
    score = jnp.where(blk > cur, -1.0, jnp.where(forced, NSA_GROUP + 1.0, imp))
    selb = _topk_select_bias(score, min(NSA_TOPK, nblk))
    blkc = lax.broadcasted_iota(jnp.int32, (nblk, ncol), 0)
    selb_ref[0:nblk, :] = jnp.where(blkc >= t0 // NSA_BLOCK, NEG_INF, jnp.concatenate([selb] * NSA_GROUP, axis=1))
    selb_ref[nblk:nblk + nbt, :] = jnp.full((nbt, ncol), NEG_INF, F32)

    qs_ref[0:dh, :] = qT
    qs_ref[dh:, :] = ext
    n1 = jl + 1
    npairs = (n1 + 1) // 2

    def qk_into(t, s_ref):
        tc = jnp.minimum(t, jl)
        b0 = pl.multiple_of(jnp.where(t < n1, tc * nbt, nblk), 8)
        qs_ref[dh:dh + nbt, :] = selb_ref[pl.ds(b0, nbt), :]
        k0 = pl.multiple_of(tc * tk, tk)
        s_ref[...] = jnp.dot(ksel_ref[pl.ds(k0, tk), :], qs_ref[...].astype(BF16), preferred_element_type=F32)

    def soft(t, s_ref, p_ref, m, l):
        cj = slope * (jnp.minimum(t, jl) * tk - tpos).astype(F32)
        sj = s_ref[...]
        m_new = jnp.maximum(m, jnp.max(sj, axis=0, keepdims=True) + cj)
        pj = jnp.exp(sj - (m_new - cj))
        alpha = jnp.exp(m - m_new)
        p_ref[...] = pj.astype(BF16)
        return m_new, alpha * l + jnp.sum(pj, axis=0, keepdims=True), alpha

    def pv(t, p_ref):
        return jnp.dot(vselT_ref[jnp.clip(t, 0, jl)], p_ref[...], preferred_element_type=F32)

    def pair(i, carry):
        m, l, accp = carry
        ta = 2 * i
        pvb = pv(ta - 1, pb_ref)
        qk_into(ta + 1, sb_ref)
        m, l, alpha = soft(ta, sa_ref, pa_ref, m, l)
        accp = alpha * (accp + pvb)
        pva = pv(ta, pa_ref)
        qk_into(ta + 2, sa_ref)
        m, l, alpha = soft(ta + 1, sb_ref, pb_ref, m, l)
        accp = alpha * (accp + pva)
        return m, l, accp

    pb_ref[...] = jnp.zeros(pb_ref.shape, BF16)
    qk_into(0, sa_ref)
    init = (jnp.full((1, ncol), NEG_INF, F32), jnp.zeros((1, ncol), F32), jnp.zeros((dh, ncol), F32))
    m_s, l_s, accp = lax.fori_loop(0, npairs, pair, init)
    acc_s = accp + pv(2 * npairs - 1, pb_ref)

    m_f = jnp.maximum(m_s, m_d)
    a_s = jnp.exp(m_s - m_f)
    a_d = jnp.exp(m_d - m_f)
    osT = (a_s * acc_s + a_d * acc_d) / (a_s * l_s + a_d * l_d)

    sg = jax.nn.sigmoid(gT_ref[...])

    def gate(c):
        return jnp.concatenate([sg[c * NSA_GROUP + g:c * NSA_GROUP + g + 1, :] for g in range(NSA_GROUP)], axis=1)

    oT = gate(0) * ocT + gate(1) * osT + gate(2) * owT
    o4 = jnp.concatenate([oT[:, g * qb:(g + 1) * qb] for g in range(NSA_GROUP)], axis=0)
    out_ref[...] = o4.T


def _alibi_slopes():
    h = jnp.arange(NSA_HEADS, dtype=F32)
    return (2.0 ** (-8.0 * (h + 1.0) / NSA_HEADS)).reshape(NSA_KV_HEADS, NSA_GROUP)


def _nsa_prompt(qT, gT, kvc, kv_sel, kv_win):
    s_len = qT.shape[1]
    dh, qb, tk, kvh = NSA_HEAD_DIM, NSA_QBLOCK, NSA_KTILE, NSA_KV_HEADS
    assert s_len % tk == 0
    nblk = s_len // NSA_BLOCK
    slopes = _alibi_slopes()
    slope_cols = jnp.repeat(slopes, qb, axis=1).reshape(kvh, 1, NSA_COLS)

    def heads_major(a):
        return a.reshape(a.shape[0], kvh, dh).transpose(1, 0, 2)

    def pad_lanes(a):
        return jnp.pad(a, ((0, 0), (0, 0), (0, NSA_CDIM - a.shape[-1])))

    kc = pad_lanes(heads_major(kvc[:, :NSA_KV_WIDTH])).astype(BF16)
    vcT = heads_major(kvc[:, NSA_KV_WIDTH:]).transpose(0, 2, 1).astype(BF16)
    r = jnp.arange(tk)
    onehot = (r[:, None] // NSA_BLOCK == jnp.arange(tk // NSA_BLOCK)[None, :]).astype(F32)
    lo = (r % (tk // 2)).astype(F32)[:, None]
    hi = (r // (tk // 2)).astype(F32)[:, None]
    kext = jnp.concatenate([onehot, lo, lo, lo, hi, hi, hi], axis=1)
    kext = jnp.tile(kext, (s_len // tk, 1))
    ksel = pad_lanes(jnp.concatenate(
        [heads_major(kv_sel[:, :NSA_KV_WIDTH]), jnp.broadcast_to(kext[None], (kvh,) + kext.shape)], axis=-1)).astype(BF16)
    vsel = heads_major(kv_sel[:, NSA_KV_WIDTH:]).astype(BF16)
    vselT = vsel.reshape(kvh, s_len // tk, tk, dh).transpose(0, 1, 3, 2)
    vselT_flat = vsel.transpose(0, 2, 1)
    kwin = heads_major(kv_win[:, :NSA_KV_WIDTH])
    kwin = jnp.concatenate([kwin, jnp.zeros((kvh, s_len, 1), F32)], axis=-1)
    padk = jnp.zeros((kvh, NSA_WINDOW, dh + 1), F32).at[:, :, dh].set(1.0)
    kwin = pad_lanes(jnp.concatenate([padk, kwin], axis=1)).astype(BF16)
    vwinT = jnp.pad(heads_major(kv_win[:, NSA_KV_WIDTH:]), ((0, 0), (NSA_WINDOW, 0), (0, 0))).transpose(0, 2, 1).astype(BF16)
    rr = jnp.arange(NSA_WKEYS)[:, None]
    cc = jnp.arange(NSA_COLS)[None, :]
    dw = (cc % qb) + NSA_WINDOW - rr
    bw = jnp.where((dw >= 0) & (dw <= NSA_WINDOW), -slope_cols * dw.astype(F32)[None], NEG_INF)

    nq = s_len // qb
    nwb = NSA_WKEYS // qb
    kw_specs = [pl.BlockSpec((None, qb, NSA_CDIM), functools.partial(lambda k, i, j: (k, i + j, 0), j=j)) for j in range(nwb)]
    vw_specs = [pl.BlockSpec((None, dh, qb), functools.partial(lambda k, i, j: (k, 0, i + j), j=j)) for j in range(nwb)]
    assert nwb == 5
    return pl.pallas_call(
        functools.partial(_nsa_prompt_kernel, nblk=nblk),
        grid=(kvh, nq),
        in_specs=[pl.BlockSpec((None, 1, NSA_COLS), lambda k, i: (k, 0, 0)),
                  pl.BlockSpec((NSA_GROUP * dh, qb), lambda k, i: (k, i)),
                  pl.BlockSpec((None, 16, qb), lambda k, i: (k, 0, i)),
                  pl.BlockSpec((None, nblk, NSA_CDIM), lambda k, i: (k, 0, 0)),
                  pl.BlockSpec((None, dh, nblk), lambda k, i: (k, 0, 0)),
                  pl.BlockSpec((None, s_len, NSA_CDIM), lambda k, i: (k, 0, 0)),
                  pl.BlockSpec((None, s_len // tk, dh, tk), lambda k, i: (k, 0, 0, 0)),
                  pl.BlockSpec((None, qb, NSA_CDIM), lambda k, i: (k, i, 0)),
                  pl.BlockSpec((None, dh, qb), lambda k, i: (k, 0, i))]
                 + kw_specs + vw_specs
                 + [pl.BlockSpec((None, NSA_WKEYS, NSA_COLS), lambda k, i: (k, 0, 0))],
        out_specs=pl.BlockSpec((qb, NSA_GROUP * dh), lambda k, i: (i, k)),
        out_shape=jax.ShapeDtypeStruct((s_len, NSA_Q_WIDTH), F32),
        scratch_shapes=[pltpu.VMEM((NSA_CDIM, NSA_COLS), F32),
                        pltpu.VMEM((nblk + tk // NSA_BLOCK, NSA_COLS), F32),
                        pltpu.VMEM((tk, NSA_COLS), F32), pltpu.VMEM((tk, NSA_COLS), F32),
                        pltpu.VMEM((tk, NSA_COLS), BF16), pltpu.VMEM((tk, NSA_COLS), BF16)],
        compiler_params=_cparams(("arbitrary", "arbitrary")), name="nsa_prompt",
    )(slope_cols, qT, gT.reshape(kvh, 16, s_len), kc, vcT, ksel, vselT, ksel, vselT_flat,
      *([kwin] * nwb), *([vwinT] * nwb), bw)


def _hgrn_chunk(qc, zf, vc, lb, st):
    c, sub = HG_CHUNK, HG_SUB
    logf = jnp.log(lb + (1.0 - lb) * jax.nn.sigmoid(zf))
    kc = (1.0 - lb) * jax.nn.sigmoid(-zf)
    tri = (lax.broadcasted_iota(jnp.int32, (c, c), 0) >= lax.broadcasted_iota(jnp.int32, (c, c), 1)).astype(F32)
    cb = jnp.dot(tri, logf, preferred_element_type=F32, precision=lax.Precision.HIGHEST)
    o = lax.dot_general((qc * jnp.exp(cb)).astype(BF16), st.astype(BF16), (((1,), (1,)), ((), ())),
                        preferred_element_type=F32)
    t3 = lax.broadcasted_iota(jnp.int32, (sub, sub, 1), 0) >= lax.broadcasted_iota(jnp.int32, (sub, sub, 1), 1)
    outs = []
    for a in range(c // sub):
        ra = slice(a * sub, (a + 1) * sub)
        cba, qa, ka, va = cb[ra], qc[ra], kc[ra], vc[ra]
        d3 = cba[:, None, :] - cba[None, :, :]
        x3 = jnp.where(t3, jnp.exp(d3), 0.0) * qa[:, None, :] * ka[None, :, :]
        att3 = jnp.sum(x3, axis=2, keepdims=True)
        oa = o[ra] + jnp.sum(att3 * va[None, :, :], axis=1)
        if a > 0:
            ref = cb[a * sub - 1:a * sub, :]
            qd = (qa * jnp.exp(cba - ref)).astype(BF16)
            kd = (kc[:a * sub] * jnp.exp(ref - cb[:a * sub])).astype(BF16)
            att = lax.dot_general(qd, kd, (((1,), (1,)), ((), ())), preferred_element_type=F32)
            oa = oa + jnp.dot(att.astype(BF16), vc[:a * sub].astype(BF16), preferred_element_type=F32)
        outs.append(oa)
    o = jnp.concatenate(outs, axis=0)
    last = cb[c - 1:c, :]
    kdec = (kc * jnp.exp(last - cb)).astype(BF16)
    st = st * jnp.exp(last) + lax.dot_general(vc.astype(BF16), kdec, (((0,), (0,)), ((), ())),
                                              preferred_element_type=F32)
    return o, st


def _hgrn_kernel(lb_ref, wn_ref, q_ref, f_ref, v_ref, g_ref, s0_ref, o_ref, sfin_ref, st_ref, *, nsub):
    c = pl.program_id(1)
    hpb = HG_HPB

    @pl.when(c == 0)
    def _():
        for j in range(hpb):
            st_ref[j] = s0_ref[j].T

    wn = wn_ref[...]

    def body(u, sts):
        rows = pl.ds(pl.multiple_of(u * HG_CHUNK, HG_CHUNK), HG_CHUNK)
        out = []
        for j in range(hpb):
            cols = slice(j * HG_DK, (j + 1) * HG_DK)
            o, st = _hgrn_chunk(q_ref[rows, cols], f_ref[rows, cols], v_ref[rows, cols], lb_ref[:, cols], sts[j])
            g = g_ref[rows, cols]
            o_ref[rows, cols] = _rms(o, wn) * (g * jax.nn.sigmoid(g))
            out.append(st)
        return tuple(out)

    sts = lax.fori_loop(0, nsub, body, tuple(st_ref[j] for j in range(hpb)))
    for j in range(hpb):
        st_ref[j] = sts[j]

    @pl.when(c == pl.num_programs(1) - 1)
    def _():
        for j in range(hpb):
            sfin_ref[j] = sts[j].T


def _hgrn_prompt(hq, hf, hi, hg, lb, wn, s0, tb):
    t = hq.shape[0]
    hpb = HG_HPB
    assert t % tb == 0 and tb % HG_CHUNK == 0 and HG_HEADS % hpb == 0 and HG_DK == HG_DV
    tok = pl.BlockSpec((tb, hpb * HG_DK), lambda h, c: (c, h))
    stt = pl.BlockSpec((hpb, HG_DK, HG_DV), lambda h, c: (h, 0, 0))
    return pl.pallas_call(
        functools.partial(_hgrn_kernel, nsub=tb // HG_CHUNK),
        grid=(HG_HEADS // hpb, t // tb),
        in_specs=[pl.BlockSpec((1, hpb * HG_DK), lambda h, c: (0, h)),
                  pl.BlockSpec((1, HG_DV), lambda h, c: (0, 0)),
                  tok, tok, tok, tok, stt],
        out_specs=[tok, stt],
        out_shape=[jax.ShapeDtypeStruct((t, HG_HEADS * HG_DV), F32),
                   jax.ShapeDtypeStruct((HG_HEADS, HG_DK, HG_DV), F32)],
        scratch_shapes=[pltpu.VMEM((hpb, HG_DV, HG_DK), F32)],
        compiler_params=_cparams(("arbitrary", "arbitrary")), name="hgrn_prompt",
    )(lb, wn, hq, hf, hi, hg, s0)


def _hgrn_step_kernel(lb_ref, wn_ref, q_ref, f_ref, v_ref, g_ref, s0_ref, o_ref, s_ref):
    wn = wn_ref[...]
    for h in range(HG_HEADS):
        rk = slice(h * HG_DK, (h + 1) * HG_DK)
        lb = lb_ref[rk, :]
        zf = f_ref[rk, :]
        f = lb + (1.0 - lb) * jax.nn.sigmoid(zf)
        kk = (1.0 - lb) * jax.nn.sigmoid(-zf)
        vrow = v_ref[:, h * HG_DV:(h + 1) * HG_DV]
        s_new = f * s0_ref[h] + kk * vrow
        s_ref[h] = s_new
        o = jnp.sum(s_new * q_ref[rk, :], axis=0, keepdims=True)
        g = g_ref[:, h * HG_DV:(h + 1) * HG_DV]
        o_ref[:, h * HG_DV:(h + 1) * HG_DV] = _rms(o, wn) * (g * jax.nn.sigmoid(g))


def _hgrn_step(hq_col, hf_col, hi, hg, lb_col, wn, s0):
    b = hi.shape[0]
    col = pl.BlockSpec((None, HG_WIDTH, 1), lambda i: (i, 0, 0))
    row = pl.BlockSpec((None, 1, HG_WIDTH), lambda i: (i, 0, 0))
    st = pl.BlockSpec((None, HG_HEADS, HG_DK, HG_DV), lambda i: (i, 0, 0, 0))
    return pl.pallas_call(
        _hgrn_step_kernel, grid=(b,),
        in_specs=[pl.BlockSpec((HG_WIDTH, 1), lambda i: (0, 0)), pl.BlockSpec((1, HG_DV), lambda i: (0, 0)),
                  col, col, row, row, st],
        out_specs=[row, st],
        out_shape=[jax.ShapeDtypeStruct((b, 1, HG_WIDTH), F32), jax.ShapeDtypeStruct(s0.shape, F32)],
        compiler_params=_cparams(("parallel",)), name="hgrn_step",
    )(lb_col, wn, hq_col, hf_col, hi, hg, s0)


def _merge_kernel(x_ref, oa_ref, ob_ref, ma_ref, mb_ref, pa_ref, pb_ref, wo_ref, y_ref):
    ya = jnp.dot(oa_ref[...].astype(BF16), pa_ref[...], preferred_element_type=F32)
    yb = jnp.dot(ob_ref[...].astype(BF16), pb_ref[...], preferred_element_type=F32)
    mix = jax.nn.sigmoid(ma_ref[...]) * ya + jax.nn.sigmoid(mb_ref[...]) * yb
    y_ref[...] = x_ref[...] + jnp.dot(mix.astype(BF16), wo_ref[...], preferred_element_type=F32)


def _merge(x, oa, ob, ma, mb, pa, pb, wo, tm):
    t, d = x.shape
    tok = pl.BlockSpec((tm, d), lambda i: (i, 0))
    wsp = pl.BlockSpec((d, d), lambda i: (0, 0))
    return pl.pallas_call(
        _merge_kernel, grid=(t // tm,),
        in_specs=[tok, tok, tok, tok, tok, wsp, wsp, wsp],
        out_specs=tok, out_shape=jax.ShapeDtypeStruct((t, d), F32),
        compiler_params=_cparams(("parallel",)), name="branch_merge",
    )(x, oa, ob, ma, mb, pa, pb, wo)


def _topk_rows(s, k):
    n = s.shape[0]
    rows = lax.broadcasted_iota(jnp.int32, s.shape, 0).astype(F32)
    rank = jnp.full(s.shape, float(n), F32)
    tops = []
    for r in range(k):
        mx = jnp.max(s, axis=0, keepdims=True)
        idx = jnp.min(jnp.where(s == mx, rows, float(n)), axis=0, keepdims=True)
        hit = rows == idx
        rank = jnp.where(hit, float(r), rank)
        s = jnp.where(hit, LOWEST, s)
        tops.append(mx)
    return jnp.concatenate(tops, axis=0), rank


_STAIR_GROUPS = ((0, 16), (1, 8), (2, 8), (3, 8))
_STAIR_QUAD = (4, 5, 6, 7)
_STAIR_TAIL = 8


def _stair_rows(t1, t2, op):
    tt = t1.shape[1]
    parts = [op(jnp.broadcast_to(t1[a:a + 1, :], (nb, tt)), t2[0:nb, :]) for a, nb in _STAIR_GROUPS]
    r16 = lax.broadcasted_iota(jnp.int32, (16, tt), 0)
    v1 = jnp.broadcast_to(t1[_STAIR_QUAD[3]:_STAIR_QUAD[3] + 1, :], (16, tt))
    v2 = jnp.broadcast_to(t2[3:4, :], (16, tt))
    for q in (2, 1, 0):
        v1 = jnp.where(r16 < 4 * (q + 1), jnp.broadcast_to(t1[_STAIR_QUAD[q]:_STAIR_QUAD[q] + 1, :], (16, tt)), v1)
        v2 = jnp.where((r16 & 3) == q, jnp.broadcast_to(t2[q:q + 1, :], (16, tt)), v2)
    parts.append(op(v1, v2))
    parts.append(op(t1[_STAIR_TAIL:, :], jnp.broadcast_to(t2[0:1, :], (PK_TOPK - _STAIR_TAIL, tt))))
    return jnp.concatenate(parts, axis=0)


def _stair_row_counts(selc):
    out, r0 = [], 0
    for _, nb in _STAIR_GROUPS:
        out.append(jnp.sum(selc[r0:r0 + nb, :], axis=0, keepdims=True))
        r0 += nb
    quad = selc[r0:r0 + 16, :]
    r16 = lax.broadcasted_iota(jnp.int32, quad.shape, 0)
    for q in range(4):
        out.append(jnp.sum(jnp.where((r16 >> 2) == q, quad, 0.0), axis=0, keepdims=True))
    r0 += 16
    for a in range(PK_TOPK - _STAIR_TAIL):
        out.append(selc[r0 + a:r0 + a + 1, :])
    return out


def _peer_kernel(x_ref, wn_ref, wq_ref, sk_ref, u_ref, vT_ref, wf_ref, y_ref,
                 hnT_ref, acc_ref, wcat_ref, wodd_ref, n_ref, a1_ref, r2_ref, e2_ref, *, ib, nkeys, final_norm):
    i = pl.program_id(1)
    kt = PK_TOPK
    hd = PK_DIM // 2

    @pl.when(i == 0)
    def _():
        hnT = _rms(x_ref[...], wn_ref[...]).T.astype(BF16)
        hnT_ref[...] = hnT
        acc_ref[...] = jnp.zeros(acc_ref.shape, F32)
        wcat_ref[...] = jnp.zeros(wcat_ref.shape, BF16)
        wodd_ref[...] = jnp.zeros(wodd_ref.shape, BF16)
        tt = hnT.shape[1]
        for h in range(PK_HEADS):
            ss, tops, ranks = [], [], []
            for c in range(2):
                r0 = (h * 2 + c) * hd
                qhc = jnp.dot(wq_ref[r0:r0 + hd, :], hnT, preferred_element_type=F32)
                s = jnp.dot(sk_ref[h * 2 + c], qhc.astype(BF16), preferred_element_type=F32)
                top, rank = _topk_rows(s, kt)
                ss.append(s)
                tops.append(top)
                ranks.append(rank)
            cand = _stair_rows(tops[0], tops[1], jnp.add)
            _, crank = _topk_rows(cand, kt)
            selc = (crank < float(kt)).astype(F32)
            n_a = _stair_row_counts(selc)
            e1t = jnp.exp(tops[0] - tops[0][0:1, :])
            e2t = jnp.exp(tops[1] - tops[1][0:1, :])
            z = jnp.sum(selc * _stair_rows(e1t, e2t, jnp.multiply), axis=0, keepdims=True)
            nfull = jnp.zeros((nkeys, tt), F32)
            for a in range(kt):
                nfull = jnp.where(ranks[0] == float(a), n_a[a], nfull)
            n_ref[h] = nfull
            a1_ref[h] = jnp.exp(ss[0] - tops[0][0:1, :]) / z
            r2_ref[h] = ranks[1].astype(BF16)
            e2_ref[h] = jnp.exp(ss[1] - tops[1][0:1, :]).astype(BF16)

    nsteps = pl.num_programs(1) - 1
    nch = 2
    cw = hnT_ref.shape[1] // nch

    def step(w_read, w_write):
        hnT = hnT_ref[...]
        for ii in range(ib):
            if ii % (ib // nch) == 1:
                cols = slice((ii // (ib // nch)) * cw, (ii // (ib // nch) + 1) * cw)
                acc_ref[:, cols] += jnp.dot(vT_ref[...], w_read[:, cols], preferred_element_type=F32)
            ig = i * ib + ii
            aT = jnp.dot(u_ref[ii * nkeys:(ii + 1) * nkeys, :], hnT, preferred_element_type=F32)
            gsum = jnp.zeros(aT.shape, BF16)
            for h in range(PK_HEADS):
                nrow = n_ref[h, pl.ds(ig, 1), :].astype(BF16)
                arow = a1_ref[h, pl.ds(ig, 1), :].astype(BF16)
                gsum = gsum + jnp.where(r2_ref[h] < nrow, arow * e2_ref[h], jnp.zeros((), BF16))
            w_write[ii * nkeys:(ii + 1) * nkeys, :] = (jax.nn.gelu(aT) * gsum.astype(F32)).astype(BF16)

    @pl.when((i < nsteps) & (i % 2 == 0))
    def _():
        step(wodd_ref, wcat_ref)

    @pl.when((i < nsteps) & (i % 2 == 1))
    def _():
        step(wcat_ref, wodd_ref)

    @pl.when(i == nsteps)
    def _():
        w_last = wodd_ref if (nkeys // ib) % 2 == 0 else wcat_ref
        acc = acc_ref[...] + jnp.dot(vT_ref[...], w_last[...], preferred_element_type=F32)
        y = x_ref[...] + acc.T
        if final_norm:
            y = _rms(y, wf_ref[...])
        y_ref[...] = y


def _peer(x, wn, wqT, sk, u, vT, wf, tt, ib, final_norm):
    t, d = x.shape
    nkeys = sk.shape[1]
    assert t % tt == 0 and nkeys % ib == 0
    nsteps = nkeys // ib
    stat = pltpu.VMEM((PK_HEADS, nkeys, tt), F32)
    stat16 = pltpu.VMEM((PK_HEADS, nkeys, tt), BF16)
    wbuf = pltpu.VMEM((ib * nkeys, tt), BF16)
    return pl.pallas_call(
        functools.partial(_peer_kernel, ib=ib, nkeys=nkeys, final_norm=final_norm),
        grid=(t // tt, nsteps + 1),
        in_specs=[pl.BlockSpec((tt, d), lambda a, i: (a, 0)),
                  pl.BlockSpec((1, d), lambda a, i: (0, 0)),
                  pl.BlockSpec(wqT.shape, lambda a, i: (0, 0)),
                  pl.BlockSpec(sk.shape, lambda a, i: (0, 0, 0)),
                  pl.BlockSpec((ib * nkeys, d), lambda a, i: (jnp.minimum(i, nsteps - 1), 0)),
                  pl.BlockSpec((None, d, ib * nkeys), lambda a, i: (jnp.maximum(i - 1, 0), 0, 0)),
                  pl.BlockSpec((1, d), lambda a, i: (0, 0))],
        out_specs=pl.BlockSpec((tt, d), lambda a, i: (a, 0)),
        out_shape=jax.ShapeDtypeStruct((t, d), F32),
        scratch_shapes=[pltpu.VMEM((d, tt), BF16), pltpu.VMEM((d, tt), F32), wbuf, wbuf,
                        stat, stat, stat16, stat16],
        compiler_params=_cparams(("arbitrary", "arbitrary")), name="peer_dense",
    )(x, wn, wqT, sk, u, vT, wf)


def _compress_pages_kernel(pt_ref, *refs, pg, bpp):
    w = refs[pg][...]
    o_ref = refs[pg + 1]
    for j in range(pg):
        x = refs[j][...]
        x5 = x.reshape((bpp, NSA_BLOCK) + x.shape[1:]) * w[None]
        o_ref[j * bpp:(j + 1) * bpp] = jnp.sum(x5, axis=1)


def _compress_pages(cache, page_table, w_rows, pg):
    page, tail = cache.shape[1], cache.shape[2:]
    db, npages = page_table.shape
    bpp = page // NSA_BLOCK
    assert npages % pg == 0
    zeros = (0,) * len(tail)
    page_specs = [pl.BlockSpec((None, page) + tail,
                               functools.partial(lambda b, g, pt, j: (pt[b, g * pg + j], 0) + zeros, j=j)) for j in range(pg)]
    return pl.pallas_call(
        functools.partial(_compress_pages_kernel, pg=pg, bpp=bpp),
        grid_spec=pltpu.PrefetchScalarGridSpec(
            num_scalar_prefetch=1, grid=(db, npages // pg),
            in_specs=page_specs + [pl.BlockSpec((NSA_BLOCK,) + tail, lambda b, g, pt: (0,) + zeros)],
            out_specs=pl.BlockSpec((None, pg * bpp) + tail, lambda b, g, pt: (b, g) + zeros)),
        out_shape=jax.ShapeDtypeStruct((db, npages * bpp) + tail, F32),
        compiler_params=_cparams(("arbitrary", "arbitrary")), name="nsa_compress_pages",
    )(page_table, *([cache] * pg), w_rows)


def _nsa_decode_head_kernel(slope_ref, qbd_ref, g_ref, kvc_ref, wnew_ref, cwin_ref, part_ref, idx_ref,
                            *, past_len, k_past):
    kvw = NSA_KV_WIDTH
    qbd = qbd_ref[...]
    qb = qbd.astype(BF16)
    slope = slope_ref[...]
    nh = qbd.shape[0]
    npb = kvc_ref.shape[0]
    nt = (((1,), (1,)), ((), ()))
    kvc = kvc_ref[...]
    sc = lax.dot_general(qb, kvc[:, :kvw].astype(BF16), nt, preferred_element_type=F32)
    c_end = lax.broadcasted_iota(jnp.int32, (1, npb), 1) * NSA_BLOCK + (NSA_BLOCK - 1)
    s = sc - slope * (past_len - c_end).astype(F32)
    e = jnp.exp(s - jnp.max(s, axis=1, keepdims=True))
    p = e / jnp.sum(e, axis=1, keepdims=True)
    oc = jnp.dot(p.astype(BF16), kvc[:, kvw:].astype(BF16), preferred_element_type=F32)
    rows = []
    for k in range(NSA_KV_HEADS):
        r = p[k * NSA_GROUP:k * NSA_GROUP + 1, :]
        for g in range(1, NSA_GROUP):
            r = r + p[k * NSA_GROUP + g:k * NSA_GROUP + g + 1, :]
        rows.append(jnp.broadcast_to(r, (NSA_GROUP, npb)))
    imp = jnp.concatenate(rows, axis=0)
    blk = lax.broadcasted_iota(jnp.int32, (nh, npb), 1)
    score = jnp.where((blk == 0) | (blk == npb - 1), NSA_GROUP + 1.0, imp)
    lanes = blk.astype(F32)
    picks = []
    for _ in range(k_past):
        mx = jnp.max(score, axis=1, keepdims=True)
        idx = jnp.min(jnp.where(score == mx, lanes, float(npb)), axis=1, keepdims=True)
        score = jnp.where(lanes == idx, LOWEST, score)
        picks.append(idx)
    idx_ref[...] = jnp.concatenate(picks, axis=1).astype(jnp.int32)
    cw = cwin_ref[...]
    nw = cw.shape[0]
    sw = lax.dot_general(qb, cw[:, :kvw].astype(BF16), nt, preferred_element_type=F32)
    dw = nw - lax.broadcasted_iota(jnp.int32, (1, nw), 1)
    sw = sw - slope * dw.astype(F32)
    wnew = wnew_ref[...]
    s_n = jnp.sum(qbd * wnew[:, :kvw], axis=1, keepdims=True)
    m = jnp.maximum(jnp.max(sw, axis=1, keepdims=True), s_n)
    ew = jnp.exp(sw - m)
    en = jnp.exp(s_n - m)
    ow = (jnp.dot(ew.astype(BF16), cw[:, kvw:].astype(BF16), preferred_element_type=F32) + en * wnew[:, kvw:]) \
        / (jnp.sum(ew, axis=1, keepdims=True) + en)
    sg = jax.nn.sigmoid(g_ref[...])
    part = sg[:, 0:1] * oc + sg[:, 2:3] * ow
    for k in range(NSA_KV_HEADS):
        part_ref[k * NSA_GROUP:(k + 1) * NSA_GROUP, :] = \
            part[k * NSA_GROUP:(k + 1) * NSA_GROUP, k * NSA_HEAD_DIM:(k + 1) * NSA_HEAD_DIM]


def _nsa_decode_gather_kernel(pt_ref, ix_ref, slope_ref, q_ref, g_ref, part_ref, new_ref, *refs, nsel, past_len):
    blocks, o_ref = refs[:nsel], refs[nsel]
    b, k = pl.program_id(0), pl.program_id(1)
    nkv, dh = new_ref.shape[1], new_ref.shape[2]
    rows = NSA_BLOCK * nkv
    q = q_ref[...]
    slope = slope_ref[...]
    kall = jnp.concatenate([blocks[s][:, 0].reshape(rows, dh) for s in range(nsel)], axis=0).astype(BF16)
    vall = jnp.concatenate([blocks[s][:, 1].reshape(rows, dh) for s in range(nsel)], axis=0).astype(BF16)
    sc = lax.dot_general(q.astype(BF16), kall, (((1,), (1,)), ((), ())), preferred_element_type=F32)
    lane = lax.broadcasted_iota(jnp.int32, (1, rows), 1)
    lkey = lane // nkv
    own = (lane - lkey * nkv) == k
    bias = jnp.concatenate(
        [jnp.where(own, -slope * (past_len - (ix_ref[b, k, s] * NSA_BLOCK + lkey)).astype(F32), NEG_INF)
         for s in range(nsel)], axis=1)
    sc = sc + bias
    knew, vnew = new_ref[0], new_ref[1]
    krow = lax.broadcasted_iota(jnp.int32, (nkv, 1), 0) == k
    k_own = jnp.sum(jnp.where(krow, knew, 0.0), axis=0, keepdims=True)
    v_own = jnp.sum(jnp.where(krow, vnew, 0.0), axis=0, keepdims=True)
    s_n = jnp.sum(q * k_own, axis=1, keepdims=True)
    m = jnp.maximum(jnp.max(sc, axis=1, keepdims=True), s_n)
    p = jnp.exp(sc - m)
    p_n = jnp.exp(s_n - m)
    o_s = (jnp.dot(p.astype(BF16), vall, preferred_element_type=F32) + p_n * v_own) \
        / (jnp.sum(p, axis=1, keepdims=True) + p_n)
    o_ref[...] = part_ref[...] + jax.nn.sigmoid(g_ref[...])[:, 1:2] * o_s


def _nsa_sample(q, kv_cmp, kv_sel, kv_win, glog, cache_cmp, cache_sel, cache_win, page_table, w_cmp):
    db = q.shape[0]
    kvh, grp, dh, kvw = NSA_KV_HEADS, NSA_GROUP, NSA_HEAD_DIM, NSA_KV_WIDTH
    n_phys, page = cache_cmp.shape[0], cache_cmp.shape[1]
    npages = page_table.shape[1]
    past_len = npages * page
    nwin = cache_win.shape[1]
    assert past_len >= nwin and past_len % NSA_BLOCK == 0
    npb = past_len // NSA_BLOCK
    pg = 8 if npages % 8 == 0 else 1
    width = 2 * kvw
    w_rows = jnp.broadcast_to(w_cmp.T[:, :, None, None], (NSA_BLOCK,) + cache_cmp.shape[2:])
    kvc = _compress_pages(cache_cmp, page_table, w_rows, pg).reshape(db, npb, width)
    q4 = q.reshape(db, kvh, grp, dh) * (dh ** -0.5)
    qbd = (q4[:, :, :, None, :] * jnp.eye(kvh, dtype=F32)[None, :, None, :, None]).reshape(db, kvh * grp, kvw)
    slope = _alibi_slopes().reshape(kvh * grp, 1)
    nh = kvh * grp
    k_past = min(NSA_TOPK, npb + 1) - 1
    full2 = lambda shape: pl.BlockSpec(shape, lambda b: (0,) * len(shape))
    per_b = lambda shape: pl.BlockSpec((None,) + shape, lambda b: (b,) + (0,) * len(shape))
    part, idx = pl.pallas_call(
        functools.partial(_nsa_decode_head_kernel, past_len=past_len, k_past=k_past),
        grid=(db,),
        in_specs=[full2((nh, 1)), per_b((nh, kvw)), per_b((nh, 3)), per_b((npb, width)), per_b((1, width)),
                  per_b((nwin, width))],
        out_specs=[per_b((nh, dh)), per_b((nh, k_past))],
        out_shape=[jax.ShapeDtypeStruct((db, nh, dh), F32), jax.ShapeDtypeStruct((db, nh, k_past), jnp.int32)],
        compiler_params=_cparams(("parallel",)), name="nsa_decode_head",
    )(slope, qbd, glog, kvc, kv_win.reshape(db, 1, width), cache_win.reshape(db, nwin, width))
    idx4 = idx.reshape(db, kvh, grp, k_past)[:, :, 0, :]
    bpp = page // NSA_BLOCK
    tail = cache_sel.shape[2:]
    zeros = (0,) * len(tail)
    blk_specs = [pl.BlockSpec((None, NSA_BLOCK) + tail,
                              functools.partial(lambda b, k, pt, ix, s: (pt[b, ix[b, k, s] // bpp], ix[b, k, s] % bpp) + zeros, s=s))
                 for s in range(k_past)]
    hsp = lambda shape: pl.BlockSpec((None, None) + shape, lambda b, k, pt, ix: (b, k) + (0,) * len(shape))
    o = pl.pallas_call(
        functools.partial(_nsa_decode_gather_kernel, nsel=k_past, past_len=past_len),
        grid_spec=pltpu.PrefetchScalarGridSpec(
            num_scalar_prefetch=2, grid=(db, kvh),
            in_specs=[pl.BlockSpec((None, grp, 1), lambda b, k, pt, ix: (k, 0, 0)), hsp((grp, dh)), hsp((grp, 3)),
                      hsp((grp, dh)), pl.BlockSpec((None,) + tail, lambda b, k, pt, ix: (b,) + zeros)] + blk_specs,
            out_specs=hsp((grp, dh))),
        out_shape=jax.ShapeDtypeStruct((db, kvh, grp, dh), F32),
        compiler_params=_cparams(("arbitrary", "arbitrary")), name="nsa_decode_gather",
    )(page_table, idx4, slope.reshape(kvh, grp, 1), q4, glog.reshape(db, kvh, grp, 3), part.reshape(db, kvh, grp, dh),
      kv_sel.reshape((db,) + tail), *([cache_sel] * k_past))
    new_win = jnp.concatenate([cache_win, kv_win.reshape((db, 1) + cache_win.shape[2:])], axis=1)[:, 1:]
    return o.reshape(db, nh * dh), new_win


def _split_w_in(w_in):
    pts, acc = [], 0
    for w in (NSA_Q_WIDTH, 6 * NSA_KV_WIDTH, 3 * NSA_HEADS, HG_WIDTH, HG_WIDTH, HG_WIDTH, HG_WIDTH, w_in.shape[0]):
        acc += w
        pts.append(acc)
    return jnp.split(w_in, pts, axis=1)


def _prep_layer(w_in, w_proj_nsa, w_proj_hgrn, w_out, w_peer_q, peer_sub_keys, peer_u, peer_v):
    d = w_in.shape[0]
    wq, wkv, wg, wbq, wbf, wbi, wbg, wma, wmb = _split_w_in(w_in)
    wg = wg.reshape(d, NSA_KV_HEADS, NSA_GROUP, 3).transpose(0, 1, 3, 2).reshape(d, NSA_KV_HEADS, 3 * NSA_GROUP)
    wg = jnp.pad(wg, ((0, 0), (0, 0), (0, 16 - 3 * NSA_GROUP))).reshape(d, NSA_KV_HEADS * 16)
    wg = jnp.pad(wg, ((0, 0), (0, LANES - NSA_KV_HEADS * 16)))
    nkeys = peer_sub_keys.shape[2]
    return dict(
        w_qg=jnp.concatenate([wq, wg], axis=1).astype(BF16),
        w_kv=wkv.astype(BF16),
        w_hg=jnp.concatenate([wbq, wbf, wbi, wbg], axis=1).astype(BF16),
        w_m=jnp.concatenate([wma, wmb], axis=1).astype(BF16),
        pa=w_proj_nsa.astype(BF16), pb=w_proj_hgrn.astype(BF16), wo=w_out.astype(BF16),
        wqT=w_peer_q.T.astype(BF16),
        sk=peer_sub_keys.reshape(PK_HEADS * 2, nkeys, PK_DIM // 2).astype(BF16),
        u=peer_u.astype(BF16),
        vT=peer_v.astype(BF16).reshape(nkeys // PEER_IB, PEER_IB * nkeys, d).transpose(0, 2, 1),
    )


def _tile(t, pref):
    return pref if t % pref == 0 else t


def _peer_tokens(x, wn, wf, prm, final_norm):
    t = x.shape[0]
    tp = -(-t // LANES) * LANES
    xp = jnp.pad(x, ((0, tp - t), (0, 0)))
    tt = PEER_TT if tp % PEER_TT == 0 else LANES
    y = _peer(xp, wn, prm["wqT"], prm["sk"], prm["u"], prm["vT"], wf, tt, PEER_IB, final_norm)
    return y[:t]


def _layer_prompt(x, lb, prm, w_norm_mix, w_cmp, w_hgrn_norm, w_norm_ffn, w_norm_final, final_norm):
    s_len, d = x.shape
    tm = _tile(s_len, 256)
    kvw = NSA_KV_WIDTH
    kv_cmp, kv_sel, kv_win = _proj(x, w_norm_mix, prm["w_kv"], [2 * kvw] * 3, [False] * 3, tm)
    qT, gT = _proj(x, w_norm_mix, prm["w_qg"], [NSA_Q_WIDTH, LANES], [True, True], tm)
    hq, hf, hi, hg = _proj(x, w_norm_mix, prm["w_hg"], [HG_WIDTH] * 4, [False] * 4, tm)
    ma, mb = _proj(x, w_norm_mix, prm["w_m"], [d, d], [False, False], tm)
    wfull = jnp.concatenate([jnp.broadcast_to(w_cmp[0][:, None], (NSA_BLOCK, kvw)),
                             jnp.broadcast_to(w_cmp[1][:, None], (NSA_BLOCK, kvw))], axis=1)
    nblk = s_len // NSA_BLOCK
    kvc = _compress(kv_cmp, wfull, 8 if nblk % 8 == 0 else nblk)
    o_a = _nsa_prompt(qT, gT[:NSA_KV_HEADS * 16], kvc, kv_sel, kv_win)
    s0 = jnp.zeros((HG_HEADS, HG_DK, HG_DV), F32)
    o_b, s_fin = _hgrn_prompt(hq, hf, hi, hg, lb, w_hgrn_norm, s0, _tile(s_len, 512))
    x1 = _merge(x, o_a, o_b, ma, mb, prm["pa"], prm["pb"], prm["wo"], tm)
    x2 = _peer_tokens(x1, w_norm_ffn, w_norm_final, prm, final_norm)
    shp = (s_len, 2, NSA_KV_HEADS, NSA_HEAD_DIM)
    return x2, kv_cmp.reshape(shp), kv_sel.reshape(shp), kv_win.reshape(shp)[-NSA_WINDOW:], s_fin


def _layer_sample(x, lb, prm, w_norm_mix, w_cmp, w_hgrn_norm, w_norm_ffn, w_norm_final, final_norm,
                  cache_cmp, cache_sel, cache_win, s0, page_table):
    b, d = x.shape
    kvw = NSA_KV_WIDTH
    w_q = prm["w_qg"][:, :NSA_Q_WIDTH]
    (q,) = _proj(x, w_norm_mix, w_q, [NSA_Q_WIDTH], [False], b)
    kv_cmp, kv_sel, kv_win = _proj(x, w_norm_mix, prm["w_kv"], [2 * kvw] * 3, [False] * 3, b)
    (gTt,) = _proj(x, w_norm_mix, prm["w_qg"][:, NSA_Q_WIDTH:], [LANES], [False], b)
    hq, hf, hi, hg = _proj(x, w_norm_mix, prm["w_hg"], [HG_WIDTH] * 4, [False] * 4, b)
    ma, mb = _proj(x, w_norm_mix, prm["w_m"], [d, d], [False, False], b)
    shp = (b, 1, 2, NSA_KV_HEADS, NSA_HEAD_DIM)
    glog = (gTt[:, :NSA_KV_HEADS * 16].reshape(b, NSA_KV_HEADS, 16)[..., :3 * NSA_GROUP]
            .reshape(b, NSA_KV_HEADS, 3, NSA_GROUP).transpose(0, 1, 3, 2).reshape(b, NSA_HEADS, 3))
    o_a, new_win = _nsa_sample(q, kv_cmp, kv_sel, kv_win, glog, cache_cmp, cache_sel, cache_win, page_table, w_cmp)
    o_b, s_new = _hgrn_step(hq.reshape(b, HG_WIDTH, 1), hf.reshape(b, HG_WIDTH, 1), hi.reshape(b, 1, HG_WIDTH),
                            hg.reshape(b, 1, HG_WIDTH), lb.reshape(HG_WIDTH, 1), w_hgrn_norm, s0)
    x1 = _merge(x, o_a.reshape(b, NSA_Q_WIDTH), o_b.reshape(b, HG_WIDTH), ma, mb, prm["pa"], prm["pb"], prm["wo"], b)
    x2 = _peer_tokens(x1, w_norm_ffn, w_norm_final, prm, final_norm)
    return x2, kv_cmp.reshape(shp), kv_sel.reshape(shp), new_win, s_new


def kernel(x_prompt, x_sample, cache_cmp_kv, cache_sel_kv, cache_win_kv, state_hgrn, page_table,
           w_norm_mix, w_in, w_cmp, w_proj_nsa, w_proj_hgrn, w_hgrn_norm, hgrn_lb_logits, w_out,
           w_norm_ffn, w_peer_q, peer_sub_keys, peer_u, peer_v, w_norm_final):
    depth = w_in.shape[0]
    bsz, s_len, d = x_prompt.shape
    db, dt, _ = x_sample.shape
    assert dt == 1
    lbs = jnp.cumsum(jax.nn.softmax(hgrn_lb_logits.astype(F32), axis=0), axis=0)
    wfin = w_norm_final.reshape(1, d)
    xp = [x_prompt[b] for b in range(bsz)]
    xs = x_sample.reshape(db, d)
    st_p, st_s = [], []
    for l in range(depth):
        last = l == depth - 1
        prm = _prep_layer(w_in[l], w_proj_nsa[l], w_proj_hgrn[l], w_out[l], w_peer_q[l], peer_sub_keys[l],
                          peer_u[l], peer_v[l])
        shared = (lbs[l].reshape(1, HG_WIDTH), prm, w_norm_mix[l].reshape(1, d), w_cmp[l],
                  w_hgrn_norm[l].reshape(1, HG_DV), w_norm_ffn[l].reshape(1, d), wfin, last)
        outs = [_layer_prompt(xp[b], *shared) for b in range(bsz)]
        xp = [o[0] for o in outs]
        st_p.append(tuple(jnp.stack([o[k] for o in outs]) for k in range(1, 5)))
        xs, *ss = _layer_sample(xs, *shared, cache_cmp_kv[l], cache_sel_kv[l], cache_win_kv[l], state_hgrn[l],
                                page_table)
        st_s.append(tuple(ss))
    y_prompt = jnp.stack(xp)
    y_sample = xs.reshape(db, dt, d)
    return (y_prompt, y_sample,
            jnp.stack([s[0] for s in st_p]), jnp.stack([s[1] for s in st_p]),
            jnp.stack([s[2] for s in st_p]), jnp.stack([s[3] for s in st_p]),
            jnp.stack([s[0] for s in st_s]), jnp.stack([s[1] for s in st_s]),
            jnp.stack([s[2] for s in st_s]), jnp.stack([s[3] for s in st_s]))
```

```python
import functools

import jax
import jax.numpy as jnp
from jax import lax
from jax.experimental import pallas as pl
from jax.experimental.pallas import tpu as pltpu

F32 = jnp.float32
BF16 = jnp.bfloat16

NSA_HEADS = 16
NSA_KV_HEADS = 4
NSA_GROUP = NSA_HEADS // NSA_KV_HEADS
NSA_HEAD_DIM = 64
NSA_BLOCK = 64
NSA_TOPK = 16
NSA_WINDOW = 512
NSA_QBLOCK = 128
HG_HEADS = 8
HG_DK = 128
HG_DV = 128
HG_CHUNK = 64
HG_SUB = 16
HG_HPB = 4
PK_HEADS = 8
PK_DIM = 256
PK_TOPK = 16
RMS_EPS = 1e-6
NEG_INF = -1e30
LOWEST = -3e38

NSA_Q_WIDTH = NSA_HEADS * NSA_HEAD_DIM
NSA_KV_WIDTH = NSA_KV_HEADS * NSA_HEAD_DIM
HG_WIDTH = HG_HEADS * HG_DK

LANES = 128
VMEM_LIMIT_BYTES = 56 * 1024 * 1024

NSA_COLS = NSA_GROUP * NSA_QBLOCK
NSA_KTILE = 512
NSA_CDIM = 128
NSA_WKEYS = NSA_WINDOW + NSA_QBLOCK
PEER_TT = 512
PEER_IB = 8


def _cparams(sem):
    return pltpu.CompilerParams(dimension_semantics=sem, vmem_limit_bytes=VMEM_LIMIT_BYTES)


def _rms(x, w):
    return x * lax.rsqrt(jnp.mean(x * x, axis=-1, keepdims=True) + RMS_EPS) * w


def _proj_kernel(x_ref, wn_ref, w_ref, *out_refs, widths, transposed, chunk):
    hb = _rms(x_ref[...], wn_ref[...]).astype(BF16)
    off = 0
    for o_ref, wd, tr in zip(out_refs, widths, transposed):
        for c0 in range(0, wd, chunk):
            cw = min(chunk, wd - c0)
            r = jnp.dot(hb, w_ref[:, off + c0:off + c0 + cw], preferred_element_type=F32)
            if tr:
                o_ref[c0:c0 + cw, :] = r.T
            else:
                o_ref[:, c0:c0 + cw] = r
        off += wd


def _proj(x, wn, w, widths, transposed, tm):
    t, d = x.shape
    n = w.shape[1]
    assert sum(widths) == n and t % tm == 0
    out_shape, out_specs = [], []
    for wd, tr in zip(widths, transposed):
        if tr:
            out_shape.append(jax.ShapeDtypeStruct((wd, t), F32))
            out_specs.append(pl.BlockSpec((wd, tm), lambda i: (0, i)))
        else:
            out_shape.append(jax.ShapeDtypeStruct((t, wd), F32))
            out_specs.append(pl.BlockSpec((tm, wd), lambda i: (i, 0)))
    return pl.pallas_call(
        functools.partial(_proj_kernel, widths=tuple(widths), transposed=tuple(transposed), chunk=512),
        grid=(t // tm,),
        in_specs=[pl.BlockSpec((tm, d), lambda i: (i, 0)),
                  pl.BlockSpec((1, d), lambda i: (0, 0)),
                  pl.BlockSpec((d, n), lambda i: (0, 0))],
        out_specs=out_specs, out_shape=out_shape,
        compiler_params=_cparams(("parallel",)), name="rms_proj",
    )(x, wn, w)


def _compress_kernel(kv_ref, w_ref, o_ref, *, nb):
    x = kv_ref[...]
    width = x.shape[-1]
    x3 = x.reshape(nb, NSA_BLOCK, width) * w_ref[...][None]
    o_ref[...] = jnp.sum(x3, axis=1)


def _compress(kv, wfull, nb):
    t, width = kv.shape
    rows = nb * NSA_BLOCK
    assert t % rows == 0
    return pl.pallas_call(
        functools.partial(_compress_kernel, nb=nb),
        grid=(t // rows,),
        in_specs=[pl.BlockSpec((rows, width), lambda i: (i, 0)),
                  pl.BlockSpec((NSA_BLOCK, width), lambda i: (0, 0))],
        out_specs=pl.BlockSpec((nb, width), lambda i: (i, 0)),
        out_shape=jax.ShapeDtypeStruct((t // NSA_BLOCK, width), F32),
        compiler_params=_cparams(("parallel",)), name="nsa_compress",
    )(kv, wfull)


def _topk_select_bias(score, k):
    n = score.shape[0]
    rows = lax.broadcasted_iota(jnp.int32, score.shape, 0).astype(F32)
    bias = jnp.full(score.shape, NEG_INF, F32)
    for _ in range(k):
        mx = jnp.max(score, axis=0, keepdims=True)
        idx = jnp.min(jnp.where(score == mx, rows, float(n)), axis=0, keepdims=True)
        hit = rows == idx
        bias = jnp.where(hit, 0.0, bias)
        score = jnp.where(hit, LOWEST, score)
    return bias


def _nsa_prompt_kernel(slope_ref, qT_ref, gT_ref, kc_ref, vcT_ref, ksel_ref, vselT_ref,
                       kd_ref, vdT_ref, kw0, kw1, kw2, kw3, kw4, vw0, vw1, vw2, vw3, vw4, bw_ref,
                       out_ref, qs_ref, selb_ref, sa_ref, sb_ref, pa_ref, pb_ref, *, nblk):
    i = pl.program_id(1)
    t0 = i * NSA_QBLOCK
    dh, qb, ncol, tk = NSA_HEAD_DIM, NSA_QBLOCK, NSA_COLS, NSA_KTILE
    slope = slope_ref[...]
    col = lax.broadcasted_iota(jnp.int32, (1, ncol), 1)
    tpos = t0 + (col & (qb - 1))
    tposf = tpos.astype(F32)
    q4 = qT_ref[...] * (dh ** -0.5)
    qT = jnp.concatenate([q4[g * dh:(g + 1) * dh, :] for g in range(NSA_GROUP)], axis=1)
    xrow = lax.broadcasted_iota(jnp.int32, (NSA_CDIM - dh, ncol), 0)

    qc = jnp.concatenate([qT, jnp.zeros((NSA_CDIM - dh, ncol), F32)], axis=0).astype(BF16)
    sc = jnp.dot(kc_ref[...], qc, preferred_element_type=F32)
    c_end = lax.broadcasted_iota(jnp.int32, (nblk, 1), 0) * NSA_BLOCK + (NSA_BLOCK - 1)
    valid = c_end <= tpos
    s = jnp.where(valid, sc - slope * (tposf - c_end.astype(F32)), NEG_INF)
    e = jnp.exp(s - jnp.max(s, axis=0, keepdims=True))
    p = jnp.where(valid, e / jnp.sum(e, axis=0, keepdims=True), 0.0)
    ocT = jnp.dot(vcT_ref[...], p.astype(BF16), preferred_element_type=F32)

    qw = jnp.concatenate([qT, jnp.where(xrow == 0, NEG_INF, 0.0)], axis=0).astype(BF16)
    kw = jnp.concatenate([kw0[...], kw1[...], kw2[...], kw3[...], kw4[...]], axis=0)
    sw = jnp.dot(kw, qw, preferred_element_type=F32) + bw_ref[...]
    ew = jnp.exp(sw - jnp.max(sw, axis=0, keepdims=True))
    vw = jnp.concatenate([vw0[...], vw1[...], vw2[...], vw3[...], vw4[...]], axis=1)
    owT = jnp.dot(vw, ew.astype(BF16), preferred_element_type=F32) / jnp.sum(ew, axis=0, keepdims=True)

    nbt = tk // NSA_BLOCK
    jl = t0 // tk
    s1 = slope.astype(BF16).astype(F32)
    r1 = slope - s1
    s2 = r1.astype(BF16).astype(F32)
    s3 = (r1 - s2).astype(BF16).astype(F32)
    half = float(tk // 2)
    ext = jnp.zeros((NSA_CDIM - dh, ncol), F32)
    for r, v in enumerate((s1, s2, s3, s1 * half, s2 * half, s3 * half)):
        ext = jnp.where(xrow == nbt + r, v, ext)

    qd = jnp.concatenate([qT, ext], axis=0).astype(BF16)
    sd = jnp.dot(kd_ref[...], qd, preferred_element_type=F32)
    kposd = t0 + lax.broadcasted_iota(jnp.int32, (qb, 1), 0)
    sd = jnp.where(kposd > tpos, NEG_INF, sd)
    mx_d = jnp.max(sd, axis=0, keepdims=True)
    pd = jnp.exp(sd - mx_d)
    m_d = mx_d + slope * (jl * tk - tpos).astype(F32)
    l_d = jnp.sum(pd, axis=0, keepdims=True)
    acc_d = jnp.dot(vdT_ref[...], pd.astype(BF16), preferred_element_type=F32)

    imp = p[:, 0:qb]
    for g in range(1, NSA_GROUP):
        imp = imp + p[:, g * qb:(g + 1) * qb]
    blk = lax.broadcasted_iota(jnp.int32, (nblk, qb), 0)
    cur = tpos[:, 0:qb] >> (NSA_BLOCK.bit_length() - 1)
    forced = (blk == 0) | (blk == cur) | (blk == cur - 1)
    score = jnp.where(blk > cur, -1.0, jnp.where(forced, NSA_GROUP + 1.0, imp))
    selb = _topk_select_bias(score, min(NSA_TOPK, nblk))
    blkc = lax.broadcasted_iota(jnp.int32, (nblk, ncol), 0)
    selb_ref[0:nblk, :] = jnp.where(blkc >= t0 // NSA_BLOCK, NEG_INF, jnp.concatenate([selb] * NSA_GROUP, axis=1))
    selb_ref[nblk:nblk + nbt, :] = jnp.full((nbt, ncol), NEG_INF, F32)

    qs_ref[0:dh, :] = qT
    qs_ref[dh:, :] = ext
    n1 = jl + 1
    npairs = (n1 + 1) // 2

    def qk_into(t, s_ref):
        tc = jnp.minimum(t, jl)
        b0 = pl.multiple_of(jnp.where(t < n1, tc * nbt, nblk), 8)
        qs_ref[dh:dh + nbt, :] = selb_ref[pl.ds(b0, nbt), :]
        k0 = pl.multiple_of(tc * tk, tk)
        s_ref[...] = jnp.dot(ksel_ref[pl.ds(k0, tk), :], qs_ref[...].astype(BF16), preferred_element_type=F32)

    def soft(t, s_ref, p_ref, m, l):
        cj = slope * (jnp.minimum(t, jl) * tk - tpos).astype(F32)
        sj = s_ref[...]
        m_new = jnp.maximum(m, jnp.max(sj, axis=0, keepdims=True) + cj)
        pj = jnp.exp(sj - (m_new - cj))
        alpha = jnp.exp(m - m_new)
        p_ref[...] = pj.astype(BF16)
        return m_new, alpha * l + jnp.sum(pj, axis=0, keepdims=True), alpha

    def pv(t, p_ref):
        return jnp.dot(vselT_ref[jnp.clip(t, 0, jl)], p_ref[...], preferred_element_type=F32)

    def pair(i, carry):
        m, l, accp = carry
        ta = 2 * i
        pvb = pv(ta - 1, pb_ref)
        qk_into(ta + 1, sb_ref)
        m, l, alpha = soft(ta, sa_ref, pa_ref, m, l)
        accp = alpha * (accp + pvb)
        pva = pv(ta, pa_ref)
        qk_into(ta + 2, sa_ref)
        m, l, alpha = soft(ta + 1, sb_ref, pb_ref, m, l)
        accp = alpha * (accp + pva)
        return m, l, accp

    pb_ref[...] = jnp.zeros(pb_ref.shape, BF16)
    qk_into(0, sa_ref)
    init = (jnp.full((1, ncol), NEG_INF, F32), jnp.zeros((1, ncol), F32), jnp.zeros((dh, ncol), F32))
    m_s, l_s, accp = lax.fori_loop(0, npairs, pair, init)
    acc_s = accp + pv(2 * npairs - 1, pb_ref)

    m_f = jnp.maximum(m_s, m_d)
    a_s = jnp.exp(m_s - m_f)
    a_d = jnp.exp(m_d - m_f)
    osT = (a_s * acc_s + a_d * acc_d) / (a_s * l_s + a_d * l_d)

    sg = jax.nn.sigmoid(gT_ref[...])

    def gate(c):
        return jnp.concatenate([sg[c * NSA_GROUP + g:c * NSA_GROUP + g + 1, :] for g in range(NSA_GROUP)], axis=1)

    oT = gate(0) * ocT + gate(1) * osT + gate(2) * owT
    o4 = jnp.concatenate([oT[:, g * qb:(g + 1) * qb] for g in range(NSA_GROUP)], axis=0)
    out_ref[...] = o4.T


def _alibi_slopes():
    h = jnp.arange(NSA_HEADS, dtype=F32)
    return (2.0 ** (-8.0 * (h + 1.0) / NSA_HEADS)).reshape(NSA_KV_HEADS, NSA_GROUP)


def _nsa_prompt(qT, gT, kvc, kv_sel, kv_win):
    s_len = qT.shape[1]
    dh, qb, tk, kvh = NSA_HEAD_DIM, NSA_QBLOCK, NSA_KTILE, NSA_KV_HEADS
    assert s_len % tk == 0
    nblk = s_len // NSA_BLOCK
    slopes = _alibi_slopes()
    slope_cols = jnp.repeat(slopes, qb, axis=1).reshape(kvh, 1, NSA_COLS)

    def heads_major(a):
        return a.reshape(a.shape[0], kvh, dh).transpose(1, 0, 2)

    def pad_lanes(a):
        return jnp.pad(a, ((0, 0), (0, 0), (0, NSA_CDIM - a.shape[-1])))

    kc = pad_lanes(heads_major(kvc[:, :NSA_KV_WIDTH])).astype(BF16)
    vcT = heads_major(kvc[:, NSA_KV_WIDTH:]).transpose(0, 2, 1).astype(BF16)
    r = jnp.arange(tk)
    onehot = (r[:, None] // NSA_BLOCK == jnp.arange(tk // NSA_BLOCK)[None, :]).astype(F32)
    lo = (r % (tk // 2)).astype(F32)[:, None]
    hi = (r // (tk // 2)).astype(F32)[:, None]
    kext = jnp.concatenate([onehot, lo, lo, lo, hi, hi, hi], axis=1)
    kext = jnp.tile(kext, (s_len // tk, 1))
    ksel = pad_lanes(jnp.concatenate(
        [heads_major(kv_sel[:, :NSA_KV_WIDTH]), jnp.broadcast_to(kext[None], (kvh,) + kext.shape)], axis=-1)).astype(BF16)
    vsel = heads_major(kv_sel[:, NSA_KV_WIDTH:]).astype(BF16)
    vselT = vsel.reshape(kvh, s_len // tk, tk, dh).transpose(0, 1, 3, 2)
    vselT_flat = vsel.transpose(0, 2, 1)
    kwin = heads_major(kv_win[:, :NSA_KV_WIDTH])
    kwin = jnp.concatenate([kwin, jnp.zeros((kvh, s_len, 1), F32)], axis=-1)
    padk = jnp.zeros((kvh, NSA_WINDOW, dh + 1), F32).at[:, :, dh].set(1.0)
    kwin = pad_lanes(jnp.concatenate([padk, kwin], axis=1)).astype(BF16)
    vwinT = jnp.pad(heads_major(kv_win[:, NSA_KV_WIDTH:]), ((0, 0), (NSA_WINDOW, 0), (0, 0))).transpose(0, 2, 1).astype(BF16)
    rr = jnp.arange(NSA_WKEYS)[:, None]
    cc = jnp.arange(NSA_COLS)[None, :]
    dw = (cc % qb) + NSA_WINDOW - rr
    bw = jnp.where((dw >= 0) & (dw <= NSA_WINDOW), -slope_cols * dw.astype(F32)[None], NEG_INF)

    nq = s_len // qb
    nwb = NSA_WKEYS // qb
    kw_specs = [pl.BlockSpec((None, qb, NSA_CDIM), functools.partial(lambda k, i, j: (k, i + j, 0), j=j)) for j in range(nwb)]
    vw_specs = [pl.BlockSpec((None, dh, qb), functools.partial(lambda k, i, j: (k, 0, i + j), j=j)) for j in range(nwb)]
    assert nwb == 5
    return pl.pallas_call(
        functools.partial(_nsa_prompt_kernel, nblk=nblk),
        grid=(kvh, nq),
        in_specs=[pl.BlockSpec((None, 1, NSA_COLS), lambda k, i: (k, 0, 0)),
                  pl.BlockSpec((NSA_GROUP * dh, qb), lambda k, i: (k, i)),
                  pl.BlockSpec((None, 16, qb), lambda k, i: (k, 0, i)),
                  pl.BlockSpec((None, nblk, NSA_CDIM), lambda k, i: (k, 0, 0)),
                  pl.BlockSpec((None, dh, nblk), lambda k, i: (k, 0, 0)),
                  pl.BlockSpec((None, s_len, NSA_CDIM), lambda k, i: (k, 0, 0)),
                  pl.BlockSpec((None, s_len // tk, dh, tk), lambda k, i: (k, 0, 0, 0)),
                  pl.BlockSpec((None, qb, NSA_CDIM), lambda k, i: (k, i, 0)),
                  pl.BlockSpec((None, dh, qb), lambda k, i: (k, 0, i))]
                 + kw_specs + vw_specs
                 + [pl.BlockSpec((None, NSA_WKEYS, NSA_COLS), lambda k, i: (k, 0, 0))],
        out_specs=pl.BlockSpec((qb, NSA_GROUP * dh), lambda k, i: (i, k)),
        out_shape=jax.ShapeDtypeStruct((s_len, NSA_Q_WIDTH), F32),
        scratch_shapes=[pltpu.VMEM((NSA_CDIM, NSA_COLS), F32),
                        pltpu.VMEM((nblk + tk // NSA_BLOCK, NSA_COLS), F32),
                        pltpu.VMEM((tk, NSA_COLS), F32), pltpu.VMEM((tk, NSA_COLS), F32),
                        pltpu.VMEM((tk, NSA_COLS), BF16), pltpu.VMEM((tk, NSA_COLS), BF16)],
        compiler_params=_cparams(("arbitrary", "arbitrary")), name="nsa_prompt",
    )(slope_cols, qT, gT.reshape(kvh, 16, s_len), kc, vcT, ksel, vselT, ksel, vselT_flat,
      *([kwin] * nwb), *([vwinT] * nwb), bw)


def _hgrn_chunk(qc, zf, vc, lb, st):
    c, sub = HG_CHUNK, HG_SUB
    logf = jnp.log(lb + (1.0 - lb) * jax.nn.sigmoid(zf))
    kc = (1.0 - lb) * jax.nn.sigmoid(-zf)
    tri = (lax.broadcasted_iota(jnp.int32, (c, c), 0) >= lax.broadcasted_iota(jnp.int32, (c, c), 1)).astype(F32)
    cb = jnp.dot(tri, logf, preferred_element_type=F32, precision=lax.Precision.HIGHEST)
    o = lax.dot_general((qc * jnp.exp(cb)).astype(BF16), st.astype(BF16), (((1,), (1,)), ((), ())),
                        preferred_element_type=F32)
    t3 = lax.broadcasted_iota(jnp.int32, (sub, sub, 1), 0) >= lax.broadcasted_iota(jnp.int32, (sub, sub, 1), 1)
    outs = []
    for a in range(c // sub):
        ra = slice(a * sub, (a + 1) * sub)
        cba, qa, ka, va = cb[ra], qc[ra], kc[ra], vc[ra]
        d3 = cba[:, None, :] - cba[None, :, :]
        x3 = jnp.where(t3, jnp.exp(d3), 0.0) * qa[:, None, :] * ka[None, :, :]
        att3 = jnp.sum(x3, axis=2, keepdims=True)
        oa = o[ra] + jnp.sum(att3 * va[None, :, :], axis=1)
        if a > 0:
            ref = cb[a * sub - 1:a * sub, :]
            qd = (qa * jnp.exp(cba - ref)).astype(BF16)
            kd = (kc[:a * sub] * jnp.exp(ref - cb[:a * sub])).astype(BF16)
            att = lax.dot_general(qd, kd, (((1,), (1,)), ((), ())), preferred_element_type=F32)
            oa = oa + jnp.dot(att.astype(BF16), vc[:a * sub].astype(BF16), preferred_element_type=F32)
        outs.append(oa)
    o = jnp.concatenate(outs, axis=0)
    last = cb[c - 1:c, :]
    kdec = (kc * jnp.exp(last - cb)).astype(BF16)
    st = st * jnp.exp(last) + lax.dot_general(vc.astype(BF16), kdec, (((0,), (0,)), ((), ())),
                                              preferred_element_type=F32)
    return o, st


def _hgrn_kernel(lb_ref, wn_ref, q_ref, f_ref, v_ref, g_ref, s0_ref, o_ref, sfin_ref, st_ref, *, nsub):
    c = pl.program_id(1)
    hpb = HG_HPB

    @pl.when(c == 0)
    def _():
        for j in range(hpb):
            st_ref[j] = s0_ref[j].T

    wn = wn_ref[...]

    def body(u, sts):
        rows = pl.ds(pl.multiple_of(u * HG_CHUNK, HG_CHUNK), HG_CHUNK)
        out = []
        for j in range(hpb):
            cols = slice(j * HG_DK, (j + 1) * HG_DK)
            o, st = _hgrn_chunk(q_ref[rows, cols], f_ref[rows, cols], v_ref[rows, cols], lb_ref[:, cols], sts[j])
            g = g_ref[rows, cols]
            o_ref[rows, cols] = _rms(o, wn) * (g * jax.nn.sigmoid(g))
            out.append(st)
        return tuple(out)

    sts = lax.fori_loop(0, nsub, body, tuple(st_ref[j] for j in range(hpb)))
    for j in range(hpb):
        st_ref[j] = sts[j]

    @pl.when(c == pl.num_programs(1) - 1)
    def _():
        for j in range(hpb):
            sfin_ref[j] = sts[j].T


def _hgrn_prompt(hq, hf, hi, hg, lb, wn, s0, tb):
    t = hq.shape[0]
    hpb = HG_HPB
    assert t % tb == 0 and tb % HG_CHUNK == 0 and HG_HEADS % hpb == 0 and HG_DK == HG_DV
    tok = pl.BlockSpec((tb, hpb * HG_DK), lambda h, c: (c, h))
    stt = pl.BlockSpec((hpb, HG_DK, HG_DV), lambda h, c: (h, 0, 0))
    return pl.pallas_call(
        functools.partial(_hgrn_kernel, nsub=tb // HG_CHUNK),
        grid=(HG_HEADS // hpb, t // tb),
        in_specs=[pl.BlockSpec((1, hpb * HG_DK), lambda h, c: (0, h)),
                  pl.BlockSpec((1, HG_DV), lambda h, c: (0, 0)),
                  tok, tok, tok, tok, stt],
        out_specs=[tok, stt],
        out_shape=[jax.ShapeDtypeStruct((t, HG_HEADS * HG_DV), F32),
                   jax.ShapeDtypeStruct((HG_HEADS, HG_DK, HG_DV), F32)],
        scratch_shapes=[pltpu.VMEM((hpb, HG_DV, HG_DK), F32)],
        compiler_params=_cparams(("arbitrary", "arbitrary")), name="hgrn_prompt",
    )(lb, wn, hq, hf, hi, hg, s0)


def _hgrn_step_kernel(lb_ref, wn_ref, q_ref, f_ref, v_ref, g_ref, s0_ref, o_ref, s_ref):
    wn = wn_ref[...]
    for h in range(HG_HEADS):
        rk = slice(h * HG_DK, (h + 1) * HG_DK)
        lb = lb_ref[rk, :]
        zf = f_ref[rk, :]
        f = lb + (1.0 - lb) * jax.nn.sigmoid(zf)
        kk = (1.0 - lb) * jax.nn.sigmoid(-zf)
        vrow = v_ref[:, h * HG_DV:(h + 1) * HG_DV]
        s_new = f * s0_ref[h] + kk * vrow
        s_ref[h] = s_new
        o = jnp.sum(s_new * q_ref[rk, :], axis=0, keepdims=True)
        g = g_ref[:, h * HG_DV:(h + 1) * HG_DV]
        o_ref[:, h * HG_DV:(h + 1) * HG_DV] = _rms(o, wn) * (g * jax.nn.sigmoid(g))


def _hgrn_step(hq_col, hf_col, hi, hg, lb_col, wn, s0):
    b = hi.shape[0]
    col = pl.BlockSpec((None, HG_WIDTH, 1), lambda i: (i, 0, 0))
    row = pl.BlockSpec((None, 1, HG_WIDTH), lambda i: (i, 0, 0))
    st = pl.BlockSpec((None, HG_HEADS, HG_DK, HG_DV), lambda i: (i, 0, 0, 0))
    return pl.pallas_call(
        _hgrn_step_kernel, grid=(b,),
        in_specs=[pl.BlockSpec((HG_WIDTH, 1), lambda i: (0, 0)), pl.BlockSpec((1, HG_DV), lambda i: (0, 0)),
                  col, col, row, row, st],
        out_specs=[row, st],
        out_shape=[jax.ShapeDtypeStruct((b, 1, HG_WIDTH), F32), jax.ShapeDtypeStruct(s0.shape, F32)],
        compiler_params=_cparams(("parallel",)), name="hgrn_step",
    )(lb_col, wn, hq_col, hf_col, hi, hg, s0)


def _merge_kernel(x_ref, oa_ref, ob_ref, ma_ref, mb_ref, pa_ref, pb_ref, wo_ref, y_ref):
    ya = jnp.dot(oa_ref[...].astype(BF16), pa_ref[...], preferred_element_type=F32)
    yb = jnp.dot(ob_ref[...].astype(BF16), pb_ref[...], preferred_element_type=F32)
    mix = jax.nn.sigmoid(ma_ref[...]) * ya + jax.nn.sigmoid(mb_ref[...]) * yb
    y_ref[...] = x_ref[...] + jnp.dot(mix.astype(BF16), wo_ref[...], preferred_element_type=F32)


def _merge(x, oa, ob, ma, mb, pa, pb, wo, tm):
    t, d = x.shape
    tok = pl.BlockSpec((tm, d), lambda i: (i, 0))
    wsp = pl.BlockSpec((d, d), lambda i: (0, 0))
    return pl.pallas_call(
        _merge_kernel, grid=(t // tm,),
        in_specs=[tok, tok, tok, tok, tok, wsp, wsp, wsp],
        out_specs=tok, out_shape=jax.ShapeDtypeStruct((t, d), F32),
        compiler_params=_cparams(("parallel",)), name="branch_merge",
    )(x, oa, ob, ma, mb, pa, pb, wo)


def _topk_rows(s, k):
    n = s.shape[0]
    rows = lax.broadcasted_iota(jnp.int32, s.shape, 0).astype(F32)
    rank = jnp.full(s.shape, float(n), F32)
    tops = []
    for r in range(k):
        mx = jnp.max(s, axis=0, keepdims=True)
        idx = jnp.min(jnp.where(s == mx, rows, float(n)), axis=0, keepdims=True)
        hit = rows == idx
        rank = jnp.where(hit, float(r), rank)
        s = jnp.where(hit, LOWEST, s)
        tops.append(mx)
    return jnp.concatenate(tops, axis=0), rank


_STAIR_GROUPS = ((0, 16), (1, 8), (2, 8), (3, 8))
_STAIR_QUAD = (4, 5, 6, 7)
_STAIR_TAIL = 8


def _stair_rows(t1, t2, op):
    tt = t1.shape[1]
    parts = [op(jnp.broadcast_to(t1[a:a + 1, :], (nb, tt)), t2[0:nb, :]) for a, nb in _STAIR_GROUPS]
    r16 = lax.broadcasted_iota(jnp.int32, (16, tt), 0)
    v1 = jnp.broadcast_to(t1[_STAIR_QUAD[3]:_STAIR_QUAD[3] + 1, :], (16, tt))
    v2 = jnp.broadcast_to(t2[3:4, :], (16, tt))
    for q in (2, 1, 0):
        v1 = jnp.where(r16 < 4 * (q + 1), jnp.broadcast_to(t1[_STAIR_QUAD[q]:_STAIR_QUAD[q] + 1, :], (16, tt)), v1)
        v2 = jnp.where((r16 & 3) == q, jnp.broadcast_to(t2[q:q + 1, :], (16, tt)), v2)
    parts.append(op(v1, v2))
    parts.append(op(t1[_STAIR_TAIL:, :], jnp.broadcast_to(t2[0:1, :], (PK_TOPK - _STAIR_TAIL, tt))))
    return jnp.concatenate(parts, axis=0)


def _stair_row_counts(selc):
    out, r0 = [], 0
    for _, nb in _STAIR_GROUPS:
        out.append(jnp.sum(selc[r0:r0 + nb, :], axis=0, keepdims=True))
        r0 += nb
    quad = selc[r0:r0 + 16, :]
    r16 = lax.broadcasted_iota(jnp.int32, quad.shape, 0)
    for q in range(4):
        out.append(jnp.sum(jnp.where((r16 >> 2) == q, quad, 0.0), axis=0, keepdims=True))
    r0 += 16
    for a in range(PK_TOPK - _STAIR_TAIL):
        out.append(selc[r0 + a:r0 + a + 1, :])
    return out


def _peer_kernel(x_ref, wn_ref, wq_ref, sk_ref, u_ref, vT_ref, wf_ref, y_ref,
                 hnT_ref, acc_ref, wcat_ref, wodd_ref, n_ref, a1_ref, r2_ref, e2_ref, *, ib, nkeys, final_norm):
    i = pl.program_id(1)
    kt = PK_TOPK
    hd = PK_DIM // 2

    @pl.when(i == 0)
    def _():
        hnT = _rms(x_ref[...], wn_ref[...]).T.astype(BF16)
        hnT_ref[...] = hnT
        acc_ref[...] = jnp.zeros(acc_ref.shape, F32)
        wcat_ref[...] = jnp.zeros(wcat_ref.shape, BF16)
        wodd_ref[...] = jnp.zeros(wodd_ref.shape, BF16)
        tt = hnT.shape[1]
        for h in range(PK_HEADS):
            ss, tops, ranks = [], [], []
            for c in range(2):
                r0 = (h * 2 + c) * hd
                qhc = jnp.dot(wq_ref[r0:r0 + hd, :], hnT, preferred_element_type=F32)
                s = jnp.dot(sk_ref[h * 2 + c], qhc.astype(BF16), preferred_element_type=F32)
                top, rank = _topk_rows(s, kt)
                ss.append(s)
                tops.append(top)
                ranks.append(rank)
            cand = _stair_rows(tops[0], tops[1], jnp.add)
            _, crank = _topk_rows(cand, kt)
            selc = (crank < float(kt)).astype(F32)
            n_a = _stair_row_counts(selc)
            e1t = jnp.exp(tops[0] - tops[0][0:1, :])
            e2t = jnp.exp(tops[1] - tops[1][0:1, :])
            z = jnp.sum(selc * _stair_rows(e1t, e2t, jnp.multiply), axis=0, keepdims=True)
            nfull = jnp.zeros((nkeys, tt), F32)
            for a in range(kt):
                nfull = jnp.where(ranks[0] == float(a), n_a[a], nfull)
            n_ref[h] = nfull
            a1_ref[h] = jnp.exp(ss[0] - tops[0][0:1, :]) / z
            r2_ref[h] = ranks[1].astype(BF16)
            e2_ref[h] = jnp.exp(ss[1] - tops[1][0:1, :]).astype(BF16)

    nsteps = pl.num_programs(1) - 1
    nch = 2
    cw = hnT_ref.shape[1] // nch

    def step(w_read, w_write):
        hnT = hnT_ref[...]
        for ii in range(ib):
            if ii % (ib // nch) == 1:
                cols = slice((ii // (ib // nch)) * cw, (ii // (ib // nch) + 1) * cw)
                acc_ref[:, cols] += jnp.dot(vT_ref[...], w_read[:, cols], preferred_element_type=F32)
            ig = i * ib + ii
            aT = jnp.dot(u_ref[ii * nkeys:(ii + 1) * nkeys, :], hnT, preferred_element_type=F32)
            gsum = jnp.zeros(aT.shape, BF16)
            for h in range(PK_HEADS):
                nrow = n_ref[h, pl.ds(ig, 1), :].astype(BF16)
                arow = a1_ref[h, pl.ds(ig, 1), :].astype(BF16)
                gsum = gsum + jnp.where(r2_ref[h] < nrow, arow * e2_ref[h], jnp.zeros((), BF16))
            w_write[ii * nkeys:(ii + 1) * nkeys, :] = (jax.nn.gelu(aT) * gsum.astype(F32)).astype(BF16)

    @pl.when((i < nsteps) & (i % 2 == 0))
    def _():
        step(wodd_ref, wcat_ref)

    @pl.when((i < nsteps) & (i % 2 == 1))
    def _():
        step(wcat_ref, wodd_ref)

    @pl.when(i == nsteps)
    def _():
        w_last = wodd_ref if (nkeys // ib) % 2 == 0 else wcat_ref
        acc = acc_ref[...] + jnp.dot(vT_ref[...], w_last[...], preferred_element_type=F32)
        y = x_ref[...] + acc.T
        if final_norm:
            y = _rms(y, wf_ref[...])
        y_ref[...] = y


def _peer(x, wn, wqT, sk, u, vT, wf, tt, ib, final_norm):
    t, d = x.shape
    nkeys = sk.shape[1]
    assert t % tt == 0 and nkeys % ib == 0
    nsteps = nkeys // ib
    stat = pltpu.VMEM((PK_HEADS, nkeys, tt), F32)
    stat16 = pltpu.VMEM((PK_HEADS, nkeys, tt), BF16)
    wbuf = pltpu.VMEM((ib * nkeys, tt), BF16)
    return pl.pallas_call(
        functools.partial(_peer_kernel, ib=ib, nkeys=nkeys, final_norm=final_norm),
        grid=(t // tt, nsteps + 1),
        in_specs=[pl.BlockSpec((tt, d), lambda a, i: (a, 0)),
                  pl.BlockSpec((1, d), lambda a, i: (0, 0)),
                  pl.BlockSpec(wqT.shape, lambda a, i: (0, 0)),
                  pl.BlockSpec(sk.shape, lambda a, i: (0, 0, 0)),
                  pl.BlockSpec((ib * nkeys, d), lambda a, i: (jnp.minimum(i, nsteps - 1), 0)),
                  pl.BlockSpec((None, d, ib * nkeys), lambda a, i: (jnp.maximum(i - 1, 0), 0, 0)),
                  pl.BlockSpec((1, d), lambda a, i: (0, 0))],
        out_specs=pl.BlockSpec((tt, d), lambda a, i: (a, 0)),
        out_shape=jax.ShapeDtypeStruct((t, d), F32),
        scratch_shapes=[pltpu.VMEM((d, tt), BF16), pltpu.VMEM((d, tt), F32), wbuf, wbuf,
                        stat, stat, stat16, stat16],
        compiler_params=_cparams(("arbitrary", "arbitrary")), name="peer_dense",
    )(x, wn, wqT, sk, u, vT, wf)


def _compress_pages_kernel(pt_ref, *refs, pg, bpp):
    w_ref, o_ref = refs[pg], refs[pg + 1]
    nt = (((1,), (1,)), ((), ()))
    for j in range(pg):
        x = refs[j][...]
        halves = [lax.dot_general(w_ref[c], x[c].reshape(-1, x.shape[-1]).astype(BF16), nt, preferred_element_type=F32)
                  for c in range(x.shape[0])]
        o_ref[j * bpp:(j + 1) * bpp, :] = jnp.concatenate(halves, axis=1)


def _compress_pages(cache_t, page_table, w_sel, pg):
    tail = cache_t.shape[1:]
    db, npages = page_table.shape
    bpp = w_sel.shape[1]
    width = tail[0] * tail[1] * tail[2]
    assert npages % pg == 0
    zeros = (0,) * len(tail)
    page_specs = [pl.BlockSpec((None,) + tail, functools.partial(lambda b, g, pt, j: (pt[b, g * pg + j],) + zeros, j=j))
                  for j in range(pg)]
    return pl.pallas_call(
        functools.partial(_compress_pages_kernel, pg=pg, bpp=bpp),
        grid_spec=pltpu.PrefetchScalarGridSpec(
            num_scalar_prefetch=1, grid=(db, npages // pg),
            in_specs=page_specs + [pl.BlockSpec(w_sel.shape, lambda b, g, pt: (0, 0, 0))],
            out_specs=pl.BlockSpec((None, pg * bpp, width), lambda b, g, pt: (b, g, 0))),
        out_shape=jax.ShapeDtypeStruct((db, npages * bpp, width), F32),
        compiler_params=_cparams(("arbitrary", "arbitrary")), name="nsa_compress_pages",
    )(page_table, *([cache_t] * pg), w_sel)


def _nsa_decode_head_kernel(slope_ref, qbd_ref, g_ref, kvc_ref, wnew_ref, cwin_ref, part_ref, idx_ref,
                            *, past_len, k_past):
    kvw, grp, dh = NSA_KV_WIDTH, NSA_GROUP, NSA_HEAD_DIM
    qbd = qbd_ref[...]
    qb = qbd.astype(BF16)
    slope = slope_ref[...]
    nh = qbd.shape[0]
    npb = kvc_ref.shape[0]
    nt = (((1,), (1,)), ((), ()))
    kvc = kvc_ref[...]
    sc = lax.dot_general(qb, kvc[:, :kvw].astype(BF16), nt, preferred_element_type=F32)
    c_end = lax.broadcasted_iota(jnp.int32, (1, npb), 1) * NSA_BLOCK + (NSA_BLOCK - 1)
    s = sc - slope * (past_len - c_end).astype(F32)
    e = jnp.exp(s - jnp.max(s, axis=1, keepdims=True))
    p = e / jnp.sum(e, axis=1, keepdims=True)
    oc = jnp.dot(p.astype(BF16), kvc[:, kvw:].astype(BF16), preferred_element_type=F32)
    rows = []
    for k in range(NSA_KV_HEADS):
        r = p[k * grp:k * grp + 1, :]
        for g in range(1, grp):
            r = r + p[k * grp + g:k * grp + g + 1, :]
        rows.append(jnp.broadcast_to(r, (grp, npb)))
    imp = jnp.concatenate(rows, axis=0)
    blk = lax.broadcasted_iota(jnp.int32, (nh, npb), 1)
    score = jnp.where((blk == 0) | (blk == npb - 1), grp + 1.0, imp)
    lanes = blk.astype(F32)
    picks = []
    for _ in range(k_past):
        mx = jnp.max(score, axis=1, keepdims=True)
        idx = jnp.min(jnp.where(score == mx, lanes, float(npb)), axis=1, keepdims=True)
        score = jnp.where(lanes == idx, LOWEST, score)
        picks.append(idx)
    idx_ref[...] = jnp.concatenate(picks, axis=1).astype(jnp.int32)
    sg = jax.nn.sigmoid(g_ref[...])
    wnew = wnew_ref[...]
    nw = cwin_ref.shape[-1]
    dw = (nw - lax.broadcasted_iota(jnp.int32, (1, nw), 1)).astype(F32)
    for k in range(NSA_KV_HEADS):
        hs, ds = slice(k * grp, (k + 1) * grp), slice(k * dh, (k + 1) * dh)
        qk = qbd[hs, ds]
        sw = jnp.dot(qk.astype(BF16), cwin_ref[0, k].astype(BF16), preferred_element_type=F32) - slope[hs] * dw
        s_n = jnp.sum(qk * wnew[:, ds], axis=1, keepdims=True)
        m = jnp.maximum(jnp.max(sw, axis=1, keepdims=True), s_n)
        ew = jnp.exp(sw - m)
        en = jnp.exp(s_n - m)
        pv = lax.dot_general(ew.astype(BF16), cwin_ref[1, k].astype(BF16), nt, preferred_element_type=F32)
        ow = (pv + en * wnew[:, kvw + k * dh:kvw + (k + 1) * dh]) / (jnp.sum(ew, axis=1, keepdims=True) + en)
        part_ref[hs, :] = sg[hs, 0:1] * oc[hs, ds] + sg[hs, 2:3] * ow


def _nsa_decode_gather_kernel(pt_ref, ix_ref, slope_ref, q_ref, g_ref, part_ref, new_ref, *refs, nsel, bpp, past_len):
    blocks, o_ref = refs[:nsel], refs[nsel]
    b, k = pl.program_id(0), pl.program_id(1)
    page = blocks[0].shape[-1]
    q = q_ref[...]
    slope = slope_ref[...]
    kt = jnp.concatenate([blocks[s][0] for s in range(nsel)], axis=1).astype(BF16)
    vt = jnp.concatenate([blocks[s][1] for s in range(nsel)], axis=1).astype(BF16)
    sc = jnp.dot(q.astype(BF16), kt, preferred_element_type=F32)
    lane = lax.broadcasted_iota(jnp.int32, (1, page), 1)
    lblk = lane // NSA_BLOCK
    lkey = lane - lblk * NSA_BLOCK
    bias = []
    for s in range(nsel):
        ib = ix_ref[b, k, s]
        kpos = ib * NSA_BLOCK + lkey
        bias.append(jnp.where(lblk == ib % bpp, -slope * (past_len - kpos).astype(F32), NEG_INF))
    sc = sc + jnp.concatenate(bias, axis=1)
    k_own, v_own = new_ref[0], new_ref[1]
    s_n = jnp.sum(q * k_own, axis=1, keepdims=True)
    m = jnp.maximum(jnp.max(sc, axis=1, keepdims=True), s_n)
    p = jnp.exp(sc - m)
    p_n = jnp.exp(s_n - m)
    pv = lax.dot_general(p.astype(BF16), vt, (((1,), (1,)), ((), ())), preferred_element_type=F32)
    o_s = (pv + p_n * v_own) / (jnp.sum(p, axis=1, keepdims=True) + p_n)
    o_ref[...] = part_ref[...] + jax.nn.sigmoid(g_ref[...])[:, 1:2] * o_s


def _rows_minor(a):
    return a.transpose(0, 2, 3, 4, 1)


def _nsa_sample(q, kv_cmp, kv_sel, kv_win, glog, cache_cmp, cache_sel, cache_win, page_table, w_cmp):
    db = q.shape[0]
    kvh, grp, dh, kvw = NSA_KV_HEADS, NSA_GROUP, NSA_HEAD_DIM, NSA_KV_WIDTH
    page = cache_cmp.shape[1]
    npages = page_table.shape[1]
    past_len = npages * page
    nwin = cache_win.shape[1]
    assert past_len >= nwin and page % NSA_BLOCK == 0
    npb = past_len // NSA_BLOCK
    bpp = page // NSA_BLOCK
    pg = 8 if npages % 8 == 0 else 1
    width = 2 * kvw
    rowblk = jnp.arange(page) // NSA_BLOCK
    w_sel = jnp.where(rowblk[None, None, :] == jnp.arange(bpp)[None, :, None],
                      jnp.tile(w_cmp, (1, bpp))[:, None, :], 0.0).astype(BF16)
    kvc = _compress_pages(_rows_minor(cache_cmp), page_table, w_sel, pg)
    q4 = q.reshape(db, kvh, grp, dh) * (dh ** -0.5)
    qbd = (q4[:, :, :, None, :] * jnp.eye(kvh, dtype=F32)[None, :, None, :, None]).reshape(db, kvh * grp, kvw)
    slope = _alibi_slopes().reshape(kvh * grp, 1)
    nh = kvh * grp
    k_past = min(NSA_TOPK, npb + 1) - 1
    full2 = lambda shape: pl.BlockSpec(shape, lambda b: (0,) * len(shape))
    per_b = lambda shape: pl.BlockSpec((None,) + shape, lambda b: (b,) + (0,) * len(shape))
    part, idx = pl.pallas_call(
        functools.partial(_nsa_decode_head_kernel, past_len=past_len, k_past=k_past),
        grid=(db,),
        in_specs=[full2((nh, 1)), per_b((nh, kvw)), per_b((nh, 3)), per_b((npb, width)), per_b((1, width)),
                  per_b((2, kvh, dh, nwin))],
        out_specs=[per_b((nh, dh)), per_b((nh, k_past))],
        out_shape=[jax.ShapeDtypeStruct((db, nh, dh), F32), jax.ShapeDtypeStruct((db, nh, k_past), jnp.int32)],
        compiler_params=_cparams(("parallel",)), name="nsa_decode_head",
    )(slope, qbd, glog, kvc, kv_win.reshape(db, 1, width), _rows_minor(cache_win))
    idx4 = idx.reshape(db, kvh, grp, k_past)[:, :, 0, :]
    blk_specs = [pl.BlockSpec((None, 2, None, dh, page),
                              functools.partial(lambda b, k, pt, ix, s: (pt[b, ix[b, k, s] // bpp], 0, k, 0, 0), s=s))
                 for s in range(k_past)]
    hsp = lambda shape: pl.BlockSpec((None, None) + shape, lambda b, k, pt, ix: (b, k) + (0,) * len(shape))
    o = pl.pallas_call(
        functools.partial(_nsa_decode_gather_kernel, nsel=k_past, bpp=bpp, past_len=past_len),
        grid_spec=pltpu.PrefetchScalarGridSpec(
            num_scalar_prefetch=2, grid=(db, kvh),
            in_specs=[pl.BlockSpec((None, grp, 1), lambda b, k, pt, ix: (k, 0, 0)), hsp((grp, dh)), hsp((grp, 3)),
                      hsp((grp, dh)),
                      pl.BlockSpec((None, 2, None, 1, dh), lambda b, k, pt, ix: (b, 0, k, 0, 0))] + blk_specs,
            out_specs=hsp((grp, dh))),
        out_shape=jax.ShapeDtypeStruct((db, kvh, grp, dh), F32),
        compiler_params=_cparams(("arbitrary", "arbitrary")), name="nsa_decode_gather",
    )(page_table, idx4, slope.reshape(kvh, grp, 1), q4, glog.reshape(db, kvh, grp, 3), part.reshape(db, kvh, grp, dh),
      kv_sel.reshape(db, 2, kvh, 1, dh), *([_rows_minor(cache_sel)] * k_past))
    new_win = jnp.concatenate([cache_win, kv_win.reshape((db, 1) + cache_win.shape[2:])], axis=1)[:, 1:]
    return o.reshape(db, nh * dh), new_win


def _split_w_in(w_in):
    pts, acc = [], 0
    for w in (NSA_Q_WIDTH, 6 * NSA_KV_WIDTH, 3 * NSA_HEADS, HG_WIDTH, HG_WIDTH, HG_WIDTH, HG_WIDTH, w_in.shape[0]):
        acc += w
        pts.append(acc)
    return jnp.split(w_in, pts, axis=1)


def _prep_layer(w_in, w_proj_nsa, w_proj_hgrn, w_out, w_peer_q, peer_sub_keys, peer_u, peer_v):
    d = w_in.shape[0]
    wq, wkv, wg, wbq, wbf, wbi, wbg, wma, wmb = _split_w_in(w_in)
    wg = wg.reshape(d, NSA_KV_HEADS, NSA_GROUP, 3).transpose(0, 1, 3, 2).reshape(d, NSA_KV_HEADS, 3 * NSA_GROUP)
    wg = jnp.pad(wg, ((0, 0), (0, 0), (0, 16 - 3 * NSA_GROUP))).reshape(d, NSA_KV_HEADS * 16)
    wg = jnp.pad(wg, ((0, 0), (0, LANES - NSA_KV_HEADS * 16)))
    nkeys = peer_sub_keys.shape[2]
    return dict(
        w_qg=jnp.concatenate([wq, wg], axis=1).astype(BF16),
        w_kv=wkv.astype(BF16),
        w_hg=jnp.concatenate([wbq, wbf, wbi, wbg], axis=1).astype(BF16),
        w_m=jnp.concatenate([wma, wmb], axis=1).astype(BF16),
        pa=w_proj_nsa.astype(BF16), pb=w_proj_hgrn.astype(BF16), wo=w_out.astype(BF16),
        wqT=w_peer_q.T.astype(BF16),
        sk=peer_sub_keys.reshape(PK_HEADS * 2, nkeys, PK_DIM // 2).astype(BF16),
        u=peer_u.astype(BF16),
        vT=peer_v.astype(BF16).reshape(nkeys // PEER_IB, PEER_IB * nkeys, d).transpose(0, 2, 1),
    )


def _tile(t, pref):
    return pref if t % pref == 0 else t


def _peer_tokens(x, wn, wf, prm, final_norm):
    t = x.shape[0]
    tp = -(-t // LANES) * LANES
    xp = jnp.pad(x, ((0, tp - t), (0, 0)))
    tt = PEER_TT if tp % PEER_TT == 0 else LANES
    y = _peer(xp, wn, prm["wqT"], prm["sk"], prm["u"], prm["vT"], wf, tt, PEER_IB, final_norm)
    return y[:t]


def _layer_prompt(x, lb, prm, w_norm_mix, w_cmp, w_hgrn_norm, w_norm_ffn, w_norm_final, final_norm):
    s_len, d = x.shape
    tm = _tile(s_len, 256)
    kvw = NSA_KV_WIDTH
    kv_cmp, kv_sel, kv_win = _proj(x, w_norm_mix, prm["w_kv"], [2 * kvw] * 3, [False] * 3, tm)
    qT, gT = _proj(x, w_norm_mix, prm["w_qg"], [NSA_Q_WIDTH, LANES], [True, True], tm)
    hq, hf, hi, hg = _proj(x, w_norm_mix, prm["w_hg"], [HG_WIDTH] * 4, [False] * 4, tm)
    ma, mb = _proj(x, w_norm_mix, prm["w_m"], [d, d], [False, False], tm)
    wfull = jnp.concatenate([jnp.broadcast_to(w_cmp[0][:, None], (NSA_BLOCK, kvw)),
                             jnp.broadcast_to(w_cmp[1][:, None], (NSA_BLOCK, kvw))], axis=1)
    nblk = s_len // NSA_BLOCK
    kvc = _compress(kv_cmp, wfull, 8 if nblk % 8 == 0 else nblk)
    o_a = _nsa_prompt(qT, gT[:NSA_KV_HEADS * 16], kvc, kv_sel, kv_win)
    s0 = jnp.zeros((HG_HEADS, HG_DK, HG_DV), F32)
    o_b, s_fin = _hgrn_prompt(hq, hf, hi, hg, lb, w_hgrn_norm, s0, _tile(s_len, 512))
    x1 = _merge(x, o_a, o_b, ma, mb, prm["pa"], prm["pb"], prm["wo"], tm)
    x2 = _peer_tokens(x1, w_norm_ffn, w_norm_final, prm, final_norm)
    shp = (s_len, 2, NSA_KV_HEADS, NSA_HEAD_DIM)
    return x2, kv_cmp.reshape(shp), kv_sel.reshape(shp), kv_win.reshape(shp)[-NSA_WINDOW:], s_fin


def _layer_sample(x, lb, prm, w_norm_mix, w_cmp, w_hgrn_norm, w_norm_ffn, w_norm_final, final_norm,
                  cache_cmp, cache_sel, cache_win, s0, page_table):
    b, d = x.shape
    kvw = NSA_KV_WIDTH
    w_q = prm["w_qg"][:, :NSA_Q_WIDTH]
    (q,) = _proj(x, w_norm_mix, w_q, [NSA_Q_WIDTH], [False], b)
    kv_cmp, kv_sel, kv_win = _proj(x, w_norm_mix, prm["w_kv"], [2 * kvw] * 3, [False] * 3, b)
    (gTt,) = _proj(x, w_norm_mix, prm["w_qg"][:, NSA_Q_WIDTH:], [LANES], [False], b)
    hq, hf, hi, hg = _proj(x, w_norm_mix, prm["w_hg"], [HG_WIDTH] * 4, [False] * 4, b)
    ma, mb = _proj(x, w_norm_mix, prm["w_m"], [d, d], [False, False], b)
    shp = (b, 1, 2, NSA_KV_HEADS, NSA_HEAD_DIM)
    glog = (gTt[:, :NSA_KV_HEADS * 16].reshape(b, NSA_KV_HEADS, 16)[..., :3 * NSA_GROUP]
            .reshape(b, NSA_KV_HEADS, 3, NSA_GROUP).transpose(0, 1, 3, 2).reshape(b, NSA_HEADS, 3))
    o_a, new_win = _nsa_sample(q, kv_cmp, kv_sel, kv_win, glog, cache_cmp, cache_sel, cache_win, page_table, w_cmp)
    o_b, s_new = _hgrn_step(hq.reshape(b, HG_WIDTH, 1), hf.reshape(b, HG_WIDTH, 1), hi.reshape(b, 1, HG_WIDTH),
                            hg.reshape(b, 1, HG_WIDTH), lb.reshape(HG_WIDTH, 1), w_hgrn_norm, s0)
    x1 = _merge(x, o_a.reshape(b, NSA_Q_WIDTH), o_b.reshape(b, HG_WIDTH), ma, mb, prm["pa"], prm["pb"], prm["wo"], b)
    x2 = _peer_tokens(x1, w_norm_ffn, w_norm_final, prm, final_norm)
    return x2, kv_cmp.reshape(shp), kv_sel.reshape(shp), new_win, s_new


def kernel(x_prompt, x_sample, cache_cmp_kv, cache_sel_kv, cache_win_kv, state_hgrn, page_table,
           w_norm_mix, w_in, w_cmp, w_proj_nsa, w_proj_hgrn, w_hgrn_norm, hgrn_lb_logits, w_out,
           w_norm_ffn, w_peer_q, peer_sub_keys, peer_u, peer_v, w_norm_final):
    depth = w_in.shape[0]
    bsz, s_len, d = x_prompt.shape
    db, dt, _ = x_sample.shape
    assert dt == 1
    lbs = jnp.cumsum(jax.nn.softmax(hgrn_lb_logits.astype(F32), axis=0), axis=0)
    wfin = w_norm_final.reshape(1, d)
    xp = [x_prompt[b] for b in range(bsz)]
    xs = x_sample.reshape(db, d)
    st_p, st_s = [], []
    for l in range(depth):
        last = l == depth - 1
        prm = _prep_layer(w_in[l], w_proj_nsa[l], w_proj_hgrn[l], w_out[l], w_peer_q[l], peer_sub_keys[l],
                          peer_u[l], peer_v[l])
        shared = (lbs[l].reshape(1, HG_WIDTH), prm, w_norm_mix[l].reshape(1, d), w_cmp[l],
                  w_hgrn_norm[l].reshape(1, HG_DV), w_norm_ffn[l].reshape(1, d), wfin, last)
        outs = [_layer_prompt(xp[b], *shared) for b in range(bsz)]
        xp = [o[0] for o in outs]
        st_p.append(tuple(jnp.stack([o[k] for o in outs]) for k in range(1, 5)))
        xs, *ss = _layer_sample(xs, *shared, cache_cmp_kv[l], cache_sel_kv[l], cache_win_kv[l], state_hgrn[l],
                                page_table)
        st_s.append(tuple(ss))
    y_prompt = jnp.stack(xp)
    y_sample = xs.reshape(db, dt, d)
    return (y_prompt, y_sample,
            jnp.stack([s[0] for s in st_p]), jnp.stack([s[1] for s in st_p]),
            jnp.stack([s[2] for s in st_p]), jnp.stack([s[3] for s in st_p]),
            jnp.stack([s[0] for s in st_s]), jnp.stack([s[1] for s in st_s]),
            jnp.stack([s[2] for s in st_s]), jnp.stack([s[3] for s in st_s]))
```

```python
import functools

import jax
import jax.numpy as jnp
from jax import lax
from jax.experimental import pallas as pl
from jax.experimental.pallas import tpu as pltpu

F32 = jnp.float32
BF16 = jnp.bfloat16

NSA_HEADS = 16
NSA_KV_HEADS = 4
NSA_GROUP = NSA_HEADS // NSA_KV_HEADS
NSA_HEAD_DIM = 64
NSA_BLOCK = 64
NSA_TOPK = 16
NSA_WINDOW = 512
NSA_QBLOCK = 128
HG_HEADS = 8
HG_DK = 128
HG_DV = 128
HG_CHUNK = 64
HG_SUB = 16
HG_HPB = 4
PK_HEADS = 8
PK_DIM = 256
PK_TOPK = 16
RMS_EPS = 1e-6
NEG_INF = -1e30
LOWEST = -3e38

NSA_Q_WIDTH = NSA_HEADS * NSA_HEAD_DIM
NSA_KV_WIDTH = NSA_KV_HEADS * NSA_HEAD_DIM
HG_WIDTH = HG_HEADS * HG_DK

LANES = 128
VMEM_LIMIT_BYTES = 56 * 1024 * 1024

NSA_COLS = NSA_GROUP * NSA_QBLOCK
NSA_KTILE = 512
NSA_CDIM = 128
NSA_WKEYS = NSA_WINDOW + NSA_QBLOCK
PEER_TT = 512
PEER_IB = 8


def _cparams(sem):
    return pltpu.CompilerParams(dimension_semantics=sem, vmem_limit_bytes=VMEM_LIMIT_BYTES)


def _rms(x, w):
    return x * lax.rsqrt(jnp.mean(x * x, axis=-1, keepdims=True) + RMS_EPS) * w


def _proj_kernel(x_ref, wn_ref, w_ref, *out_refs, widths, transposed, chunk):
    hb = _rms(x_ref[...], wn_ref[...]).astype(BF16)
    off = 0
    for o_ref, wd, tr in zip(out_refs, widths, transposed):
        for c0 in range(0, wd, chunk):
            cw = min(chunk, wd - c0)
            r = jnp.dot(hb, w_ref[:, off + c0:off + c0 + cw], preferred_element_type=F32)
            if tr:
                o_ref[c0:c0 + cw, :] = r.T
            else:
                o_ref[:, c0:c0 + cw] = r
        off += wd


def _proj(x, wn, w, widths, transposed, tm):
    t, d = x.shape
    n = w.shape[1]
    assert sum(widths) == n and t % tm == 0
    out_shape, out_specs = [], []
    for wd, tr in zip(widths, transposed):
        if tr:
            out_shape.append(jax.ShapeDtypeStruct((wd, t), F32))
            out_specs.append(pl.BlockSpec((wd, tm), lambda i: (0, i)))
        else:
            out_shape.append(jax.ShapeDtypeStruct((t, wd), F32))
            out_specs.append(pl.BlockSpec((tm, wd), lambda i: (i, 0)))
    return pl.pallas_call(
        functools.partial(_proj_kernel, widths=tuple(widths), transposed=tuple(transposed), chunk=512),
        grid=(t // tm,),
        in_specs=[pl.BlockSpec((tm, d), lambda i: (i, 0)),
                  pl.BlockSpec((1, d), lambda i: (0, 0)),
                  pl.BlockSpec((d, n), lambda i: (0, 0))],
        out_specs=out_specs, out_shape=out_shape,
        compiler_params=_cparams(("parallel",)), name="rms_proj",
    )(x, wn, w)


def _compress_kernel(kv_ref, w_ref, o_ref, *, nb):
    x = kv_ref[...]
    width = x.shape[-1]
    x3 = x.reshape(nb, NSA_BLOCK, width) * w_ref[...][None]
    o_ref[...] = jnp.sum(x3, axis=1)


def _compress(kv, wfull, nb):
    t, width = kv.shape
    rows = nb * NSA_BLOCK
    assert t % rows == 0
    return pl.pallas_call(
        functools.partial(_compress_kernel, nb=nb),
        grid=(t // rows,),
        in_specs=[pl.BlockSpec((rows, width), lambda i: (i, 0)),
                  pl.BlockSpec((NSA_BLOCK, width), lambda i: (0, 0))],
        out_specs=pl.BlockSpec((nb, width), lambda i: (i, 0)),
        out_shape=jax.ShapeDtypeStruct((t // NSA_BLOCK, width), F32),
        compiler_params=_cparams(("parallel",)), name="nsa_compress",
    )(kv, wfull)


def _topk_select_bias(score, k):
    n = score.shape[0]
    rows = lax.broadcasted_iota(jnp.int32, score.shape, 0).astype(F32)
    bias = jnp.full(score.shape, NEG_INF, F32)
    for _ in range(k):
        mx = jnp.max(score, axis=0, keepdims=True)
        idx = jnp.min(jnp.where(score == mx, rows, float(n)), axis=0, keepdims=True)
        hit = rows == idx
        bias = jnp.where(hit, 0.0, bias)
        score = jnp.where(hit, LOWEST, score)
    return bias


def _nsa_prompt_kernel(slope_ref, qT_ref, gT_ref, kc_ref, vcT_ref, ksel_ref, vselT_ref,
                       kd_ref, vdT_ref, kw0, kw1, kw2, kw3, kw4, vw0, vw1, vw2, vw3, vw4, bw_ref,
                       out_ref, qs_ref, selb_ref, sa_ref, sb_ref, pa_ref, pb_ref, tiles_ref, *, nblk):
    i = pl.program_id(1)
    t0 = i * NSA_QBLOCK
    dh, qb, ncol, tk = NSA_HEAD_DIM, NSA_QBLOCK, NSA_COLS, NSA_KTILE
    slope = slope_ref[...]
    col = lax.broadcasted_iota(jnp.int32, (1, ncol), 1)
    tpos = t0 + (col & (qb - 1))
    tposf = tpos.astype(F32)
    q4 = qT_ref[...] * (dh ** -0.5)
    qT = jnp.concatenate([q4[g * dh:(g + 1) * dh, :] for g in range(NSA_GROUP)], axis=1)
    xrow = lax.broadcasted_iota(jnp.int32, (NSA_CDIM - dh, ncol), 0)

    qc = jnp.concatenate([qT, jnp.zeros((NSA_CDIM - dh, ncol), F32)], axis=0).astype(BF16)
    sc = jnp.dot(kc_ref[...], qc, preferred_element_type=F32)
    c_end = lax.broadcasted_iota(jnp.int32, (nblk, 1), 0) * NSA_BLOCK + (NSA_BLOCK - 1)
    valid = c_end <= tpos
    s = jnp.where(valid, sc - slope * (tposf - c_end.astype(F32)), NEG_INF)
    e = jnp.exp(s - jnp.max(s, axis=0, keepdims=True))
    p = jnp.where(valid, e / jnp.sum(e, axis=0, keepdims=True), 0.0)
    ocT = jnp.dot(vcT_ref[...], p.astype(BF16), preferred_element_type=F32)

    qw = jnp.concatenate([qT, jnp.where(xrow == 0, NEG_INF, 0.0)], axis=0).astype(BF16)
    kw = jnp.concatenate([kw0[...], kw1[...], kw2[...], kw3[...], kw4[...]], axis=0)
    sw = jnp.dot(kw, qw, preferred_element_type=F32) + bw_ref[...]
    ew = jnp.exp(sw - jnp.max(sw, axis=0, keepdims=True))
    vw = jnp.concatenate([vw0[...], vw1[...], vw2[...], vw3[...], vw4[...]], axis=1)
    owT = jnp.dot(vw, ew.astype(BF16), preferred_element_type=F32) / jnp.sum(ew, axis=0, keepdims=True)

    nbt = tk // NSA_BLOCK
    jl = t0 // tk
    s1 = slope.astype(BF16).astype(F32)
    r1 = slope - s1
    s2 = r1.astype(BF16).astype(F32)
    s3 = (r1 - s2).astype(BF16).astype(F32)
    half = float(tk // 2)
    ext = jnp.zeros((NSA_CDIM - dh, ncol), F32)
    for r, v in enumerate((s1, s2, s3, s1 * half, s2 * half, s3 * half)):
        ext = jnp.where(xrow == nbt + r, v, ext)

    qd = jnp.concatenate([qT, ext], axis=0).astype(BF16)
    sd = jnp.dot(kd_ref[...], qd, preferred_element_type=F32)
    kposd = t0 + lax.broadcasted_iota(jnp.int32, (qb, 1), 0)
    sd = jnp.where(kposd > tpos, NEG_INF, sd)
    mx_d = jnp.max(sd, axis=0, keepdims=True)
    pd = jnp.exp(sd - mx_d)
    m_d = mx_d + slope * (jl * tk - tpos).astype(F32)
    l_d = jnp.sum(pd, axis=0, keepdims=True)
    acc_d = jnp.dot(vdT_ref[...], pd.astype(BF16), preferred_element_type=F32)

    imp = p[:, 0:qb]
    for g in range(1, NSA_GROUP):
        imp = imp + p[:, g * qb:(g + 1) * qb]
    blk = lax.broadcasted_iota(jnp.int32, (nblk, qb), 0)
    cur = tpos[:, 0:qb] >> (NSA_BLOCK.bit_length() - 1)
    forced = (blk == 0) | (blk == cur) | (blk == cur - 1)
    score = jnp.where(blk > cur, -1.0, jnp.where(forced, NSA_GROUP + 1.0, imp))
    selb = _topk_select_bias(score, min(NSA_TOPK, nblk))
    blkc = lax.broadcasted_iota(jnp.int32, (nblk, ncol), 0)
    selb_ref[0:nblk, :] = jnp.where(blkc >= t0 // NSA_BLOCK, NEG_INF, jnp.concatenate([selb] * NSA_GROUP, axis=1))
    selb_ref[nblk:nblk + nbt, :] = jnp.full((nbt, ncol), NEG_INF, F32)

    fl = jnp.max(selb_ref[0:nblk, 0:qb].reshape(nblk // nbt, nbt, qb), axis=1)
    tiles_ref[0] = 0
    n_act = jnp.int32(0)
    for j in range(nblk // nbt):
        tiles_ref[n_act] = j
        n_act = n_act + (jnp.max(fl[j:j + 1, :]) > 0.5 * NEG_INF).astype(jnp.int32)
    npairs = (n_act + 1) // 2

    qs_ref[0:dh, :] = qT
    qs_ref[dh:, :] = ext

    def tile_id(t):
        return tiles_ref[jnp.clip(t, 0, jnp.maximum(n_act - 1, 0))]

    def qk_into(t, s_ref):
        tid = tile_id(t)
        b0 = pl.multiple_of(jnp.where(t < n_act, tid * nbt, nblk), 8)
        qs_ref[dh:dh + nbt, :] = selb_ref[pl.ds(b0, nbt), :]
        k0 = pl.multiple_of(tid * tk, tk)
        s_ref[...] = jnp.dot(ksel_ref[pl.ds(k0, tk), :], qs_ref[...].astype(BF16), preferred_element_type=F32)

    def soft(t, s_ref, p_ref, m):
        cj = slope * (tile_id(t) * tk - tpos).astype(F32)
        sj = s_ref[...]
        m_new = jnp.maximum(m, jnp.max(sj, axis=0, keepdims=True) + cj)
        p_ref[...] = jnp.exp(sj - (m_new - cj)).astype(BF16)
        return m_new, jnp.exp(m - m_new)

    def pv(t, p_ref):
        return jnp.dot(vselT_ref[tile_id(t)], p_ref[...], preferred_element_type=F32)

    def pair(i, carry):
        m, accp = carry
        ta = 2 * i
        pvb = pv(ta - 1, pb_ref)
        qk_into(ta + 1, sb_ref)
        m, alpha = soft(ta, sa_ref, pa_ref, m)
        accp = alpha * (accp + pvb)
        pva = pv(ta, pa_ref)
        qk_into(ta + 2, sa_ref)
        m, alpha = soft(ta + 1, sb_ref, pb_ref, m)
        accp = alpha * (accp + pva)
        return m, accp

    pb_ref[...] = jnp.zeros(pb_ref.shape, BF16)
    qk_into(0, sa_ref)
    vrows = vselT_ref.shape[1]
    init = (jnp.full((1, ncol), NEG_INF, F32), jnp.zeros((vrows, ncol), F32))
    m_s, accp = lax.fori_loop(0, npairs, pair, init)
    accl = accp + pv(2 * npairs - 1, pb_ref)
    acc_s, l_s = accl[0:dh], accl[dh:dh + 1]

    m_f = jnp.maximum(m_s, m_d)
    a_s = jnp.exp(m_s - m_f)
    a_d = jnp.exp(m_d - m_f)
    osT = (a_s * acc_s + a_d * acc_d) / (a_s * l_s + a_d * l_d)

    sg = jax.nn.sigmoid(gT_ref[...])

    def gate(c):
        return jnp.concatenate([sg[c * NSA_GROUP + g:c * NSA_GROUP + g + 1, :] for g in range(NSA_GROUP)], axis=1)

    oT = gate(0) * ocT + gate(1) * osT + gate(2) * owT
    o4 = jnp.concatenate([oT[:, g * qb:(g + 1) * qb] for g in range(NSA_GROUP)], axis=0)
    out_ref[...] = o4.T


def _alibi_slopes():
    h = jnp.arange(NSA_HEADS, dtype=F32)
    return (2.0 ** (-8.0 * (h + 1.0) / NSA_HEADS)).reshape(NSA_KV_HEADS, NSA_GROUP)


def _nsa_prompt(qT, gT, kvc, kv_sel, kv_win):
    s_len = qT.shape[1]
    dh, qb, tk, kvh = NSA_HEAD_DIM, NSA_QBLOCK, NSA_KTILE, NSA_KV_HEADS
    assert s_len % tk == 0
    nblk = s_len // NSA_BLOCK
    slopes = _alibi_slopes()
    slope_cols = jnp.repeat(slopes, qb, axis=1).reshape(kvh, 1, NSA_COLS)

    def heads_major(a):
        return a.reshape(a.shape[0], kvh, dh).transpose(1, 0, 2)

    def pad_lanes(a):
        return jnp.pad(a, ((0, 0), (0, 0), (0, NSA_CDIM - a.shape[-1])))

    kc = pad_lanes(heads_major(kvc[:, :NSA_KV_WIDTH])).astype(BF16)
    vcT = heads_major(kvc[:, NSA_KV_WIDTH:]).transpose(0, 2, 1).astype(BF16)
    r = jnp.arange(tk)
    onehot = (r[:, None] // NSA_BLOCK == jnp.arange(tk // NSA_BLOCK)[None, :]).astype(F32)
    lo = (r % (tk // 2)).astype(F32)[:, None]
    hi = (r // (tk // 2)).astype(F32)[:, None]
    kext = jnp.concatenate([onehot, lo, lo, lo, hi, hi, hi], axis=1)
    kext = jnp.tile(kext, (s_len // tk, 1))
    ksel = pad_lanes(jnp.concatenate(
        [heads_major(kv_sel[:, :NSA_KV_WIDTH]), jnp.broadcast_to(kext[None], (kvh,) + kext.shape)], axis=-1)).astype(BF16)
    vsel = heads_major(kv_sel[:, NSA_KV_WIDTH:]).astype(BF16)
    vselT = vsel.reshape(kvh, s_len // tk, tk, dh).transpose(0, 1, 3, 2)
    ones_rows = jnp.zeros((kvh, s_len // tk, 8, tk), BF16).at[:, :, 0, :].set(1.0)
    vselT = jnp.concatenate([vselT, ones_rows], axis=2)
    vselT_flat = vsel.transpose(0, 2, 1)
    kwin = heads_major(kv_win[:, :NSA_KV_WIDTH])
    kwin = jnp.concatenate([kwin, jnp.zeros((kvh, s_len, 1), F32)], axis=-1)
    padk = jnp.zeros((kvh, NSA_WINDOW, dh + 1), F32).at[:, :, dh].set(1.0)
    kwin = pad_lanes(jnp.concatenate([padk, kwin], axis=1)).astype(BF16)
    vwinT = jnp.pad(heads_major(kv_win[:, NSA_KV_WIDTH:]), ((0, 0), (NSA_WINDOW, 0), (0, 0))).transpose(0, 2, 1).astype(BF16)
    rr = jnp.arange(NSA_WKEYS)[:, None]
    cc = jnp.arange(NSA_COLS)[None, :]
    dw = (cc % qb) + NSA_WINDOW - rr
    bw = jnp.where((dw >= 0) & (dw <= NSA_WINDOW), -slope_cols * dw.astype(F32)[None], NEG_INF)

    nq = s_len // qb
    nwb = NSA_WKEYS // qb
    kw_specs = [pl.BlockSpec((None, qb, NSA_CDIM), functools.partial(lambda k, i, j: (k, i + j, 0), j=j)) for j in range(nwb)]
    vw_specs = [pl.BlockSpec((None, dh, qb), functools.partial(lambda k, i, j: (k, 0, i + j), j=j)) for j in range(nwb)]
    assert nwb == 5
    return pl.pallas_call(
        functools.partial(_nsa_prompt_kernel, nblk=nblk),
        grid=(kvh, nq),
        in_specs=[pl.BlockSpec((None, 1, NSA_COLS), lambda k, i: (k, 0, 0)),
                  pl.BlockSpec((NSA_GROUP * dh, qb), lambda k, i: (k, i)),
                  pl.BlockSpec((None, 16, qb), lambda k, i: (k, 0, i)),
                  pl.BlockSpec((None, nblk, NSA_CDIM), lambda k, i: (k, 0, 0)),
                  pl.BlockSpec((None, dh, nblk), lambda k, i: (k, 0, 0)),
                  pl.BlockSpec((None, s_len, NSA_CDIM), lambda k, i: (k, 0, 0)),
                  pl.BlockSpec((None, s_len // tk, dh + 8, tk), lambda k, i: (k, 0, 0, 0)),
                  pl.BlockSpec((None, qb, NSA_CDIM), lambda k, i: (k, i, 0)),
                  pl.BlockSpec((None, dh, qb), lambda k, i: (k, 0, i))]
                 + kw_specs + vw_specs
                 + [pl.BlockSpec((None, NSA_WKEYS, NSA_COLS), lambda k, i: (k, 0, 0))],
        out_specs=pl.BlockSpec((qb, NSA_GROUP * dh), lambda k, i: (i, k)),
        out_shape=jax.ShapeDtypeStruct((s_len, NSA_Q_WIDTH), F32),
        scratch_shapes=[pltpu.VMEM((NSA_CDIM, NSA_COLS), F32),
                        pltpu.VMEM((nblk + tk // NSA_BLOCK, NSA_COLS), F32),
                        pltpu.VMEM((tk, NSA_COLS), F32), pltpu.VMEM((tk, NSA_COLS), F32),
                        pltpu.VMEM((tk, NSA_COLS), BF16), pltpu.VMEM((tk, NSA_COLS), BF16),
                        pltpu.SMEM((s_len // tk + 1,), jnp.int32)],
        compiler_params=_cparams(("arbitrary", "arbitrary")), name="nsa_prompt",
    )(slope_cols, qT, gT.reshape(kvh, 16, s_len), kc, vcT, ksel, vselT, ksel, vselT_flat,
      *([kwin] * nwb), *([vwinT] * nwb), bw)


def _hgrn_chunk(qc, zf, vc, lb, st):
    c, sub = HG_CHUNK, HG_SUB
    logf = jnp.log(lb + (1.0 - lb) * jax.nn.sigmoid(zf))
    kc = (1.0 - lb) * jax.nn.sigmoid(-zf)
    tri = (lax.broadcasted_iota(jnp.int32, (c, c), 0) >= lax.broadcasted_iota(jnp.int32, (c, c), 1)).astype(F32)
    cb = jnp.dot(tri, logf, preferred_element_type=F32, precision=lax.Precision.HIGHEST)
    o = lax.dot_general((qc * jnp.exp(cb)).astype(BF16), st.astype(BF16), (((1,), (1,)), ((), ())),
                        preferred_element_type=F32)
    t3 = lax.broadcasted_iota(jnp.int32, (sub, sub, 1), 0) >= lax.broadcasted_iota(jnp.int32, (sub, sub, 1), 1)
    outs = []
    for a in range(c // sub):
        ra = slice(a * sub, (a + 1) * sub)
        cba, qa, ka, va = cb[ra], qc[ra], kc[ra], vc[ra]
        d3 = cba[:, None, :] - cba[None, :, :]
        x3 = jnp.where(t3, jnp.exp(d3), 0.0) * qa[:, None, :] * ka[None, :, :]
        att3 = jnp.sum(x3, axis=2, keepdims=True)
        oa = o[ra] + jnp.sum(att3 * va[None, :, :], axis=1)
        if a > 0:
            ref = cb[a * sub - 1:a * sub, :]
            qd = (qa * jnp.exp(cba - ref)).astype(BF16)
            kd = (kc[:a * sub] * jnp.exp(ref - cb[:a * sub])).astype(BF16)
            att = lax.dot_general(qd, kd, (((1,), (1,)), ((), ())), preferred_element_type=F32)
            oa = oa + jnp.dot(att.astype(BF16), vc[:a * sub].astype(BF16), preferred_element_type=F32)
        outs.append(oa)
    o = jnp.concatenate(outs, axis=0)
    last = cb[c - 1:c, :]
    kdec = (kc * jnp.exp(last - cb)).astype(BF16)
    st = st * jnp.exp(last) + lax.dot_general(vc.astype(BF16), kdec, (((0,), (0,)), ((), ())),
                                              preferred_element_type=F32)
    return o, st


def _hgrn_kernel(lb_ref, wn_ref, q_ref, f_ref, v_ref, g_ref, s0_ref, o_ref, sfin_ref, st_ref, *, nsub):
    c = pl.program_id(1)
    hpb = HG_HPB

    @pl.when(c == 0)
    def _():
        for j in range(hpb):
            st_ref[j] = s0_ref[j].T

    wn = wn_ref[...]

    def body(u, sts):
        rows = pl.ds(pl.multiple_of(u * HG_CHUNK, HG_CHUNK), HG_CHUNK)
        out = []
        for j in range(hpb):
            cols = slice(j * HG_DK, (j + 1) * HG_DK)
            o, st = _hgrn_chunk(q_ref[rows, cols], f_ref[rows, cols], v_ref[rows, cols], lb_ref[:, cols], sts[j])
            g = g_ref[rows, cols]
            o_ref[rows, cols] = _rms(o, wn) * (g * jax.nn.sigmoid(g))
            out.append(st)
        return tuple(out)

    sts = lax.fori_loop(0, nsub, body, tuple(st_ref[j] for j in range(hpb)))
    for j in range(hpb):
        st_ref[j] = sts[j]

    @pl.when(c == pl.num_programs(1) - 1)
    def _():
        for j in range(hpb):
            sfin_ref[j] = sts[j].T


def _hgrn_prompt(hq, hf, hi, hg, lb, wn, s0, tb):
    t = hq.shape[0]
    hpb = HG_HPB
    assert t % tb == 0 and tb % HG_CHUNK == 0 and HG_HEADS % hpb == 0 and HG_DK == HG_DV
    tok = pl.BlockSpec((tb, hpb * HG_DK), lambda h, c: (c, h))
    stt = pl.BlockSpec((hpb, HG_DK, HG_DV), lambda h, c: (h, 0, 0))
    return pl.pallas_call(
        functools.partial(_hgrn_kernel, nsub=tb // HG_CHUNK),
        grid=(HG_HEADS // hpb, t // tb),
        in_specs=[pl.BlockSpec((1, hpb * HG_DK), lambda h, c: (0, h)),
                  pl.BlockSpec((1, HG_DV), lambda h, c: (0, 0)),
                  tok, tok, tok, tok, stt],
        out_specs=[tok, stt],
        out_shape=[jax.ShapeDtypeStruct((t, HG_HEADS * HG_DV), F32),
                   jax.ShapeDtypeStruct((HG_HEADS, HG_DK, HG_DV), F32)],
        scratch_shapes=[pltpu.VMEM((hpb, HG_DV, HG_DK), F32)],
        compiler_params=_cparams(("arbitrary", "arbitrary")), name="hgrn_prompt",
    )(lb, wn, hq, hf, hi, hg, s0)


def _hgrn_step_kernel(lb_ref, wn_ref, q_ref, f_ref, v_ref, g_ref, s0_ref, o_ref, s_ref):
    wn = wn_ref[...]
    for h in range(HG_HEADS):
        rk = slice(h * HG_DK, (h + 1) * HG_DK)
        lb = lb_ref[rk, :]
        zf = f_ref[rk, :]
        f = lb + (1.0 - lb) * jax.nn.sigmoid(zf)
        kk = (1.0 - lb) * jax.nn.sigmoid(-zf)
        vrow = v_ref[:, h * HG_DV:(h + 1) * HG_DV]
        s_new = f * s0_ref[h] + kk * vrow
        s_ref[h] = s_new
        o = jnp.sum(s_new * q_ref[rk, :], axis=0, keepdims=True)
        g = g_ref[:, h * HG_DV:(h + 1) * HG_DV]
        o_ref[:, h * HG_DV:(h + 1) * HG_DV] = _rms(o, wn) * (g * jax.nn.sigmoid(g))


def _hgrn_step(hq_col, hf_col, hi, hg, lb_col, wn, s0):
    b = hi.shape[0]
    col = pl.BlockSpec((None, HG_WIDTH, 1), lambda i: (i, 0, 0))
    row = pl.BlockSpec((None, 1, HG_WIDTH), lambda i: (i, 0, 0))
    st = pl.BlockSpec((None, HG_HEADS, HG_DK, HG_DV), lambda i: (i, 0, 0, 0))
    return pl.pallas_call(
        _hgrn_step_kernel, grid=(b,),
        in_specs=[pl.BlockSpec((HG_WIDTH, 1), lambda i: (0, 0)), pl.BlockSpec((1, HG_DV), lambda i: (0, 0)),
                  col, col, row, row, st],
        out_specs=[row, st],
        out_shape=[jax.ShapeDtypeStruct((b, 1, HG_WIDTH), F32), jax.ShapeDtypeStruct(s0.shape, F32)],
        compiler_params=_cparams(("parallel",)), name="hgrn_step",
    )(lb_col, wn, hq_col, hf_col, hi, hg, s0)


def _merge_kernel(x_ref, oa_ref, ob_ref, ma_ref, mb_ref, pa_ref, pb_ref, wo_ref, y_ref):
    ya = jnp.dot(oa_ref[...].astype(BF16), pa_ref[...], preferred_element_type=F32)
    yb = jnp.dot(ob_ref[...].astype(BF16), pb_ref[...], preferred_element_type=F32)
    mix = jax.nn.sigmoid(ma_ref[...]) * ya + jax.nn.sigmoid(mb_ref[...]) * yb
    y_ref[...] = x_ref[...] + jnp.dot(mix.astype(BF16), wo_ref[...], preferred_element_type=F32)


def _merge(x, oa, ob, ma, mb, pa, pb, wo, tm):
    t, d = x.shape
    tok = pl.BlockSpec((tm, d), lambda i: (i, 0))
    wsp = pl.BlockSpec((d, d), lambda i: (0, 0))
    return pl.pallas_call(
        _merge_kernel, grid=(t // tm,),
        in_specs=[tok, tok, tok, tok, tok, wsp, wsp, wsp],
        out_specs=tok, out_shape=jax.ShapeDtypeStruct((t, d), F32),
        compiler_params=_cparams(("parallel",)), name="branch_merge",
    )(x, oa, ob, ma, mb, pa, pb, wo)


def _topk_rows(s, k):
    n = s.shape[0]
    rows = lax.broadcasted_iota(jnp.int32, s.shape, 0).astype(F32)
    rank = jnp.full(s.shape, float(n), F32)
    tops = []
    for r in range(k):
        mx = jnp.max(s, axis=0, keepdims=True)
        idx = jnp.min(jnp.where(s == mx, rows, float(n)), axis=0, keepdims=True)
        hit = rows == idx
        rank = jnp.where(hit, float(r), rank)
        s = jnp.where(hit, LOWEST, s)
        tops.append(mx)
    return jnp.concatenate(tops, axis=0), rank


_STAIR_GROUPS = ((0, 16), (1, 8), (2, 8), (3, 8))
_STAIR_QUAD = (4, 5, 6, 7)
_STAIR_TAIL = 8


def _stair_rows(t1, t2, op):
    tt = t1.shape[1]
    parts = [op(jnp.broadcast_to(t1[a:a + 1, :], (nb, tt)), t2[0:nb, :]) for a, nb in _STAIR_GROUPS]
    r16 = lax.broadcasted_iota(jnp.int32, (16, tt), 0)
    v1 = jnp.broadcast_to(t1[_STAIR_QUAD[3]:_STAIR_QUAD[3] + 1, :], (16, tt))
    v2 = jnp.broadcast_to(t2[3:4, :], (16, tt))
    for q in (2, 1, 0):
        v1 = jnp.where(r16 < 4 * (q + 1), jnp.broadcast_to(t1[_STAIR_QUAD[q]:_STAIR_QUAD[q] + 1, :], (16, tt)), v1)
        v2 = jnp.where((r16 & 3) == q, jnp.broadcast_to(t2[q:q + 1, :], (16, tt)), v2)
    parts.append(op(v1, v2))
    parts.append(op(t1[_STAIR_TAIL:, :], jnp.broadcast_to(t2[0:1, :], (PK_TOPK - _STAIR_TAIL, tt))))
    return jnp.concatenate(parts, axis=0)


def _stair_row_counts(selc):
    out, r0 = [], 0
    for _, nb in _STAIR_GROUPS:
        out.append(jnp.sum(selc[r0:r0 + nb, :], axis=0, keepdims=True))
        r0 += nb
    quad = selc[r0:r0 + 16, :]
    r16 = lax.broadcasted_iota(jnp.int32, quad.shape, 0)
    for q in range(4):
        out.append(jnp.sum(jnp.where((r16 >> 2) == q, quad, 0.0), axis=0, keepdims=True))
    r0 += 16
    for a in range(PK_TOPK - _STAIR_TAIL):
        out.append(selc[r0 + a:r0 + a + 1, :])
    return out


def _peer_kernel(x_ref, wn_ref, wq_ref, sk_ref, u_ref, vT_ref, wf_ref, y_ref,
                 hnT_ref, acc_ref, wcat_ref, wodd_ref, n_ref, a1_ref, r2_ref, e2_ref, *, ib, nkeys, final_norm):
    i = pl.program_id(1)
    kt = PK_TOPK
    hd = PK_DIM // 2

    @pl.when(i == 0)
    def _():
        hnT = _rms(x_ref[...], wn_ref[...]).T.astype(BF16)
        hnT_ref[...] = hnT
        acc_ref[...] = jnp.zeros(acc_ref.shape, F32)
        wcat_ref[...] = jnp.zeros(wcat_ref.shape, BF16)
        wodd_ref[...] = jnp.zeros(wodd_ref.shape, BF16)
        tt = hnT.shape[1]
        for h in range(PK_HEADS):
            ss, tops, ranks = [], [], []
            for c in range(2):
                r0 = (h * 2 + c) * hd
                qhc = jnp.dot(wq_ref[r0:r0 + hd, :], hnT, preferred_element_type=F32)
                s = jnp.dot(sk_ref[h * 2 + c], qhc.astype(BF16), preferred_element_type=F32)
                top, rank = _topk_rows(s, kt)
                ss.append(s)
                tops.append(top)
                ranks.append(rank)
            cand = _stair_rows(tops[0], tops[1], jnp.add)
            _, crank = _topk_rows(cand, kt)
            selc = (crank < float(kt)).astype(F32)
            n_a = _stair_row_counts(selc)
            e1t = jnp.exp(tops[0] - tops[0][0:1, :])
            e2t = jnp.exp(tops[1] - tops[1][0:1, :])
            z = jnp.sum(selc * _stair_rows(e1t, e2t, jnp.multiply), axis=0, keepdims=True)
            nfull = jnp.zeros((nkeys, tt), F32)
            for a in range(kt):
                nfull = jnp.where(ranks[0] == float(a), n_a[a], nfull)
            n_ref[h] = nfull
            a1_ref[h] = jnp.exp(ss[0] - tops[0][0:1, :]) / z
            r2_ref[h] = ranks[1].astype(BF16)
            e2_ref[h] = jnp.exp(ss[1] - tops[1][0:1, :]).astype(BF16)

    nsteps = pl.num_programs(1) - 1
    nch = 2
    cw = hnT_ref.shape[1] // nch

    def step(w_read, w_write):
        hnT = hnT_ref[...]
        for ii in range(ib):
            if ii % (ib // nch) == 1:
                cols = slice((ii // (ib // nch)) * cw, (ii // (ib // nch) + 1) * cw)
                acc_ref[:, cols] += jnp.dot(vT_ref[...], w_read[:, cols], preferred_element_type=F32)
            ig = i * ib + ii
            aT = jnp.dot(u_ref[ii * nkeys:(ii + 1) * nkeys, :], hnT, preferred_element_type=F32)
            gsum = jnp.zeros(aT.shape, BF16)
            for h in range(PK_HEADS):
                nrow = n_ref[h, pl.ds(ig, 1), :].astype(BF16)
                arow = a1_ref[h, pl.ds(ig, 1), :].astype(BF16)
                gsum = gsum + jnp.where(r2_ref[h] < nrow, arow * e2_ref[h], jnp.zeros((), BF16))
            w_write[ii * nkeys:(ii + 1) * nkeys, :] = (jax.nn.gelu(aT) * gsum.astype(F32)).astype(BF16)

    @pl.when((i < nsteps) & (i % 2 == 0))
    def _():
        step(wodd_ref, wcat_ref)

    @pl.when((i < nsteps) & (i % 2 == 1))
    def _():
        step(wcat_ref, wodd_ref)

    @pl.when(i == nsteps)
    def _():
        w_last = wodd_ref if (nkeys // ib) % 2 == 0 else wcat_ref
        acc = acc_ref[...] + jnp.dot(vT_ref[...], w_last[...], preferred_element_type=F32)
        y = x_ref[...] + acc.T
        if final_norm:
            y = _rms(y, wf_ref[...])
        y_ref[...] = y


def _peer(x, wn, wqT, sk, u, vT, wf, tt, ib, final_norm):
    t, d = x.shape
    nkeys = sk.shape[1]
    assert t % tt == 0 and nkeys % ib == 0
    nsteps = nkeys // ib
    stat = pltpu.VMEM((PK_HEADS, nkeys, tt), F32)
    stat16 = pltpu.VMEM((PK_HEADS, nkeys, tt), BF16)
    wbuf = pltpu.VMEM((ib * nkeys, tt), BF16)
    return pl.pallas_call(
        functools.partial(_peer_kernel, ib=ib, nkeys=nkeys, final_norm=final_norm),
        grid=(t // tt, nsteps + 1),
        in_specs=[pl.BlockSpec((tt, d), lambda a, i: (a, 0)),
                  pl.BlockSpec((1, d), lambda a, i: (0, 0)),
                  pl.BlockSpec(wqT.shape, lambda a, i: (0, 0)),
                  pl.BlockSpec(sk.shape, lambda a, i: (0, 0, 0)),
                  pl.BlockSpec((ib * nkeys, d), lambda a, i: (jnp.minimum(i, nsteps - 1), 0)),
                  pl.BlockSpec((None, d, ib * nkeys), lambda a, i: (jnp.maximum(i - 1, 0), 0, 0)),
                  pl.BlockSpec((1, d), lambda a, i: (0, 0))],
        out_specs=pl.BlockSpec((tt, d), lambda a, i: (a, 0)),
        out_shape=jax.ShapeDtypeStruct((t, d), F32),
        scratch_shapes=[pltpu.VMEM((d, tt), BF16), pltpu.VMEM((d, tt), F32), wbuf, wbuf,
                        stat, stat, stat16, stat16],
        compiler_params=_cparams(("arbitrary", "arbitrary")), name="peer_dense",
    )(x, wn, wqT, sk, u, vT, wf)


def _compress_pages_kernel(pt_ref, *refs, pg, bpp):
    w_ref, o_ref = refs[pg], refs[pg + 1]
    nt = (((1,), (1,)), ((), ()))
    for j in range(pg):
        x = refs[j][...]
        halves = [lax.dot_general(w_ref[c], x[c].reshape(-1, x.shape[-1]).astype(BF16), nt, preferred_element_type=F32)
                  for c in range(x.shape[0])]
        o_ref[j * bpp:(j + 1) * bpp, :] = jnp.concatenate(halves, axis=1)


def _compress_pages(cache_t, page_table, w_sel, pg):
    tail = cache_t.shape[1:]
    db, npages = page_table.shape
    bpp = w_sel.shape[1]
    width = tail[0] * tail[1] * tail[2]
    assert npages % pg == 0
    zeros = (0,) * len(tail)
    page_specs = [pl.BlockSpec((None,) + tail, functools.partial(lambda b, g, pt, j: (pt[b, g * pg + j],) + zeros, j=j))
                  for j in range(pg)]
    return pl.pallas_call(
        functools.partial(_compress_pages_kernel, pg=pg, bpp=bpp),
        grid_spec=pltpu.PrefetchScalarGridSpec(
            num_scalar_prefetch=1, grid=(db, npages // pg),
            in_specs=page_specs + [pl.BlockSpec(w_sel.shape, lambda b, g, pt: (0, 0, 0))],
            out_specs=pl.BlockSpec((None, pg * bpp, width), lambda b, g, pt: (b, g, 0))),
        out_shape=jax.ShapeDtypeStruct((db, npages * bpp, width), F32),
        compiler_params=_cparams(("arbitrary", "arbitrary")), name="nsa_compress_pages",
    )(page_table, *([cache_t] * pg), w_sel)


def _nsa_decode_head_kernel(slope_ref, qbd_ref, g_ref, kvc_ref, wnew_ref, cwin_ref, part_ref, idx_ref,
                            *, past_len, k_past):
    kvw, grp, dh = NSA_KV_WIDTH, NSA_GROUP, NSA_HEAD_DIM
    qbd = qbd_ref[...]
    qb = qbd.astype(BF16)
    slope = slope_ref[...]
    nh = qbd.shape[0]
    npb = kvc_ref.shape[0]
    nt = (((1,), (1,)), ((), ()))
    kvc = kvc_ref[...]
    sc = lax.dot_general(qb, kvc[:, :kvw].astype(BF16), nt, preferred_element_type=F32)
    c_end = lax.broadcasted_iota(jnp.int32, (1, npb), 1) * NSA_BLOCK + (NSA_BLOCK - 1)
    s = sc - slope * (past_len - c_end).astype(F32)
    e = jnp.exp(s - jnp.max(s, axis=1, keepdims=True))
    p = e / jnp.sum(e, axis=1, keepdims=True)
    oc = jnp.dot(p.astype(BF16), kvc[:, kvw:].astype(BF16), preferred_element_type=F32)
    rows = []
    for k in range(NSA_KV_HEADS):
        r = p[k * grp:k * grp + 1, :]
        for g in range(1, grp):
            r = r + p[k * grp + g:k * grp + g + 1, :]
        rows.append(jnp.broadcast_to(r, (grp, npb)))
    imp = jnp.concatenate(rows, axis=0)
    blk = lax.broadcasted_iota(jnp.int32, (nh, npb), 1)
    score = jnp.where((blk == 0) | (blk == npb - 1), grp + 1.0, imp)
    lanes = blk.astype(F32)
    picks = []
    for _ in range(k_past):
        mx = jnp.max(score, axis=1, keepdims=True)
        idx = jnp.min(jnp.where(score == mx, lanes, float(npb)), axis=1, keepdims=True)
        score = jnp.where(lanes == idx, LOWEST, score)
        picks.append(idx)
    idx_ref[...] = jnp.concatenate(picks, axis=1).astype(jnp.int32)
    sg = jax.nn.sigmoid(g_ref[...])
    wnew = wnew_ref[...]
    nw = cwin_ref.shape[-1]
    dw = (nw - lax.broadcasted_iota(jnp.int32, (1, nw), 1)).astype(F32)
    for k in range(NSA_KV_HEADS):
        hs, ds = slice(k * grp, (k + 1) * grp), slice(k * dh, (k + 1) * dh)
        qk = qbd[hs, ds]
        sw = jnp.dot(qk.astype(BF16), cwin_ref[0, k].astype(BF16), preferred_element_type=F32) - slope[hs] * dw
        s_n = jnp.sum(qk * wnew[:, ds], axis=1, keepdims=True)
        m = jnp.maximum(jnp.max(sw, axis=1, keepdims=True), s_n)
        ew = jnp.exp(sw - m)
        en = jnp.exp(s_n - m)
        pv = lax.dot_general(ew.astype(BF16), cwin_ref[1, k].astype(BF16), nt, preferred_element_type=F32)
        ow = (pv + en * wnew[:, kvw + k * dh:kvw + (k + 1) * dh]) / (jnp.sum(ew, axis=1, keepdims=True) + en)
        part_ref[hs, :] = sg[hs, 0:1] * oc[hs, ds] + sg[hs, 2:3] * ow


def _nsa_decode_gather_kernel(pt_ref, ix_ref, slope_ref, q_ref, g_ref, part_ref, new_ref, *refs, nsel, bpp, past_len):
    blocks, o_ref = refs[:nsel], refs[nsel]
    b, k = pl.program_id(0), pl.program_id(1)
    page = blocks[0].shape[-1]
    q = q_ref[...]
    slope = slope_ref[...]
    kt = jnp.concatenate([blocks[s][0] for s in range(nsel)], axis=1).astype(BF16)
    vt = jnp.concatenate([blocks[s][1] for s in range(nsel)], axis=1).astype(BF16)
    sc = jnp.dot(q.astype(BF16), kt, preferred_element_type=F32)
    lane = lax.broadcasted_iota(jnp.int32, (1, page), 1)
    lblk = lane // NSA_BLOCK
    lkey = lane - lblk * NSA_BLOCK
    bias = []
    for s in range(nsel):
        ib = ix_ref[b, k, s]
        kpos = ib * NSA_BLOCK + lkey
        bias.append(jnp.where(lblk == ib % bpp, -slope * (past_len - kpos).astype(F32), NEG_INF))
    sc = sc + jnp.concatenate(bias, axis=1)
    k_own, v_own = new_ref[0], new_ref[1]
    s_n = jnp.sum(q * k_own, axis=1, keepdims=True)
    m = jnp.maximum(jnp.max(sc, axis=1, keepdims=True), s_n)
    p = jnp.exp(sc - m)
    p_n = jnp.exp(s_n - m)
    pv = lax.dot_general(p.astype(BF16), vt, (((1,), (1,)), ((), ())), preferred_element_type=F32)
    o_s = (pv + p_n * v_own) / (jnp.sum(p, axis=1, keepdims=True) + p_n)
    o_ref[...] = part_ref[...] + jax.nn.sigmoid(g_ref[...])[:, 1:2] * o_s


def _rows_minor(a):
    return a.transpose(0, 2, 3, 4, 1)


def _nsa_sample(q, kv_cmp, kv_sel, kv_win, glog, cache_cmp, cache_sel, cache_win, page_table, w_cmp):
    db = q.shape[0]
    kvh, grp, dh, kvw = NSA_KV_HEADS, NSA_GROUP, NSA_HEAD_DIM, NSA_KV_WIDTH
    page = cache_cmp.shape[1]
    npages = page_table.shape[1]
    past_len = npages * page
    nwin = cache_win.shape[1]
    assert past_len >= nwin and page % NSA_BLOCK == 0
    npb = past_len // NSA_BLOCK
    bpp = page // NSA_BLOCK
    pg = 8 if npages % 8 == 0 else 1
    width = 2 * kvw
    rowblk = jnp.arange(page) // NSA_BLOCK
    w_sel = jnp.where(rowblk[None, None, :] == jnp.arange(bpp)[None, :, None],
                      jnp.tile(w_cmp, (1, bpp))[:, None, :], 0.0).astype(BF16)
    kvc = _compress_pages(_rows_minor(cache_cmp), page_table, w_sel, pg)
    q4 = q.reshape(db, kvh, grp, dh) * (dh ** -0.5)
    qbd = (q4[:, :, :, None, :] * jnp.eye(kvh, dtype=F32)[None, :, None, :, None]).reshape(db, kvh * grp, kvw)
    slope = _alibi_slopes().reshape(kvh * grp, 1)
    nh = kvh * grp
    k_past = min(NSA_TOPK, npb + 1) - 1
    full2 = lambda shape: pl.BlockSpec(shape, lambda b: (0,) * len(shape))
    per_b = lambda shape: pl.BlockSpec((None,) + shape, lambda b: (b,) + (0,) * len(shape))
    part, idx = pl.pallas_call(
        functools.partial(_nsa_decode_head_kernel, past_len=past_len, k_past=k_past),
        grid=(db,),
        in_specs=[full2((nh, 1)), per_b((nh, kvw)), per_b((nh, 3)), per_b((npb, width)), per_b((1, width)),
                  per_b((2, kvh, dh, nwin))],
        out_specs=[per_b((nh, dh)), per_b((nh, k_past))],
        out_shape=[jax.ShapeDtypeStruct((db, nh, dh), F32), jax.ShapeDtypeStruct((db, nh, k_past), jnp.int32)],
        compiler_params=_cparams(("parallel",)), name="nsa_decode_head",
    )(slope, qbd, glog, kvc, kv_win.reshape(db, 1, width), _rows_minor(cache_win))
    idx4 = idx.reshape(db, kvh, grp, k_past)[:, :, 0, :]
    blk_specs = [pl.BlockSpec((None, 2, None, dh, page),
                              functools.partial(lambda b, k, pt, ix, s: (pt[b, ix[b, k, s] // bpp], 0, k, 0, 0), s=s))
                 for s in range(k_past)]
    hsp = lambda shape: pl.BlockSpec((None, None) + shape, lambda b, k, pt, ix: (b, k) + (0,) * len(shape))
    o = pl.pallas_call(
        functools.partial(_nsa_decode_gather_kernel, nsel=k_past, bpp=bpp, past_len=past_len),
        grid_spec=pltpu.PrefetchScalarGridSpec(
            num_scalar_prefetch=2, grid=(db, kvh),
            in_specs=[pl.BlockSpec((None, grp, 1), lambda b, k, pt, ix: (k, 0, 0)), hsp((grp, dh)), hsp((grp, 3)),
                      hsp((grp, dh)),
                      pl.BlockSpec((None, 2, None, 1, dh), lambda b, k, pt, ix: (b, 0, k, 0, 0))] + blk_specs,
            out_specs=hsp((grp, dh))),
        out_shape=jax.ShapeDtypeStruct((db, kvh, grp, dh), F32),
        compiler_params=_cparams(("arbitrary", "arbitrary")), name="nsa_decode_gather",
    )(page_table, idx4, slope.reshape(kvh, grp, 1), q4, glog.reshape(db, kvh, grp, 3), part.reshape(db, kvh, grp, dh),
      kv_sel.reshape(db, 2, kvh, 1, dh), *([_rows_minor(cache_sel)] * k_past))
    new_win = jnp.concatenate([cache_win, kv_win.reshape((db, 1) + cache_win.shape[2:])], axis=1)[:, 1:]
    return o.reshape(db, nh * dh), new_win


def _split_w_in(w_in):
    pts, acc = [], 0
    for w in (NSA_Q_WIDTH, 6 * NSA_KV_WIDTH, 3 * NSA_HEADS, HG_WIDTH, HG_WIDTH, HG_WIDTH, HG_WIDTH, w_in.shape[0]):
        acc += w
        pts.append(acc)
    return jnp.split(w_in, pts, axis=1)


def _prep_layer(w_in, w_proj_nsa, w_proj_hgrn, w_out, w_peer_q, peer_sub_keys, peer_u, peer_v):
    d = w_in.shape[0]
    wq, wkv, wg, wbq, wbf, wbi, wbg, wma, wmb = _split_w_in(w_in)
    wg = wg.reshape(d, NSA_KV_HEADS, NSA_GROUP, 3).transpose(0, 1, 3, 2).reshape(d, NSA_KV_HEADS, 3 * NSA_GROUP)
    wg = jnp.pad(wg, ((0, 0), (0, 0), (0, 16 - 3 * NSA_GROUP))).reshape(d, NSA_KV_HEADS * 16)
    wg = jnp.pad(wg, ((0, 0), (0, LANES - NSA_KV_HEADS * 16)))
    nkeys = peer_sub_keys.shape[2]
    return dict(
        w_qg=jnp.concatenate([wq, wg], axis=1).astype(BF16),
        w_kv=wkv.astype(BF16),
        w_hg=jnp.concatenate([wbq, wbf, wbi, wbg], axis=1).astype(BF16),
        w_m=jnp.concatenate([wma, wmb], axis=1).astype(BF16),
        pa=w_proj_nsa.astype(BF16), pb=w_proj_hgrn.astype(BF16), wo=w_out.astype(BF16),
        wqT=w_peer_q.T.astype(BF16),
        sk=peer_sub_keys.reshape(PK_HEADS * 2, nkeys, PK_DIM // 2).astype(BF16),
        u=peer_u.astype(BF16),
        vT=peer_v.astype(BF16).reshape(nkeys // PEER_IB, PEER_IB * nkeys, d).transpose(0, 2, 1),
    )


def _tile(t, pref):
    return pref if t % pref == 0 else t


def _peer_tokens(x, wn, wf, prm, final_norm):
    t = x.shape[0]
    tp = -(-t // LANES) * LANES
    xp = jnp.pad(x, ((0, tp - t), (0, 0)))
    tt = PEER_TT if tp % PEER_TT == 0 else LANES
    y = _peer(xp, wn, prm["wqT"], prm["sk"], prm["u"], prm["vT"], wf, tt, PEER_IB, final_norm)
    return y[:t]


def _layer_prompt(x, lb, prm, w_norm_mix, w_cmp, w_hgrn_norm, w_norm_ffn, w_norm_final, final_norm):
    s_len, d = x.shape
    tm = _tile(s_len, 256)
    kvw = NSA_KV_WIDTH
    kv_cmp, kv_sel, kv_win = _proj(x, w_norm_mix, prm["w_kv"], [2 * kvw] * 3, [False] * 3, tm)
    qT, gT = _proj(x, w_norm_mix, prm["w_qg"], [NSA_Q_WIDTH, LANES], [True, True], tm)
    hq, hf, hi, hg = _proj(x, w_norm_mix, prm["w_hg"], [HG_WIDTH] * 4, [False] * 4, tm)
    ma, mb = _proj(x, w_norm_mix, prm["w_m"], [d, d], [False, False], tm)
    wfull = jnp.concatenate([jnp.broadcast_to(w_cmp[0][:, None], (NSA_BLOCK, kvw)),
                             jnp.broadcast_to(w_cmp[1][:, None], (NSA_BLOCK, kvw))], axis=1)
    nblk = s_len // NSA_BLOCK
    kvc = _compress(kv_cmp, wfull, 8 if nblk % 8 == 0 else nblk)
    o_a = _nsa_prompt(qT, gT[:NSA_KV_HEADS * 16], kvc, kv_sel, kv_win)
    s0 = jnp.zeros((HG_HEADS, HG_DK, HG_DV), F32)
    o_b, s_fin = _hgrn_prompt(hq, hf, hi, hg, lb, w_hgrn_norm, s0, _tile(s_len, 512))
    x1 = _merge(x, o_a, o_b, ma, mb, prm["pa"], prm["pb"], prm["wo"], tm)
    x2 = _peer_tokens(x1, w_norm_ffn, w_norm_final, prm, final_norm)
    shp = (s_len, 2, NSA_KV_HEADS, NSA_HEAD_DIM)
    return x2, kv_cmp.reshape(shp), kv_sel.reshape(shp), kv_win.reshape(shp)[-NSA_WINDOW:], s_fin


def _layer_sample(x, lb, prm, w_norm_mix, w_cmp, w_hgrn_norm, w_norm_ffn, w_norm_final, final_norm,
                  cache_cmp, cache_sel, cache_win, s0, page_table):
    b, d = x.shape
    kvw = NSA_KV_WIDTH
    w_q = prm["w_qg"][:, :NSA_Q_WIDTH]
    (q,) = _proj(x, w_norm_mix, w_q, [NSA_Q_WIDTH], [False], b)
    kv_cmp, kv_sel, kv_win = _proj(x, w_norm_mix, prm["w_kv"], [2 * kvw] * 3, [False] * 3, b)
    (gTt,) = _proj(x, w_norm_mix, prm["w_qg"][:, NSA_Q_WIDTH:], [LANES], [False], b)
    hq, hf, hi, hg = _proj(x, w_norm_mix, prm["w_hg"], [HG_WIDTH] * 4, [False] * 4, b)
    ma, mb = _proj(x, w_norm_mix, prm["w_m"], [d, d], [False, False], b)
    shp = (b, 1, 2, NSA_KV_HEADS, NSA_HEAD_DIM)
    glog = (gTt[:, :NSA_KV_HEADS * 16].reshape(b, NSA_KV_HEADS, 16)[..., :3 * NSA_GROUP]
            .reshape(b, NSA_KV_HEADS, 3, NSA_GROUP).transpose(0, 1, 3, 2).reshape(b, NSA_HEADS, 3))
    o_a, new_win = _nsa_sample(q, kv_cmp, kv_sel, kv_win, glog, cache_cmp, cache_sel, cache_win, page_table, w_cmp)
    o_b, s_new = _hgrn_step(hq.reshape(b, HG_WIDTH, 1), hf.reshape(b, HG_WIDTH, 1), hi.reshape(b, 1, HG_WIDTH),
                            hg.reshape(b, 1, HG_WIDTH), lb.reshape(HG_WIDTH, 1), w_hgrn_norm, s0)
    x1 = _merge(x, o_a.reshape(b, NSA_Q_WIDTH), o_b.reshape(b, HG_WIDTH), ma, mb, prm["pa"], prm["pb"], prm["wo"], b)
    x2 = _peer_tokens(x1, w_norm_ffn, w_norm_final, prm, final_norm)
    return x2, kv_cmp.reshape(shp), kv_sel.reshape(shp), new_win, s_new


def kernel(x_prompt, x_sample, cache_cmp_kv, cache_sel_kv, cache_win_kv, state_hgrn, page_table,
           w_norm_mix, w_in, w_cmp, w_proj_nsa, w_proj_hgrn, w_hgrn_norm, hgrn_lb_logits, w_out,
           w_norm_ffn, w_peer_q, peer_sub_keys, peer_u, peer_v, w_norm_final):
    depth = w_in.shape[0]
    bsz, s_len, d = x_prompt.shape
    db, dt, _ = x_sample.shape
    assert dt == 1
    lbs = jnp.cumsum(jax.nn.softmax(hgrn_lb_logits.astype(F32), axis=0), axis=0)
    wfin = w_norm_final.reshape(1, d)
    xp = [x_prompt[b] for b in range(bsz)]
    xs = x_sample.reshape(db, d)
    st_p, st_s = [], []
    for l in range(depth):
        last = l == depth - 1
        prm = _prep_layer(w_in[l], w_proj_nsa[l], w_proj_hgrn[l], w_out[l], w_peer_q[l], peer_sub_keys[l],
                          peer_u[l], peer_v[l])
        shared = (lbs[l].reshape(1, HG_WIDTH), prm, w_norm_mix[l].reshape(1, d), w_cmp[l],
                  w_hgrn_norm[l].reshape(1, HG_DV), w_norm_ffn[l].reshape(1, d), wfin, last)
        outs = [_layer_prompt(xp[b], *shared) for b in range(bsz)]
        xp = [o[0] for o in outs]
        st_p.append(tuple(jnp.stack([o[k] for o in outs]) for k in range(1, 5)))
        xs, *ss = _layer_sample(xs, *shared, cache_cmp_kv[l], cache_sel_kv[l], cache_win_kv[l], state_hgrn[l],
                                page_table)
        st_s.append(tuple(ss))
    y_prompt = jnp.stack(xp)
    y_sample = xs.reshape(db, dt, d)
    return (y_prompt, y_sample,
            jnp.stack([s[0] for s in st_p]), jnp.stack([s[1] for s in st_p]),
            jnp.stack([s[2] for s in st_p]), jnp.stack([s[3] for s in st_p]),
            jnp.stack([s[0] for s in st_s]), jnp.stack([s[1] for s in st_s]),
            jnp.stack([s[2] for s in st_s]), jnp.stack([s[3] for s in st_s]))
```

```python
import functools

import jax
import jax.numpy as jnp
from jax import lax
from jax.experimental import pallas as pl
from jax.experimental.pallas import tpu as pltpu

F32 = jnp.float32
BF16 = jnp.bfloat16

NSA_HEADS = 16
NSA_KV_HEADS = 4
NSA_GROUP = NSA_HEADS // NSA_KV_HEADS
NSA_HEAD_DIM = 64
NSA_BLOCK = 64
NSA_TOPK = 16
NSA_WINDOW = 512
NSA_QBLOCK = 128
HG_HEADS = 8
HG_DK = 128
HG_DV = 128
HG_CHUNK = 64
HG_SUB = 16
HG_HPB = 4
PK_HEADS = 8
PK_DIM = 256
PK_TOPK = 16
RMS_EPS = 1e-6
NEG_INF = -1e30
LOWEST = -3e38

NSA_Q_WIDTH = NSA_HEADS * NSA_HEAD_DIM
NSA_KV_WIDTH = NSA_KV_HEADS * NSA_HEAD_DIM
HG_WIDTH = HG_HEADS * HG_DK

LANES = 128
VMEM_LIMIT_BYTES = 56 * 1024 * 1024

NSA_COLS = NSA_GROUP * NSA_QBLOCK
NSA_KTILE = 512
NSA_CDIM = 128
NSA_WKEYS = NSA_WINDOW + NSA_QBLOCK
PEER_TT = 512
PEER_IB = 8


def _cparams(sem):
    return pltpu.CompilerParams(dimension_semantics=sem, vmem_limit_bytes=VMEM_LIMIT_BYTES)


def _rms(x, w):
    return x * lax.rsqrt(jnp.mean(x * x, axis=-1, keepdims=True) + RMS_EPS) * w


def _proj_kernel(x_ref, wn_ref, w_ref, *out_refs, widths, transposed, chunk):
    hb = _rms(x_ref[...], wn_ref[...]).astype(BF16)
    off = 0
    for o_ref, wd, tr in zip(out_refs, widths, transposed):
        for c0 in range(0, wd, chunk):
            cw = min(chunk, wd - c0)
            r = jnp.dot(hb, w_ref[:, off + c0:off + c0 + cw], preferred_element_type=F32)
            if tr:
                o_ref[c0:c0 + cw, :] = r.T
            else:
                o_ref[:, c0:c0 + cw] = r
        off += wd


def _proj(x, wn, w, widths, transposed, tm):
    t, d = x.shape
    n = w.shape[1]
    assert sum(widths) == n and t % tm == 0
    out_shape, out_specs = [], []
    for wd, tr in zip(widths, transposed):
        if tr:
            out_shape.append(jax.ShapeDtypeStruct((wd, t), F32))
            out_specs.append(pl.BlockSpec((wd, tm), lambda i: (0, i)))
        else:
            out_shape.append(jax.ShapeDtypeStruct((t, wd), F32))
            out_specs.append(pl.BlockSpec((tm, wd), lambda i: (i, 0)))
    return pl.pallas_call(
        functools.partial(_proj_kernel, widths=tuple(widths), transposed=tuple(transposed), chunk=512),
        grid=(t // tm,),
        in_specs=[pl.BlockSpec((tm, d), lambda i: (i, 0)),
                  pl.BlockSpec((1, d), lambda i: (0, 0)),
                  pl.BlockSpec((d, n), lambda i: (0, 0))],
        out_specs=out_specs, out_shape=out_shape,
        compiler_params=_cparams(("parallel",)), name="rms_proj",
    )(x, wn, w)


def _compress_kernel(kv_ref, w_ref, o_ref, *, nb):
    x = kv_ref[...]
    width = x.shape[-1]
    x3 = x.reshape(nb, NSA_BLOCK, width) * w_ref[...][None]
    o_ref[...] = jnp.sum(x3, axis=1)


def _compress(kv, wfull, nb):
    t, width = kv.shape
    rows = nb * NSA_BLOCK
    assert t % rows == 0
    return pl.pallas_call(
        functools.partial(_compress_kernel, nb=nb),
        grid=(t // rows,),
        in_specs=[pl.BlockSpec((rows, width), lambda i: (i, 0)),
                  pl.BlockSpec((NSA_BLOCK, width), lambda i: (0, 0))],
        out_specs=pl.BlockSpec((nb, width), lambda i: (i, 0)),
        out_shape=jax.ShapeDtypeStruct((t // NSA_BLOCK, width), F32),
        compiler_params=_cparams(("parallel",)), name="nsa_compress",
    )(kv, wfull)


def _topk_select_bias(score, k):
    n = score.shape[0]
    rows = lax.broadcasted_iota(jnp.int32, score.shape, 0).astype(F32)
    bias = jnp.full(score.shape, NEG_INF, F32)
    for _ in range(k):
        mx = jnp.max(score, axis=0, keepdims=True)
        idx = jnp.min(jnp.where(score == mx, rows, float(n)), axis=0, keepdims=True)
        hit = rows == idx
        bias = jnp.where(hit, 0.0, bias)
        score = jnp.where(hit, LOWEST, score)
    return bias


def _nsa_prompt_kernel(slope_ref, qT_ref, gT_ref, kc_ref, vcT_ref, ksel_ref, vselT_ref,
                       kd_ref, vdT_ref, kw0, kw1, kw2, kw3, kw4, vw0, vw1, vw2, vw3, vw4, bw_ref,
                       out_ref, qs_ref, selb_ref, sa_ref, sb_ref, pa_ref, pb_ref, tiles_ref, *, nblk):
    i = pl.program_id(1)
    t0 = i * NSA_QBLOCK
    dh, qb, ncol, tk = NSA_HEAD_DIM, NSA_QBLOCK, NSA_COLS, NSA_KTILE
    slope = slope_ref[...]
    col = lax.broadcasted_iota(jnp.int32, (1, ncol), 1)
    tpos = t0 + (col & (qb - 1))
    tposf = tpos.astype(F32)
    q4 = qT_ref[...] * (dh ** -0.5)
    qT = jnp.concatenate([q4[g * dh:(g + 1) * dh, :] for g in range(NSA_GROUP)], axis=1)
    xrow = lax.broadcasted_iota(jnp.int32, (NSA_CDIM - dh, ncol), 0)

    qc = jnp.concatenate([qT, jnp.zeros((NSA_CDIM - dh, ncol), F32)], axis=0).astype(BF16)
    sc = jnp.dot(kc_ref[...], qc, preferred_element_type=F32)
    c_end = lax.broadcasted_iota(jnp.int32, (nblk, 1), 0) * NSA_BLOCK + (NSA_BLOCK - 1)
    valid = c_end <= tpos
    s = jnp.where(valid, sc - slope * (tposf - c_end.astype(F32)), NEG_INF)
    e = jnp.exp(s - jnp.max(s, axis=0, keepdims=True))
    p = jnp.where(valid, e / jnp.sum(e, axis=0, keepdims=True), 0.0)
    ocT = jnp.dot(vcT_ref[...], p.astype(BF16), preferred_element_type=F32)

    qw = jnp.concatenate([qT, jnp.where(xrow == 0, NEG_INF, 0.0)], axis=0).astype(BF16)
    kw = jnp.concatenate([kw0[...], kw1[...], kw2[...], kw3[...], kw4[...]], axis=0)
    sw = jnp.dot(kw, qw, preferred_element_type=F32) + bw_ref[...]
    ew = jnp.exp(sw - jnp.max(sw, axis=0, keepdims=True))
    vw = jnp.concatenate([vw0[...], vw1[...], vw2[...], vw3[...], vw4[...]], axis=1)
    owT = jnp.dot(vw, ew.astype(BF16), preferred_element_type=F32) / jnp.sum(ew, axis=0, keepdims=True)

    nbt = tk // NSA_BLOCK
    jl = t0 // tk
    s1 = slope.astype(BF16).astype(F32)
    r1 = slope - s1
    s2 = r1.astype(BF16).astype(F32)
    s3 = (r1 - s2).astype(BF16).astype(F32)
    half = float(tk // 2)
    ext = jnp.zeros((NSA_CDIM - dh, ncol), F32)
    for r, v in enumerate((s1, s2, s3, s1 * half, s2 * half, s3 * half)):
        ext = jnp.where(xrow == nbt + r, v, ext)

    qd = jnp.concatenate([qT, ext], axis=0).astype(BF16)
    sd = jnp.dot(kd_ref[...], qd, preferred_element_type=F32)
    kposd = t0 + lax.broadcasted_iota(jnp.int32, (qb, 1), 0)
    sd = jnp.where(kposd > tpos, NEG_INF, sd)
    mx_d = jnp.max(sd, axis=0, keepdims=True)
    pd = jnp.exp(sd - mx_d)
    m_d = mx_d + slope * (jl * tk - tpos).astype(F32)
    l_d = jnp.sum(pd, axis=0, keepdims=True)
    acc_d = jnp.dot(vdT_ref[...], pd.astype(BF16), preferred_element_type=F32)

    imp = p[:, 0:qb]
    for g in range(1, NSA_GROUP):
        imp = imp + p[:, g * qb:(g + 1) * qb]
    blk = lax.broadcasted_iota(jnp.int32, (nblk, qb), 0)
    cur = tpos[:, 0:qb] >> (NSA_BLOCK.bit_length() - 1)
    forced = (blk == 0) | (blk == cur) | (blk == cur - 1)
    score = jnp.where(blk > cur, -1.0, jnp.where(forced, NSA_GROUP + 1.0, imp))
    selb = _topk_select_bias(score, min(NSA_TOPK, nblk))
    blkc = lax.broadcasted_iota(jnp.int32, (nblk, ncol), 0)
    selb_ref[0:nblk, :] = jnp.where(blkc >= t0 // NSA_BLOCK, NEG_INF, jnp.concatenate([selb] * NSA_GROUP, axis=1))
    selb_ref[nblk:nblk + nbt, :] = jnp.full((nbt, ncol), NEG_INF, F32)

    fl = jnp.max(selb_ref[0:nblk, 0:qb].reshape(nblk // nbt, nbt, qb), axis=1)
    tiles_ref[0] = 0
    n_act = jnp.int32(0)
    for j in range(nblk // nbt):
        tiles_ref[n_act] = j
        n_act = n_act + (jnp.max(fl[j:j + 1, :]) > 0.5 * NEG_INF).astype(jnp.int32)
    npairs = (n_act + 1) // 2

    qs_ref[0:dh, :] = qT
    qs_ref[dh:, :] = ext

    def tile_id(t):
        return tiles_ref[jnp.clip(t, 0, jnp.maximum(n_act - 1, 0))]

    def qk_into(t, s_ref):
        tid = tile_id(t)
        b0 = pl.multiple_of(jnp.where(t < n_act, tid * nbt, nblk), 8)
        qs_ref[dh:dh + nbt, :] = selb_ref[pl.ds(b0, nbt), :]
        k0 = pl.multiple_of(tid * tk, tk)
        s_ref[...] = jnp.dot(ksel_ref[pl.ds(k0, tk), :], qs_ref[...].astype(BF16), preferred_element_type=F32)

    def soft(t, s_ref, p_ref, m):
        cj = slope * (tile_id(t) * tk - tpos).astype(F32)
        sj = s_ref[...]
        m_new = jnp.maximum(m, jnp.max(sj, axis=0, keepdims=True) + cj)
        p_ref[...] = jnp.exp(sj - (m_new - cj)).astype(BF16)
        return m_new, jnp.exp(m - m_new)

    def pv(t, p_ref):
        return jnp.dot(vselT_ref[tile_id(t)], p_ref[...], preferred_element_type=F32)

    def pair(i, carry):
        m, accp = carry
        ta = 2 * i
        pvb = pv(ta - 1, pb_ref)
        qk_into(ta + 1, sb_ref)
        m, alpha = soft(ta, sa_ref, pa_ref, m)
        accp = alpha * (accp + pvb)
        pva = pv(ta, pa_ref)
        qk_into(ta + 2, sa_ref)
        m, alpha = soft(ta + 1, sb_ref, pb_ref, m)
        accp = alpha * (accp + pva)
        return m, accp

    pb_ref[...] = jnp.zeros(pb_ref.shape, BF16)
    qk_into(0, sa_ref)
    vrows = vselT_ref.shape[1]
    init = (jnp.full((1, ncol), NEG_INF, F32), jnp.zeros((vrows, ncol), F32))
    m_s, accp = lax.fori_loop(0, npairs, pair, init)
    accl = accp + pv(2 * npairs - 1, pb_ref)
    acc_s, l_s = accl[0:dh], accl[dh:dh + 1]

    m_f = jnp.maximum(m_s, m_d)
    a_s = jnp.exp(m_s - m_f)
    a_d = jnp.exp(m_d - m_f)
    osT = (a_s * acc_s + a_d * acc_d) / (a_s * l_s + a_d * l_d)

    sg = jax.nn.sigmoid(gT_ref[...])

    def gate(c):
        return jnp.concatenate([sg[c * NSA_GROUP + g:c * NSA_GROUP + g + 1, :] for g in range(NSA_GROUP)], axis=1)

    oT = gate(0) * ocT + gate(1) * osT + gate(2) * owT
    o4 = jnp.concatenate([oT[:, g * qb:(g + 1) * qb] for g in range(NSA_GROUP)], axis=0)
    out_ref[...] = o4.T


def _alibi_slopes():
    h = jnp.arange(NSA_HEADS, dtype=F32)
    return (2.0 ** (-8.0 * (h + 1.0) / NSA_HEADS)).reshape(NSA_KV_HEADS, NSA_GROUP)


def _nsa_prompt(qT, gT, kvc, kv_sel, kv_win):
    s_len = qT.shape[1]
    dh, qb, tk, kvh = NSA_HEAD_DIM, NSA_QBLOCK, NSA_KTILE, NSA_KV_HEADS
    assert s_len % tk == 0
    nblk = s_len // NSA_BLOCK
    slopes = _alibi_slopes()
    slope_cols = jnp.repeat(slopes, qb, axis=1).reshape(kvh, 1, NSA_COLS)

    def heads_major(a):
        return a.reshape(a.shape[0], kvh, dh).transpose(1, 0, 2)

    def pad_lanes(a):
        return jnp.pad(a, ((0, 0), (0, 0), (0, NSA_CDIM - a.shape[-1])))

    kc = pad_lanes(heads_major(kvc[:, :NSA_KV_WIDTH])).astype(BF16)
    vcT = heads_major(kvc[:, NSA_KV_WIDTH:]).transpose(0, 2, 1).astype(BF16)
    r = jnp.arange(tk)
    onehot = (r[:, None] // NSA_BLOCK == jnp.arange(tk // NSA_BLOCK)[None, :]).astype(F32)
    lo = (r % (tk // 2)).astype(F32)[:, None]
    hi = (r // (tk // 2)).astype(F32)[:, None]
    kext = jnp.concatenate([onehot, lo, lo, lo, hi, hi, hi], axis=1)
    kext = jnp.tile(kext, (s_len // tk, 1))
    ksel = pad_lanes(jnp.concatenate(
        [heads_major(kv_sel[:, :NSA_KV_WIDTH]), jnp.broadcast_to(kext[None], (kvh,) + kext.shape)], axis=-1)).astype(BF16)
    vsel = heads_major(kv_sel[:, NSA_KV_WIDTH:]).astype(BF16)
    vselT = vsel.reshape(kvh, s_len // tk, tk, dh).transpose(0, 1, 3, 2)
    ones_rows = jnp.zeros((kvh, s_len // tk, 8, tk), BF16).at[:, :, 0, :].set(1.0)
    vselT = jnp.concatenate([vselT, ones_rows], axis=2)
    vselT_flat = vsel.transpose(0, 2, 1)
    kwin = heads_major(kv_win[:, :NSA_KV_WIDTH])
    kwin = jnp.concatenate([kwin, jnp.zeros((kvh, s_len, 1), F32)], axis=-1)
    padk = jnp.zeros((kvh, NSA_WINDOW, dh + 1), F32).at[:, :, dh].set(1.0)
    kwin = pad_lanes(jnp.concatenate([padk, kwin], axis=1)).astype(BF16)
    vwinT = jnp.pad(heads_major(kv_win[:, NSA_KV_WIDTH:]), ((0, 0), (NSA_WINDOW, 0), (0, 0))).transpose(0, 2, 1).astype(BF16)
    rr = jnp.arange(NSA_WKEYS)[:, None]
    cc = jnp.arange(NSA_COLS)[None, :]
    dw = (cc % qb) + NSA_WINDOW - rr
    bw = jnp.where((dw >= 0) & (dw <= NSA_WINDOW), -slope_cols * dw.astype(F32)[None], NEG_INF)

    nq = s_len // qb
    nwb = NSA_WKEYS // qb
    kw_specs = [pl.BlockSpec((None, qb, NSA_CDIM), functools.partial(lambda k, i, j: (k, i + j, 0), j=j)) for j in range(nwb)]
    vw_specs = [pl.BlockSpec((None, dh, qb), functools.partial(lambda k, i, j: (k, 0, i + j), j=j)) for j in range(nwb)]
    assert nwb == 5
    return pl.pallas_call(
        functools.partial(_nsa_prompt_kernel, nblk=nblk),
        grid=(kvh, nq),
        in_specs=[pl.BlockSpec((None, 1, NSA_COLS), lambda k, i: (k, 0, 0)),
                  pl.BlockSpec((NSA_GROUP * dh, qb), lambda k, i: (k, i)),
                  pl.BlockSpec((None, 16, qb), lambda k, i: (k, 0, i)),
                  pl.BlockSpec((None, nblk, NSA_CDIM), lambda k, i: (k, 0, 0)),
                  pl.BlockSpec((None, dh, nblk), lambda k, i: (k, 0, 0)),
                  pl.BlockSpec((None, s_len, NSA_CDIM), lambda k, i: (k, 0, 0)),
                  pl.BlockSpec((None, s_len // tk, dh + 8, tk), lambda k, i: (k, 0, 0, 0)),
                  pl.BlockSpec((None, qb, NSA_CDIM), lambda k, i: (k, i, 0)),
                  pl.BlockSpec((None, dh, qb), lambda k, i: (k, 0, i))]
                 + kw_specs + vw_specs
                 + [pl.BlockSpec((None, NSA_WKEYS, NSA_COLS), lambda k, i: (k, 0, 0))],
        out_specs=pl.BlockSpec((qb, NSA_GROUP * dh), lambda k, i: (i, k)),
        out_shape=jax.ShapeDtypeStruct((s_len, NSA_Q_WIDTH), F32),
        scratch_shapes=[pltpu.VMEM((NSA_CDIM, NSA_COLS), F32),
                        pltpu.VMEM((nblk + tk // NSA_BLOCK, NSA_COLS), F32),
                        pltpu.VMEM((tk, NSA_COLS), F32), pltpu.VMEM((tk, NSA_COLS), F32),
                        pltpu.VMEM((tk, NSA_COLS), BF16), pltpu.VMEM((tk, NSA_COLS), BF16),
                        pltpu.SMEM((s_len // tk + 1,), jnp.int32)],
        compiler_params=_cparams(("arbitrary", "arbitrary")), name="nsa_prompt",
    )(slope_cols, qT, gT.reshape(kvh, 16, s_len), kc, vcT, ksel, vselT, ksel, vselT_flat,
      *([kwin] * nwb), *([vwinT] * nwb), bw)


def _hgrn_chunk(qc, zf, vc, lb, st):
    c, sub = HG_CHUNK, HG_SUB
    logf = jnp.log(lb + (1.0 - lb) * jax.nn.sigmoid(zf))
    kc = (1.0 - lb) * jax.nn.sigmoid(-zf)
    tri = (lax.broadcasted_iota(jnp.int32, (c, c), 0) >= lax.broadcasted_iota(jnp.int32, (c, c), 1)).astype(F32)
    cb = jnp.dot(tri, logf, preferred_element_type=F32, precision=lax.Precision.HIGHEST)
    o = lax.dot_general((qc * jnp.exp(cb)).astype(BF16), st.astype(BF16), (((1,), (1,)), ((), ())),
                        preferred_element_type=F32)
    t3 = lax.broadcasted_iota(jnp.int32, (sub, sub, 1), 0) >= lax.broadcasted_iota(jnp.int32, (sub, sub, 1), 1)
    outs = []
    for a in range(c // sub):
        ra = slice(a * sub, (a + 1) * sub)
        cba, qa, ka, va = cb[ra], qc[ra], kc[ra], vc[ra]
        d3 = cba[:, None, :] - cba[None, :, :]
        x3 = jnp.where(t3, jnp.exp(d3), 0.0) * qa[:, None, :] * ka[None, :, :]
        att3 = jnp.sum(x3, axis=2, keepdims=True)
        oa = o[ra] + jnp.sum(att3 * va[None, :, :], axis=1)
        if a > 0:
            ref = cb[a * sub - 1:a * sub, :]
            qd = (qa * jnp.exp(cba - ref)).astype(BF16)
            kd = (kc[:a * sub] * jnp.exp(ref - cb[:a * sub])).astype(BF16)
            att = lax.dot_general(qd, kd, (((1,), (1,)), ((), ())), preferred_element_type=F32)
            oa = oa + jnp.dot(att.astype(BF16), vc[:a * sub].astype(BF16), preferred_element_type=F32)
        outs.append(oa)
    o = jnp.concatenate(outs, axis=0)
    last = cb[c - 1:c, :]
    kdec = (kc * jnp.exp(last - cb)).astype(BF16)
    st = st * jnp.exp(last) + lax.dot_general(vc.astype(BF16), kdec, (((0,), (0,)), ((), ())),
                                              preferred_element_type=F32)
    return o, st


def _hgrn_kernel(lb_ref, wn_ref, q_ref, f_ref, v_ref, g_ref, s0_ref, o_ref, sfin_ref, st_ref, *, nsub):
    c = pl.program_id(1)
    hpb = HG_HPB

    @pl.when(c == 0)
    def _():
        for j in range(hpb):
            st_ref[j] = s0_ref[j].T

    wn = wn_ref[...]

    def body(u, sts):
        rows = pl.ds(pl.multiple_of(u * HG_CHUNK, HG_CHUNK), HG_CHUNK)
        out = []
        for j in range(hpb):
            cols = slice(j * HG_DK, (j + 1) * HG_DK)
            o, st = _hgrn_chunk(q_ref[rows, cols], f_ref[rows, cols], v_ref[rows, cols], lb_ref[:, cols], sts[j])
            g = g_ref[rows, cols]
            o_ref[rows, cols] = _rms(o, wn) * (g * jax.nn.sigmoid(g))
            out.append(st)
        return tuple(out)

    sts = lax.fori_loop(0, nsub, body, tuple(st_ref[j] for j in range(hpb)))
    for j in range(hpb):
        st_ref[j] = sts[j]

    @pl.when(c == pl.num_programs(1) - 1)
    def _():
        for j in range(hpb):
            sfin_ref[j] = sts[j].T


def _hgrn_prompt(hq, hf, hi, hg, lb, wn, s0, tb):
    t = hq.shape[0]
    hpb = HG_HPB
    assert t % tb == 0 and tb % HG_CHUNK == 0 and HG_HEADS % hpb == 0 and HG_DK == HG_DV
    tok = pl.BlockSpec((tb, hpb * HG_DK), lambda h, c: (c, h))
    stt = pl.BlockSpec((hpb, HG_DK, HG_DV), lambda h, c: (h, 0, 0))
    return pl.pallas_call(
        functools.partial(_hgrn_kernel, nsub=tb // HG_CHUNK),
        grid=(HG_HEADS // hpb, t // tb),
        in_specs=[pl.BlockSpec((1, hpb * HG_DK), lambda h, c: (0, h)),
                  pl.BlockSpec((1, HG_DV), lambda h, c: (0, 0)),
                  tok, tok, tok, tok, stt],
        out_specs=[tok, stt],
        out_shape=[jax.ShapeDtypeStruct((t, HG_HEADS * HG_DV), F32),
                   jax.ShapeDtypeStruct((HG_HEADS, HG_DK, HG_DV), F32)],
        scratch_shapes=[pltpu.VMEM((hpb, HG_DV, HG_DK), F32)],
        compiler_params=_cparams(("arbitrary", "arbitrary")), name="hgrn_prompt",
    )(lb, wn, hq, hf, hi, hg, s0)


def _hgrn_step_kernel(lb_ref, wn_ref, q_ref, f_ref, v_ref, g_ref, s0_ref, o_ref, s_ref):
    wn = wn_ref[...]
    for h in range(HG_HEADS):
        rk = slice(h * HG_DK, (h + 1) * HG_DK)
        lb = lb_ref[rk, :]
        zf = f_ref[rk, :]
        f = lb + (1.0 - lb) * jax.nn.sigmoid(zf)
        kk = (1.0 - lb) * jax.nn.sigmoid(-zf)
        vrow = v_ref[:, h * HG_DV:(h + 1) * HG_DV]
        s_new = f * s0_ref[h] + kk * vrow
        s_ref[h] = s_new
        o = jnp.sum(s_new * q_ref[rk, :], axis=0, keepdims=True)
        g = g_ref[:, h * HG_DV:(h + 1) * HG_DV]
        o_ref[:, h * HG_DV:(h + 1) * HG_DV] = _rms(o, wn) * (g * jax.nn.sigmoid(g))


def _hgrn_step(hq_col, hf_col, hi, hg, lb_col, wn, s0):
    b = hi.shape[0]
    col = pl.BlockSpec((None, HG_WIDTH, 1), lambda i: (i, 0, 0))
    row = pl.BlockSpec((None, 1, HG_WIDTH), lambda i: (i, 0, 0))
    st = pl.BlockSpec((None, HG_HEADS, HG_DK, HG_DV), lambda i: (i, 0, 0, 0))
    return pl.pallas_call(
        _hgrn_step_kernel, grid=(b,),
        in_specs=[pl.BlockSpec((HG_WIDTH, 1), lambda i: (0, 0)), pl.BlockSpec((1, HG_DV), lambda i: (0, 0)),
                  col, col, row, row, st],
        out_specs=[row, st],
        out_shape=[jax.ShapeDtypeStruct((b, 1, HG_WIDTH), F32), jax.ShapeDtypeStruct(s0.shape, F32)],
        compiler_params=_cparams(("parallel",)), name="hgrn_step",
    )(lb_col, wn, hq_col, hf_col, hi, hg, s0)


def _merge_kernel(x_ref, oa_ref, ob_ref, ma_ref, mb_ref, pa_ref, pb_ref, wo_ref, y_ref):
    ya = jnp.dot(oa_ref[...].astype(BF16), pa_ref[...], preferred_element_type=F32)
    yb = jnp.dot(ob_ref[...].astype(BF16), pb_ref[...], preferred_element_type=F32)
    mix = jax.nn.sigmoid(ma_ref[...]) * ya + jax.nn.sigmoid(mb_ref[...]) * yb
    y_ref[...] = x_ref[...] + jnp.dot(mix.astype(BF16), wo_ref[...], preferred_element_type=F32)


def _merge(x, oa, ob, ma, mb, pa, pb, wo, tm):
    t, d = x.shape
    tok = pl.BlockSpec((tm, d), lambda i: (i, 0))
    wsp = pl.BlockSpec((d, d), lambda i: (0, 0))
    return pl.pallas_call(
        _merge_kernel, grid=(t // tm,),
        in_specs=[tok, tok, tok, tok, tok, wsp, wsp, wsp],
        out_specs=tok, out_shape=jax.ShapeDtypeStruct((t, d), F32),
        compiler_params=_cparams(("parallel",)), name="branch_merge",
    )(x, oa, ob, ma, mb, pa, pb, wo)


def _topk_rows(s, k):
    n = s.shape[0]
    rows = lax.broadcasted_iota(jnp.int32, s.shape, 0).astype(F32)
    rank = jnp.full(s.shape, float(n), F32)
    tops = []
    for r in range(k):
        mx = jnp.max(s, axis=0, keepdims=True)
        idx = jnp.min(jnp.where(s == mx, rows, float(n)), axis=0, keepdims=True)
        hit = rows == idx
        rank = jnp.where(hit, float(r), rank)
        s = jnp.where(hit, LOWEST, s)
        tops.append(mx)
    return jnp.concatenate(tops, axis=0), rank


_STAIR_GROUPS = ((0, 16), (1, 8), (2, 8), (3, 8))
_STAIR_QUAD = (4, 5, 6, 7)
_STAIR_TAIL = 8


def _stair_rows(t1, t2, op):
    tt = t1.shape[1]
    parts = [op(jnp.broadcast_to(t1[a:a + 1, :], (nb, tt)), t2[0:nb, :]) for a, nb in _STAIR_GROUPS]
    r16 = lax.broadcasted_iota(jnp.int32, (16, tt), 0)
    v1 = jnp.broadcast_to(t1[_STAIR_QUAD[3]:_STAIR_QUAD[3] + 1, :], (16, tt))
    v2 = jnp.broadcast_to(t2[3:4, :], (16, tt))
    for q in (2, 1, 0):
        v1 = jnp.where(r16 < 4 * (q + 1), jnp.broadcast_to(t1[_STAIR_QUAD[q]:_STAIR_QUAD[q] + 1, :], (16, tt)), v1)
        v2 = jnp.where((r16 & 3) == q, jnp.broadcast_to(t2[q:q + 1, :], (16, tt)), v2)
    parts.append(op(v1, v2))
    parts.append(op(t1[_STAIR_TAIL:, :], jnp.broadcast_to(t2[0:1, :], (PK_TOPK - _STAIR_TAIL, tt))))
    return jnp.concatenate(parts, axis=0)


def _stair_row_counts(selc):
    out, r0 = [], 0
    for _, nb in _STAIR_GROUPS:
        out.append(jnp.sum(selc[r0:r0 + nb, :], axis=0, keepdims=True))
        r0 += nb
    quad = selc[r0:r0 + 16, :]
    r16 = lax.broadcasted_iota(jnp.int32, quad.shape, 0)
    for q in range(4):
        out.append(jnp.sum(jnp.where((r16 >> 2) == q, quad, 0.0), axis=0, keepdims=True))
    r0 += 16
    for a in range(PK_TOPK - _STAIR_TAIL):
        out.append(selc[r0 + a:r0 + a + 1, :])
    return out


def _peer_kernel(x_ref, wn_ref, wq_ref, sk_ref, u_ref, vT_ref, wf_ref, y_ref,
                 hnT_ref, acc_ref, wcat_ref, wodd_ref, s_ref, n_ref, a1_ref, r2_ref, e2_ref, *, ib, nkeys, final_norm):
    i = pl.program_id(1)
    kt = PK_TOPK
    hd = PK_DIM // 2

    @pl.when(i == 0)
    def _():
        hnT = _rms(x_ref[...], wn_ref[...]).T.astype(BF16)
        hnT_ref[...] = hnT
        acc_ref[...] = jnp.zeros(acc_ref.shape, F32)
        wcat_ref[...] = jnp.zeros(wcat_ref.shape, BF16)
        wodd_ref[...] = jnp.zeros(wodd_ref.shape, BF16)
        tt = hnT.shape[1]
        nlc = tt // LANES
        lcu = 2 if nlc % 2 == 0 else 1
        for h in range(PK_HEADS):
            for c in range(2):
                r0 = (h * 2 + c) * hd
                qhc = jnp.dot(wq_ref[r0:r0 + hd, :], hnT, preferred_element_type=F32)
                s = jnp.dot(sk_ref[h * 2 + c], qhc.astype(BF16), preferred_element_type=F32)
                for lc in range(nlc):
                    s_ref[c, lc] = s[:, lc * LANES:(lc + 1) * LANES]

            def chunk(lc, carry):
                for sub in range(lcu):
                    chunk_one(lc * lcu + sub)
                return carry

            def chunk_one(lc):
                s0, s1 = s_ref[0, lc], s_ref[1, lc]
                top0, rank0 = _topk_rows(s0, kt)
                top1, rank1 = _topk_rows(s1, kt)
                cand = _stair_rows(top0, top1, jnp.add)
                _, crank = _topk_rows(cand, kt)
                selc = (crank < float(kt)).astype(F32)
                n_a = _stair_row_counts(selc)
                e1t = jnp.exp(top0 - top0[0:1, :])
                e2t = jnp.exp(top1 - top1[0:1, :])
                z = jnp.sum(selc * _stair_rows(e1t, e2t, jnp.multiply), axis=0, keepdims=True)
                nfull = jnp.zeros(s0.shape, F32)
                for a in range(kt):
                    nfull = jnp.where(rank0 == float(a), n_a[a], nfull)
                n_ref[h, lc] = nfull
                a1_ref[h, lc] = jnp.exp(s0 - top0[0:1, :]) / z
                r2_ref[h, lc] = rank1.astype(BF16)
                e2_ref[h, lc] = jnp.exp(s1 - top1[0:1, :]).astype(BF16)

            lax.fori_loop(0, nlc // lcu, chunk, 0)

    nsteps = pl.num_programs(1) - 1
    nch = 2
    cw = hnT_ref.shape[1] // nch

    def step(w_read, w_write):
        hnT = hnT_ref[...]
        nlc = hnT.shape[1] // LANES

        def lanes_cat(ref, h, rows=slice(None)):
            return jnp.concatenate([ref[h, lc, rows, :] for lc in range(nlc)], axis=1)

        for ip in range(ib // 2):
            if ip % (ib // 2 // nch) == 0:
                ch = ip // (ib // 2 // nch)
                cols = slice(ch * cw, (ch + 1) * cw)
                acc_ref[:, cols] += jnp.dot(vT_ref[...], w_read[:, cols], preferred_element_type=F32)
            iis = (2 * ip, 2 * ip + 1)
            aTs = [jnp.dot(u_ref[ii * nkeys:(ii + 1) * nkeys, :], hnT, preferred_element_type=F32) for ii in iis]
            gsums = [jnp.zeros(aTs[0].shape, BF16) for _ in iis]
            for h in range(PK_HEADS):
                r2 = lanes_cat(r2_ref, h)
                e2 = lanes_cat(e2_ref, h)
                for n, ii in enumerate(iis):
                    row = pl.ds(i * ib + ii, 1)
                    nrow = lanes_cat(n_ref, h, row).astype(BF16)
                    arow = lanes_cat(a1_ref, h, row).astype(BF16)
                    gsums[n] = gsums[n] + jnp.where(r2 < nrow, arow * e2, jnp.zeros((), BF16))
            for n, ii in enumerate(iis):
                w_write[ii * nkeys:(ii + 1) * nkeys, :] = (jax.nn.gelu(aTs[n]) * gsums[n].astype(F32)).astype(BF16)

    @pl.when((i < nsteps) & (i % 2 == 0))
    def _():
        step(wodd_ref, wcat_ref)

    @pl.when((i < nsteps) & (i % 2 == 1))
    def _():
        step(wcat_ref, wodd_ref)

    @pl.when(i == nsteps)
    def _():
        w_last = wodd_ref if (nkeys // ib) % 2 == 0 else wcat_ref
        acc = acc_ref[...] + jnp.dot(vT_ref[...], w_last[...], preferred_element_type=F32)
        y = x_ref[...] + acc.T
        if final_norm:
            y = _rms(y, wf_ref[...])
        y_ref[...] = y


def _peer(x, wn, wqT, sk, u, vT, wf, tt, ib, final_norm):
    t, d = x.shape
    nkeys = sk.shape[1]
    assert t % tt == 0 and nkeys % ib == 0
    nsteps = nkeys // ib
    nlc = tt // LANES
    stat = pltpu.VMEM((PK_HEADS, nlc, nkeys, LANES), F32)
    stat16 = pltpu.VMEM((PK_HEADS, nlc, nkeys, LANES), BF16)
    wbuf = pltpu.VMEM((ib * nkeys, tt), BF16)
    return pl.pallas_call(
        functools.partial(_peer_kernel, ib=ib, nkeys=nkeys, final_norm=final_norm),
        grid=(t // tt, nsteps + 1),
        in_specs=[pl.BlockSpec((tt, d), lambda a, i: (a, 0)),
                  pl.BlockSpec((1, d), lambda a, i: (0, 0)),
                  pl.BlockSpec(wqT.shape, lambda a, i: (0, 0)),
                  pl.BlockSpec(sk.shape, lambda a, i: (0, 0, 0)),
                  pl.BlockSpec((ib * nkeys, d), lambda a, i: (jnp.minimum(i, nsteps - 1), 0)),
                  pl.BlockSpec((None, d, ib * nkeys), lambda a, i: (jnp.maximum(i - 1, 0), 0, 0)),
                  pl.BlockSpec((1, d), lambda a, i: (0, 0))],
        out_specs=pl.BlockSpec((tt, d), lambda a, i: (a, 0)),
        out_shape=jax.ShapeDtypeStruct((t, d), F32),
        scratch_shapes=[pltpu.VMEM((d, tt), BF16), pltpu.VMEM((d, tt), F32), wbuf, wbuf,
                        pltpu.VMEM((2, nlc, nkeys, LANES), F32), stat, stat, stat16, stat16],
        compiler_params=_cparams(("arbitrary", "arbitrary")), name="peer_dense",
    )(x, wn, wqT, sk, u, vT, wf)


def _compress_pages_kernel(pt_ref, *refs, pg, bpp):
    w_ref, o_ref = refs[pg], refs[pg + 1]
    nt = (((1,), (1,)), ((), ()))
    for j in range(pg):
        x = refs[j][...]
        halves = [lax.dot_general(w_ref[c], x[c].reshape(-1, x.shape[-1]).astype(BF16), nt, preferred_element_type=F32)
                  for c in range(x.shape[0])]
        o_ref[j * bpp:(j + 1) * bpp, :] = jnp.concatenate(halves, axis=1)


def _compress_pages(cache_t, page_table, w_sel, pg):
    tail = cache_t.shape[1:]
    db, npages = page_table.shape
    bpp = w_sel.shape[1]
    width = tail[0] * tail[1] * tail[2]
    assert npages % pg == 0
    zeros = (0,) * len(tail)
    page_specs = [pl.BlockSpec((None,) + tail, functools.partial(lambda b, g, pt, j: (pt[b, g * pg + j],) + zeros, j=j))
                  for j in range(pg)]
    return pl.pallas_call(
        functools.partial(_compress_pages_kernel, pg=pg, bpp=bpp),
        grid_spec=pltpu.PrefetchScalarGridSpec(
            num_scalar_prefetch=1, grid=(db, npages // pg),
            in_specs=page_specs + [pl.BlockSpec(w_sel.shape, lambda b, g, pt: (0, 0, 0))],
            out_specs=pl.BlockSpec((None, pg * bpp, width), lambda b, g, pt: (b, g, 0))),
        out_shape=jax.ShapeDtypeStruct((db, npages * bpp, width), F32),
        compiler_params=_cparams(("arbitrary", "arbitrary")), name="nsa_compress_pages",
    )(page_table, *([cache_t] * pg), w_sel)


def _nsa_decode_head_kernel(slope_ref, qbd_ref, g_ref, kvc_ref, wnew_ref, cwin_ref, part_ref, idx_ref,
                            *, past_len, k_past):
    kvw, grp, dh = NSA_KV_WIDTH, NSA_GROUP, NSA_HEAD_DIM
    qbd = qbd_ref[...]
    qb = qbd.astype(BF16)
    slope = slope_ref[...]
    nh = qbd.shape[0]
    npb = kvc_ref.shape[0]
    nt = (((1,), (1,)), ((), ()))
    kvc = kvc_ref[...]
    sc = lax.dot_general(qb, kvc[:, :kvw].astype(BF16), nt, preferred_element_type=F32)
    c_end = lax.broadcasted_iota(jnp.int32, (1, npb), 1) * NSA_BLOCK + (NSA_BLOCK - 1)
    s = sc - slope * (past_len - c_end).astype(F32)
    e = jnp.exp(s - jnp.max(s, axis=1, keepdims=True))
    p = e / jnp.sum(e, axis=1, keepdims=True)
    oc = jnp.dot(p.astype(BF16), kvc[:, kvw:].astype(BF16), preferred_element_type=F32)
    rows = []
    for k in range(NSA_KV_HEADS):
        r = p[k * grp:k * grp + 1, :]
        for g in range(1, grp):
            r = r + p[k * grp + g:k * grp + g + 1, :]
        rows.append(jnp.broadcast_to(r, (grp, npb)))
    imp = jnp.concatenate(rows, axis=0)
    blk = lax.broadcasted_iota(jnp.int32, (nh, npb), 1)
    score = jnp.where((blk == 0) | (blk == npb - 1), grp + 1.0, imp)
    lanes = blk.astype(F32)
    picks = []
    for _ in range(k_past):
        mx = jnp.max(score, axis=1, keepdims=True)
        idx = jnp.min(jnp.where(score == mx, lanes, float(npb)), axis=1, keepdims=True)
        score = jnp.where(lanes == idx, LOWEST, score)
        picks.append(idx)
    idx_ref[...] = jnp.concatenate(picks, axis=1).astype(jnp.int32)
    sg = jax.nn.sigmoid(g_ref[...])
    wnew = wnew_ref[...]
    nw = cwin_ref.shape[-1]
    dw = (nw - lax.broadcasted_iota(jnp.int32, (1, nw), 1)).astype(F32)
    for k in range(NSA_KV_HEADS):
        hs, ds = slice(k * grp, (k + 1) * grp), slice(k * dh, (k + 1) * dh)
        qk = qbd[hs, ds]
        sw = jnp.dot(qk.astype(BF16), cwin_ref[0, k].astype(BF16), preferred_element_type=F32) - slope[hs] * dw
        s_n = jnp.sum(qk * wnew[:, ds], axis=1, keepdims=True)
        m = jnp.maximum(jnp.max(sw, axis=1, keepdims=True), s_n)
        ew = jnp.exp(sw - m)
        en = jnp.exp(s_n - m)
        pv = lax.dot_general(ew.astype(BF16), cwin_ref[1, k].astype(BF16), nt, preferred_element_type=F32)
        ow = (pv + en * wnew[:, kvw + k * dh:kvw + (k + 1) * dh]) / (jnp.sum(ew, axis=1, keepdims=True) + en)
        part_ref[hs, :] = sg[hs, 0:1] * oc[hs, ds] + sg[hs, 2:3] * ow


def _nsa_decode_gather_kernel(pt_ref, ix_ref, slope_ref, q_ref, g_ref, part_ref, new_ref, *refs, nsel, bpp, past_len):
    blocks, o_ref = refs[:nsel], refs[nsel]
    b, k = pl.program_id(0), pl.program_id(1)
    page = blocks[0].shape[-1]
    q = q_ref[...]
    slope = slope_ref[...]
    kt = jnp.concatenate([blocks[s][0] for s in range(nsel)], axis=1).astype(BF16)
    vt = jnp.concatenate([blocks[s][1] for s in range(nsel)], axis=1).astype(BF16)
    sc = jnp.dot(q.astype(BF16), kt, preferred_element_type=F32)
    lane = lax.broadcasted_iota(jnp.int32, (1, page), 1)
    lblk = lane // NSA_BLOCK
    lkey = lane - lblk * NSA_BLOCK
    bias = []
    for s in range(nsel):
        ib = ix_ref[b, k, s]
        kpos = ib * NSA_BLOCK + lkey
        bias.append(jnp.where(lblk == ib % bpp, -slope * (past_len - kpos).astype(F32), NEG_INF))
    sc = sc + jnp.concatenate(bias, axis=1)
    k_own, v_own = new_ref[0], new_ref[1]
    s_n = jnp.sum(q * k_own, axis=1, keepdims=True)
    m = jnp.maximum(jnp.max(sc, axis=1, keepdims=True), s_n)
    p = jnp.exp(sc - m)
    p_n = jnp.exp(s_n - m)
    pv = lax.dot_general(p.astype(BF16), vt, (((1,), (1,)), ((), ())), preferred_element_type=F32)
    o_s = (pv + p_n * v_own) / (jnp.sum(p, axis=1, keepdims=True) + p_n)
    o_ref[...] = part_ref[...] + jax.nn.sigmoid(g_ref[...])[:, 1:2] * o_s


def _rows_minor(a):
    return a.transpose(0, 2, 3, 4, 1)


def _nsa_sample(q, kv_cmp, kv_sel, kv_win, glog, cache_cmp, cache_sel, cache_win, page_table, w_cmp):
    db = q.shape[0]
    kvh, grp, dh, kvw = NSA_KV_HEADS, NSA_GROUP, NSA_HEAD_DIM, NSA_KV_WIDTH
    page = cache_cmp.shape[1]
    npages = page_table.shape[1]
    past_len = npages * page
    nwin = cache_win.shape[1]
    assert past_len >= nwin and page % NSA_BLOCK == 0
    npb = past_len // NSA_BLOCK
    bpp = page // NSA_BLOCK
    pg = 8 if npages % 8 == 0 else 1
    width = 2 * kvw
    rowblk = jnp.arange(page) // NSA_BLOCK
    w_sel = jnp.where(rowblk[None, None, :] == jnp.arange(bpp)[None, :, None],
                      jnp.tile(w_cmp, (1, bpp))[:, None, :], 0.0).astype(BF16)
    kvc = _compress_pages(_rows_minor(cache_cmp), page_table, w_sel, pg)
    q4 = q.reshape(db, kvh, grp, dh) * (dh ** -0.5)
    qbd = (q4[:, :, :, None, :] * jnp.eye(kvh, dtype=F32)[None, :, None, :, None]).reshape(db, kvh * grp, kvw)
    slope = _alibi_slopes().reshape(kvh * grp, 1)
    nh = kvh * grp
    k_past = min(NSA_TOPK, npb + 1) - 1
    full2 = lambda shape: pl.BlockSpec(shape, lambda b: (0,) * len(shape))
    per_b = lambda shape: pl.BlockSpec((None,) + shape, lambda b: (b,) + (0,) * len(shape))
    part, idx = pl.pallas_call(
        functools.partial(_nsa_decode_head_kernel, past_len=past_len, k_past=k_past),
        grid=(db,),
        in_specs=[full2((nh, 1)), per_b((nh, kvw)), per_b((nh, 3)), per_b((npb, width)), per_b((1, width)),
                  per_b((2, kvh, dh, nwin))],
        out_specs=[per_b((nh, dh)), per_b((nh, k_past))],
        out_shape=[jax.ShapeDtypeStruct((db, nh, dh), F32), jax.ShapeDtypeStruct((db, nh, k_past), jnp.int32)],
        compiler_params=_cparams(("parallel",)), name="nsa_decode_head",
    )(slope, qbd, glog, kvc, kv_win.reshape(db, 1, width), _rows_minor(cache_win))
    idx4 = idx.reshape(db, kvh, grp, k_past)[:, :, 0, :]
    blk_specs = [pl.BlockSpec((None, 2, None, dh, page),
                              functools.partial(lambda b, k, pt, ix, s: (pt[b, ix[b, k, s] // bpp], 0, k, 0, 0), s=s))
                 for s in range(k_past)]
    hsp = lambda shape: pl.BlockSpec((None, None) + shape, lambda b, k, pt, ix: (b, k) + (0,) * len(shape))
    o = pl.pallas_call(
        functools.partial(_nsa_decode_gather_kernel, nsel=k_past, bpp=bpp, past_len=past_len),
        grid_spec=pltpu.PrefetchScalarGridSpec(
            num_scalar_prefetch=2, grid=(db, kvh),
            in_specs=[pl.BlockSpec((None, grp, 1), lambda b, k, pt, ix: (k, 0, 0)), hsp((grp, dh)), hsp((grp, 3)),
                      hsp((grp, dh)),
                      pl.BlockSpec((None, 2, None, 1, dh), lambda b, k, pt, ix: (b, 0, k, 0, 0))] + blk_specs,
            out_specs=hsp((grp, dh))),
        out_shape=jax.ShapeDtypeStruct((db, kvh, grp, dh), F32),
        compiler_params=_cparams(("arbitrary", "arbitrary")), name="nsa_decode_gather",
    )(page_table, idx4, slope.reshape(kvh, grp, 1), q4, glog.reshape(db, kvh, grp, 3), part.reshape(db, kvh, grp, dh),
      kv_sel.reshape(db, 2, kvh, 1, dh), *([_rows_minor(cache_sel)] * k_past))
    new_win = jnp.concatenate([cache_win, kv_win.reshape((db, 1) + cache_win.shape[2:])], axis=1)[:, 1:]
    return o.reshape(db, nh * dh), new_win


def _split_w_in(w_in):
    pts, acc = [], 0
    for w in (NSA_Q_WIDTH, 6 * NSA_KV_WIDTH, 3 * NSA_HEADS, HG_WIDTH, HG_WIDTH, HG_WIDTH, HG_WIDTH, w_in.shape[0]):
        acc += w
        pts.append(acc)
    return jnp.split(w_in, pts, axis=1)


def _prep_layer(w_in, w_proj_nsa, w_proj_hgrn, w_out, w_peer_q, peer_sub_keys, peer_u, peer_v):
    d = w_in.shape[0]
    wq, wkv, wg, wbq, wbf, wbi, wbg, wma, wmb = _split_w_in(w_in)
    wg = wg.reshape(d, NSA_KV_HEADS, NSA_GROUP, 3).transpose(0, 1, 3, 2).reshape(d, NSA_KV_HEADS, 3 * NSA_GROUP)
    wg = jnp.pad(wg, ((0, 0), (0, 0), (0, 16 - 3 * NSA_GROUP))).reshape(d, NSA_KV_HEADS * 16)
    wg = jnp.pad(wg, ((0, 0), (0, LANES - NSA_KV_HEADS * 16)))
    nkeys = peer_sub_keys.shape[2]
    return dict(
        w_qg=jnp.concatenate([wq, wg], axis=1).astype(BF16),
        w_kv=wkv.astype(BF16),
        w_hg=jnp.concatenate([wbq, wbf, wbi, wbg], axis=1).astype(BF16),
        w_m=jnp.concatenate([wma, wmb], axis=1).astype(BF16),
        pa=w_proj_nsa.astype(BF16), pb=w_proj_hgrn.astype(BF16), wo=w_out.astype(BF16),
        wqT=w_peer_q.T.astype(BF16),
        sk=peer_sub_keys.reshape(PK_HEADS * 2, nkeys, PK_DIM // 2).astype(BF16),
        u=peer_u.astype(BF16),
        vT=peer_v.astype(BF16).reshape(nkeys // PEER_IB, PEER_IB * nkeys, d).transpose(0, 2, 1),
    )


def _tile(t, pref):
    return pref if t % pref == 0 else t


def _peer_tokens(x, wn, wf, prm, final_norm):
    t = x.shape[0]
    tp = -(-t // LANES) * LANES
    xp = jnp.pad(x, ((0, tp - t), (0, 0)))
    tt = PEER_TT if tp % PEER_TT == 0 else LANES
    y = _peer(xp, wn, prm["wqT"], prm["sk"], prm["u"], prm["vT"], wf, tt, PEER_IB, final_norm)
    return y[:t]


def _layer_prompt(x, lb, prm, w_norm_mix, w_cmp, w_hgrn_norm, w_norm_ffn, w_norm_final, final_norm):
    s_len, d = x.shape
    tm = _tile(s_len, 256)
    kvw = NSA_KV_WIDTH
    kv_cmp, kv_sel, kv_win = _proj(x, w_norm_mix, prm["w_kv"], [2 * kvw] * 3, [False] * 3, tm)
    qT, gT = _proj(x, w_norm_mix, prm["w_qg"], [NSA_Q_WIDTH, LANES], [True, True], tm)
    hq, hf, hi, hg = _proj(x, w_norm_mix, prm["w_hg"], [HG_WIDTH] * 4, [False] * 4, tm)
    ma, mb = _proj(x, w_norm_mix, prm["w_m"], [d, d], [False, False], tm)
    wfull = jnp.concatenate([jnp.broadcast_to(w_cmp[0][:, None], (NSA_BLOCK, kvw)),
                             jnp.broadcast_to(w_cmp[1][:, None], (NSA_BLOCK, kvw))], axis=1)
    nblk = s_len // NSA_BLOCK
    kvc = _compress(kv_cmp, wfull, 8 if nblk % 8 == 0 else nblk)
    o_a = _nsa_prompt(qT, gT[:NSA_KV_HEADS * 16], kvc, kv_sel, kv_win)
    s0 = jnp.zeros((HG_HEADS, HG_DK, HG_DV), F32)
    o_b, s_fin = _hgrn_prompt(hq, hf, hi, hg, lb, w_hgrn_norm, s0, _tile(s_len, 512))
    x1 = _merge(x, o_a, o_b, ma, mb, prm["pa"], prm["pb"], prm["wo"], tm)
    x2 = _peer_tokens(x1, w_norm_ffn, w_norm_final, prm, final_norm)
    shp = (s_len, 2, NSA_KV_HEADS, NSA_HEAD_DIM)
    return x2, kv_cmp.reshape(shp), kv_sel.reshape(shp), kv_win.reshape(shp)[-NSA_WINDOW:], s_fin


def _layer_sample(x, lb, prm, w_norm_mix, w_cmp, w_hgrn_norm, w_norm_ffn, w_norm_final, final_norm,
                  cache_cmp, cache_sel, cache_win, s0, page_table):
    b, d = x.shape
    kvw = NSA_KV_WIDTH
    w_q = prm["w_qg"][:, :NSA_Q_WIDTH]
    (q,) = _proj(x, w_norm_mix, w_q, [NSA_Q_WIDTH], [False], b)
    kv_cmp, kv_sel, kv_win = _proj(x, w_norm_mix, prm["w_kv"], [2 * kvw] * 3, [False] * 3, b)
    (gTt,) = _proj(x, w_norm_mix, prm["w_qg"][:, NSA_Q_WIDTH:], [LANES], [False], b)
    hq, hf, hi, hg = _proj(x, w_norm_mix, prm["w_hg"], [HG_WIDTH] * 4, [False] * 4, b)
    ma, mb = _proj(x, w_norm_mix, prm["w_m"], [d, d], [False, False], b)
    shp = (b, 1, 2, NSA_KV_HEADS, NSA_HEAD_DIM)
    glog = (gTt[:, :NSA_KV_HEADS * 16].reshape(b, NSA_KV_HEADS, 16)[..., :3 * NSA_GROUP]
            .reshape(b, NSA_KV_HEADS, 3, NSA_GROUP).transpose(0, 1, 3, 2).reshape(b, NSA_HEADS, 3))
    o_a, new_win = _nsa_sample(q, kv_cmp, kv_sel, kv_win, glog, cache_cmp, cache_sel, cache_win, page_table, w_cmp)
    o_b, s_new = _hgrn_step(hq.reshape(b, HG_WIDTH, 1), hf.reshape(b, HG_WIDTH, 1), hi.reshape(b, 1, HG_WIDTH),
                            hg.reshape(b, 1, HG_WIDTH), lb.reshape(HG_WIDTH, 1), w_hgrn_norm, s0)
    x1 = _merge(x, o_a.reshape(b, NSA_Q_WIDTH), o_b.reshape(b, HG_WIDTH), ma, mb, prm["pa"], prm["pb"], prm["wo"], b)
    x2 = _peer_tokens(x1, w_norm_ffn, w_norm_final, prm, final_norm)
    return x2, kv_cmp.reshape(shp), kv_sel.reshape(shp), new_win, s_new


def kernel(x_prompt, x_sample, cache_cmp_kv, cache_sel_kv, cache_win_kv, state_hgrn, page_table,
           w_norm_mix, w_in, w_cmp, w_proj_nsa, w_proj_hgrn, w_hgrn_norm, hgrn_lb_logits, w_out,
           w_norm_ffn, w_peer_q, peer_sub_keys, peer_u, peer_v, w_norm_final):
    depth = w_in.shape[0]
    bsz, s_len, d = x_prompt.shape
    db, dt, _ = x_sample.shape
    assert dt == 1
    lbs = jnp.cumsum(jax.nn.softmax(hgrn_lb_logits.astype(F32), axis=0), axis=0)
    wfin = w_norm_final.reshape(1, d)
    xp = [x_prompt[b] for b in range(bsz)]
    xs = x_sample.reshape(db, d)
    st_p, st_s = [], []
    for l in range(depth):
        last = l == depth - 1
        prm = _prep_layer(w_in[l], w_proj_nsa[l], w_proj_hgrn[l], w_out[l], w_peer_q[l], peer_sub_keys[l],
                          peer_u[l], peer_v[l])
        shared = (lbs[l].reshape(1, HG_WIDTH), prm, w_norm_mix[l].reshape(1, d), w_cmp[l],
                  w_hgrn_norm[l].reshape(1, HG_DV), w_norm_ffn[l].reshape(1, d), wfin, last)
        outs = [_layer_prompt(xp[b], *shared) for b in range(bsz)]
        xp = [o[0] for o in outs]
        st_p.append(tuple(jnp.stack([o[k] for o in outs]) for k in range(1, 5)))
        xs, *ss = _layer_sample(xs, *shared, cache_cmp_kv[l], cache_sel_kv[l], cache_win_kv[l], state_hgrn[l],
                                page_table)
        st_s.append(tuple(ss))
    y_prompt = jnp.stack(xp)
    y_sample = xs.reshape(db, dt, d)
    return (y_prompt, y_sample,
            jnp.stack([s[0] for s in st_p]), jnp.stack([s[1] for s in st_p]),
            jnp.stack([s[2] for s in st_p]), jnp.stack([s[3] for s in st_p]),
            jnp.stack([s[0] for s in st_s]), jnp.stack([s[1] for s in st_s]),
            jnp.stack([s[2] for s in st_s]), jnp.stack([s[3] for s in st_s]))
```

```python
import functools

import jax
import jax.numpy as jnp
from jax import lax
from jax.experimental import pallas as pl
from jax.experimental.pallas import tpu as pltpu

F32 = jnp.float32
BF16 = jnp.bfloat16

NSA_HEADS = 16
NSA_KV_HEADS = 4
NSA_GROUP = NSA_HEADS // NSA_KV_HEADS
NSA_HEAD_DIM = 64
NSA_BLOCK = 64
NSA_TOPK = 16
NSA_WINDOW = 512
NSA_QBLOCK = 128
HG_HEADS = 8
HG_DK = 128
HG_DV = 128
HG_CHUNK = 64
HG_SUB = 16
HG_HPB = 8
PK_HEADS = 8
PK_DIM = 256
PK_TOPK = 16
RMS_EPS = 1e-6
NEG_INF = -1e30
LOWEST = -3e38
TAKEN = 2.0 ** 100

NSA_Q_WIDTH = NSA_HEADS * NSA_HEAD_DIM
NSA_KV_WIDTH = NSA_KV_HEADS * NSA_HEAD_DIM
HG_WIDTH = HG_HEADS * HG_DK

LANES = 128
VMEM_LIMIT_BYTES = 56 * 1024 * 1024

NSA_COLS = NSA_GROUP * NSA_QBLOCK
NSA_KTILE = 512
NSA_CDIM = 128
NSA_WKEYS = NSA_WINDOW + NSA_QBLOCK
PEER_TT = 512
PEER_IB = 8


def _cparams(sem):
    return pltpu.CompilerParams(dimension_semantics=sem, vmem_limit_bytes=VMEM_LIMIT_BYTES)


def _rms(x, w):
    return x * lax.rsqrt(jnp.mean(x * x, axis=-1, keepdims=True) + RMS_EPS) * w


def _proj_kernel(x_ref, wn_ref, w_ref, *out_refs, widths, transposed, chunk):
    hb = _rms(x_ref[...], wn_ref[...]).astype(BF16)
    off = 0
    for o_ref, wd, tr in zip(out_refs, widths, transposed):
        for c0 in range(0, wd, chunk):
            cw = min(chunk, wd - c0)
            r = jnp.dot(hb, w_ref[:, off + c0:off + c0 + cw], preferred_element_type=F32)
            if tr:
                o_ref[c0:c0 + cw, :] = r.T
            else:
                o_ref[:, c0:c0 + cw] = r
        off += wd


def _proj(x, wn, w, widths, transposed, tm):
    t, d = x.shape
    n = w.shape[1]
    assert sum(widths) == n and t % tm == 0
    out_shape, out_specs = [], []
    for wd, tr in zip(widths, transposed):
        if tr:
            out_shape.append(jax.ShapeDtypeStruct((wd, t), F32))
            out_specs.append(pl.BlockSpec((wd, tm), lambda i: (0, i)))
        else:
            out_shape.append(jax.ShapeDtypeStruct((t, wd), F32))
            out_specs.append(pl.BlockSpec((tm, wd), lambda i: (i, 0)))
    return pl.pallas_call(
        functools.partial(_proj_kernel, widths=tuple(widths), transposed=tuple(transposed), chunk=512),
        grid=(t // tm,),
        in_specs=[pl.BlockSpec((tm, d), lambda i: (i, 0)),
                  pl.BlockSpec((1, d), lambda i: (0, 0)),
                  pl.BlockSpec((d, n), lambda i: (0, 0))],
        out_specs=out_specs, out_shape=out_shape,
        compiler_params=_cparams(("parallel",)), name="rms_proj",
    )(x, wn, w)


def _compress_kernel(kv_ref, w_ref, o_ref, *, nb):
    x = kv_ref[...]
    width = x.shape[-1]
    x3 = x.reshape(nb, NSA_BLOCK, width) * w_ref[...][None]
    o_ref[...] = jnp.sum(x3, axis=1)


def _compress(kv, wfull, nb):
    t, width = kv.shape
    rows = nb * NSA_BLOCK
    assert t % rows == 0
    return pl.pallas_call(
        functools.partial(_compress_kernel, nb=nb),
        grid=(t // rows,),
        in_specs=[pl.BlockSpec((rows, width), lambda i: (i, 0)),
                  pl.BlockSpec((NSA_BLOCK, width), lambda i: (0, 0))],
        out_specs=pl.BlockSpec((nb, width), lambda i: (i, 0)),
        out_shape=jax.ShapeDtypeStruct((t // NSA_BLOCK, width), F32),
        compiler_params=_cparams(("parallel",)), name="nsa_compress",
    )(kv, wfull)


def _topk_select_bias(score, k):
    n = score.shape[0]
    rows = lax.broadcasted_iota(jnp.int32, score.shape, 0).astype(F32)
    for _ in range(k):
        mx = jnp.max(score, axis=0, keepdims=True)
        idx = jnp.min(jnp.where(score == mx, rows, float(n)), axis=0, keepdims=True)
        score = jnp.where(rows == idx, -TAKEN, score)
    return jnp.where(score < -0.5 * TAKEN, 0.0, NEG_INF)


def _nsa_prompt_kernel(slope_ref, qT_ref, gT_ref, kc_ref, vcT_ref, ksel_ref, vselT_ref,
                       kd_ref, vdT_ref, kw0, kw1, kw2, kw3, kw4, vw0, vw1, vw2, vw3, vw4, bw_ref,
                       out_ref, qs_ref, selb_ref, sa_ref, sb_ref, pa_ref, pb_ref, tiles_ref, *, nblk):
    i = pl.program_id(1)
    t0 = i * NSA_QBLOCK
    dh, qb, ncol, tk = NSA_HEAD_DIM, NSA_QBLOCK, NSA_COLS, NSA_KTILE
    slope = slope_ref[...]
    col = lax.broadcasted_iota(jnp.int32, (1, ncol), 1)
    tpos = t0 + (col & (qb - 1))
    tposf = tpos.astype(F32)
    q4 = qT_ref[...] * (dh ** -0.5)
    qT = jnp.concatenate([q4[g * dh:(g + 1) * dh, :] for g in range(NSA_GROUP)], axis=1)
    xrow = lax.broadcasted_iota(jnp.int32, (NSA_CDIM - dh, ncol), 0)

    qc = jnp.concatenate([qT, jnp.zeros((NSA_CDIM - dh, ncol), F32)], axis=0).astype(BF16)
    sc = jnp.dot(kc_ref[...], qc, preferred_element_type=F32)
    c_end = lax.broadcasted_iota(jnp.int32, (nblk, 1), 0) * NSA_BLOCK + (NSA_BLOCK - 1)
    valid = c_end <= tpos
    s = jnp.where(valid, sc - slope * (tposf - c_end.astype(F32)), NEG_INF)
    e = jnp.exp(s - jnp.max(s, axis=0, keepdims=True))
    p = jnp.where(valid, e / jnp.sum(e, axis=0, keepdims=True), 0.0)
    ocT = jnp.dot(vcT_ref[...], p.astype(BF16), preferred_element_type=F32)

    qw = jnp.concatenate([qT, jnp.where(xrow == 0, NEG_INF, 0.0)], axis=0).astype(BF16)
    kw = jnp.concatenate([kw0[...], kw1[...], kw2[...], kw3[...], kw4[...]], axis=0)
    sw = jnp.dot(kw, qw, preferred_element_type=F32) + bw_ref[...]
    ew = jnp.exp(sw - jnp.max(sw, axis=0, keepdims=True))
    vw = jnp.concatenate([vw0[...], vw1[...], vw2[...], vw3[...], vw4[...]], axis=1)
    owT = jnp.dot(vw, ew.astype(BF16), preferred_element_type=F32) / jnp.sum(ew, axis=0, keepdims=True)

    nbt = tk // NSA_BLOCK
    jl = t0 // tk
    s1 = slope.astype(BF16).astype(F32)
    r1 = slope - s1
    s2 = r1.astype(BF16).astype(F32)
    s3 = (r1 - s2).astype(BF16).astype(F32)
    half = float(tk // 2)
    ext = jnp.zeros((NSA_CDIM - dh, ncol), F32)
    for r, v in enumerate((s1, s2, s3, s1 * half, s2 * half, s3 * half)):
        ext = jnp.where(xrow == nbt + r, v, ext)

    qd = jnp.concatenate([qT, ext], axis=0).astype(BF16)
    sd = jnp.dot(kd_ref[...], qd, preferred_element_type=F32)
    kposd = t0 + lax.broadcasted_iota(jnp.int32, (qb, 1), 0)
    sd = jnp.where(kposd > tpos, NEG_INF, sd)
    mx_d = jnp.max(sd, axis=0, keepdims=True)
    pd = jnp.exp(sd - mx_d)
    m_d = mx_d + slope * (jl * tk - tpos).astype(F32)
    l_d = jnp.sum(pd, axis=0, keepdims=True)
    acc_d = jnp.dot(vdT_ref[...], pd.astype(BF16), preferred_element_type=F32)

    imp = p[:, 0:qb]
    for g in range(1, NSA_GROUP):
        imp = imp + p[:, g * qb:(g + 1) * qb]
    blk = lax.broadcasted_iota(jnp.int32, (nblk, qb), 0)
    cur = tpos[:, 0:qb] >> (NSA_BLOCK.bit_length() - 1)
    forced = (blk == 0) | (blk == cur) | (blk == cur - 1)
    score = jnp.where(blk > cur, -1.0, jnp.where(forced, NSA_GROUP + 1.0, imp))
    selb = _topk_select_bias(score, min(NSA_TOPK, nblk))
    blkc = lax.broadcasted_iota(jnp.int32, (nblk, ncol), 0)
    selb_ref[0:nblk, :] = jnp.where(blkc >= t0 // NSA_BLOCK, NEG_INF, jnp.concatenate([selb] * NSA_GROUP, axis=1))
    selb_ref[nblk:nblk + nbt, :] = jnp.full((nbt, ncol), NEG_INF, F32)

    fl = jnp.max(selb_ref[0:nblk, 0:qb].reshape(nblk // nbt, nbt, qb), axis=1)
    tiles_ref[0] = 0
    n_act = jnp.int32(0)
    for j in range(nblk // nbt):
        tiles_ref[n_act] = j
        n_act = n_act + (jnp.max(fl[j:j + 1, :]) > 0.5 * NEG_INF).astype(jnp.int32)
    npairs = (n_act + 1) // 2

    qs_ref[0:dh, :] = qT
    qs_ref[dh:, :] = ext

    def tile_id(t):
        return tiles_ref[jnp.clip(t, 0, jnp.maximum(n_act - 1, 0))]

    def qk_into(t, s_ref):
        tid = tile_id(t)
        b0 = pl.multiple_of(jnp.where(t < n_act, tid * nbt, nblk), 8)
        qs_ref[dh:dh + nbt, :] = selb_ref[pl.ds(b0, nbt), :]
        k0 = pl.multiple_of(tid * tk, tk)
        s_ref[...] = jnp.dot(ksel_ref[pl.ds(k0, tk), :], qs_ref[...].astype(BF16), preferred_element_type=F32)

    def soft(t, s_ref, p_ref, m):
        cj = slope * (tile_id(t) * tk - tpos).astype(F32)
        sj = s_ref[...]
        m_new = jnp.maximum(m, jnp.max(sj, axis=0, keepdims=True) + cj)
        p_ref[...] = jnp.exp(sj - (m_new - cj)).astype(BF16)
        return m_new, jnp.exp(m - m_new)

    def pv(t, p_ref):
        return jnp.dot(vselT_ref[tile_id(t)], p_ref[...], preferred_element_type=F32)

    def pair(i, carry):
        m, accp = carry
        ta = 2 * i
        pvb = pv(ta - 1, pb_ref)
        qk_into(ta + 1, sb_ref)
        m, alpha = soft(ta, sa_ref, pa_ref, m)
        accp = alpha * (accp + pvb)
        pva = pv(ta, pa_ref)
        qk_into(ta + 2, sa_ref)
        m, alpha = soft(ta + 1, sb_ref, pb_ref, m)
        accp = alpha * (accp + pva)
        return m, accp

    pb_ref[...] = jnp.zeros(pb_ref.shape, BF16)
    qk_into(0, sa_ref)
    vrows = vselT_ref.shape[1]
    init = (jnp.full((1, ncol), NEG_INF, F32), jnp.zeros((vrows, ncol), F32))
    m_s, accp = lax.fori_loop(0, npairs, pair, init)
    accl = accp + pv(2 * npairs - 1, pb_ref)
    acc_s, l_s = accl[0:dh], accl[dh:dh + 1]

    m_f = jnp.maximum(m_s, m_d)
    a_s = jnp.exp(m_s - m_f)
    a_d = jnp.exp(m_d - m_f)
    osT = (a_s * acc_s + a_d * acc_d) / (a_s * l_s + a_d * l_d)

    sg = jax.nn.sigmoid(gT_ref[...])

    def gate(c):
        return jnp.concatenate([sg[c * NSA_GROUP + g:c * NSA_GROUP + g + 1, :] for g in range(NSA_GROUP)], axis=1)

    oT = gate(0) * ocT + gate(1) * osT + gate(2) * owT
    o4 = jnp.concatenate([oT[:, g * qb:(g + 1) * qb] for g in range(NSA_GROUP)], axis=0)
    out_ref[...] = o4.T


def _alibi_slopes():
    h = jnp.arange(NSA_HEADS, dtype=F32)
    return (2.0 ** (-8.0 * (h + 1.0) / NSA_HEADS)).reshape(NSA_KV_HEADS, NSA_GROUP)


def _nsa_prompt(qT, gT, kvc, kv_sel, kv_win):
    s_len = qT.shape[1]
    dh, qb, tk, kvh = NSA_HEAD_DIM, NSA_QBLOCK, NSA_KTILE, NSA_KV_HEADS
    assert s_len % tk == 0
    nblk = s_len // NSA_BLOCK
    slopes = _alibi_slopes()
    slope_cols = jnp.repeat(slopes, qb, axis=1).reshape(kvh, 1, NSA_COLS)

    def heads_major(a):
        return a.reshape(a.shape[0], kvh, dh).transpose(1, 0, 2)

    def pad_lanes(a):
        return jnp.pad(a, ((0, 0), (0, 0), (0, NSA_CDIM - a.shape[-1])))

    kc = pad_lanes(heads_major(kvc[:, :NSA_KV_WIDTH])).astype(BF16)
    vcT = heads_major(kvc[:, NSA_KV_WIDTH:]).transpose(0, 2, 1).astype(BF16)
    r = jnp.arange(tk)
    onehot = (r[:, None] // NSA_BLOCK == jnp.arange(tk // NSA_BLOCK)[None, :]).astype(F32)
    lo = (r % (tk // 2)).astype(F32)[:, None]
    hi = (r // (tk // 2)).astype(F32)[:, None]
    kext = jnp.concatenate([onehot, lo, lo, lo, hi, hi, hi], axis=1)
    kext = jnp.tile(kext, (s_len // tk, 1))
    ksel = pad_lanes(jnp.concatenate(
        [heads_major(kv_sel[:, :NSA_KV_WIDTH]), jnp.broadcast_to(kext[None], (kvh,) + kext.shape)], axis=-1)).astype(BF16)
    vsel = heads_major(kv_sel[:, NSA_KV_WIDTH:]).astype(BF16)
    vselT = vsel.reshape(kvh, s_len // tk, tk, dh).transpose(0, 1, 3, 2)
    ones_rows = jnp.zeros((kvh, s_len // tk, 8, tk), BF16).at[:, :, 0, :].set(1.0)
    vselT = jnp.concatenate([vselT, ones_rows], axis=2)
    vselT_flat = vsel.transpose(0, 2, 1)
    kwin = heads_major(kv_win[:, :NSA_KV_WIDTH])
    kwin = jnp.concatenate([kwin, jnp.zeros((kvh, s_len, 1), F32)], axis=-1)
    padk = jnp.zeros((kvh, NSA_WINDOW, dh + 1), F32).at[:, :, dh].set(1.0)
    kwin = pad_lanes(jnp.concatenate([padk, kwin], axis=1)).astype(BF16)
    vwinT = jnp.pad(heads_major(kv_win[:, NSA_KV_WIDTH:]), ((0, 0), (NSA_WINDOW, 0), (0, 0))).transpose(0, 2, 1).astype(BF16)
    rr = jnp.arange(NSA_WKEYS)[:, None]
    cc = jnp.arange(NSA_COLS)[None, :]
    dw = (cc % qb) + NSA_WINDOW - rr
    bw = jnp.where((dw >= 0) & (dw <= NSA_WINDOW), -slope_cols * dw.astype(F32)[None], NEG_INF)

    nq = s_len // qb
    nwb = NSA_WKEYS // qb
    kw_specs = [pl.BlockSpec((None, qb, NSA_CDIM), functools.partial(lambda k, i, j: (k, i + j, 0), j=j)) for j in range(nwb)]
    vw_specs = [pl.BlockSpec((None, dh, qb), functools.partial(lambda k, i, j: (k, 0, i + j), j=j)) for j in range(nwb)]
    assert nwb == 5
    return pl.pallas_call(
        functools.partial(_nsa_prompt_kernel, nblk=nblk),
        grid=(kvh, nq),
        in_specs=[pl.BlockSpec((None, 1, NSA_COLS), lambda k, i: (k, 0, 0)),
                  pl.BlockSpec((NSA_GROUP * dh, qb), lambda k, i: (k, i)),
                  pl.BlockSpec((None, 16, qb), lambda k, i: (k, 0, i)),
                  pl.BlockSpec((None, nblk, NSA_CDIM), lambda k, i: (k, 0, 0)),
                  pl.BlockSpec((None, dh, nblk), lambda k, i: (k, 0, 0)),
                  pl.BlockSpec((None, s_len, NSA_CDIM), lambda k, i: (k, 0, 0)),
                  pl.BlockSpec((None, s_len // tk, dh + 8, tk), lambda k, i: (k, 0, 0, 0)),
                  pl.BlockSpec((None, qb, NSA_CDIM), lambda k, i: (k, i, 0)),
                  pl.BlockSpec((None, dh, qb), lambda k, i: (k, 0, i))]
                 + kw_specs + vw_specs
                 + [pl.BlockSpec((None, NSA_WKEYS, NSA_COLS), lambda k, i: (k, 0, 0))],
        out_specs=pl.BlockSpec((qb, NSA_GROUP * dh), lambda k, i: (i, k)),
        out_shape=jax.ShapeDtypeStruct((s_len, NSA_Q_WIDTH), F32),
        scratch_shapes=[pltpu.VMEM((NSA_CDIM, NSA_COLS), F32),
                        pltpu.VMEM((nblk + tk // NSA_BLOCK, NSA_COLS), F32),
                        pltpu.VMEM((tk, NSA_COLS), F32), pltpu.VMEM((tk, NSA_COLS), F32),
                        pltpu.VMEM((tk, NSA_COLS), BF16), pltpu.VMEM((tk, NSA_COLS), BF16),
                        pltpu.SMEM((s_len // tk + 1,), jnp.int32)],
        compiler_params=_cparams(("arbitrary", "arbitrary")), name="nsa_prompt",
    )(slope_cols, qT, gT.reshape(kvh, 16, s_len), kc, vcT, ksel, vselT, ksel, vselT_flat,
      *([kwin] * nwb), *([vwinT] * nwb), bw)


def _hgrn_chunk(qc, zf, vc, lb, st):
    c, sub = HG_CHUNK, HG_SUB
    logf = jnp.log(lb + (1.0 - lb) * jax.nn.sigmoid(zf))
    kc = (1.0 - lb) * jax.nn.sigmoid(-zf)
    tri = (lax.broadcasted_iota(jnp.int32, (c, c), 0) >= lax.broadcasted_iota(jnp.int32, (c, c), 1)).astype(F32)
    cb = jnp.dot(tri, logf, preferred_element_type=F32, precision=lax.Precision.HIGHEST)
    o = lax.dot_general((qc * jnp.exp(cb)).astype(BF16), st.astype(BF16), (((1,), (1,)), ((), ())),
                        preferred_element_type=F32)
    t3 = lax.broadcasted_iota(jnp.int32, (sub, sub, 1), 0) >= lax.broadcasted_iota(jnp.int32, (sub, sub, 1), 1)
    outs = []
    for a in range(c // sub):
        ra = slice(a * sub, (a + 1) * sub)
        cba, qa, ka, va = cb[ra], qc[ra], kc[ra], vc[ra]
        d3 = cba[:, None, :] - cba[None, :, :]
        x3 = jnp.where(t3, jnp.exp(d3), 0.0) * qa[:, None, :] * ka[None, :, :]
        att3 = jnp.sum(x3, axis=2, keepdims=True)
        oa = o[ra] + jnp.sum(att3 * va[None, :, :], axis=1)
        if a > 0:
            ref = cb[a * sub - 1:a * sub, :]
            qd = (qa * jnp.exp(cba - ref)).astype(BF16)
            kd = (kc[:a * sub] * jnp.exp(ref - cb[:a * sub])).astype(BF16)
            att = lax.dot_general(qd, kd, (((1,), (1,)), ((), ())), preferred_element_type=F32)
            oa = oa + jnp.dot(att.astype(BF16), vc[:a * sub].astype(BF16), preferred_element_type=F32)
        outs.append(oa)
    o = jnp.concatenate(outs, axis=0)
    last = cb[c - 1:c, :]
    kdec = (kc * jnp.exp(last - cb)).astype(BF16)
    st = st * jnp.exp(last) + lax.dot_general(vc.astype(BF16), kdec, (((0,), (0,)), ((), ())),
                                              preferred_element_type=F32)
    return o, st


def _hgrn_kernel(lb_ref, wn_ref, q_ref, f_ref, v_ref, g_ref, s0_ref, o_ref, sfin_ref, st_ref, *, nsub):
    c = pl.program_id(1)
    hpb = HG_HPB

    @pl.when(c == 0)
    def _():
        for j in range(hpb):
            st_ref[j] = s0_ref[j].T

    wn = wn_ref[...]

    def body(u, sts):
        rows = pl.ds(pl.multiple_of(u * HG_CHUNK, HG_CHUNK), HG_CHUNK)
        out = []
        for j in range(hpb):
            cols = slice(j * HG_DK, (j + 1) * HG_DK)
            o, st = _hgrn_chunk(q_ref[rows, cols], f_ref[rows, cols], v_ref[rows, cols], lb_ref[:, cols], sts[j])
            g = g_ref[rows, cols]
            o_ref[rows, cols] = _rms(o, wn) * (g * jax.nn.sigmoid(g))
            out.append(st)
        return tuple(out)

    sts = lax.fori_loop(0, nsub, body, tuple(st_ref[j] for j in range(hpb)))
    for j in range(hpb):
        st_ref[j] = sts[j]

    @pl.when(c == pl.num_programs(1) - 1)
    def _():
        for j in range(hpb):
            sfin_ref[j] = sts[j].T


def _hgrn_prompt(hq, hf, hi, hg, lb, wn, s0, tb):
    t = hq.shape[0]
    hpb = HG_HPB
    assert t % tb == 0 and tb % HG_CHUNK == 0 and HG_HEADS % hpb == 0 and HG_DK == HG_DV
    tok = pl.BlockSpec((tb, hpb * HG_DK), lambda h, c: (c, h))
    stt = pl.BlockSpec((hpb, HG_DK, HG_DV), lambda h, c: (h, 0, 0))
    return pl.pallas_call(
        functools.partial(_hgrn_kernel, nsub=tb // HG_CHUNK),
        grid=(HG_HEADS // hpb, t // tb),
        in_specs=[pl.BlockSpec((1, hpb * HG_DK), lambda h, c: (0, h)),
                  pl.BlockSpec((1, HG_DV), lambda h, c: (0, 0)),
                  tok, tok, tok, tok, stt],
        out_specs=[tok, stt],
        out_shape=[jax.ShapeDtypeStruct((t, HG_HEADS * HG_DV), F32),
                   jax.ShapeDtypeStruct((HG_HEADS, HG_DK, HG_DV), F32)],
        scratch_shapes=[pltpu.VMEM((hpb, HG_DV, HG_DK), F32)],
        compiler_params=_cparams(("arbitrary", "arbitrary")), name="hgrn_prompt",
    )(lb, wn, hq, hf, hi, hg, s0)


def _hgrn_step_kernel(lb_ref, wn_ref, q_ref, f_ref, v_ref, g_ref, s0_ref, o_ref, s_ref):
    wn = wn_ref[...]
    for h in range(HG_HEADS):
        rk = slice(h * HG_DK, (h + 1) * HG_DK)
        lb = lb_ref[rk, :]
        zf = f_ref[rk, :]
        f = lb + (1.0 - lb) * jax.nn.sigmoid(zf)
        kk = (1.0 - lb) * jax.nn.sigmoid(-zf)
        vrow = v_ref[:, h * HG_DV:(h + 1) * HG_DV]
        s_new = f * s0_ref[h] + kk * vrow
        s_ref[h] = s_new
        o = jnp.sum(s_new * q_ref[rk, :], axis=0, keepdims=True)
        g = g_ref[:, h * HG_DV:(h + 1) * HG_DV]
        o_ref[:, h * HG_DV:(h + 1) * HG_DV] = _rms(o, wn) * (g * jax.nn.sigmoid(g))


def _hgrn_step(hq_col, hf_col, hi, hg, lb_col, wn, s0):
    b = hi.shape[0]
    col = pl.BlockSpec((None, HG_WIDTH, 1), lambda i: (i, 0, 0))
    row = pl.BlockSpec((None, 1, HG_WIDTH), lambda i: (i, 0, 0))
    st = pl.BlockSpec((None, HG_HEADS, HG_DK, HG_DV), lambda i: (i, 0, 0, 0))
    return pl.pallas_call(
        _hgrn_step_kernel, grid=(b,),
        in_specs=[pl.BlockSpec((HG_WIDTH, 1), lambda i: (0, 0)), pl.BlockSpec((1, HG_DV), lambda i: (0, 0)),
                  col, col, row, row, st],
        out_specs=[row, st],
        out_shape=[jax.ShapeDtypeStruct((b, 1, HG_WIDTH), F32), jax.ShapeDtypeStruct(s0.shape, F32)],
        compiler_params=_cparams(("parallel",)), name="hgrn_step",
    )(lb_col, wn, hq_col, hf_col, hi, hg, s0)


def _merge_kernel(x_ref, oa_ref, ob_ref, ma_ref, mb_ref, pa_ref, pb_ref, wo_ref, y_ref):
    ya = jnp.dot(oa_ref[...].astype(BF16), pa_ref[...], preferred_element_type=F32)
    yb = jnp.dot(ob_ref[...].astype(BF16), pb_ref[...], preferred_element_type=F32)
    mix = jax.nn.sigmoid(ma_ref[...]) * ya + jax.nn.sigmoid(mb_ref[...]) * yb
    y_ref[...] = x_ref[...] + jnp.dot(mix.astype(BF16), wo_ref[...], preferred_element_type=F32)


def _merge(x, oa, ob, ma, mb, pa, pb, wo, tm):
    t, d = x.shape
    tok = pl.BlockSpec((tm, d), lambda i: (i, 0))
    wsp = pl.BlockSpec((d, d), lambda i: (0, 0))
    return pl.pallas_call(
        _merge_kernel, grid=(t // tm,),
        in_specs=[tok, tok, tok, tok, tok, wsp, wsp, wsp],
        out_specs=tok, out_shape=jax.ShapeDtypeStruct((t, d), F32),
        compiler_params=_cparams(("parallel",)), name="branch_merge",
    )(x, oa, ob, ma, mb, pa, pb, wo)


def _topk_rows(s, k):
    n = s.shape[0]
    rows = lax.broadcasted_iota(jnp.int32, s.shape, 0).astype(F32)
    tops = []
    for r in range(k):
        mx = jnp.max(s, axis=0, keepdims=True)
        idx = jnp.min(jnp.where(s == mx, rows, float(n)), axis=0, keepdims=True)
        s = jnp.where(rows == idx, -TAKEN * (1.0 + r / 64.0), s)
        tops.append(mx)
    rank = jnp.where(s < -0.5 * TAKEN, (s * (-1.0 / TAKEN) - 1.0) * 64.0, float(n))
    return jnp.concatenate(tops, axis=0), rank


_STAIR_GROUPS = ((0, 16), (1, 8), (2, 8), (3, 8))
_STAIR_QUAD = (4, 5, 6, 7)
_STAIR_TAIL = 8


def _stair_rows(t1, t2, op):
    tt = t1.shape[1]
    parts = [op(jnp.broadcast_to(t1[a:a + 1, :], (nb, tt)), t2[0:nb, :]) for a, nb in _STAIR_GROUPS]
    r16 = lax.broadcasted_iota(jnp.int32, (16, tt), 0)
    v1 = jnp.broadcast_to(t1[_STAIR_QUAD[3]:_STAIR_QUAD[3] + 1, :], (16, tt))
    v2 = jnp.broadcast_to(t2[3:4, :], (16, tt))
    for q in (2, 1, 0):
        v1 = jnp.where(r16 < 4 * (q + 1), jnp.broadcast_to(t1[_STAIR_QUAD[q]:_STAIR_QUAD[q] + 1, :], (16, tt)), v1)
        v2 = jnp.where((r16 & 3) == q, jnp.broadcast_to(t2[q:q + 1, :], (16, tt)), v2)
    parts.append(op(v1, v2))
    parts.append(op(t1[_STAIR_TAIL:, :], jnp.broadcast_to(t2[0:1, :], (PK_TOPK - _STAIR_TAIL, tt))))
    return jnp.concatenate(parts, axis=0)


def _stair_row_counts(selc):
    out, r0 = [], 0
    for _, nb in _STAIR_GROUPS:
        out.append(jnp.sum(selc[r0:r0 + nb, :], axis=0, keepdims=True))
        r0 += nb
    quad = selc[r0:r0 + 16, :]
    r16 = lax.broadcasted_iota(jnp.int32, quad.shape, 0)
    for q in range(4):
        out.append(jnp.sum(jnp.where((r16 >> 2) == q, quad, 0.0), axis=0, keepdims=True))
    r0 += 16
    for a in range(PK_TOPK - _STAIR_TAIL):
        out.append(selc[r0 + a:r0 + a + 1, :])
    return out


def _peer_kernel(x_ref, wn_ref, wq_ref, sk_ref, u_ref, vT_ref, wf_ref, y_ref,
                 hnT_ref, acc_ref, wcat_ref, wodd_ref, s_ref, n_ref, a1_ref, r2_ref, e2_ref, *, ib, nkeys, final_norm):
    i = pl.program_id(1)
    kt = PK_TOPK
    hd = PK_DIM // 2

    @pl.when(i == 0)
    def _():
        hnT = _rms(x_ref[...], wn_ref[...]).T.astype(BF16)
        hnT_ref[...] = hnT
        acc_ref[...] = jnp.zeros(acc_ref.shape, F32)
        wcat_ref[...] = jnp.zeros(wcat_ref.shape, BF16)
        wodd_ref[...] = jnp.zeros(wodd_ref.shape, BF16)
        tt = hnT.shape[1]
        nlc = tt // LANES
        lcu = 2 if nlc % 2 == 0 else 1
        for h in range(PK_HEADS):
            for c in range(2):
                r0 = (h * 2 + c) * hd
                qhc = jnp.dot(wq_ref[r0:r0 + hd, :], hnT, preferred_element_type=F32)
                s = jnp.dot(sk_ref[h * 2 + c], qhc.astype(BF16), preferred_element_type=F32)
                for lc in range(nlc):
                    s_ref[c, lc] = s[:, lc * LANES:(lc + 1) * LANES]

            def chunk(lc, carry):
                for sub in range(lcu):
                    chunk_one(lc * lcu + sub)
                return carry

            def chunk_one(lc):
                s0, s1 = s_ref[0, lc], s_ref[1, lc]
                top0, rank0 = _topk_rows(s0, kt)
                top1, rank1 = _topk_rows(s1, kt)
                cand = _stair_rows(top0, top1, jnp.add)
                _, crank = _topk_rows(cand, kt)
                selc = (crank < float(kt)).astype(F32)
                n_a = _stair_row_counts(selc)
                e1t = jnp.exp(top0 - top0[0:1, :])
                e2t = jnp.exp(top1 - top1[0:1, :])
                z = jnp.sum(selc * _stair_rows(e1t, e2t, jnp.multiply), axis=0, keepdims=True)
                nfull = jnp.zeros(s0.shape, F32)
                for a in range(kt):
                    nfull = jnp.where(rank0 == float(a), n_a[a], nfull)
                n_ref[h, lc] = nfull
                a1_ref[h, lc] = jnp.exp(s0 - top0[0:1, :]) / z
                r2_ref[h, lc] = rank1.astype(BF16)
                e2_ref[h, lc] = jnp.exp(s1 - top1[0:1, :]).astype(BF16)

            lax.fori_loop(0, nlc // lcu, chunk, 0)

    nsteps = pl.num_programs(1) - 1
    nch = 2
    cw = hnT_ref.shape[1] // nch

    def step(w_read, w_write):
        hnT = hnT_ref[...]
        nlc = hnT.shape[1] // LANES

        def lanes_cat(ref, h, rows=slice(None)):
            return jnp.concatenate([ref[h, lc, rows, :] for lc in range(nlc)], axis=1)

        for ip in range(ib // 2):
            if ip % (ib // 2 // nch) == 0:
                ch = ip // (ib // 2 // nch)
                cols = slice(ch * cw, (ch + 1) * cw)
                acc_ref[:, cols] += jnp.dot(vT_ref[...], w_read[:, cols], preferred_element_type=F32)
            iis = (2 * ip, 2 * ip + 1)
            aTs = [jnp.dot(u_ref[ii * nkeys:(ii + 1) * nkeys, :], hnT, preferred_element_type=F32) for ii in iis]
            gsums = [jnp.zeros(aTs[0].shape, BF16) for _ in iis]
            for h in range(PK_HEADS):
                r2 = lanes_cat(r2_ref, h)
                e2 = lanes_cat(e2_ref, h)
                for n, ii in enumerate(iis):
                    row = pl.ds(i * ib + ii, 1)
                    nrow = lanes_cat(n_ref, h, row).astype(BF16)
                    arow = lanes_cat(a1_ref, h, row).astype(BF16)
                    gsums[n] = gsums[n] + jnp.where(r2 < nrow, arow * e2, jnp.zeros((), BF16))
            for n, ii in enumerate(iis):
                w_write[ii * nkeys:(ii + 1) * nkeys, :] = jax.nn.gelu(aTs[n].astype(BF16)) * gsums[n]

    @pl.when((i < nsteps) & (i % 2 == 0))
    def _():
        step(wodd_ref, wcat_ref)

    @pl.when((i < nsteps) & (i % 2 == 1))
    def _():
        step(wcat_ref, wodd_ref)

    @pl.when(i == nsteps)
    def _():
        w_last = wodd_ref if (nkeys // ib) % 2 == 0 else wcat_ref
        acc = acc_ref[...] + jnp.dot(vT_ref[...], w_last[...], preferred_element_type=F32)
        y = x_ref[...] + acc.T
        if final_norm:
            y = _rms(y, wf_ref[...])
        y_ref[...] = y


def _peer(x, wn, wqT, sk, u, vT, wf, tt, ib, final_norm):
    t, d = x.shape
    nkeys = sk.shape[1]
    assert t % tt == 0 and nkeys % ib == 0
    nsteps = nkeys // ib
    nlc = tt // LANES
    stat = pltpu.VMEM((PK_HEADS, nlc, nkeys, LANES), F32)
    stat16 = pltpu.VMEM((PK_HEADS, nlc, nkeys, LANES), BF16)
    wbuf = pltpu.VMEM((ib * nkeys, tt), BF16)
    return pl.pallas_call(
        functools.partial(_peer_kernel, ib=ib, nkeys=nkeys, final_norm=final_norm),
        grid=(t // tt, nsteps + 1),
        in_specs=[pl.BlockSpec((tt, d), lambda a, i: (a, 0)),
                  pl.BlockSpec((1, d), lambda a, i: (0, 0)),
                  pl.BlockSpec(wqT.shape, lambda a, i: (0, 0)),
                  pl.BlockSpec(sk.shape, lambda a, i: (0, 0, 0)),
                  pl.BlockSpec((ib * nkeys, d), lambda a, i: (jnp.minimum(i, nsteps - 1), 0)),
                  pl.BlockSpec((None, d, ib * nkeys), lambda a, i: (jnp.maximum(i - 1, 0), 0, 0)),
                  pl.BlockSpec((1, d), lambda a, i: (0, 0))],
        out_specs=pl.BlockSpec((tt, d), lambda a, i: (a, 0)),
        out_shape=jax.ShapeDtypeStruct((t, d), F32),
        scratch_shapes=[pltpu.VMEM((d, tt), BF16), pltpu.VMEM((d, tt), F32), wbuf, wbuf,
                        pltpu.VMEM((2, nlc, nkeys, LANES), F32), stat, stat, stat16, stat16],
        compiler_params=_cparams(("arbitrary", "arbitrary")), name="peer_dense",
    )(x, wn, wqT, sk, u, vT, wf)


def _compress_pages_kernel(pt_ref, *refs, pg, bpp):
    w_ref, o_ref = refs[pg], refs[pg + 1]
    nt = (((1,), (1,)), ((), ()))
    for j in range(pg):
        x = refs[j][...]
        halves = [lax.dot_general(w_ref[c], x[c].reshape(-1, x.shape[-1]).astype(BF16), nt, preferred_element_type=F32)
                  for c in range(x.shape[0])]
        o_ref[j * bpp:(j + 1) * bpp, :] = jnp.concatenate(halves, axis=1)


def _compress_pages(cache_t, page_table, w_sel, pg):
    tail = cache_t.shape[1:]
    db, npages = page_table.shape
    bpp = w_sel.shape[1]
    width = tail[0] * tail[1] * tail[2]
    assert npages % pg == 0
    zeros = (0,) * len(tail)
    page_specs = [pl.BlockSpec((None,) + tail, functools.partial(lambda b, g, pt, j: (pt[b, g * pg + j],) + zeros, j=j))
                  for j in range(pg)]
    return pl.pallas_call(
        functools.partial(_compress_pages_kernel, pg=pg, bpp=bpp),
        grid_spec=pltpu.PrefetchScalarGridSpec(
            num_scalar_prefetch=1, grid=(db, npages // pg),
            in_specs=page_specs + [pl.BlockSpec(w_sel.shape, lambda b, g, pt: (0, 0, 0))],
            out_specs=pl.BlockSpec((None, pg * bpp, width), lambda b, g, pt: (b, g, 0))),
        out_shape=jax.ShapeDtypeStruct((db, npages * bpp, width), F32),
        compiler_params=_cparams(("arbitrary", "arbitrary")), name="nsa_compress_pages",
    )(page_table, *([cache_t] * pg), w_sel)


def _nsa_decode_head_kernel(slope_ref, qbd_ref, g_ref, kvc_ref, wnew_ref, cwin_ref, part_ref, idx_ref,
                            *, past_len, k_past):
    kvw, grp, dh = NSA_KV_WIDTH, NSA_GROUP, NSA_HEAD_DIM
    qbd = qbd_ref[...]
    qb = qbd.astype(BF16)
    slope = slope_ref[...]
    nh = qbd.shape[0]
    npb = kvc_ref.shape[0]
    nt = (((1,), (1,)), ((), ()))
    kvc = kvc_ref[...]
    sc = lax.dot_general(qb, kvc[:, :kvw].astype(BF16), nt, preferred_element_type=F32)
    c_end = lax.broadcasted_iota(jnp.int32, (1, npb), 1) * NSA_BLOCK + (NSA_BLOCK - 1)
    s = sc - slope * (past_len - c_end).astype(F32)
    e = jnp.exp(s - jnp.max(s, axis=1, keepdims=True))
    p = e / jnp.sum(e, axis=1, keepdims=True)
    oc = jnp.dot(p.astype(BF16), kvc[:, kvw:].astype(BF16), preferred_element_type=F32)
    rows = []
    for k in range(NSA_KV_HEADS):
        r = p[k * grp:k * grp + 1, :]
        for g in range(1, grp):
            r = r + p[k * grp + g:k * grp + g + 1, :]
        rows.append(jnp.broadcast_to(r, (grp, npb)))
    imp = jnp.concatenate(rows, axis=0)
    blk = lax.broadcasted_iota(jnp.int32, (nh, npb), 1)
    score = jnp.where((blk == 0) | (blk == npb - 1), grp + 1.0, imp)
    lanes = blk.astype(F32)
    picks = []
    for _ in range(k_past):
        mx = jnp.max(score, axis=1, keepdims=True)
        idx = jnp.min(jnp.where(score == mx, lanes, float(npb)), axis=1, keepdims=True)
        score = jnp.where(lanes == idx, LOWEST, score)
        picks.append(idx)
    idx_ref[...] = jnp.concatenate(picks, axis=1).astype(jnp.int32)
    sg = jax.nn.sigmoid(g_ref[...])
    wnew = wnew_ref[...]
    nw = cwin_ref.shape[-1]
    dw = (nw - lax.broadcasted_iota(jnp.int32, (1, nw), 1)).astype(F32)
    for k in range(NSA_KV_HEADS):
        hs, ds = slice(k * grp, (k + 1) * grp), slice(k * dh, (k + 1) * dh)
        qk = qbd[hs, ds]
        sw = jnp.dot(qk.astype(BF16), cwin_ref[0, k].astype(BF16), preferred_element_type=F32) - slope[hs] * dw
        s_n = jnp.sum(qk * wnew[:, ds], axis=1, keepdims=True)
        m = jnp.maximum(jnp.max(sw, axis=1, keepdims=True), s_n)
        ew = jnp.exp(sw - m)
        en = jnp.exp(s_n - m)
        pv = lax.dot_general(ew.astype(BF16), cwin_ref[1, k].astype(BF16), nt, preferred_element_type=F32)
        ow = (pv + en * wnew[:, kvw + k * dh:kvw + (k + 1) * dh]) / (jnp.sum(ew, axis=1, keepdims=True) + en)
        part_ref[hs, :] = sg[hs, 0:1] * oc[hs, ds] + sg[hs, 2:3] * ow


def _nsa_decode_gather_kernel(pt_ref, ix_ref, slope_ref, q_ref, g_ref, part_ref, new_ref, *refs, nsel, bpp, past_len):
    blocks, o_ref = refs[:nsel], refs[nsel]
    b, k = pl.program_id(0), pl.program_id(1)
    page = blocks[0].shape[-1]
    q = q_ref[...]
    slope = slope_ref[...]
    kt = jnp.concatenate([blocks[s][0] for s in range(nsel)], axis=1).astype(BF16)
    vt = jnp.concatenate([blocks[s][1] for s in range(nsel)], axis=1).astype(BF16)
    sc = jnp.dot(q.astype(BF16), kt, preferred_element_type=F32)
    lane = lax.broadcasted_iota(jnp.int32, (1, page), 1)
    lblk = lane // NSA_BLOCK
    lkey = lane - lblk * NSA_BLOCK
    bias = []
    for s in range(nsel):
        ib = ix_ref[b, k, s]
        kpos = ib * NSA_BLOCK + lkey
        bias.append(jnp.where(lblk == ib % bpp, -slope * (past_len - kpos).astype(F32), NEG_INF))
    sc = sc + jnp.concatenate(bias, axis=1)
    k_own, v_own = new_ref[0], new_ref[1]
    s_n = jnp.sum(q * k_own, axis=1, keepdims=True)
    m = jnp.maximum(jnp.max(sc, axis=1, keepdims=True), s_n)
    p = jnp.exp(sc - m)
    p_n = jnp.exp(s_n - m)
    pv = lax.dot_general(p.astype(BF16), vt, (((1,), (1,)), ((), ())), preferred_element_type=F32)
    o_s = (pv + p_n * v_own) / (jnp.sum(p, axis=1, keepdims=True) + p_n)
    o_ref[...] = part_ref[...] + jax.nn.sigmoid(g_ref[...])[:, 1:2] * o_s


def _rows_minor(a):
    return a.transpose(0, 2, 3, 4, 1)


def _nsa_sample(q, kv_cmp, kv_sel, kv_win, glog, cache_cmp, cache_sel, cache_win, page_table, w_cmp):
    db = q.shape[0]
    kvh, grp, dh, kvw = NSA_KV_HEADS, NSA_GROUP, NSA_HEAD_DIM, NSA_KV_WIDTH
    page = cache_cmp.shape[1]
    npages = page_table.shape[1]
    past_len = npages * page
    nwin = cache_win.shape[1]
    assert past_len >= nwin and page % NSA_BLOCK == 0
    npb = past_len // NSA_BLOCK
    bpp = page // NSA_BLOCK
    pg = 8 if npages % 8 == 0 else 1
    width = 2 * kvw
    rowblk = jnp.arange(page) // NSA_BLOCK
    w_sel = jnp.where(rowblk[None, None, :] == jnp.arange(bpp)[None, :, None],
                      jnp.tile(w_cmp, (1, bpp))[:, None, :], 0.0).astype(BF16)
    kvc = _compress_pages(_rows_minor(cache_cmp), page_table, w_sel, pg)
    q4 = q.reshape(db, kvh, grp, dh) * (dh ** -0.5)
    qbd = (q4[:, :, :, None, :] * jnp.eye(kvh, dtype=F32)[None, :, None, :, None]).reshape(db, kvh * grp, kvw)
    slope = _alibi_slopes().reshape(kvh * grp, 1)
    nh = kvh * grp
    k_past = min(NSA_TOPK, npb + 1) - 1
    full2 = lambda shape: pl.BlockSpec(shape, lambda b: (0,) * len(shape))
    per_b = lambda shape: pl.BlockSpec((None,) + shape, lambda b: (b,) + (0,) * len(shape))
    part, idx = pl.pallas_call(
        functools.partial(_nsa_decode_head_kernel, past_len=past_len, k_past=k_past),
        grid=(db,),
        in_specs=[full2((nh, 1)), per_b((nh, kvw)), per_b((nh, 3)), per_b((npb, width)), per_b((1, width)),
                  per_b((2, kvh, dh, nwin))],
        out_specs=[per_b((nh, dh)), per_b((nh, k_past))],
        out_shape=[jax.ShapeDtypeStruct((db, nh, dh), F32), jax.ShapeDtypeStruct((db, nh, k_past), jnp.int32)],
        compiler_params=_cparams(("parallel",)), name="nsa_decode_head",
    )(slope, qbd, glog, kvc, kv_win.reshape(db, 1, width), _rows_minor(cache_win))
    idx4 = idx.reshape(db, kvh, grp, k_past)[:, :, 0, :]
    blk_specs = [pl.BlockSpec((None, 2, None, dh, page),
                              functools.partial(lambda b, k, pt, ix, s: (pt[b, ix[b, k, s] // bpp], 0, k, 0, 0), s=s))
                 for s in range(k_past)]
    hsp = lambda shape: pl.BlockSpec((None, None) + shape, lambda b, k, pt, ix: (b, k) + (0,) * len(shape))
    o = pl.pallas_call(
        functools.partial(_nsa_decode_gather_kernel, nsel=k_past, bpp=bpp, past_len=past_len),
        grid_spec=pltpu.PrefetchScalarGridSpec(
            num_scalar_prefetch=2, grid=(db, kvh),
            in_specs=[pl.BlockSpec((None, grp, 1), lambda b, k, pt, ix: (k, 0, 0)), hsp((grp, dh)), hsp((grp, 3)),
                      hsp((grp, dh)),
                      pl.BlockSpec((None, 2, None, 1, dh), lambda b, k, pt, ix: (b, 0, k, 0, 0))] + blk_specs,
            out_specs=hsp((grp, dh))),
        out_shape=jax.ShapeDtypeStruct((db, kvh, grp, dh), F32),
        compiler_params=_cparams(("arbitrary", "arbitrary")), name="nsa_decode_gather",
    )(page_table, idx4, slope.reshape(kvh, grp, 1), q4, glog.reshape(db, kvh, grp, 3), part.reshape(db, kvh, grp, dh),
      kv_sel.reshape(db, 2, kvh, 1, dh), *([_rows_minor(cache_sel)] * k_past))
    new_win = jnp.concatenate([cache_win, kv_win.reshape((db, 1) + cache_win.shape[2:])], axis=1)[:, 1:]
    return o.reshape(db, nh * dh), new_win


def _split_w_in(w_in):
    pts, acc = [], 0
    for w in (NSA_Q_WIDTH, 6 * NSA_KV_WIDTH, 3 * NSA_HEADS, HG_WIDTH, HG_WIDTH, HG_WIDTH, HG_WIDTH, w_in.shape[0]):
        acc += w
        pts.append(acc)
    return jnp.split(w_in, pts, axis=1)


def _prep_layer(w_in, w_proj_nsa, w_proj_hgrn, w_out, w_peer_q, peer_sub_keys, peer_u, peer_v):
    d = w_in.shape[0]
    wq, wkv, wg, wbq, wbf, wbi, wbg, wma, wmb = _split_w_in(w_in)
    wg = wg.reshape(d, NSA_KV_HEADS, NSA_GROUP, 3).transpose(0, 1, 3, 2).reshape(d, NSA_KV_HEADS, 3 * NSA_GROUP)
    wg = jnp.pad(wg, ((0, 0), (0, 0), (0, 16 - 3 * NSA_GROUP))).reshape(d, NSA_KV_HEADS * 16)
    wg = jnp.pad(wg, ((0, 0), (0, LANES - NSA_KV_HEADS * 16)))
    nkeys = peer_sub_keys.shape[2]
    return dict(
        w_qg=jnp.concatenate([wq, wg], axis=1).astype(BF16),
        w_kv=wkv.astype(BF16),
        w_hg=jnp.concatenate([wbq, wbf, wbi, wbg], axis=1).astype(BF16),
        w_m=jnp.concatenate([wma, wmb], axis=1).astype(BF16),
        pa=w_proj_nsa.astype(BF16), pb=w_proj_hgrn.astype(BF16), wo=w_out.astype(BF16),
        wqT=w_peer_q.T.astype(BF16),
        sk=peer_sub_keys.reshape(PK_HEADS * 2, nkeys, PK_DIM // 2).astype(BF16),
        u=peer_u.astype(BF16),
        vT=peer_v.astype(BF16).reshape(nkeys // PEER_IB, PEER_IB * nkeys, d).transpose(0, 2, 1),
    )


def _tile(t, pref):
    return pref if t % pref == 0 else t


def _peer_tokens(x, wn, wf, prm, final_norm):
    t = x.shape[0]
    tp = -(-t // LANES) * LANES
    xp = jnp.pad(x, ((0, tp - t), (0, 0)))
    tt = PEER_TT if tp % PEER_TT == 0 else LANES
    y = _peer(xp, wn, prm["wqT"], prm["sk"], prm["u"], prm["vT"], wf, tt, PEER_IB, final_norm)
    return y[:t]


def _layer_prompt(x, lb, prm, w_norm_mix, w_cmp, w_hgrn_norm, w_norm_ffn, w_norm_final, final_norm):
    s_len, d = x.shape
    tm = _tile(s_len, 256)
    kvw = NSA_KV_WIDTH
    kv_cmp, kv_sel, kv_win = _proj(x, w_norm_mix, prm["w_kv"], [2 * kvw] * 3, [False] * 3, tm)
    qT, gT = _proj(x, w_norm_mix, prm["w_qg"], [NSA_Q_WIDTH, LANES], [True, True], tm)
    hq, hf, hi, hg = _proj(x, w_norm_mix, prm["w_hg"], [HG_WIDTH] * 4, [False] * 4, tm)
    ma, mb = _proj(x, w_norm_mix, prm["w_m"], [d, d], [False, False], tm)
    wfull = jnp.concatenate([jnp.broadcast_to(w_cmp[0][:, None], (NSA_BLOCK, kvw)),
                             jnp.broadcast_to(w_cmp[1][:, None], (NSA_BLOCK, kvw))], axis=1)
    nblk = s_len // NSA_BLOCK
    kvc = _compress(kv_cmp, wfull, 8 if nblk % 8 == 0 else nblk)
    o_a = _nsa_prompt(qT, gT[:NSA_KV_HEADS * 16], kvc, kv_sel, kv_win)
    s0 = jnp.zeros((HG_HEADS, HG_DK, HG_DV), F32)
    o_b, s_fin = _hgrn_prompt(hq, hf, hi, hg, lb, w_hgrn_norm, s0, _tile(s_len, 512))
    x1 = _merge(x, o_a, o_b, ma, mb, prm["pa"], prm["pb"], prm["wo"], tm)
    x2 = _peer_tokens(x1, w_norm_ffn, w_norm_final, prm, final_norm)
    shp = (s_len, 2, NSA_KV_HEADS, NSA_HEAD_DIM)
    return x2, kv_cmp.reshape(shp), kv_sel.reshape(shp), kv_win.reshape(shp)[-NSA_WINDOW:], s_fin


def _layer_sample(x, lb, prm, w_norm_mix, w_cmp, w_hgrn_norm, w_norm_ffn, w_norm_final, final_norm,
                  cache_cmp, cache_sel, cache_win, s0, page_table):
    b, d = x.shape
    kvw = NSA_KV_WIDTH
    w_q = prm["w_qg"][:, :NSA_Q_WIDTH]
    (q,) = _proj(x, w_norm_mix, w_q, [NSA_Q_WIDTH], [False], b)
    kv_cmp, kv_sel, kv_win = _proj(x, w_norm_mix, prm["w_kv"], [2 * kvw] * 3, [False] * 3, b)
    (gTt,) = _proj(x, w_norm_mix, prm["w_qg"][:, NSA_Q_WIDTH:], [LANES], [False], b)
    hq, hf, hi, hg = _proj(x, w_norm_mix, prm["w_hg"], [HG_WIDTH] * 4, [False] * 4, b)
    ma, mb = _proj(x, w_norm_mix, prm["w_m"], [d, d], [False, False], b)
    shp = (b, 1, 2, NSA_KV_HEADS, NSA_HEAD_DIM)
    glog = (gTt[:, :NSA_KV_HEADS * 16].reshape(b, NSA_KV_HEADS, 16)[..., :3 * NSA_GROUP]
            .reshape(b, NSA_KV_HEADS, 3, NSA_GROUP).transpose(0, 1, 3, 2).reshape(b, NSA_HEADS, 3))
    o_a, new_win = _nsa_sample(q, kv_cmp, kv_sel, kv_win, glog, cache_cmp, cache_sel, cache_win, page_table, w_cmp)
    o_b, s_new = _hgrn_step(hq.reshape(b, HG_WIDTH, 1), hf.reshape(b, HG_WIDTH, 1), hi.reshape(b, 1, HG_WIDTH),
                            hg.reshape(b, 1, HG_WIDTH), lb.reshape(HG_WIDTH, 1), w_hgrn_norm, s0)
    x1 = _merge(x, o_a.reshape(b, NSA_Q_WIDTH), o_b.reshape(b, HG_WIDTH), ma, mb, prm["pa"], prm["pb"], prm["wo"], b)
    x2 = _peer_tokens(x1, w_norm_ffn, w_norm_final, prm, final_norm)
    return x2, kv_cmp.reshape(shp), kv_sel.reshape(shp), new_win, s_new


def kernel(x_prompt, x_sample, cache_cmp_kv, cache_sel_kv, cache_win_kv, state_hgrn, page_table,
           w_norm_mix, w_in, w_cmp, w_proj_nsa, w_proj_hgrn, w_hgrn_norm, hgrn_lb_logits, w_out,
           w_norm_ffn, w_peer_q, peer_sub_keys, peer_u, peer_v, w_norm_final):
    depth = w_in.shape[0]
    bsz, s_len, d = x_prompt.shape
    db, dt, _ = x_sample.shape
    assert dt == 1
    lbs = jnp.cumsum(jax.nn.softmax(hgrn_lb_logits.astype(F32), axis=0), axis=0)
    wfin = w_norm_final.reshape(1, d)
    xp = [x_prompt[b] for b in range(bsz)]
    xs = x_sample.reshape(db, d)
    st_p, st_s = [], []
    for l in range(depth):
        last = l == depth - 1
        prm = _prep_layer(w_in[l], w_proj_nsa[l], w_proj_hgrn[l], w_out[l], w_peer_q[l], peer_sub_keys[l],
                          peer_u[l], peer_v[l])
        shared = (lbs[l].reshape(1, HG_WIDTH), prm, w_norm_mix[l].reshape(1, d), w_cmp[l],
                  w_hgrn_norm[l].reshape(1, HG_DV), w_norm_ffn[l].reshape(1, d), wfin, last)
        outs = [_layer_prompt(xp[b], *shared) for b in range(bsz)]
        xp = [o[0] for o in outs]
        st_p.append(tuple(jnp.stack([o[k] for o in outs]) for k in range(1, 5)))
        xs, *ss = _layer_sample(xs, *shared, cache_cmp_kv[l], cache_sel_kv[l], cache_win_kv[l], state_hgrn[l],
                                page_table)
        st_s.append(tuple(ss))
    y_prompt = jnp.stack(xp)
    y_sample = xs.reshape(db, dt, d)
    return (y_prompt, y_sample,
            jnp.stack([s[0] for s in st_p]), jnp.stack([s[1] for s in st_p]),
            jnp.stack([s[2] for s in st_p]), jnp.stack([s[3] for s in st_p]),
            jnp.stack([s[0] for s in st_s]), jnp.stack([s[1] for s in st_s]),
            jnp.stack([s[2] for s in st_s]), jnp.stack([s[3] for s in st_s]))
```

```python
import functools

import jax
import jax.numpy as jnp
from jax import lax
from jax.experimental import pallas as pl
from jax.experimental.pallas import tpu as pltpu

F32 = jnp.float32
BF16 = jnp.bfloat16

NSA_HEADS = 16
NSA_KV_HEADS = 4
NSA_GROUP = NSA_HEADS // NSA_KV_HEADS
NSA_HEAD_DIM = 64
NSA_BLOCK = 64
NSA_TOPK = 16
NSA_WINDOW = 512
NSA_QBLOCK = 256
HG_HEADS = 8
HG_DK = 128
HG_DV = 128
HG_CHUNK = 64
HG_SUB = 16
HG_HPB = 8
PK_HEADS = 8
PK_DIM = 256
PK_TOPK = 16
RMS_EPS = 1e-6
NEG_INF = -1e30
LOWEST = -3e38
TAKEN = 2.0 ** 100

NSA_Q_WIDTH = NSA_HEADS * NSA_HEAD_DIM
NSA_KV_WIDTH = NSA_KV_HEADS * NSA_HEAD_DIM
HG_WIDTH = HG_HEADS * HG_DK

LANES = 128
VMEM_LIMIT_BYTES = 56 * 1024 * 1024

NSA_COLS = NSA_GROUP * NSA_QBLOCK
NSA_KTILE = 512
NSA_CDIM = 128
NSA_WKEYS = NSA_WINDOW + NSA_QBLOCK
PEER_TT = 512
PEER_IB = 8


def _cparams(sem):
    return pltpu.CompilerParams(dimension_semantics=sem, vmem_limit_bytes=VMEM_LIMIT_BYTES)


def _rms(x, w):
    return x * lax.rsqrt(jnp.mean(x * x, axis=-1, keepdims=True) + RMS_EPS) * w


def _proj_kernel(x_ref, wn_ref, w_ref, *out_refs, widths, transposed, chunk):
    hb = _rms(x_ref[...], wn_ref[...]).astype(BF16)
    off = 0
    for o_ref, wd, tr in zip(out_refs, widths, transposed):
        for c0 in range(0, wd, chunk):
            cw = min(chunk, wd - c0)
            r = jnp.dot(hb, w_ref[:, off + c0:off + c0 + cw], preferred_element_type=F32)
            if tr:
                o_ref[c0:c0 + cw, :] = r.T
            else:
                o_ref[:, c0:c0 + cw] = r
        off += wd


def _proj(x, wn, w, widths, transposed, tm):
    t, d = x.shape
    n = w.shape[1]
    assert sum(widths) == n and t % tm == 0
    out_shape, out_specs = [], []
    for wd, tr in zip(widths, transposed):
        if tr:
            out_shape.append(jax.ShapeDtypeStruct((wd, t), F32))
            out_specs.append(pl.BlockSpec((wd, tm), lambda i: (0, i)))
        else:
            out_shape.append(jax.ShapeDtypeStruct((t, wd), F32))
            out_specs.append(pl.BlockSpec((tm, wd), lambda i: (i, 0)))
    return pl.pallas_call(
        functools.partial(_proj_kernel, widths=tuple(widths), transposed=tuple(transposed), chunk=512),
        grid=(t // tm,),
        in_specs=[pl.BlockSpec((tm, d), lambda i: (i, 0)),
                  pl.BlockSpec((1, d), lambda i: (0, 0)),
                  pl.BlockSpec((d, n), lambda i: (0, 0))],
        out_specs=out_specs, out_shape=out_shape,
        compiler_params=_cparams(("parallel",)), name="rms_proj",
    )(x, wn, w)


def _compress_kernel(kv_ref, w_ref, o_ref, *, nb):
    x = kv_ref[...]
    width = x.shape[-1]
    x3 = x.reshape(nb, NSA_BLOCK, width) * w_ref[...][None]
    o_ref[...] = jnp.sum(x3, axis=1)


def _compress(kv, wfull, nb):
    t, width = kv.shape
    rows = nb * NSA_BLOCK
    assert t % rows == 0
    return pl.pallas_call(
        functools.partial(_compress_kernel, nb=nb),
        grid=(t // rows,),
        in_specs=[pl.BlockSpec((rows, width), lambda i: (i, 0)),
                  pl.BlockSpec((NSA_BLOCK, width), lambda i: (0, 0))],
        out_specs=pl.BlockSpec((nb, width), lambda i: (i, 0)),
        out_shape=jax.ShapeDtypeStruct((t // NSA_BLOCK, width), F32),
        compiler_params=_cparams(("parallel",)), name="nsa_compress",
    )(kv, wfull)


def _topk_select_bias(score, k):
    n = score.shape[0]
    rows = lax.broadcasted_iota(jnp.int32, score.shape, 0).astype(F32)
    for _ in range(k):
        mx = jnp.max(score, axis=0, keepdims=True)
        idx = jnp.min(jnp.where(score == mx, rows, float(n)), axis=0, keepdims=True)
        score = jnp.where(rows == idx, -TAKEN, score)
    return jnp.where(score < -0.5 * TAKEN, 0.0, NEG_INF)


def _nsa_prompt_kernel(slope_ref, qT_ref, gT_ref, kc_ref, vcT_ref, ksel_ref, vselT_ref,
                       kd_ref, vdT_ref, *rest, nblk, nwb):
    kws, vws = rest[:nwb], rest[nwb:2 * nwb]
    bw_ref, out_ref, qs_ref, selb_ref, sa_ref, sb_ref, pa_ref, pb_ref, tiles_ref = rest[2 * nwb:]
    i = pl.program_id(1)
    t0 = i * NSA_QBLOCK
    dh, qb, ncol, tk = NSA_HEAD_DIM, NSA_QBLOCK, NSA_COLS, NSA_KTILE
    slope = slope_ref[...]
    col = lax.broadcasted_iota(jnp.int32, (1, ncol), 1)
    tpos = t0 + (col & (qb - 1))
    tposf = tpos.astype(F32)
    q4 = qT_ref[...] * (dh ** -0.5)
    qT = jnp.concatenate([q4[g * dh:(g + 1) * dh, :] for g in range(NSA_GROUP)], axis=1)
    xrow = lax.broadcasted_iota(jnp.int32, (NSA_CDIM - dh, ncol), 0)

    qc = jnp.concatenate([qT, jnp.zeros((NSA_CDIM - dh, ncol), F32)], axis=0).astype(BF16)
    sc = jnp.dot(kc_ref[...], qc, preferred_element_type=F32)
    c_end = lax.broadcasted_iota(jnp.int32, (nblk, 1), 0) * NSA_BLOCK + (NSA_BLOCK - 1)
    valid = c_end <= tpos
    s = jnp.where(valid, sc - slope * (tposf - c_end.astype(F32)), NEG_INF)
    e = jnp.exp(s - jnp.max(s, axis=0, keepdims=True))
    p = jnp.where(valid, e / jnp.sum(e, axis=0, keepdims=True), 0.0)
    ocT = jnp.dot(vcT_ref[...], p.astype(BF16), preferred_element_type=F32)

    qw = jnp.concatenate([qT, jnp.where(xrow == 0, NEG_INF, 0.0)], axis=0).astype(BF16)
    kw = jnp.concatenate([r[...] for r in kws], axis=0)
    sw = jnp.dot(kw, qw, preferred_element_type=F32) + bw_ref[...]
    ew = jnp.exp(sw - jnp.max(sw, axis=0, keepdims=True))
    vw = jnp.concatenate([r[...] for r in vws], axis=1)
    owT = jnp.dot(vw, ew.astype(BF16), preferred_element_type=F32) / jnp.sum(ew, axis=0, keepdims=True)

    nbt = tk // NSA_BLOCK
    jl = t0 // tk
    s1 = slope.astype(BF16).astype(F32)
    r1 = slope - s1
    s2 = r1.astype(BF16).astype(F32)
    s3 = (r1 - s2).astype(BF16).astype(F32)
    half = float(tk // 2)
    ext = jnp.zeros((NSA_CDIM - dh, ncol), F32)
    for r, v in enumerate((s1, s2, s3, s1 * half, s2 * half, s3 * half)):
        ext = jnp.where(xrow == nbt + r, v, ext)

    qd = jnp.concatenate([qT, ext], axis=0).astype(BF16)
    sd = jnp.dot(kd_ref[...], qd, preferred_element_type=F32)
    kposd = t0 + lax.broadcasted_iota(jnp.int32, (qb, 1), 0)
    sd = jnp.where(kposd > tpos, NEG_INF, sd)
    mx_d = jnp.max(sd, axis=0, keepdims=True)
    pd = jnp.exp(sd - mx_d)
    m_d = mx_d + slope * (jl * tk - tpos).astype(F32)
    l_d = jnp.sum(pd, axis=0, keepdims=True)
    acc_d = jnp.dot(vdT_ref[...], pd.astype(BF16), preferred_element_type=F32)

    imp = p[:, 0:qb]
    for g in range(1, NSA_GROUP):
        imp = imp + p[:, g * qb:(g + 1) * qb]
    blk = lax.broadcasted_iota(jnp.int32, (nblk, qb), 0)
    cur = tpos[:, 0:qb] >> (NSA_BLOCK.bit_length() - 1)
    forced = (blk == 0) | (blk == cur) | (blk == cur - 1)
    score = jnp.where(blk > cur, -1.0, jnp.where(forced, NSA_GROUP + 1.0, imp))
    if nblk >= NSA_TOPK:
        selb = _topk_select_bias(jnp.where(forced, -TAKEN, score), NSA_TOPK - 3)
    else:
        selb = _topk_select_bias(score, nblk)
    blkc = lax.broadcasted_iota(jnp.int32, (nblk, ncol), 0)
    selb_ref[0:nblk, :] = jnp.where(blkc >= t0 // NSA_BLOCK, NEG_INF, jnp.concatenate([selb] * NSA_GROUP, axis=1))
    selb_ref[nblk:nblk + nbt, :] = jnp.full((nbt, ncol), NEG_INF, F32)

    fl = jnp.max(selb_ref[0:nblk, 0:qb].reshape(nblk // nbt, nbt, qb), axis=1)
    tiles_ref[0] = 0
    n_act = jnp.int32(0)
    for j in range(nblk // nbt):
        tiles_ref[n_act] = j
        n_act = n_act + (jnp.max(fl[j:j + 1, :]) > 0.5 * NEG_INF).astype(jnp.int32)
    npairs = (n_act + 1) // 2

    qs_ref[0:dh, :] = qT
    qs_ref[dh:, :] = ext

    def tile_id(t):
        return tiles_ref[jnp.clip(t, 0, jnp.maximum(n_act - 1, 0))]

    def qk_into(t, s_ref):
        tid = tile_id(t)
        b0 = pl.multiple_of(jnp.where(t < n_act, tid * nbt, nblk), 8)
        qs_ref[dh:dh + nbt, :] = selb_ref[pl.ds(b0, nbt), :]
        k0 = pl.multiple_of(tid * tk, tk)
        s_ref[...] = jnp.dot(ksel_ref[pl.ds(k0, tk), :], qs_ref[...].astype(BF16), preferred_element_type=F32)

    def soft(t, s_ref, p_ref, m):
        cj = slope * (tile_id(t) * tk - tpos).astype(F32)
        sj = s_ref[...]
        m_new = jnp.maximum(m, jnp.max(sj, axis=0, keepdims=True) + cj)
        p_ref[...] = jnp.exp(sj - (m_new - cj)).astype(BF16)
        return m_new, jnp.exp(m - m_new)

    def pv(t, p_ref):
        return jnp.dot(vselT_ref[tile_id(t)], p_ref[...], preferred_element_type=F32)

    def pair(i, carry):
        m, accp = carry
        ta = 2 * i
        pvb = pv(ta - 1, pb_ref)
        qk_into(ta + 1, sb_ref)
        m, alpha = soft(ta, sa_ref, pa_ref, m)
        accp = alpha * (accp + pvb)
        pva = pv(ta, pa_ref)
        qk_into(ta + 2, sa_ref)
        m, alpha = soft(ta + 1, sb_ref, pb_ref, m)
        accp = alpha * (accp + pva)
        return m, accp

    pb_ref[...] = jnp.zeros(pb_ref.shape, BF16)
    qk_into(0, sa_ref)
    vrows = vselT_ref.shape[1]
    init = (jnp.full((1, ncol), NEG_INF, F32), jnp.zeros((vrows, ncol), F32))
    m_s, accp = lax.fori_loop(0, npairs, pair, init)
    accl = accp + pv(2 * npairs - 1, pb_ref)
    acc_s, l_s = accl[0:dh], accl[dh:dh + 1]

    m_f = jnp.maximum(m_s, m_d)
    a_s = jnp.exp(m_s - m_f)
    a_d = jnp.exp(m_d - m_f)
    osT = (a_s * acc_s + a_d * acc_d) / (a_s * l_s + a_d * l_d)

    sg = jax.nn.sigmoid(gT_ref[...])

    def gate(c):
        return jnp.concatenate([sg[c * NSA_GROUP + g:c * NSA_GROUP + g + 1, :] for g in range(NSA_GROUP)], axis=1)

    oT = gate(0) * ocT + gate(1) * osT + gate(2) * owT
    o4 = jnp.concatenate([oT[:, g * qb:(g + 1) * qb] for g in range(NSA_GROUP)], axis=0)
    out_ref[...] = o4.T


def _alibi_slopes():
    h = jnp.arange(NSA_HEADS, dtype=F32)
    return (2.0 ** (-8.0 * (h + 1.0) / NSA_HEADS)).reshape(NSA_KV_HEADS, NSA_GROUP)


def _nsa_prompt(qT, gT, kvc, kv_sel, kv_win):
    s_len = qT.shape[1]
    dh, qb, tk, kvh = NSA_HEAD_DIM, NSA_QBLOCK, NSA_KTILE, NSA_KV_HEADS
    assert s_len % tk == 0
    nblk = s_len // NSA_BLOCK
    slopes = _alibi_slopes()
    slope_cols = jnp.repeat(slopes, qb, axis=1).reshape(kvh, 1, NSA_COLS)

    def heads_major(a):
        return a.reshape(a.shape[0], kvh, dh).transpose(1, 0, 2)

    def pad_lanes(a):
        return jnp.pad(a, ((0, 0), (0, 0), (0, NSA_CDIM - a.shape[-1])))

    kc = pad_lanes(heads_major(kvc[:, :NSA_KV_WIDTH])).astype(BF16)
    vcT = heads_major(kvc[:, NSA_KV_WIDTH:]).transpose(0, 2, 1).astype(BF16)
    r = jnp.arange(tk)
    onehot = (r[:, None] // NSA_BLOCK == jnp.arange(tk // NSA_BLOCK)[None, :]).astype(F32)
    lo = (r % (tk // 2)).astype(F32)[:, None]
    hi = (r // (tk // 2)).astype(F32)[:, None]
    kext = jnp.concatenate([onehot, lo, lo, lo, hi, hi, hi], axis=1)
    kext = jnp.tile(kext, (s_len // tk, 1))
    ksel = pad_lanes(jnp.concatenate(
        [heads_major(kv_sel[:, :NSA_KV_WIDTH]), jnp.broadcast_to(kext[None], (kvh,) + kext.shape)], axis=-1)).astype(BF16)
    vsel = heads_major(kv_sel[:, NSA_KV_WIDTH:]).astype(BF16)
    vselT = vsel.reshape(kvh, s_len // tk, tk, dh).transpose(0, 1, 3, 2)
    ones_rows = jnp.zeros((kvh, s_len // tk, 8, tk), BF16).at[:, :, 0, :].set(1.0)
    vselT = jnp.concatenate([vselT, ones_rows], axis=2)
    vselT_flat = vsel.transpose(0, 2, 1)
    kwin = heads_major(kv_win[:, :NSA_KV_WIDTH])
    kwin = jnp.concatenate([kwin, jnp.zeros((kvh, s_len, 1), F32)], axis=-1)
    padk = jnp.zeros((kvh, NSA_WINDOW, dh + 1), F32).at[:, :, dh].set(1.0)
    kwin = pad_lanes(jnp.concatenate([padk, kwin], axis=1)).astype(BF16)
    vwinT = jnp.pad(heads_major(kv_win[:, NSA_KV_WIDTH:]), ((0, 0), (NSA_WINDOW, 0), (0, 0))).transpose(0, 2, 1).astype(BF16)
    rr = jnp.arange(NSA_WKEYS)[:, None]
    cc = jnp.arange(NSA_COLS)[None, :]
    dw = (cc % qb) + NSA_WINDOW - rr
    bw = jnp.where((dw >= 0) & (dw <= NSA_WINDOW), -slope_cols * dw.astype(F32)[None], NEG_INF)

    nq = s_len // qb
    nwb = NSA_WKEYS // qb
    kw_specs = [pl.BlockSpec((None, qb, NSA_CDIM), functools.partial(lambda k, i, j: (k, i + j, 0), j=j)) for j in range(nwb)]
    vw_specs = [pl.BlockSpec((None, dh, qb), functools.partial(lambda k, i, j: (k, 0, i + j), j=j)) for j in range(nwb)]
    assert NSA_WINDOW % qb == 0 and s_len % qb == 0 and tk % qb == 0
    return pl.pallas_call(
        functools.partial(_nsa_prompt_kernel, nblk=nblk, nwb=nwb),
        grid=(kvh, nq),
        in_specs=[pl.BlockSpec((None, 1, NSA_COLS), lambda k, i: (k, 0, 0)),
                  pl.BlockSpec((NSA_GROUP * dh, qb), lambda k, i: (k, i)),
                  pl.BlockSpec((None, 16, qb), lambda k, i: (k, 0, i)),
                  pl.BlockSpec((None, nblk, NSA_CDIM), lambda k, i: (k, 0, 0)),
                  pl.BlockSpec((None, dh, nblk), lambda k, i: (k, 0, 0)),
                  pl.BlockSpec((None, s_len, NSA_CDIM), lambda k, i: (k, 0, 0)),
                  pl.BlockSpec((None, s_len // tk, dh + 8, tk), lambda k, i: (k, 0, 0, 0)),
                  pl.BlockSpec((None, qb, NSA_CDIM), lambda k, i: (k, i, 0)),
                  pl.BlockSpec((None, dh, qb), lambda k, i: (k, 0, i))]
                 + kw_specs + vw_specs
                 + [pl.BlockSpec((None, NSA_WKEYS, NSA_COLS), lambda k, i: (k, 0, 0))],
        out_specs=pl.BlockSpec((qb, NSA_GROUP * dh), lambda k, i: (i, k)),
        out_shape=jax.ShapeDtypeStruct((s_len, NSA_Q_WIDTH), F32),
        scratch_shapes=[pltpu.VMEM((NSA_CDIM, NSA_COLS), F32),
                        pltpu.VMEM((nblk + tk // NSA_BLOCK, NSA_COLS), F32),
                        pltpu.VMEM((tk, NSA_COLS), F32), pltpu.VMEM((tk, NSA_COLS), F32),
                        pltpu.VMEM((tk, NSA_COLS), BF16), pltpu.VMEM((tk, NSA_COLS), BF16),
                        pltpu.SMEM((s_len // tk + 1,), jnp.int32)],
        compiler_params=_cparams(("arbitrary", "arbitrary")), name="nsa_prompt",
    )(slope_cols, qT, gT.reshape(kvh, 16, s_len), kc, vcT, ksel, vselT, ksel, vselT_flat,
      *([kwin] * nwb), *([vwinT] * nwb), bw)


def _hgrn_chunk(qc, zf, vc, lb, st):
    c, sub = HG_CHUNK, HG_SUB
    logf = jnp.log(lb + (1.0 - lb) * jax.nn.sigmoid(zf))
    kc = (1.0 - lb) * jax.nn.sigmoid(-zf)
    tri = (lax.broadcasted_iota(jnp.int32, (c, c), 0) >= lax.broadcasted_iota(jnp.int32, (c, c), 1)).astype(F32)
    cb = jnp.dot(tri, logf, preferred_element_type=F32, precision=lax.Precision.HIGHEST)
    o = lax.dot_general((qc * jnp.exp(cb)).astype(BF16), st.astype(BF16), (((1,), (1,)), ((), ())),
                        preferred_element_type=F32)
    t3 = lax.broadcasted_iota(jnp.int32, (sub, sub, 1), 0) >= lax.broadcasted_iota(jnp.int32, (sub, sub, 1), 1)
    outs = []
    for a in range(c // sub):
        ra = slice(a * sub, (a + 1) * sub)
        cba, qa, ka, va = cb[ra], qc[ra], kc[ra], vc[ra]
        d3 = cba[:, None, :] - cba[None, :, :]
        x3 = jnp.where(t3, jnp.exp(d3), 0.0) * qa[:, None, :] * ka[None, :, :]
        att3 = jnp.sum(x3, axis=2, keepdims=True)
        oa = o[ra] + jnp.sum(att3 * va[None, :, :], axis=1)
        if a > 0:
            ref = cb[a * sub - 1:a * sub, :]
            qd = (qa * jnp.exp(cba - ref)).astype(BF16)
            kd = (kc[:a * sub] * jnp.exp(ref - cb[:a * sub])).astype(BF16)
            att = lax.dot_general(qd, kd, (((1,), (1,)), ((), ())), preferred_element_type=F32)
            oa = oa + jnp.dot(att.astype(BF16), vc[:a * sub].astype(BF16), preferred_element_type=F32)
        outs.append(oa)
    o = jnp.concatenate(outs, axis=0)
    last = cb[c - 1:c, :]
    kdec = (kc * jnp.exp(last - cb)).astype(BF16)
    st = st * jnp.exp(last) + lax.dot_general(vc.astype(BF16), kdec, (((0,), (0,)), ((), ())),
                                              preferred_element_type=F32)
    return o, st


def _hgrn_kernel(lb_ref, wn_ref, q_ref, f_ref, v_ref, g_ref, s0_ref, o_ref, sfin_ref, st_ref, *, nsub):
    c = pl.program_id(1)
    hpb = HG_HPB

    @pl.when(c == 0)
    def _():
        for j in range(hpb):
            st_ref[j] = s0_ref[j].T

    wn = wn_ref[...]

    def body(u, sts):
        rows = pl.ds(pl.multiple_of(u * HG_CHUNK, HG_CHUNK), HG_CHUNK)
        out = []
        for j in range(hpb):
            cols = slice(j * HG_DK, (j + 1) * HG_DK)
            o, st = _hgrn_chunk(q_ref[rows, cols], f_ref[rows, cols], v_ref[rows, cols], lb_ref[:, cols], sts[j])
            g = g_ref[rows, cols]
            o_ref[rows, cols] = _rms(o, wn) * (g * jax.nn.sigmoid(g))
            out.append(st)
        return tuple(out)

    sts = lax.fori_loop(0, nsub, body, tuple(st_ref[j] for j in range(hpb)))
    for j in range(hpb):
        st_ref[j] = sts[j]

    @pl.when(c == pl.num_programs(1) - 1)
    def _():
        for j in range(hpb):
            sfin_ref[j] = sts[j].T


def _hgrn_prompt(hq, hf, hi, hg, lb, wn, s0, tb):
    t = hq.shape[0]
    hpb = HG_HPB
    assert t % tb == 0 and tb % HG_CHUNK == 0 and HG_HEADS % hpb == 0 and HG_DK == HG_DV
    tok = pl.BlockSpec((tb, hpb * HG_DK), lambda h, c: (c, h))
    stt = pl.BlockSpec((hpb, HG_DK, HG_DV), lambda h, c: (h, 0, 0))
    return pl.pallas_call(
        functools.partial(_hgrn_kernel, nsub=tb // HG_CHUNK),
        grid=(HG_HEADS // hpb, t // tb),
        in_specs=[pl.BlockSpec((1, hpb * HG_DK), lambda h, c: (0, h)),
                  pl.BlockSpec((1, HG_DV), lambda h, c: (0, 0)),
                  tok, tok, tok, tok, stt],
        out_specs=[tok, stt],
        out_shape=[jax.ShapeDtypeStruct((t, HG_HEADS * HG_DV), F32),
                   jax.ShapeDtypeStruct((HG_HEADS, HG_DK, HG_DV), F32)],
        scratch_shapes=[pltpu.VMEM((hpb, HG_DV, HG_DK), F32)],
        compiler_params=_cparams(("arbitrary", "arbitrary")), name="hgrn_prompt",
    )(lb, wn, hq, hf, hi, hg, s0)


def _hgrn_step_kernel(lb_ref, wn_ref, q_ref, f_ref, v_ref, g_ref, s0_ref, o_ref, s_ref):
    wn = wn_ref[...]
    for h in range(HG_HEADS):
        rk = slice(h * HG_DK, (h + 1) * HG_DK)
        lb = lb_ref[rk, :]
        zf = f_ref[rk, :]
        f = lb + (1.0 - lb) * jax.nn.sigmoid(zf)
        kk = (1.0 - lb) * jax.nn.sigmoid(-zf)
        vrow = v_ref[:, h * HG_DV:(h + 1) * HG_DV]
        s_new = f * s0_ref[h] + kk * vrow
        s_ref[h] = s_new
        o = jnp.sum(s_new * q_ref[rk, :], axis=0, keepdims=True)
        g = g_ref[:, h * HG_DV:(h + 1) * HG_DV]
        o_ref[:, h * HG_DV:(h + 1) * HG_DV] = _rms(o, wn) * (g * jax.nn.sigmoid(g))


def _hgrn_step(hq_col, hf_col, hi, hg, lb_col, wn, s0):
    b = hi.shape[0]
    col = pl.BlockSpec((None, HG_WIDTH, 1), lambda i: (i, 0, 0))
    row = pl.BlockSpec((None, 1, HG_WIDTH), lambda i: (i, 0, 0))
    st = pl.BlockSpec((None, HG_HEADS, HG_DK, HG_DV), lambda i: (i, 0, 0, 0))
    return pl.pallas_call(
        _hgrn_step_kernel, grid=(b,),
        in_specs=[pl.BlockSpec((HG_WIDTH, 1), lambda i: (0, 0)), pl.BlockSpec((1, HG_DV), lambda i: (0, 0)),
                  col, col, row, row, st],
        out_specs=[row, st],
        out_shape=[jax.ShapeDtypeStruct((b, 1, HG_WIDTH), F32), jax.ShapeDtypeStruct(s0.shape, F32)],
        compiler_params=_cparams(("parallel",)), name="hgrn_step",
    )(lb_col, wn, hq_col, hf_col, hi, hg, s0)


def _merge_kernel(x_ref, oa_ref, ob_ref, ma_ref, mb_ref, pa_ref, pb_ref, wo_ref, y_ref):
    ya = jnp.dot(oa_ref[...].astype(BF16), pa_ref[...], preferred_element_type=F32)
    yb = jnp.dot(ob_ref[...].astype(BF16), pb_ref[...], preferred_element_type=F32)
    mix = jax.nn.sigmoid(ma_ref[...]) * ya + jax.nn.sigmoid(mb_ref[...]) * yb
    y_ref[...] = x_ref[...] + jnp.dot(mix.astype(BF16), wo_ref[...], preferred_element_type=F32)


def _merge(x, oa, ob, ma, mb, pa, pb, wo, tm):
    t, d = x.shape
    tok = pl.BlockSpec((tm, d), lambda i: (i, 0))
    wsp = pl.BlockSpec((d, d), lambda i: (0, 0))
    return pl.pallas_call(
        _merge_kernel, grid=(t // tm,),
        in_specs=[tok, tok, tok, tok, tok, wsp, wsp, wsp],
        out_specs=tok, out_shape=jax.ShapeDtypeStruct((t, d), F32),
        compiler_params=_cparams(("parallel",)), name="branch_merge",
    )(x, oa, ob, ma, mb, pa, pb, wo)


def _topk_rows(s, k):
    n = s.shape[0]
    rows = lax.broadcasted_iota(jnp.int32, s.shape, 0).astype(F32)
    tops = []
    for r in range(k):
        mx = jnp.max(s, axis=0, keepdims=True)
        idx = jnp.min(jnp.where(s == mx, rows, float(n)), axis=0, keepdims=True)
        s = jnp.where(rows == idx, -TAKEN * (1.0 + r / 64.0), s)
        tops.append(mx)
    rank = jnp.where(s < -0.5 * TAKEN, (s * (-1.0 / TAKEN) - 1.0) * 64.0, float(n))
    return jnp.concatenate(tops, axis=0), rank


_STAIR_GROUPS = ((0, 16), (1, 8), (2, 8), (3, 8))
_STAIR_QUAD = (4, 5, 6, 7)
_STAIR_TAIL = 8


def _stair_rows(t1, t2, op):
    tt = t1.shape[1]
    parts = [op(jnp.broadcast_to(t1[a:a + 1, :], (nb, tt)), t2[0:nb, :]) for a, nb in _STAIR_GROUPS]
    r16 = lax.broadcasted_iota(jnp.int32, (16, tt), 0)
    v1 = jnp.broadcast_to(t1[_STAIR_QUAD[3]:_STAIR_QUAD[3] + 1, :], (16, tt))
    v2 = jnp.broadcast_to(t2[3:4, :], (16, tt))
    for q in (2, 1, 0):
        v1 = jnp.where(r16 < 4 * (q + 1), jnp.broadcast_to(t1[_STAIR_QUAD[q]:_STAIR_QUAD[q] + 1, :], (16, tt)), v1)
        v2 = jnp.where((r16 & 3) == q, jnp.broadcast_to(t2[q:q + 1, :], (16, tt)), v2)
    parts.append(op(v1, v2))
    parts.append(op(t1[_STAIR_TAIL:, :], jnp.broadcast_to(t2[0:1, :], (PK_TOPK - _STAIR_TAIL, tt))))
    return jnp.concatenate(parts, axis=0)


def _stair_row_counts(selc):
    out, r0 = [], 0
    for _, nb in _STAIR_GROUPS:
        out.append(jnp.sum(selc[r0:r0 + nb, :], axis=0, keepdims=True))
        r0 += nb
    quad = selc[r0:r0 + 16, :]
    r16 = lax.broadcasted_iota(jnp.int32, quad.shape, 0)
    for q in range(4):
        out.append(jnp.sum(jnp.where((r16 >> 2) == q, quad, 0.0), axis=0, keepdims=True))
    r0 += 16
    for a in range(PK_TOPK - _STAIR_TAIL):
        out.append(selc[r0 + a:r0 + a + 1, :])
    return out


def _peer_kernel(x_ref, wn_ref, wq_ref, sk_ref, u_ref, vT_ref, wf_ref, y_ref,
                 hnT_ref, acc_ref, wcat_ref, wodd_ref, s_ref, n_ref, a1_ref, r2_ref, e2_ref, *, ib, nkeys, final_norm):
    i = pl.program_id(1)
    kt = PK_TOPK
    hd = PK_DIM // 2

    @pl.when(i == 0)
    def _():
        hnT = _rms(x_ref[...], wn_ref[...]).T.astype(BF16)
        hnT_ref[...] = hnT
        acc_ref[...] = jnp.zeros(acc_ref.shape, F32)
        wcat_ref[...] = jnp.zeros(wcat_ref.shape, BF16)
        wodd_ref[...] = jnp.zeros(wodd_ref.shape, BF16)
        tt = hnT.shape[1]
        nlc = tt // LANES
        lcu = 2 if nlc % 2 == 0 else 1
        for h in range(PK_HEADS):
            for c in range(2):
                r0 = (h * 2 + c) * hd
                qhc = jnp.dot(wq_ref[r0:r0 + hd, :], hnT, preferred_element_type=F32)
                s = jnp.dot(sk_ref[h * 2 + c], qhc.astype(BF16), preferred_element_type=F32)
                for lc in range(nlc):
                    s_ref[c, lc] = s[:, lc * LANES:(lc + 1) * LANES]

            def chunk(lc, carry):
                for sub in range(lcu):
                    chunk_one(lc * lcu + sub)
                return carry

            def chunk_one(lc):
                s0, s1 = s_ref[0, lc], s_ref[1, lc]
                top0, rank0 = _topk_rows(s0, kt)
                top1, rank1 = _topk_rows(s1, kt)
                cand = _stair_rows(top0, top1, jnp.add)
                _, crank = _topk_rows(cand, kt)
                selc = (crank < float(kt)).astype(F32)
                n_a = _stair_row_counts(selc)
                e1t = jnp.exp(top0 - top0[0:1, :])
                e2t = jnp.exp(top1 - top1[0:1, :])
                z = jnp.sum(selc * _stair_rows(e1t, e2t, jnp.multiply), axis=0, keepdims=True)
                nfull = jnp.zeros(s0.shape, F32)
                for a in range(kt):
                    nfull = jnp.where(rank0 == float(a), n_a[a], nfull)
                n_ref[h, lc] = nfull
                a1_ref[h, lc] = jnp.exp(s0 - top0[0:1, :]) / z
                r2_ref[h, lc] = rank1.astype(BF16)
                e2_ref[h, lc] = jnp.exp(s1 - top1[0:1, :]).astype(BF16)

            lax.fori_loop(0, nlc // lcu, chunk, 0)

    nsteps = pl.num_programs(1) - 1
    nch = 2
    cw = hnT_ref.shape[1] // nch

    def step(w_read, w_write):
        hnT = hnT_ref[...]
        nlc = hnT.shape[1] // LANES

        def lanes_cat(ref, h, rows=slice(None)):
            return jnp.concatenate([ref[h, lc, rows, :] for lc in range(nlc)], axis=1)

        for ip in range(ib // 2):
            if ip % (ib // 2 // nch) == 0:
                ch = ip // (ib // 2 // nch)
                cols = slice(ch * cw, (ch + 1) * cw)
                acc_ref[:, cols] += jnp.dot(vT_ref[...], w_read[:, cols], preferred_element_type=F32)
            iis = (2 * ip, 2 * ip + 1)
            aTs = [jnp.dot(u_ref[ii * nkeys:(ii + 1) * nkeys, :], hnT, preferred_element_type=F32) for ii in iis]
            gsums = [jnp.zeros(aTs[0].shape, BF16) for _ in iis]
            for h in range(PK_HEADS):
                r2 = lanes_cat(r2_ref, h)
                e2 = lanes_cat(e2_ref, h)
                for n, ii in enumerate(iis):
                    row = pl.ds(i * ib + ii, 1)
                    nrow = lanes_cat(n_ref, h, row).astype(BF16)
                    arow = lanes_cat(a1_ref, h, row).astype(BF16)
                    gsums[n] = gsums[n] + jnp.where(r2 < nrow, arow * e2, jnp.zeros((), BF16))
            for n, ii in enumerate(iis):
                w_write[ii * nkeys:(ii + 1) * nkeys, :] = jax.nn.gelu(aTs[n].astype(BF16)) * gsums[n]

    @pl.when((i < nsteps) & (i % 2 == 0))
    def _():
        step(wodd_ref, wcat_ref)

    @pl.when((i < nsteps) & (i % 2 == 1))
    def _():
        step(wcat_ref, wodd_ref)

    @pl.when(i == nsteps)
    def _():
        w_last = wodd_ref if (nkeys // ib) % 2 == 0 else wcat_ref
        acc = acc_ref[...] + jnp.dot(vT_ref[...], w_last[...], preferred_element_type=F32)
        y = x_ref[...] + acc.T
        if final_norm:
            y = _rms(y, wf_ref[...])
        y_ref[...] = y


def _peer(x, wn, wqT, sk, u, vT, wf, tt, ib, final_norm):
    t, d = x.shape
    nkeys = sk.shape[1]
    assert t % tt == 0 and nkeys % ib == 0
    nsteps = nkeys // ib
    nlc = tt // LANES
    stat = pltpu.VMEM((PK_HEADS, nlc, nkeys, LANES), F32)
    stat16 = pltpu.VMEM((PK_HEADS, nlc, nkeys, LANES), BF16)
    wbuf = pltpu.VMEM((ib * nkeys, tt), BF16)
    return pl.pallas_call(
        functools.partial(_peer_kernel, ib=ib, nkeys=nkeys, final_norm=final_norm),
        grid=(t // tt, nsteps + 1),
        in_specs=[pl.BlockSpec((tt, d), lambda a, i: (a, 0)),
                  pl.BlockSpec((1, d), lambda a, i: (0, 0)),
                  pl.BlockSpec(wqT.shape, lambda a, i: (0, 0)),
                  pl.BlockSpec(sk.shape, lambda a, i: (0, 0, 0)),
                  pl.BlockSpec((ib * nkeys, d), lambda a, i: (jnp.minimum(i, nsteps - 1), 0)),
                  pl.BlockSpec((None, d, ib * nkeys), lambda a, i: (jnp.maximum(i - 1, 0), 0, 0)),
                  pl.BlockSpec((1, d), lambda a, i: (0, 0))],
        out_specs=pl.BlockSpec((tt, d), lambda a, i: (a, 0)),
        out_shape=jax.ShapeDtypeStruct((t, d), F32),
        scratch_shapes=[pltpu.VMEM((d, tt), BF16), pltpu.VMEM((d, tt), F32), wbuf, wbuf,
                        pltpu.VMEM((2, nlc, nkeys, LANES), F32), stat, stat, stat16, stat16],
        compiler_params=_cparams(("arbitrary", "arbitrary")), name="peer_dense",
    )(x, wn, wqT, sk, u, vT, wf)


def _compress_pages_kernel(pt_ref, *refs, pg, bpp):
    w_ref, o_ref = refs[pg], refs[pg + 1]
    nt = (((1,), (1,)), ((), ()))
    for j in range(pg):
        x = refs[j][...]
        halves = [lax.dot_general(w_ref[c], x[c].reshape(-1, x.shape[-1]).astype(BF16), nt, preferred_element_type=F32)
                  for c in range(x.shape[0])]
        o_ref[j * bpp:(j + 1) * bpp, :] = jnp.concatenate(halves, axis=1)


def _compress_pages(cache_t, page_table, w_sel, pg):
    tail = cache_t.shape[1:]
    db, npages = page_table.shape
    bpp = w_sel.shape[1]
    width = tail[0] * tail[1] * tail[2]
    assert npages % pg == 0
    zeros = (0,) * len(tail)
    page_specs = [pl.BlockSpec((None,) + tail, functools.partial(lambda b, g, pt, j: (pt[b, g * pg + j],) + zeros, j=j))
                  for j in range(pg)]
    return pl.pallas_call(
        functools.partial(_compress_pages_kernel, pg=pg, bpp=bpp),
        grid_spec=pltpu.PrefetchScalarGridSpec(
            num_scalar_prefetch=1, grid=(db, npages // pg),
            in_specs=page_specs + [pl.BlockSpec(w_sel.shape, lambda b, g, pt: (0, 0, 0))],
            out_specs=pl.BlockSpec((None, pg * bpp, width), lambda b, g, pt: (b, g, 0))),
        out_shape=jax.ShapeDtypeStruct((db, npages * bpp, width), F32),
        compiler_params=_cparams(("arbitrary", "arbitrary")), name="nsa_compress_pages",
    )(page_table, *([cache_t] * pg), w_sel)


def _nsa_decode_head_kernel(slope_ref, qbd_ref, g_ref, kvc_ref, wnew_ref, cwin_ref, part_ref, idx_ref,
                            *, past_len, k_past):
    kvw, grp, dh = NSA_KV_WIDTH, NSA_GROUP, NSA_HEAD_DIM
    qbd = qbd_ref[...]
    qb = qbd.astype(BF16)
    slope = slope_ref[...]
    nh = qbd.shape[0]
    npb = kvc_ref.shape[0]
    nt = (((1,), (1,)), ((), ()))
    kvc = kvc_ref[...]
    sc = lax.dot_general(qb, kvc[:, :kvw].astype(BF16), nt, preferred_element_type=F32)
    c_end = lax.broadcasted_iota(jnp.int32, (1, npb), 1) * NSA_BLOCK + (NSA_BLOCK - 1)
    s = sc - slope * (past_len - c_end).astype(F32)
    e = jnp.exp(s - jnp.max(s, axis=1, keepdims=True))
    p = e / jnp.sum(e, axis=1, keepdims=True)
    oc = jnp.dot(p.astype(BF16), kvc[:, kvw:].astype(BF16), preferred_element_type=F32)
    rows = []
    for k in range(NSA_KV_HEADS):
        r = p[k * grp:k * grp + 1, :]
        for g in range(1, grp):
            r = r + p[k * grp + g:k * grp + g + 1, :]
        rows.append(jnp.broadcast_to(r, (grp, npb)))
    imp = jnp.concatenate(rows, axis=0)
    blk = lax.broadcasted_iota(jnp.int32, (nh, npb), 1)
    score = jnp.where((blk == 0) | (blk == npb - 1), grp + 1.0, imp)
    lanes = blk.astype(F32)
    picks = []
    for _ in range(k_past):
        mx = jnp.max(score, axis=1, keepdims=True)
        idx = jnp.min(jnp.where(score == mx, lanes, float(npb)), axis=1, keepdims=True)
        score = jnp.where(lanes == idx, LOWEST, score)
        picks.append(idx)
    idx_ref[...] = jnp.concatenate(picks, axis=1).astype(jnp.int32)
    sg = jax.nn.sigmoid(g_ref[...])
    wnew = wnew_ref[...]
    nw = cwin_ref.shape[-1]
    dw = (nw - lax.broadcasted_iota(jnp.int32, (1, nw), 1)).astype(F32)
    for k in range(NSA_KV_HEADS):
        hs, ds = slice(k * grp, (k + 1) * grp), slice(k * dh, (k + 1) * dh)
        qk = qbd[hs, ds]
        sw = jnp.dot(qk.astype(BF16), cwin_ref[0, k].astype(BF16), preferred_element_type=F32) - slope[hs] * dw
        s_n = jnp.sum(qk * wnew[:, ds], axis=1, keepdims=True)
        m = jnp.maximum(jnp.max(sw, axis=1, keepdims=True), s_n)
        ew = jnp.exp(sw - m)
        en = jnp.exp(s_n - m)
        pv = lax.dot_general(ew.astype(BF16), cwin_ref[1, k].astype(BF16), nt, preferred_element_type=F32)
        ow = (pv + en * wnew[:, kvw + k * dh:kvw + (k + 1) * dh]) / (jnp.sum(ew, axis=1, keepdims=True) + en)
        part_ref[hs, :] = sg[hs, 0:1] * oc[hs, ds] + sg[hs, 2:3] * ow


def _nsa_decode_gather_kernel(pt_ref, ix_ref, slope_ref, q_ref, g_ref, part_ref, new_ref, *refs, nsel, bpp, past_len):
    blocks, o_ref = refs[:nsel], refs[nsel]
    b, k = pl.program_id(0), pl.program_id(1)
    page = blocks[0].shape[-1]
    q = q_ref[...]
    slope = slope_ref[...]
    kt = jnp.concatenate([blocks[s][0] for s in range(nsel)], axis=1).astype(BF16)
    vt = jnp.concatenate([blocks[s][1] for s in range(nsel)], axis=1).astype(BF16)
    sc = jnp.dot(q.astype(BF16), kt, preferred_element_type=F32)
    lane = lax.broadcasted_iota(jnp.int32, (1, page), 1)
    lblk = lane // NSA_BLOCK
    lkey = lane - lblk * NSA_BLOCK
    bias = []
    for s in range(nsel):
        ib = ix_ref[b, k, s]
        kpos = ib * NSA_BLOCK + lkey
        bias.append(jnp.where(lblk == ib % bpp, -slope * (past_len - kpos).astype(F32), NEG_INF))
    sc = sc + jnp.concatenate(bias, axis=1)
    k_own, v_own = new_ref[0], new_ref[1]
    s_n = jnp.sum(q * k_own, axis=1, keepdims=True)
    m = jnp.maximum(jnp.max(sc, axis=1, keepdims=True), s_n)
    p = jnp.exp(sc - m)
    p_n = jnp.exp(s_n - m)
    pv = lax.dot_general(p.astype(BF16), vt, (((1,), (1,)), ((), ())), preferred_element_type=F32)
    o_s = (pv + p_n * v_own) / (jnp.sum(p, axis=1, keepdims=True) + p_n)
    o_ref[...] = part_ref[...] + jax.nn.sigmoid(g_ref[...])[:, 1:2] * o_s


def _rows_minor(a):
    return a.transpose(0, 2, 3, 4, 1)


def _nsa_sample(q, kv_cmp, kv_sel, kv_win, glog, cache_cmp, cache_sel, cache_win, page_table, w_cmp):
    db = q.shape[0]
    kvh, grp, dh, kvw = NSA_KV_HEADS, NSA_GROUP, NSA_HEAD_DIM, NSA_KV_WIDTH
    page = cache_cmp.shape[1]
    npages = page_table.shape[1]
    past_len = npages * page
    nwin = cache_win.shape[1]
    assert past_len >= nwin and page % NSA_BLOCK == 0
    npb = past_len // NSA_BLOCK
    bpp = page // NSA_BLOCK
    pg = 16 if npages % 16 == 0 else 1
    width = 2 * kvw
    rowblk = jnp.arange(page) // NSA_BLOCK
    w_sel = jnp.where(rowblk[None, None, :] == jnp.arange(bpp)[None, :, None],
                      jnp.tile(w_cmp, (1, bpp))[:, None, :], 0.0).astype(BF16)
    kvc = _compress_pages(_rows_minor(cache_cmp), page_table, w_sel, pg)
    q4 = q.reshape(db, kvh, grp, dh) * (dh ** -0.5)
    qbd = (q4[:, :, :, None, :] * jnp.eye(kvh, dtype=F32)[None, :, None, :, None]).reshape(db, kvh * grp, kvw)
    slope = _alibi_slopes().reshape(kvh * grp, 1)
    nh = kvh * grp
    k_past = min(NSA_TOPK, npb + 1) - 1
    full2 = lambda shape: pl.BlockSpec(shape, lambda b: (0,) * len(shape))
    per_b = lambda shape: pl.BlockSpec((None,) + shape, lambda b: (b,) + (0,) * len(shape))
    part, idx = pl.pallas_call(
        functools.partial(_nsa_decode_head_kernel, past_len=past_len, k_past=k_past),
        grid=(db,),
        in_specs=[full2((nh, 1)), per_b((nh, kvw)), per_b((nh, 3)), per_b((npb, width)), per_b((1, width)),
                  per_b((2, kvh, dh, nwin))],
        out_specs=[per_b((nh, dh)), per_b((nh, k_past))],
        out_shape=[jax.ShapeDtypeStruct((db, nh, dh), F32), jax.ShapeDtypeStruct((db, nh, k_past), jnp.int32)],
        compiler_params=_cparams(("parallel",)), name="nsa_decode_head",
    )(slope, qbd, glog, kvc, kv_win.reshape(db, 1, width), _rows_minor(cache_win))
    idx4 = idx.reshape(db, kvh, grp, k_past)[:, :, 0, :]
    blk_specs = [pl.BlockSpec((None, 2, None, dh, page),
                              functools.partial(lambda b, k, pt, ix, s: (pt[b, ix[b, k, s] // bpp], 0, k, 0, 0), s=s))
                 for s in range(k_past)]
    hsp = lambda shape: pl.BlockSpec((None, None) + shape, lambda b, k, pt, ix: (b, k) + (0,) * len(shape))
    o = pl.pallas_call(
        functools.partial(_nsa_decode_gather_kernel, nsel=k_past, bpp=bpp, past_len=past_len),
        grid_spec=pltpu.PrefetchScalarGridSpec(
            num_scalar_prefetch=2, grid=(db, kvh),
            in_specs=[pl.BlockSpec((None, grp, 1), lambda b, k, pt, ix: (k, 0, 0)), hsp((grp, dh)), hsp((grp, 3)),
                      hsp((grp, dh)),
                      pl.BlockSpec((None, 2, None, 1, dh), lambda b, k, pt, ix: (b, 0, k, 0, 0))] + blk_specs,
            out_specs=hsp((grp, dh))),
        out_shape=jax.ShapeDtypeStruct((db, kvh, grp, dh), F32),
        compiler_params=_cparams(("arbitrary", "arbitrary")), name="nsa_decode_gather",
    )(page_table, idx4, slope.reshape(kvh, grp, 1), q4, glog.reshape(db, kvh, grp, 3), part.reshape(db, kvh, grp, dh),
      kv_sel.reshape(db, 2, kvh, 1, dh), *([_rows_minor(cache_sel)] * k_past))
    new_win = jnp.concatenate([cache_win, kv_win.reshape((db, 1) + cache_win.shape[2:])], axis=1)[:, 1:]
    return o.reshape(db, nh * dh), new_win


def _split_w_in(w_in):
    pts, acc = [], 0
    for w in (NSA_Q_WIDTH, 6 * NSA_KV_WIDTH, 3 * NSA_HEADS, HG_WIDTH, HG_WIDTH, HG_WIDTH, HG_WIDTH, w_in.shape[0]):
        acc += w
        pts.append(acc)
    return jnp.split(w_in, pts, axis=1)


def _prep_layer(w_in, w_proj_nsa, w_proj_hgrn, w_out, w_peer_q, peer_sub_keys, peer_u, peer_v):
    d = w_in.shape[0]
    wq, wkv, wg, wbq, wbf, wbi, wbg, wma, wmb = _split_w_in(w_in)
    wg = wg.reshape(d, NSA_KV_HEADS, NSA_GROUP, 3).transpose(0, 1, 3, 2).reshape(d, NSA_KV_HEADS, 3 * NSA_GROUP)
    wg = jnp.pad(wg, ((0, 0), (0, 0), (0, 16 - 3 * NSA_GROUP))).reshape(d, NSA_KV_HEADS * 16)
    wg = jnp.pad(wg, ((0, 0), (0, LANES - NSA_KV_HEADS * 16)))
    nkeys = peer_sub_keys.shape[2]
    return dict(
        w_qg=jnp.concatenate([wq, wg], axis=1).astype(BF16),
        w_kv=wkv.astype(BF16),
        w_hg=jnp.concatenate([wbq, wbf, wbi, wbg], axis=1).astype(BF16),
        w_m=jnp.concatenate([wma, wmb], axis=1).astype(BF16),
        pa=w_proj_nsa.astype(BF16), pb=w_proj_hgrn.astype(BF16), wo=w_out.astype(BF16),
        wqT=w_peer_q.T.astype(BF16),
        sk=peer_sub_keys.reshape(PK_HEADS * 2, nkeys, PK_DIM // 2).astype(BF16),
        u=peer_u.astype(BF16),
        vT=peer_v.astype(BF16).reshape(nkeys // PEER_IB, PEER_IB * nkeys, d).transpose(0, 2, 1),
    )


def _tile(t, pref):
    return pref if t % pref == 0 else t


def _peer_tokens(x, wn, wf, prm, final_norm):
    t = x.shape[0]
    tp = -(-t // LANES) * LANES
    xp = jnp.pad(x, ((0, tp - t), (0, 0)))
    tt = PEER_TT if tp % PEER_TT == 0 else LANES
    y = _peer(xp, wn, prm["wqT"], prm["sk"], prm["u"], prm["vT"], wf, tt, PEER_IB, final_norm)
    return y[:t]


def _layer_prompt(x, lb, prm, w_norm_mix, w_cmp, w_hgrn_norm, w_norm_ffn, w_norm_final, final_norm):
    s_len, d = x.shape
    tm = _tile(s_len, 256)
    kvw = NSA_KV_WIDTH
    kv_cmp, kv_sel, kv_win = _proj(x, w_norm_mix, prm["w_kv"], [2 * kvw] * 3, [False] * 3, tm)
    qT, gT = _proj(x, w_norm_mix, prm["w_qg"], [NSA_Q_WIDTH, LANES], [True, True], tm)
    hq, hf, hi, hg = _proj(x, w_norm_mix, prm["w_hg"], [HG_WIDTH] * 4, [False] * 4, tm)
    ma, mb = _proj(x, w_norm_mix, prm["w_m"], [d, d], [False, False], tm)
    wfull = jnp.concatenate([jnp.broadcast_to(w_cmp[0][:, None], (NSA_BLOCK, kvw)),
                             jnp.broadcast_to(w_cmp[1][:, None], (NSA_BLOCK, kvw))], axis=1)
    nblk = s_len // NSA_BLOCK
    kvc = _compress(kv_cmp, wfull, 8 if nblk % 8 == 0 else nblk)
    o_a = _nsa_prompt(qT, gT[:NSA_KV_HEADS * 16], kvc, kv_sel, kv_win)
    s0 = jnp.zeros((HG_HEADS, HG_DK, HG_DV), F32)
    o_b, s_fin = _hgrn_prompt(hq, hf, hi, hg, lb, w_hgrn_norm, s0, _tile(s_len, 512))
    x1 = _merge(x, o_a, o_b, ma, mb, prm["pa"], prm["pb"], prm["wo"], tm)
    x2 = _peer_tokens(x1, w_norm_ffn, w_norm_final, prm, final_norm)
    shp = (s_len, 2, NSA_KV_HEADS, NSA_HEAD_DIM)
    return x2, kv_cmp.reshape(shp), kv_sel.reshape(shp), kv_win.reshape(shp)[-NSA_WINDOW:], s_fin


def _layer_sample(x, lb, prm, w_norm_mix, w_cmp, w_hgrn_norm, w_norm_ffn, w_norm_final, final_norm,
                  cache_cmp, cache_sel, cache_win, s0, page_table):
    b, d = x.shape
    kvw = NSA_KV_WIDTH
    w_q = prm["w_qg"][:, :NSA_Q_WIDTH]
    (q,) = _proj(x, w_norm_mix, w_q, [NSA_Q_WIDTH], [False], b)
    kv_cmp, kv_sel, kv_win = _proj(x, w_norm_mix, prm["w_kv"], [2 * kvw] * 3, [False] * 3, b)
    (gTt,) = _proj(x, w_norm_mix, prm["w_qg"][:, NSA_Q_WIDTH:], [LANES], [False], b)
    hq, hf, hi, hg = _proj(x, w_norm_mix, prm["w_hg"], [HG_WIDTH] * 4, [False] * 4, b)
    ma, mb = _proj(x, w_norm_mix, prm["w_m"], [d, d], [False, False], b)
    shp = (b, 1, 2, NSA_KV_HEADS, NSA_HEAD_DIM)
    glog = (gTt[:, :NSA_KV_HEADS * 16].reshape(b, NSA_KV_HEADS, 16)[..., :3 * NSA_GROUP]
            .reshape(b, NSA_KV_HEADS, 3, NSA_GROUP).transpose(0, 1, 3, 2).reshape(b, NSA_HEADS, 3))
    o_a, new_win = _nsa_sample(q, kv_cmp, kv_sel, kv_win, glog, cache_cmp, cache_sel, cache_win, page_table, w_cmp)
    o_b, s_new = _hgrn_step(hq.reshape(b, HG_WIDTH, 1), hf.reshape(b, HG_WIDTH, 1), hi.reshape(b, 1, HG_WIDTH),
                            hg.reshape(b, 1, HG_WIDTH), lb.reshape(HG_WIDTH, 1), w_hgrn_norm, s0)
    x1 = _merge(x, o_a.reshape(b, NSA_Q_WIDTH), o_b.reshape(b, HG_WIDTH), ma, mb, prm["pa"], prm["pb"], prm["wo"], b)
    x2 = _peer_tokens(x1, w_norm_ffn, w_norm_final, prm, final_norm)
    return x2, kv_cmp.reshape(shp), kv_sel.reshape(shp), new_win, s_new


def kernel(x_prompt, x_sample, cache_cmp_kv, cache_sel_kv, cache_win_kv, state_hgrn, page_table,
           w_norm_mix, w_in, w_cmp, w_proj_nsa, w_proj_hgrn, w_hgrn_norm, hgrn_lb_logits, w_out,
           w_norm_ffn, w_peer_q, peer_sub_keys, peer_u, peer_v, w_norm_final):
    depth = w_in.shape[0]
    bsz, s_len, d = x_prompt.shape
    db, dt, _ = x_sample.shape
    assert dt == 1
    lbs = jnp.cumsum(jax.nn.softmax(hgrn_lb_logits.astype(F32), axis=0), axis=0)
    wfin = w_norm_final.reshape(1, d)
    xp = [x_prompt[b] for b in range(bsz)]
    xs = x_sample.reshape(db, d)
    st_p, st_s = [], []
    for l in range(depth):
        last = l == depth - 1
        prm = _prep_layer(w_in[l], w_proj_nsa[l], w_proj_hgrn[l], w_out[l], w_peer_q[l], peer_sub_keys[l],
                          peer_u[l], peer_v[l])
        shared = (lbs[l].reshape(1, HG_WIDTH), prm, w_norm_mix[l].reshape(1, d), w_cmp[l],
                  w_hgrn_norm[l].reshape(1, HG_DV), w_norm_ffn[l].reshape(1, d), wfin, last)
        outs = [_layer_prompt(xp[b], *shared) for b in range(bsz)]
        xp = [o[0] for o in outs]
        st_p.append(tuple(jnp.stack([o[k] for o in outs]) for k in range(1, 5)))
        xs, *ss = _layer_sample(xs, *shared, cache_cmp_kv[l], cache_sel_kv[l], cache_win_kv[l], state_hgrn[l],
                                page_table)
        st_s.append(tuple(ss))
    y_prompt = jnp.stack(xp)
    y_sample = xs.reshape(db, dt, d)
    return (y_prompt, y_sample,
            jnp.stack([s[0] for s in st_p]), jnp.stack([s[1] for s in st_p]),
            jnp.stack([s[2] for s in st_p]), jnp.stack([s[3] for s in st_p]),
            jnp.stack([s[0] for s in st_s]), jnp.stack([s[1] for s in st_s]),
            jnp.stack([s[2] for s in st_s]), jnp.stack([s[3] for s in st_s]))
```

```python
import functools

import jax
import jax.numpy as jnp
from jax import lax
from jax.experimental import pallas as pl
from jax.experimental.pallas import tpu as pltpu

F32 = jnp.float32
BF16 = jnp.bfloat16

NSA_HEADS = 16
NSA_KV_HEADS = 4
NSA_GROUP = NSA_HEADS // NSA_KV_HEADS
NSA_HEAD_DIM = 64
NSA_BLOCK = 64
NSA_TOPK = 16
NSA_WINDOW = 512
NSA_QBLOCK = 256
HG_HEADS = 8
HG_DK = 128
HG_DV = 128
HG_CHUNK = 64
HG_SUB = 16
HG_HPB = 8
PK_HEADS = 8
PK_DIM = 256
PK_TOPK = 16
RMS_EPS = 1e-6
NEG_INF = -1e30
LOWEST = -3e38
TAKEN = 2.0 ** 100

NSA_Q_WIDTH = NSA_HEADS * NSA_HEAD_DIM
NSA_KV_WIDTH = NSA_KV_HEADS * NSA_HEAD_DIM
HG_WIDTH = HG_HEADS * HG_DK

LANES = 128
VMEM_LIMIT_BYTES = 56 * 1024 * 1024

NSA_COLS = NSA_GROUP * NSA_QBLOCK
NSA_KTILE = 512
NSA_CDIM = 128
NSA_WKEYS = NSA_WINDOW + NSA_QBLOCK
PEER_TT = 512
PEER_IB = 8


def _cparams(sem):
    return pltpu.CompilerParams(dimension_semantics=sem, vmem_limit_bytes=VMEM_LIMIT_BYTES)


def _rms(x, w):
    return x * lax.rsqrt(jnp.mean(x * x, axis=-1, keepdims=True) + RMS_EPS) * w


def _proj_kernel(x_ref, wn_ref, w_ref, *out_refs, widths, transposed, chunk):
    hb = _rms(x_ref[...], wn_ref[...]).astype(BF16)
    off = 0
    for o_ref, wd, tr in zip(out_refs, widths, transposed):
        for c0 in range(0, wd, chunk):
            cw = min(chunk, wd - c0)
            r = jnp.dot(hb, w_ref[:, off + c0:off + c0 + cw], preferred_element_type=F32)
            if tr:
                o_ref[c0:c0 + cw, :] = r.T
            else:
                o_ref[:, c0:c0 + cw] = r
        off += wd


def _proj(x, wn, w, widths, transposed, tm):
    t, d = x.shape
    n = w.shape[1]
    assert sum(widths) == n and t % tm == 0
    out_shape, out_specs = [], []
    for wd, tr in zip(widths, transposed):
        if tr:
            out_shape.append(jax.ShapeDtypeStruct((wd, t), F32))
            out_specs.append(pl.BlockSpec((wd, tm), lambda i: (0, i)))
        else:
            out_shape.append(jax.ShapeDtypeStruct((t, wd), F32))
            out_specs.append(pl.BlockSpec((tm, wd), lambda i: (i, 0)))
    return pl.pallas_call(
        functools.partial(_proj_kernel, widths=tuple(widths), transposed=tuple(transposed), chunk=512),
        grid=(t // tm,),
        in_specs=[pl.BlockSpec((tm, d), lambda i: (i, 0)),
                  pl.BlockSpec((1, d), lambda i: (0, 0)),
                  pl.BlockSpec((d, n), lambda i: (0, 0))],
        out_specs=out_specs, out_shape=out_shape,
        compiler_params=_cparams(("parallel",)), name="rms_proj",
    )(x, wn, w)


def _compress_kernel(kv_ref, w_ref, o_ref, *, nb):
    x = kv_ref[...]
    width = x.shape[-1]
    x3 = x.reshape(nb, NSA_BLOCK, width) * w_ref[...][None]
    o_ref[...] = jnp.sum(x3, axis=1)


def _compress(kv, wfull, nb):
    t, width = kv.shape
    rows = nb * NSA_BLOCK
    assert t % rows == 0
    return pl.pallas_call(
        functools.partial(_compress_kernel, nb=nb),
        grid=(t // rows,),
        in_specs=[pl.BlockSpec((rows, width), lambda i: (i, 0)),
                  pl.BlockSpec((NSA_BLOCK, width), lambda i: (0, 0))],
        out_specs=pl.BlockSpec((nb, width), lambda i: (i, 0)),
        out_shape=jax.ShapeDtypeStruct((t // NSA_BLOCK, width), F32),
        compiler_params=_cparams(("parallel",)), name="nsa_compress",
    )(kv, wfull)


def _topk_select_bias(score, k):
    n = score.shape[0]
    rows = lax.broadcasted_iota(jnp.int32, score.shape, 0).astype(F32)
    for _ in range(k):
        mx = jnp.max(score, axis=0, keepdims=True)
        idx = jnp.min(jnp.where(score == mx, rows, float(n)), axis=0, keepdims=True)
        score = jnp.where(rows == idx, -TAKEN, score)
    return jnp.where(score < -0.5 * TAKEN, 0.0, NEG_INF)


def _nsa_prompt_kernel(slope_ref, qT_ref, gT_ref, kc_ref, vcT_ref, ksel_ref, vselT_ref,
                       kd_ref, vdT_ref, *rest, nblk, nwb):
    kws, vws = rest[:nwb], rest[nwb:2 * nwb]
    bw_ref, out_ref, qs_ref, selb_ref, sa_ref, sb_ref, pa_ref, pb_ref, tiles_ref = rest[2 * nwb:]
    i = pl.program_id(1)
    t0 = i * NSA_QBLOCK
    dh, qb, ncol, tk = NSA_HEAD_DIM, NSA_QBLOCK, NSA_COLS, NSA_KTILE
    slope = slope_ref[...]
    col = lax.broadcasted_iota(jnp.int32, (1, ncol), 1)
    tpos = t0 + (col & (qb - 1))
    tposf = tpos.astype(F32)
    q4 = qT_ref[...] * (dh ** -0.5)
    qT = jnp.concatenate([q4[g * dh:(g + 1) * dh, :] for g in range(NSA_GROUP)], axis=1)
    xrow = lax.broadcasted_iota(jnp.int32, (NSA_CDIM - dh, ncol), 0)

    qc = jnp.concatenate([qT, jnp.zeros((NSA_CDIM - dh, ncol), F32)], axis=0).astype(BF16)
    sc = jnp.dot(kc_ref[...], qc, preferred_element_type=F32)
    c_end = lax.broadcasted_iota(jnp.int32, (nblk, 1), 0) * NSA_BLOCK + (NSA_BLOCK - 1)
    valid = c_end <= tpos
    s = jnp.where(valid, sc - slope * (tposf - c_end.astype(F32)), NEG_INF)
    e = jnp.exp(s - jnp.max(s, axis=0, keepdims=True))
    p = jnp.where(valid, e / jnp.sum(e, axis=0, keepdims=True), 0.0)
    ocT = jnp.dot(vcT_ref[...], p.astype(BF16), preferred_element_type=F32)

    qw = jnp.concatenate([qT, jnp.where(xrow == 0, NEG_INF, 0.0)], axis=0).astype(BF16)
    kw = jnp.concatenate([r[...] for r in kws], axis=0)
    sw = jnp.dot(kw, qw, preferred_element_type=F32) + bw_ref[...]
    ew = jnp.exp(sw - jnp.max(sw, axis=0, keepdims=True))
    vw = jnp.concatenate([r[...] for r in vws], axis=1)
    owT = jnp.dot(vw, ew.astype(BF16), preferred_element_type=F32) / jnp.sum(ew, axis=0, keepdims=True)

    nbt = tk // NSA_BLOCK
    jl = t0 // tk
    s1 = slope.astype(BF16).astype(F32)
    r1 = slope - s1
    s2 = r1.astype(BF16).astype(F32)
    s3 = (r1 - s2).astype(BF16).astype(F32)
    half = float(tk // 2)
    ext = jnp.zeros((NSA_CDIM - dh, ncol), F32)
    for r, v in enumerate((s1, s2, s3, s1 * half, s2 * half, s3 * half)):
        ext = jnp.where(xrow == nbt + r, v, ext)

    qd = jnp.concatenate([qT, ext], axis=0).astype(BF16)
    sd = jnp.dot(kd_ref[...], qd, preferred_element_type=F32)
    kposd = t0 + lax.broadcasted_iota(jnp.int32, (qb, 1), 0)
    sd = jnp.where(kposd > tpos, NEG_INF, sd)
    mx_d = jnp.max(sd, axis=0, keepdims=True)
    pd = jnp.exp(sd - mx_d)
    m_d = mx_d + slope * (jl * tk - tpos).astype(F32)
    l_d = jnp.sum(pd, axis=0, keepdims=True)
    acc_d = jnp.dot(vdT_ref[...], pd.astype(BF16), preferred_element_type=F32)

    imp = p[:, 0:qb]
    for g in range(1, NSA_GROUP):
        imp = imp + p[:, g * qb:(g + 1) * qb]
    blk = lax.broadcasted_iota(jnp.int32, (nblk, qb), 0)
    cur = tpos[:, 0:qb] >> (NSA_BLOCK.bit_length() - 1)
    forced = (blk == 0) | (blk == cur) | (blk == cur - 1)
    score = jnp.where(blk > cur, -1.0, jnp.where(forced, NSA_GROUP + 1.0, imp))
    if nblk >= NSA_TOPK:
        selb = _topk_select_bias(jnp.where(forced, -TAKEN, score), NSA_TOPK - 3)
    else:
        selb = _topk_select_bias(score, nblk)
    blkc = lax.broadcasted_iota(jnp.int32, (nblk, ncol), 0)
    selb_ref[0:nblk, :] = jnp.where(blkc >= t0 // NSA_BLOCK, NEG_INF, jnp.concatenate([selb] * NSA_GROUP, axis=1))
    selb_ref[nblk:nblk + nbt, :] = jnp.full((nbt, ncol), NEG_INF, F32)

    fl = jnp.max(selb_ref[0:nblk, 0:qb].reshape(nblk // nbt, nbt, qb), axis=1)
    tiles_ref[0] = 0
    n_act = jnp.int32(0)
    for j in range(nblk // nbt):
        tiles_ref[n_act] = j
        n_act = n_act + (jnp.max(fl[j:j + 1, :]) > 0.5 * NEG_INF).astype(jnp.int32)
    npairs = (n_act + 1) // 2

    qs_ref[0:dh, :] = qT
    qs_ref[dh:, :] = ext

    def tile_id(t):
        return tiles_ref[jnp.clip(t, 0, jnp.maximum(n_act - 1, 0))]

    def qk_into(t, s_ref):
        tid = tile_id(t)
        b0 = pl.multiple_of(jnp.where(t < n_act, tid * nbt, nblk), 8)
        qs_ref[dh:dh + nbt, :] = selb_ref[pl.ds(b0, nbt), :]
        k0 = pl.multiple_of(tid * tk, tk)
        s_ref[...] = jnp.dot(ksel_ref[pl.ds(k0, tk), :], qs_ref[...].astype(BF16), preferred_element_type=F32)

    def soft(t, s_ref, p_ref, m):
        cj = slope * (tile_id(t) * tk - tpos).astype(F32)
        sj = s_ref[...]
        m_new = jnp.maximum(m, jnp.max(sj, axis=0, keepdims=True) + cj)
        p_ref[...] = jnp.exp(sj - (m_new - cj)).astype(BF16)
        return m_new, jnp.exp(m - m_new)

    def pv(t, p_ref):
        return jnp.dot(vselT_ref[tile_id(t)], p_ref[...], preferred_element_type=F32)

    def pair(i, carry):
        m, accp = carry
        ta = 2 * i
        pvb = pv(ta - 1, pb_ref)
        qk_into(ta + 1, sb_ref)
        m, alpha = soft(ta, sa_ref, pa_ref, m)
        accp = alpha * (accp + pvb)
        pva = pv(ta, pa_ref)
        qk_into(ta + 2, sa_ref)
        m, alpha = soft(ta + 1, sb_ref, pb_ref, m)
        accp = alpha * (accp + pva)
        return m, accp

    pb_ref[...] = jnp.zeros(pb_ref.shape, BF16)
    qk_into(0, sa_ref)
    vrows = vselT_ref.shape[1]
    init = (jnp.full((1, ncol), NEG_INF, F32), jnp.zeros((vrows, ncol), F32))
    m_s, accp = lax.fori_loop(0, npairs, pair, init)
    accl = accp + pv(2 * npairs - 1, pb_ref)
    acc_s, l_s = accl[0:dh], accl[dh:dh + 1]

    m_f = jnp.maximum(m_s, m_d)
    a_s = jnp.exp(m_s - m_f)
    a_d = jnp.exp(m_d - m_f)
    osT = (a_s * acc_s + a_d * acc_d) / (a_s * l_s + a_d * l_d)

    sg = jax.nn.sigmoid(gT_ref[...])

    def gate(c):
        return jnp.concatenate([sg[c * NSA_GROUP + g:c * NSA_GROUP + g + 1, :] for g in range(NSA_GROUP)], axis=1)

    oT = gate(0) * ocT + gate(1) * osT + gate(2) * owT
    o4 = jnp.concatenate([oT[:, g * qb:(g + 1) * qb] for g in range(NSA_GROUP)], axis=0)
    out_ref[...] = o4.T


def _alibi_slopes():
    h = jnp.arange(NSA_HEADS, dtype=F32)
    return (2.0 ** (-8.0 * (h + 1.0) / NSA_HEADS)).reshape(NSA_KV_HEADS, NSA_GROUP)


def _nsa_prompt(qT, gT, kvc, kv_sel, kv_win):
    s_len = qT.shape[1]
    dh, qb, tk, kvh = NSA_HEAD_DIM, NSA_QBLOCK, NSA_KTILE, NSA_KV_HEADS
    assert s_len % tk == 0
    nblk = s_len // NSA_BLOCK
    slopes = _alibi_slopes()
    slope_cols = jnp.repeat(slopes, qb, axis=1).reshape(kvh, 1, NSA_COLS)

    def heads_major(a):
        return a.reshape(a.shape[0], kvh, dh).transpose(1, 0, 2)

    def pad_lanes(a):
        return jnp.pad(a, ((0, 0), (0, 0), (0, NSA_CDIM - a.shape[-1])))

    kc = pad_lanes(heads_major(kvc[:, :NSA_KV_WIDTH])).astype(BF16)
    vcT = heads_major(kvc[:, NSA_KV_WIDTH:]).transpose(0, 2, 1).astype(BF16)
    r = jnp.arange(tk)
    onehot = (r[:, None] // NSA_BLOCK == jnp.arange(tk // NSA_BLOCK)[None, :]).astype(F32)
    lo = (r % (tk // 2)).astype(F32)[:, None]
    hi = (r // (tk // 2)).astype(F32)[:, None]
    kext = jnp.concatenate([onehot, lo, lo, lo, hi, hi, hi], axis=1)
    kext = jnp.tile(kext, (s_len // tk, 1))
    ksel = pad_lanes(jnp.concatenate(
        [heads_major(kv_sel[:, :NSA_KV_WIDTH]), jnp.broadcast_to(kext[None], (kvh,) + kext.shape)], axis=-1)).astype(BF16)
    vsel = heads_major(kv_sel[:, NSA_KV_WIDTH:]).astype(BF16)
    vselT = vsel.reshape(kvh, s_len // tk, tk, dh).transpose(0, 1, 3, 2)
    ones_rows = jnp.zeros((kvh, s_len // tk, 8, tk), BF16).at[:, :, 0, :].set(1.0)
    vselT = jnp.concatenate([vselT, ones_rows], axis=2)
    vselT_flat = vsel.transpose(0, 2, 1)
    kwin = heads_major(kv_win[:, :NSA_KV_WIDTH])
    kwin = jnp.concatenate([kwin, jnp.zeros((kvh, s_len, 1), F32)], axis=-1)
    padk = jnp.zeros((kvh, NSA_WINDOW, dh + 1), F32).at[:, :, dh].set(1.0)
    kwin = pad_lanes(jnp.concatenate([padk, kwin], axis=1)).astype(BF16)
    vwinT = jnp.pad(heads_major(kv_win[:, NSA_KV_WIDTH:]), ((0, 0), (NSA_WINDOW, 0), (0, 0))).transpose(0, 2, 1).astype(BF16)
    rr = jnp.arange(NSA_WKEYS)[:, None]
    cc = jnp.arange(NSA_COLS)[None, :]
    dw = (cc % qb) + NSA_WINDOW - rr
    bw = jnp.where((dw >= 0) & (dw <= NSA_WINDOW), -slope_cols * dw.astype(F32)[None], NEG_INF)

    nq = s_len // qb
    nwb = NSA_WKEYS // qb
    kw_specs = [pl.BlockSpec((None, qb, NSA_CDIM), functools.partial(lambda k, i, j: (k, i + j, 0), j=j)) for j in range(nwb)]
    vw_specs = [pl.BlockSpec((None, dh, qb), functools.partial(lambda k, i, j: (k, 0, i + j), j=j)) for j in range(nwb)]
    assert NSA_WINDOW % qb == 0 and s_len % qb == 0 and tk % qb == 0
    return pl.pallas_call(
        functools.partial(_nsa_prompt_kernel, nblk=nblk, nwb=nwb),
        grid=(kvh, nq),
        in_specs=[pl.BlockSpec((None, 1, NSA_COLS), lambda k, i: (k, 0, 0)),
                  pl.BlockSpec((NSA_GROUP * dh, qb), lambda k, i: (k, i)),
                  pl.BlockSpec((None, 16, qb), lambda k, i: (k, 0, i)),
                  pl.BlockSpec((None, nblk, NSA_CDIM), lambda k, i: (k, 0, 0)),
                  pl.BlockSpec((None, dh, nblk), lambda k, i: (k, 0, 0)),
                  pl.BlockSpec((None, s_len, NSA_CDIM), lambda k, i: (k, 0, 0)),
                  pl.BlockSpec((None, s_len // tk, dh + 8, tk), lambda k, i: (k, 0, 0, 0)),
                  pl.BlockSpec((None, qb, NSA_CDIM), lambda k, i: (k, i, 0)),
                  pl.BlockSpec((None, dh, qb), lambda k, i: (k, 0, i))]
                 + kw_specs + vw_specs
                 + [pl.BlockSpec((None, NSA_WKEYS, NSA_COLS), lambda k, i: (k, 0, 0))],
        out_specs=pl.BlockSpec((qb, NSA_GROUP * dh), lambda k, i: (i, k)),
        out_shape=jax.ShapeDtypeStruct((s_len, NSA_Q_WIDTH), F32),
        scratch_shapes=[pltpu.VMEM((NSA_CDIM, NSA_COLS), F32),
                        pltpu.VMEM((nblk + tk // NSA_BLOCK, NSA_COLS), F32),
                        pltpu.VMEM((tk, NSA_COLS), F32), pltpu.VMEM((tk, NSA_COLS), F32),
                        pltpu.VMEM((tk, NSA_COLS), BF16), pltpu.VMEM((tk, NSA_COLS), BF16),
                        pltpu.SMEM((s_len // tk + 1,), jnp.int32)],
        compiler_params=_cparams(("arbitrary", "arbitrary")), name="nsa_prompt",
    )(slope_cols, qT, gT.reshape(kvh, 16, s_len), kc, vcT, ksel, vselT, ksel, vselT_flat,
      *([kwin] * nwb), *([vwinT] * nwb), bw)


def _hgrn_chunk(qc, zf, vc, lb, st):
    c, sub = HG_CHUNK, HG_SUB
    logf = jnp.log(lb + (1.0 - lb) * jax.nn.sigmoid(zf))
    kc = (1.0 - lb) * jax.nn.sigmoid(-zf)
    tri = (lax.broadcasted_iota(jnp.int32, (c, c), 0) >= lax.broadcasted_iota(jnp.int32, (c, c), 1)).astype(F32)
    cb = jnp.dot(tri, logf, preferred_element_type=F32, precision=lax.Precision.HIGHEST)
    o = lax.dot_general((qc * jnp.exp(cb)).astype(BF16), st.astype(BF16), (((1,), (1,)), ((), ())),
                        preferred_element_type=F32)
    t3 = lax.broadcasted_iota(jnp.int32, (sub, sub, 1), 0) >= lax.broadcasted_iota(jnp.int32, (sub, sub, 1), 1)
    outs = []
    for a in range(c // sub):
        ra = slice(a * sub, (a + 1) * sub)
        cba, qa, ka, va = cb[ra], qc[ra], kc[ra], vc[ra]
        d3 = cba[:, None, :] - cba[None, :, :]
        x3 = jnp.where(t3, jnp.exp(d3), 0.0) * qa[:, None, :] * ka[None, :, :]
        att3 = jnp.sum(x3, axis=2, keepdims=True)
        oa = o[ra] + jnp.sum(att3 * va[None, :, :], axis=1)
        if a > 0:
            ref = cb[a * sub - 1:a * sub, :]
            qd = (qa * jnp.exp(cba - ref)).astype(BF16)
            kd = (kc[:a * sub] * jnp.exp(ref - cb[:a * sub])).astype(BF16)
            att = lax.dot_general(qd, kd, (((1,), (1,)), ((), ())), preferred_element_type=F32)
            oa = oa + jnp.dot(att.astype(BF16), vc[:a * sub].astype(BF16), preferred_element_type=F32)
        outs.append(oa)
    o = jnp.concatenate(outs, axis=0)
    last = cb[c - 1:c, :]
    kdec = (kc * jnp.exp(last - cb)).astype(BF16)
    st = st * jnp.exp(last) + lax.dot_general(vc.astype(BF16), kdec, (((0,), (0,)), ((), ())),
                                              preferred_element_type=F32)
    return o, st


def _hgrn_kernel(lb_ref, wn_ref, q_ref, f_ref, v_ref, g_ref, s0_ref, o_ref, sfin_ref, st_ref, *, nsub):
    c = pl.program_id(1)
    hpb = HG_HPB

    @pl.when(c == 0)
    def _():
        for j in range(hpb):
            st_ref[j] = s0_ref[j].T

    wn = wn_ref[...]

    def body(u, sts):
        rows = pl.ds(pl.multiple_of(u * HG_CHUNK, HG_CHUNK), HG_CHUNK)
        out = []
        for j in range(hpb):
            cols = slice(j * HG_DK, (j + 1) * HG_DK)
            o, st = _hgrn_chunk(q_ref[rows, cols], f_ref[rows, cols], v_ref[rows, cols], lb_ref[:, cols], sts[j])
            g = g_ref[rows, cols]
            o_ref[rows, cols] = _rms(o, wn) * (g * jax.nn.sigmoid(g))
            out.append(st)
        return tuple(out)

    sts = lax.fori_loop(0, nsub, body, tuple(st_ref[j] for j in range(hpb)))
    for j in range(hpb):
        st_ref[j] = sts[j]

    @pl.when(c == pl.num_programs(1) - 1)
    def _():
        for j in range(hpb):
            sfin_ref[j] = sts[j].T


def _hgrn_prompt(hq, hf, hi, hg, lb, wn, s0, tb):
    t = hq.shape[0]
    hpb = HG_HPB
    assert t % tb == 0 and tb % HG_CHUNK == 0 and HG_HEADS % hpb == 0 and HG_DK == HG_DV
    tok = pl.BlockSpec((tb, hpb * HG_DK), lambda h, c: (c, h))
    stt = pl.BlockSpec((hpb, HG_DK, HG_DV), lambda h, c: (h, 0, 0))
    return pl.pallas_call(
        functools.partial(_hgrn_kernel, nsub=tb // HG_CHUNK),
        grid=(HG_HEADS // hpb, t // tb),
        in_specs=[pl.BlockSpec((1, hpb * HG_DK), lambda h, c: (0, h)),
                  pl.BlockSpec((1, HG_DV), lambda h, c: (0, 0)),
                  tok, tok, tok, tok, stt],
        out_specs=[tok, stt],
        out_shape=[jax.ShapeDtypeStruct((t, HG_HEADS * HG_DV), F32),
                   jax.ShapeDtypeStruct((HG_HEADS, HG_DK, HG_DV), F32)],
        scratch_shapes=[pltpu.VMEM((hpb, HG_DV, HG_DK), F32)],
        compiler_params=_cparams(("arbitrary", "arbitrary")), name="hgrn_prompt",
    )(lb, wn, hq, hf, hi, hg, s0)


def _hgrn_step_kernel(lb_ref, wn_ref, q_ref, f_ref, v_ref, g_ref, s0_ref, o_ref, s_ref):
    wn = wn_ref[...]
    for h in range(HG_HEADS):
        rk = slice(h * HG_DK, (h + 1) * HG_DK)
        lb = lb_ref[rk, :]
        zf = f_ref[rk, :]
        f = lb + (1.0 - lb) * jax.nn.sigmoid(zf)
        kk = (1.0 - lb) * jax.nn.sigmoid(-zf)
        vrow = v_ref[:, h * HG_DV:(h + 1) * HG_DV]
        s_new = f * s0_ref[h] + kk * vrow
        s_ref[h] = s_new
        o = jnp.sum(s_new * q_ref[rk, :], axis=0, keepdims=True)
        g = g_ref[:, h * HG_DV:(h + 1) * HG_DV]
        o_ref[:, h * HG_DV:(h + 1) * HG_DV] = _rms(o, wn) * (g * jax.nn.sigmoid(g))


def _hgrn_step(hq_col, hf_col, hi, hg, lb_col, wn, s0):
    b = hi.shape[0]
    col = pl.BlockSpec((None, HG_WIDTH, 1), lambda i: (i, 0, 0))
    row = pl.BlockSpec((None, 1, HG_WIDTH), lambda i: (i, 0, 0))
    st = pl.BlockSpec((None, HG_HEADS, HG_DK, HG_DV), lambda i: (i, 0, 0, 0))
    return pl.pallas_call(
        _hgrn_step_kernel, grid=(b,),
        in_specs=[pl.BlockSpec((HG_WIDTH, 1), lambda i: (0, 0)), pl.BlockSpec((1, HG_DV), lambda i: (0, 0)),
                  col, col, row, row, st],
        out_specs=[row, st],
        out_shape=[jax.ShapeDtypeStruct((b, 1, HG_WIDTH), F32), jax.ShapeDtypeStruct(s0.shape, F32)],
        compiler_params=_cparams(("parallel",)), name="hgrn_step",
    )(lb_col, wn, hq_col, hf_col, hi, hg, s0)


def _merge_kernel(x_ref, oa_ref, ob_ref, ma_ref, mb_ref, pa_ref, pb_ref, wo_ref, y_ref):
    ya = jnp.dot(oa_ref[...].astype(BF16), pa_ref[...], preferred_element_type=F32)
    yb = jnp.dot(ob_ref[...].astype(BF16), pb_ref[...], preferred_element_type=F32)
    mix = jax.nn.sigmoid(ma_ref[...]) * ya + jax.nn.sigmoid(mb_ref[...]) * yb
    y_ref[...] = x_ref[...] + jnp.dot(mix.astype(BF16), wo_ref[...], preferred_element_type=F32)


def _merge(x, oa, ob, ma, mb, pa, pb, wo, tm):
    t, d = x.shape
    tok = pl.BlockSpec((tm, d), lambda i: (i, 0))
    wsp = pl.BlockSpec((d, d), lambda i: (0, 0))
    return pl.pallas_call(
        _merge_kernel, grid=(t // tm,),
        in_specs=[tok, tok, tok, tok, tok, wsp, wsp, wsp],
        out_specs=tok, out_shape=jax.ShapeDtypeStruct((t, d), F32),
        compiler_params=_cparams(("parallel",)), name="branch_merge",
    )(x, oa, ob, ma, mb, pa, pb, wo)


def _topk_rows(s, k):
    n = s.shape[0]
    rows = lax.broadcasted_iota(jnp.int32, s.shape, 0).astype(F32)
    tops = []
    for r in range(k):
        mx = jnp.max(s, axis=0, keepdims=True)
        idx = jnp.min(jnp.where(s == mx, rows, float(n)), axis=0, keepdims=True)
        s = jnp.where(rows == idx, -TAKEN * (1.0 + r / 64.0), s)
        tops.append(mx)
    rank = jnp.where(s < -0.5 * TAKEN, (s * (-1.0 / TAKEN) - 1.0) * 64.0, float(n))
    return jnp.concatenate(tops, axis=0), rank


_STAIR_GROUPS = ((0, 16), (1, 8), (2, 8), (3, 8))
_STAIR_QUAD = (4, 5, 6, 7)
_STAIR_TAIL = 8


def _stair_rows(t1, t2, op):
    tt = t1.shape[1]
    parts = [op(jnp.broadcast_to(t1[a:a + 1, :], (nb, tt)), t2[0:nb, :]) for a, nb in _STAIR_GROUPS]
    r16 = lax.broadcasted_iota(jnp.int32, (16, tt), 0)
    v1 = jnp.broadcast_to(t1[_STAIR_QUAD[3]:_STAIR_QUAD[3] + 1, :], (16, tt))
    v2 = jnp.broadcast_to(t2[3:4, :], (16, tt))
    for q in (2, 1, 0):
        v1 = jnp.where(r16 < 4 * (q + 1), jnp.broadcast_to(t1[_STAIR_QUAD[q]:_STAIR_QUAD[q] + 1, :], (16, tt)), v1)
        v2 = jnp.where((r16 & 3) == q, jnp.broadcast_to(t2[q:q + 1, :], (16, tt)), v2)
    parts.append(op(v1, v2))
    parts.append(op(t1[_STAIR_TAIL:, :], jnp.broadcast_to(t2[0:1, :], (PK_TOPK - _STAIR_TAIL, tt))))
    return jnp.concatenate(parts, axis=0)


def _stair_row_counts(selc):
    out, r0 = [], 0
    for _, nb in _STAIR_GROUPS:
        out.append(jnp.sum(selc[r0:r0 + nb, :], axis=0, keepdims=True))
        r0 += nb
    quad = selc[r0:r0 + 16, :]
    r16 = lax.broadcasted_iota(jnp.int32, quad.shape, 0)
    for q in range(4):
        out.append(jnp.sum(jnp.where((r16 >> 2) == q, quad, 0.0), axis=0, keepdims=True))
    r0 += 16
    for a in range(PK_TOPK - _STAIR_TAIL):
        out.append(selc[r0 + a:r0 + a + 1, :])
    return out


def _peer_kernel(x_ref, wn_ref, wq_ref, sk_ref, u_ref, vT_ref, wf_ref, y_ref,
                 hnT_ref, acc_ref, wcat_ref, wodd_ref, s_ref, n_ref, a1_ref, r2_ref, e2_ref, *, ib, nkeys, final_norm):
    i = pl.program_id(1)
    kt = PK_TOPK
    hd = PK_DIM // 2

    @pl.when(i == 0)
    def _():
        hnT = _rms(x_ref[...], wn_ref[...]).T.astype(BF16)
        hnT_ref[...] = hnT
        acc_ref[...] = jnp.zeros(acc_ref.shape, F32)
        wcat_ref[...] = jnp.zeros(wcat_ref.shape, BF16)
        wodd_ref[...] = jnp.zeros(wodd_ref.shape, BF16)
        tt = hnT.shape[1]
        nlc = tt // LANES
        lcu = 2 if nlc % 2 == 0 else 1
        for h in range(PK_HEADS):
            for c in range(2):
                r0 = (h * 2 + c) * hd
                qhc = jnp.dot(wq_ref[r0:r0 + hd, :], hnT, preferred_element_type=F32)
                s = jnp.dot(sk_ref[h * 2 + c], qhc.astype(BF16), preferred_element_type=F32)
                for lc in range(nlc):
                    s_ref[c, lc] = s[:, lc * LANES:(lc + 1) * LANES]

            def chunk(lc, carry):
                for sub in range(lcu):
                    chunk_one(lc * lcu + sub)
                return carry

            def chunk_one(lc):
                s0, s1 = s_ref[0, lc], s_ref[1, lc]
                top0, rank0 = _topk_rows(s0, kt)
                top1, rank1 = _topk_rows(s1, kt)
                cand = _stair_rows(top0, top1, jnp.add)
                _, crank = _topk_rows(cand, kt)
                selc = (crank < float(kt)).astype(F32)
                n_a = _stair_row_counts(selc)
                e1t = jnp.exp(top0 - top0[0:1, :])
                e2t = jnp.exp(top1 - top1[0:1, :])
                z = jnp.sum(selc * _stair_rows(e1t, e2t, jnp.multiply), axis=0, keepdims=True)
                nfull = jnp.zeros(s0.shape, F32)
                for a in range(kt):
                    nfull = jnp.where(rank0 == float(a), n_a[a], nfull)
                n_ref[h, lc] = nfull
                a1_ref[h, lc] = jnp.exp(s0 - top0[0:1, :]) / z
                r2_ref[h, lc] = rank1.astype(BF16)
                e2_ref[h, lc] = jnp.exp(s1 - top1[0:1, :]).astype(BF16)

            lax.fori_loop(0, nlc // lcu, chunk, 0)

    nsteps = pl.num_programs(1) - 1
    nch = 2
    cw = hnT_ref.shape[1] // nch

    def step(w_read, w_write):
        hnT = hnT_ref[...]
        nlc = hnT.shape[1] // LANES

        def lanes_cat(ref, h, rows=slice(None)):
            return jnp.concatenate([ref[h, lc, rows, :] for lc in range(nlc)], axis=1)

        for ip in range(ib // 2):
            if ip % (ib // 2 // nch) == 0:
                ch = ip // (ib // 2 // nch)
                cols = slice(ch * cw, (ch + 1) * cw)
                acc_ref[:, cols] += jnp.dot(vT_ref[...], w_read[:, cols], preferred_element_type=F32)
            iis = (2 * ip, 2 * ip + 1)
            aTs = [jnp.dot(u_ref[ii * nkeys:(ii + 1) * nkeys, :], hnT, preferred_element_type=F32) for ii in iis]
            gsums = [jnp.zeros(aTs[0].shape, BF16) for _ in iis]
            for h in range(PK_HEADS):
                r2 = lanes_cat(r2_ref, h)
                e2 = lanes_cat(e2_ref, h)
                for n, ii in enumerate(iis):
                    row = pl.ds(i * ib + ii, 1)
                    nrow = lanes_cat(n_ref, h, row).astype(BF16)
                    arow = lanes_cat(a1_ref, h, row).astype(BF16)
                    gsums[n] = gsums[n] + jnp.where(r2 < nrow, arow * e2, jnp.zeros((), BF16))
            for n, ii in enumerate(iis):
                w_write[ii * nkeys:(ii + 1) * nkeys, :] = jax.nn.gelu(aTs[n].astype(BF16)) * gsums[n]

    @pl.when((i < nsteps) & (i % 2 == 0))
    def _():
        step(wodd_ref, wcat_ref)

    @pl.when((i < nsteps) & (i % 2 == 1))
    def _():
        step(wcat_ref, wodd_ref)

    @pl.when(i == nsteps)
    def _():
        w_last = wodd_ref if (nkeys // ib) % 2 == 0 else wcat_ref
        acc = acc_ref[...] + jnp.dot(vT_ref[...], w_last[...], preferred_element_type=F32)
        y = x_ref[...] + acc.T
        if final_norm:
            y = _rms(y, wf_ref[...])
        y_ref[...] = y


def _peer(x, wn, wqT, sk, u, vT, wf, tt, ib, final_norm):
    t, d = x.shape
    nkeys = sk.shape[1]
    assert t % tt == 0 and nkeys % ib == 0
    nsteps = nkeys // ib
    nlc = tt // LANES
    stat = pltpu.VMEM((PK_HEADS, nlc, nkeys, LANES), F32)
    stat16 = pltpu.VMEM((PK_HEADS, nlc, nkeys, LANES), BF16)
    wbuf = pltpu.VMEM((ib * nkeys, tt), BF16)
    return pl.pallas_call(
        functools.partial(_peer_kernel, ib=ib, nkeys=nkeys, final_norm=final_norm),
        grid=(t // tt, nsteps + 1),
        in_specs=[pl.BlockSpec((tt, d), lambda a, i: (a, 0)),
                  pl.BlockSpec((1, d), lambda a, i: (0, 0)),
                  pl.BlockSpec(wqT.shape, lambda a, i: (0, 0)),
                  pl.BlockSpec(sk.shape, lambda a, i: (0, 0, 0)),
                  pl.BlockSpec((ib * nkeys, d), lambda a, i: (jnp.minimum(i, nsteps - 1), 0)),
                  pl.BlockSpec((None, d, ib * nkeys), lambda a, i: (jnp.maximum(i - 1, 0), 0, 0)),
                  pl.BlockSpec((1, d), lambda a, i: (0, 0))],
        out_specs=pl.BlockSpec((tt, d), lambda a, i: (a, 0)),
        out_shape=jax.ShapeDtypeStruct((t, d), F32),
        scratch_shapes=[pltpu.VMEM((d, tt), BF16), pltpu.VMEM((d, tt), F32), wbuf, wbuf,
                        pltpu.VMEM((2, nlc, nkeys, LANES), F32), stat, stat, stat16, stat16],
        compiler_params=_cparams(("arbitrary", "arbitrary")), name="peer_dense",
    )(x, wn, wqT, sk, u, vT, wf)


def _compress_pages_kernel(pt_ref, *refs, pg, bpp):
    w_ref, o_ref = refs[pg], refs[pg + 1]
    nt = (((1,), (1,)), ((), ()))
    for j in range(pg):
        x = refs[j][...]
        halves = [lax.dot_general(w_ref[c], x[c].reshape(-1, x.shape[-1]).astype(BF16), nt, preferred_element_type=F32)
                  for c in range(x.shape[0])]
        o_ref[j * bpp:(j + 1) * bpp, :] = jnp.concatenate(halves, axis=1)


def _compress_pages(cache_t, page_table, w_sel, pg):
    tail = cache_t.shape[1:]
    db, npages = page_table.shape
    bpp = w_sel.shape[1]
    width = tail[0] * tail[1] * tail[2]
    assert npages % pg == 0
    zeros = (0,) * len(tail)
    page_specs = [pl.BlockSpec((None,) + tail, functools.partial(lambda b, g, pt, j: (pt[b, g * pg + j],) + zeros, j=j))
                  for j in range(pg)]
    return pl.pallas_call(
        functools.partial(_compress_pages_kernel, pg=pg, bpp=bpp),
        grid_spec=pltpu.PrefetchScalarGridSpec(
            num_scalar_prefetch=1, grid=(db, npages // pg),
            in_specs=page_specs + [pl.BlockSpec(w_sel.shape, lambda b, g, pt: (0, 0, 0))],
            out_specs=pl.BlockSpec((None, pg * bpp, width), lambda b, g, pt: (b, g, 0))),
        out_shape=jax.ShapeDtypeStruct((db, npages * bpp, width), F32),
        compiler_params=_cparams(("arbitrary", "arbitrary")), name="nsa_compress_pages",
    )(page_table, *([cache_t] * pg), w_sel)


def _nsa_decode_head_kernel(slope_ref, qbd_ref, g_ref, kvc_ref, wnew_ref, cwin_ref, part_ref, idx_ref,
                            *, past_len, k_past):
    kvw, grp, dh = NSA_KV_WIDTH, NSA_GROUP, NSA_HEAD_DIM
    qbd = qbd_ref[...]
    qb = qbd.astype(BF16)
    slope = slope_ref[...]
    nh = qbd.shape[0]
    npb = kvc_ref.shape[0]
    nt = (((1,), (1,)), ((), ()))
    kvc = kvc_ref[...]
    sc = lax.dot_general(qb, kvc[:, :kvw].astype(BF16), nt, preferred_element_type=F32)
    c_end = lax.broadcasted_iota(jnp.int32, (1, npb), 1) * NSA_BLOCK + (NSA_BLOCK - 1)
    s = sc - slope * (past_len - c_end).astype(F32)
    e = jnp.exp(s - jnp.max(s, axis=1, keepdims=True))
    p = e / jnp.sum(e, axis=1, keepdims=True)
    oc = jnp.dot(p.astype(BF16), kvc[:, kvw:].astype(BF16), preferred_element_type=F32)
    rows = []
    for k in range(NSA_KV_HEADS):
        r = p[k * grp:k * grp + 1, :]
        for g in range(1, grp):
            r = r + p[k * grp + g:k * grp + g + 1, :]
        rows.append(jnp.broadcast_to(r, (grp, npb)))
    imp = jnp.concatenate(rows, axis=0)
    blk = lax.broadcasted_iota(jnp.int32, (nh, npb), 1)
    score = jnp.where((blk == 0) | (blk == npb - 1), grp + 1.0, imp)
    lanes = blk.astype(F32)
    picks = []
    for _ in range(k_past):
        mx = jnp.max(score, axis=1, keepdims=True)
        idx = jnp.min(jnp.where(score == mx, lanes, float(npb)), axis=1, keepdims=True)
        score = jnp.where(lanes == idx, LOWEST, score)
        picks.append(idx)
    idx_ref[...] = jnp.concatenate(picks, axis=1).astype(jnp.int32)
    sg = jax.nn.sigmoid(g_ref[...])
    wnew = wnew_ref[...]
    nw = cwin_ref.shape[-1]
    dw = (nw - lax.broadcasted_iota(jnp.int32, (1, nw), 1)).astype(F32)
    for k in range(NSA_KV_HEADS):
        hs, ds = slice(k * grp, (k + 1) * grp), slice(k * dh, (k + 1) * dh)
        qk = qbd[hs, ds]
        sw = jnp.dot(qk.astype(BF16), cwin_ref[0, k].astype(BF16), preferred_element_type=F32) - slope[hs] * dw
        s_n = jnp.sum(qk * wnew[:, ds], axis=1, keepdims=True)
        m = jnp.maximum(jnp.max(sw, axis=1, keepdims=True), s_n)
        ew = jnp.exp(sw - m)
        en = jnp.exp(s_n - m)
        pv = lax.dot_general(ew.astype(BF16), cwin_ref[1, k].astype(BF16), nt, preferred_element_type=F32)
        ow = (pv + en * wnew[:, kvw + k * dh:kvw + (k + 1) * dh]) / (jnp.sum(ew, axis=1, keepdims=True) + en)
        part_ref[hs, :] = sg[hs, 0:1] * oc[hs, ds] + sg[hs, 2:3] * ow


def _nsa_decode_gather_kernel(pt_ref, ix_ref, slope_ref, q_ref, g_ref, part_ref, new_ref, *refs, nsel, bpp, past_len):
    blocks, o_ref = refs[:nsel], refs[nsel]
    b, k = pl.program_id(0), pl.program_id(1)
    page = blocks[0].shape[-1]
    q = q_ref[...]
    slope = slope_ref[...]
    kt = jnp.concatenate([blocks[s][0] for s in range(nsel)], axis=1).astype(BF16)
    vt = jnp.concatenate([blocks[s][1] for s in range(nsel)], axis=1).astype(BF16)
    sc = jnp.dot(q.astype(BF16), kt, preferred_element_type=F32)
    lane = lax.broadcasted_iota(jnp.int32, (1, page), 1)
    lblk = lane // NSA_BLOCK
    lkey = lane - lblk * NSA_BLOCK
    bias = []
    for s in range(nsel):
        ib = ix_ref[b, k, s]
        kpos = ib * NSA_BLOCK + lkey
        bias.append(jnp.where(lblk == ib % bpp, -slope * (past_len - kpos).astype(F32), NEG_INF))
    sc = sc + jnp.concatenate(bias, axis=1)
    k_own, v_own = new_ref[0], new_ref[1]
    s_n = jnp.sum(q * k_own, axis=1, keepdims=True)
    m = jnp.maximum(jnp.max(sc, axis=1, keepdims=True), s_n)
    p = jnp.exp(sc - m)
    p_n = jnp.exp(s_n - m)
    pv = lax.dot_general(p.astype(BF16), vt, (((1,), (1,)), ((), ())), preferred_element_type=F32)
    o_s = (pv + p_n * v_own) / (jnp.sum(p, axis=1, keepdims=True) + p_n)
    o_ref[...] = part_ref[...] + jax.nn.sigmoid(g_ref[...])[:, 1:2] * o_s


def _rows_minor(a):
    return a.transpose(0, 2, 3, 4, 1)


def _nsa_sample(q, kv_cmp, kv_sel, kv_win, glog, cache_cmp, cache_sel, cache_win, page_table, w_cmp):
    db = q.shape[0]
    kvh, grp, dh, kvw = NSA_KV_HEADS, NSA_GROUP, NSA_HEAD_DIM, NSA_KV_WIDTH
    page = cache_cmp.shape[1]
    npages = page_table.shape[1]
    past_len = npages * page
    nwin = cache_win.shape[1]
    assert past_len >= nwin and page % NSA_BLOCK == 0
    npb = past_len // NSA_BLOCK
    bpp = page // NSA_BLOCK
    pg = 32 if npages % 32 == 0 else 1
    width = 2 * kvw
    rowblk = jnp.arange(page) // NSA_BLOCK
    w_sel = jnp.where(rowblk[None, None, :] == jnp.arange(bpp)[None, :, None],
                      jnp.tile(w_cmp, (1, bpp))[:, None, :], 0.0).astype(BF16)
    kvc = _compress_pages(_rows_minor(cache_cmp), page_table, w_sel, pg)
    q4 = q.reshape(db, kvh, grp, dh) * (dh ** -0.5)
    qbd = (q4[:, :, :, None, :] * jnp.eye(kvh, dtype=F32)[None, :, None, :, None]).reshape(db, kvh * grp, kvw)
    slope = _alibi_slopes().reshape(kvh * grp, 1)
    nh = kvh * grp
    k_past = min(NSA_TOPK, npb + 1) - 1
    full2 = lambda shape: pl.BlockSpec(shape, lambda b: (0,) * len(shape))
    per_b = lambda shape: pl.BlockSpec((None,) + shape, lambda b: (b,) + (0,) * len(shape))
    part, idx = pl.pallas_call(
        functools.partial(_nsa_decode_head_kernel, past_len=past_len, k_past=k_past),
        grid=(db,),
        in_specs=[full2((nh, 1)), per_b((nh, kvw)), per_b((nh, 3)), per_b((npb, width)), per_b((1, width)),
                  per_b((2, kvh, dh, nwin))],
        out_specs=[per_b((nh, dh)), per_b((nh, k_past))],
        out_shape=[jax.ShapeDtypeStruct((db, nh, dh), F32), jax.ShapeDtypeStruct((db, nh, k_past), jnp.int32)],
        compiler_params=_cparams(("parallel",)), name="nsa_decode_head",
    )(slope, qbd, glog, kvc, kv_win.reshape(db, 1, width), _rows_minor(cache_win))
    idx4 = idx.reshape(db, kvh, grp, k_past)[:, :, 0, :]
    blk_specs = [pl.BlockSpec((None, 2, None, dh, page),
                              functools.partial(lambda b, k, pt, ix, s: (pt[b, ix[b, k, s] // bpp], 0, k, 0, 0), s=s))
                 for s in range(k_past)]
    hsp = lambda shape: pl.BlockSpec((None, None) + shape, lambda b, k, pt, ix: (b, k) + (0,) * len(shape))
    o = pl.pallas_call(
        functools.partial(_nsa_decode_gather_kernel, nsel=k_past, bpp=bpp, past_len=past_len),
        grid_spec=pltpu.PrefetchScalarGridSpec(
            num_scalar_prefetch=2, grid=(db, kvh),
            in_specs=[pl.BlockSpec((None, grp, 1), lambda b, k, pt, ix: (k, 0, 0)), hsp((grp, dh)), hsp((grp, 3)),
                      hsp((grp, dh)),
                      pl.BlockSpec((None, 2, None, 1, dh), lambda b, k, pt, ix: (b, 0, k, 0, 0))] + blk_specs,
            out_specs=hsp((grp, dh))),
        out_shape=jax.ShapeDtypeStruct((db, kvh, grp, dh), F32),
        compiler_params=_cparams(("arbitrary", "arbitrary")), name="nsa_decode_gather",
    )(page_table, idx4, slope.reshape(kvh, grp, 1), q4, glog.reshape(db, kvh, grp, 3), part.reshape(db, kvh, grp, dh),
      kv_sel.reshape(db, 2, kvh, 1, dh), *([_rows_minor(cache_sel)] * k_past))
    new_win = jnp.concatenate([cache_win, kv_win.reshape((db, 1) + cache_win.shape[2:])], axis=1)[:, 1:]
    return o.reshape(db, nh * dh), new_win


def _split_w_in(w_in):
    pts, acc = [], 0
    for w in (NSA_Q_WIDTH, 6 * NSA_KV_WIDTH, 3 * NSA_HEADS, HG_WIDTH, HG_WIDTH, HG_WIDTH, HG_WIDTH, w_in.shape[0]):
        acc += w
        pts.append(acc)
    return jnp.split(w_in, pts, axis=1)


def _prep_layer(w_in, w_proj_nsa, w_proj_hgrn, w_out, w_peer_q, peer_sub_keys, peer_u, peer_v):
    d = w_in.shape[0]
    wq, wkv, wg, wbq, wbf, wbi, wbg, wma, wmb = _split_w_in(w_in)
    wg = wg.reshape(d, NSA_KV_HEADS, NSA_GROUP, 3).transpose(0, 1, 3, 2).reshape(d, NSA_KV_HEADS, 3 * NSA_GROUP)
    wg = jnp.pad(wg, ((0, 0), (0, 0), (0, 16 - 3 * NSA_GROUP))).reshape(d, NSA_KV_HEADS * 16)
    wg = jnp.pad(wg, ((0, 0), (0, LANES - NSA_KV_HEADS * 16)))
    nkeys = peer_sub_keys.shape[2]
    return dict(
        w_qg=jnp.concatenate([wq, wg], axis=1).astype(BF16),
        w_kv=wkv.astype(BF16),
        w_hg=jnp.concatenate([wbq, wbf, wbi, wbg], axis=1).astype(BF16),
        w_m=jnp.concatenate([wma, wmb], axis=1).astype(BF16),
        pa=w_proj_nsa.astype(BF16), pb=w_proj_hgrn.astype(BF16), wo=w_out.astype(BF16),
        wqT=w_peer_q.T.astype(BF16),
        sk=peer_sub_keys.reshape(PK_HEADS * 2, nkeys, PK_DIM // 2).astype(BF16),
        u=peer_u.astype(BF16),
        vT=peer_v.astype(BF16).reshape(nkeys // PEER_IB, PEER_IB * nkeys, d).transpose(0, 2, 1),
    )


def _tile(t, pref):
    return pref if t % pref == 0 else t


def _peer_tokens(x, wn, wf, prm, final_norm):
    t = x.shape[0]
    tp = -(-t // LANES) * LANES
    xp = jnp.pad(x, ((0, tp - t), (0, 0)))
    tt = PEER_TT if tp % PEER_TT == 0 else LANES
    y = _peer(xp, wn, prm["wqT"], prm["sk"], prm["u"], prm["vT"], wf, tt, PEER_IB, final_norm)
    return y[:t]


def _layer_prompt(x, lb, prm, w_norm_mix, w_cmp, w_hgrn_norm, w_norm_ffn, w_norm_final, final_norm):
    s_len, d = x.shape
    tm = _tile(s_len, 512)
    kvw = NSA_KV_WIDTH
    kv_cmp, kv_sel, kv_win = _proj(x, w_norm_mix, prm["w_kv"], [2 * kvw] * 3, [False] * 3, tm)
    qT, gT = _proj(x, w_norm_mix, prm["w_qg"], [NSA_Q_WIDTH, LANES], [True, True], tm)
    hq, hf, hi, hg = _proj(x, w_norm_mix, prm["w_hg"], [HG_WIDTH] * 4, [False] * 4, tm)
    ma, mb = _proj(x, w_norm_mix, prm["w_m"], [d, d], [False, False], tm)
    wfull = jnp.concatenate([jnp.broadcast_to(w_cmp[0][:, None], (NSA_BLOCK, kvw)),
                             jnp.broadcast_to(w_cmp[1][:, None], (NSA_BLOCK, kvw))], axis=1)
    nblk = s_len // NSA_BLOCK
    kvc = _compress(kv_cmp, wfull, 8 if nblk % 8 == 0 else nblk)
    o_a = _nsa_prompt(qT, gT[:NSA_KV_HEADS * 16], kvc, kv_sel, kv_win)
    s0 = jnp.zeros((HG_HEADS, HG_DK, HG_DV), F32)
    o_b, s_fin = _hgrn_prompt(hq, hf, hi, hg, lb, w_hgrn_norm, s0, _tile(s_len, 512))
    x1 = _merge(x, o_a, o_b, ma, mb, prm["pa"], prm["pb"], prm["wo"], tm)
    x2 = _peer_tokens(x1, w_norm_ffn, w_norm_final, prm, final_norm)
    shp = (s_len, 2, NSA_KV_HEADS, NSA_HEAD_DIM)
    return x2, kv_cmp.reshape(shp), kv_sel.reshape(shp), kv_win.reshape(shp)[-NSA_WINDOW:], s_fin


def _layer_sample(x, lb, prm, w_norm_mix, w_cmp, w_hgrn_norm, w_norm_ffn, w_norm_final, final_norm,
                  cache_cmp, cache_sel, cache_win, s0, page_table):
    b, d = x.shape
    kvw = NSA_KV_WIDTH
    w_q = prm["w_qg"][:, :NSA_Q_WIDTH]
    (q,) = _proj(x, w_norm_mix, w_q, [NSA_Q_WIDTH], [False], b)
    kv_cmp, kv_sel, kv_win = _proj(x, w_norm_mix, prm["w_kv"], [2 * kvw] * 3, [False] * 3, b)
    (gTt,) = _proj(x, w_norm_mix, prm["w_qg"][:, NSA_Q_WIDTH:], [LANES], [False], b)
    hq, hf, hi, hg = _proj(x, w_norm_mix, prm["w_hg"], [HG_WIDTH] * 4, [False] * 4, b)
    ma, mb = _proj(x, w_norm_mix, prm["w_m"], [d, d], [False, False], b)
    shp = (b, 1, 2, NSA_KV_HEADS, NSA_HEAD_DIM)
    glog = (gTt[:, :NSA_KV_HEADS * 16].reshape(b, NSA_KV_HEADS, 16)[..., :3 * NSA_GROUP]
            .reshape(b, NSA_KV_HEADS, 3, NSA_GROUP).transpose(0, 1, 3, 2).reshape(b, NSA_HEADS, 3))
    o_a, new_win = _nsa_sample(q, kv_cmp, kv_sel, kv_win, glog, cache_cmp, cache_sel, cache_win, page_table, w_cmp)
    o_b, s_new = _hgrn_step(hq.reshape(b, HG_WIDTH, 1), hf.reshape(b, HG_WIDTH, 1), hi.reshape(b, 1, HG_WIDTH),
                            hg.reshape(b, 1, HG_WIDTH), lb.reshape(HG_WIDTH, 1), w_hgrn_norm, s0)
    x1 = _merge(x, o_a.reshape(b, NSA_Q_WIDTH), o_b.reshape(b, HG_WIDTH), ma, mb, prm["pa"], prm["pb"], prm["wo"], b)
    x2 = _peer_tokens(x1, w_norm_ffn, w_norm_final, prm, final_norm)
    return x2, kv_cmp.reshape(shp), kv_sel.reshape(shp), new_win, s_new


def kernel(x_prompt, x_sample, cache_cmp_kv, cache_sel_kv, cache_win_kv, state_hgrn, page_table,
           w_norm_mix, w_in, w_cmp, w_proj_nsa, w_proj_hgrn, w_hgrn_norm, hgrn_lb_logits, w_out,
           w_norm_ffn, w_peer_q, peer_sub_keys, peer_u, peer_v, w_norm_final):
    depth = w_in.shape[0]
    bsz, s_len, d = x_prompt.shape
    db, dt, _ = x_sample.shape
    assert dt == 1
    lbs = jnp.cumsum(jax.nn.softmax(hgrn_lb_logits.astype(F32), axis=0), axis=0)
    wfin = w_norm_final.reshape(1, d)
    xp = [x_prompt[b] for b in range(bsz)]
    xs = x_sample.reshape(db, d)
    st_p, st_s = [], []
    for l in range(depth):
        last = l == depth - 1
        prm = _prep_layer(w_in[l], w_proj_nsa[l], w_proj_hgrn[l], w_out[l], w_peer_q[l], peer_sub_keys[l],
                          peer_u[l], peer_v[l])
        shared = (lbs[l].reshape(1, HG_WIDTH), prm, w_norm_mix[l].reshape(1, d), w_cmp[l],
                  w_hgrn_norm[l].reshape(1, HG_DV), w_norm_ffn[l].reshape(1, d), wfin, last)
        outs = [_layer_prompt(xp[b], *shared) for b in range(bsz)]
        xp = [o[0] for o in outs]
        st_p.append(tuple(jnp.stack([o[k] for o in outs]) for k in range(1, 5)))
        xs, *ss = _layer_sample(xs, *shared, cache_cmp_kv[l], cache_sel_kv[l], cache_win_kv[l], state_hgrn[l],
                                page_table)
        st_s.append(tuple(ss))
    y_prompt = jnp.stack(xp)
    y_sample = xs.reshape(db, dt, d)
    return (y_prompt, y_sample,
            jnp.stack([s[0] for s in st_p]), jnp.stack([s[1] for s in st_p]),
            jnp.stack([s[2] for s in st_p]), jnp.stack([s[3] for s in st_p]),
            jnp.stack([s[0] for s in st_s]), jnp.stack([s[1] for s in st_s]),
            jnp.stack([s[2] for s in st_s]), jnp.stack([s[3] for s in st_s]))
```

```python
import functools

import jax
import jax.numpy as jnp
from jax import lax
from jax.experimental import pallas as pl
from jax.experimental.pallas import tpu as pltpu

F32 = jnp.float32
BF16 = jnp.bfloat16

NSA_HEADS = 16
NSA_KV_HEADS = 4
NSA_GROUP = NSA_HEADS // NSA_KV_HEADS
NSA_HEAD_DIM = 64
NSA_BLOCK = 64
NSA_TOPK = 16
NSA_WINDOW = 512
NSA_QBLOCK = 256
HG_HEADS = 8
HG_DK = 128
HG_DV = 128
HG_CHUNK = 64
HG_SUB = 16
HG_HPB = 8
PK_HEADS = 8
PK_DIM = 256
PK_TOPK = 16
RMS_EPS = 1e-6
NEG_INF = -1e30
LOWEST = -3e38
TAKEN = 2.0 ** 100

NSA_Q_WIDTH = NSA_HEADS * NSA_HEAD_DIM
NSA_KV_WIDTH = NSA_KV_HEADS * NSA_HEAD_DIM
HG_WIDTH = HG_HEADS * HG_DK

LANES = 128
VMEM_LIMIT_BYTES = 56 * 1024 * 1024

NSA_COLS = NSA_GROUP * NSA_QBLOCK
NSA_KTILE = 512
NSA_CDIM = 128
NSA_WKEYS = NSA_WINDOW + NSA_QBLOCK
PEER_TT = 512
PEER_IB = 16


def _cparams(sem):
    return pltpu.CompilerParams(dimension_semantics=sem, vmem_limit_bytes=VMEM_LIMIT_BYTES)


def _rms(x, w):
    return x * lax.rsqrt(jnp.mean(x * x, axis=-1, keepdims=True) + RMS_EPS) * w


def _proj_kernel(x_ref, wn_ref, w_ref, *out_refs, widths, transposed, chunk):
    hb = _rms(x_ref[...], wn_ref[...]).astype(BF16)
    off = 0
    for o_ref, wd, tr in zip(out_refs, widths, transposed):
        for c0 in range(0, wd, chunk):
            cw = min(chunk, wd - c0)
            r = jnp.dot(hb, w_ref[:, off + c0:off + c0 + cw], preferred_element_type=F32)
            if tr:
                o_ref[c0:c0 + cw, :] = r.T
            else:
                o_ref[:, c0:c0 + cw] = r
        off += wd


def _proj(x, wn, w, widths, transposed, tm):
    t, d = x.shape
    n = w.shape[1]
    assert sum(widths) == n and t % tm == 0
    out_shape, out_specs = [], []
    for wd, tr in zip(widths, transposed):
        if tr:
            out_shape.append(jax.ShapeDtypeStruct((wd, t), F32))
            out_specs.append(pl.BlockSpec((wd, tm), lambda i: (0, i)))
        else:
            out_shape.append(jax.ShapeDtypeStruct((t, wd), F32))
            out_specs.append(pl.BlockSpec((tm, wd), lambda i: (i, 0)))
    return pl.pallas_call(
        functools.partial(_proj_kernel, widths=tuple(widths), transposed=tuple(transposed), chunk=512),
        grid=(t // tm,),
        in_specs=[pl.BlockSpec((tm, d), lambda i: (i, 0)),
                  pl.BlockSpec((1, d), lambda i: (0, 0)),
                  pl.BlockSpec((d, n), lambda i: (0, 0))],
        out_specs=out_specs, out_shape=out_shape,
        compiler_params=_cparams(("parallel",)), name="rms_proj",
    )(x, wn, w)


def _compress_kernel(kv_ref, w_ref, o_ref, *, nb):
    x = kv_ref[...]
    width = x.shape[-1]
    x3 = x.reshape(nb, NSA_BLOCK, width) * w_ref[...][None]
    o_ref[...] = jnp.sum(x3, axis=1)


def _compress(kv, wfull, nb):
    t, width = kv.shape
    rows = nb * NSA_BLOCK
    assert t % rows == 0
    return pl.pallas_call(
        functools.partial(_compress_kernel, nb=nb),
        grid=(t // rows,),
        in_specs=[pl.BlockSpec((rows, width), lambda i: (i, 0)),
                  pl.BlockSpec((NSA_BLOCK, width), lambda i: (0, 0))],
        out_specs=pl.BlockSpec((nb, width), lambda i: (i, 0)),
        out_shape=jax.ShapeDtypeStruct((t // NSA_BLOCK, width), F32),
        compiler_params=_cparams(("parallel",)), name="nsa_compress",
    )(kv, wfull)


def _topk_select_bias(score, k):
    n = score.shape[0]
    rows = lax.broadcasted_iota(jnp.int32, score.shape, 0).astype(F32)
    for _ in range(k):
        mx = jnp.max(score, axis=0, keepdims=True)
        idx = jnp.min(jnp.where(score == mx, rows, float(n)), axis=0, keepdims=True)
        score = jnp.where(rows == idx, -TAKEN, score)
    return jnp.where(score < -0.5 * TAKEN, 0.0, NEG_INF)


def _nsa_prompt_kernel(slope_ref, qT_ref, gT_ref, kc_ref, vcT_ref, ksel_ref, vselT_ref,
                       kd_ref, vdT_ref, *rest, nblk, nwb):
    kws, vws = rest[:nwb], rest[nwb:2 * nwb]
    bw_ref, out_ref, qs_ref, selb_ref, sa_ref, sb_ref, pa_ref, pb_ref, tiles_ref = rest[2 * nwb:]
    i = pl.program_id(1)
    t0 = i * NSA_QBLOCK
    dh, qb, ncol, tk = NSA_HEAD_DIM, NSA_QBLOCK, NSA_COLS, NSA_KTILE
    slope = slope_ref[...]
    col = lax.broadcasted_iota(jnp.int32, (1, ncol), 1)
    tpos = t0 + (col & (qb - 1))
    tposf = tpos.astype(F32)
    q4 = qT_ref[...] * (dh ** -0.5)
    qT = jnp.concatenate([q4[g * dh:(g + 1) * dh, :] for g in range(NSA_GROUP)], axis=1)
    xrow = lax.broadcasted_iota(jnp.int32, (NSA_CDIM - dh, ncol), 0)

    qc = jnp.concatenate([qT, jnp.zeros((NSA_CDIM - dh, ncol), F32)], axis=0).astype(BF16)
    sc = jnp.dot(kc_ref[...], qc, preferred_element_type=F32)
    c_end = lax.broadcasted_iota(jnp.int32, (nblk, 1), 0) * NSA_BLOCK + (NSA_BLOCK - 1)
    valid = c_end <= tpos
    s = jnp.where(valid, sc - slope * (tposf - c_end.astype(F32)), NEG_INF)
    e = jnp.exp(s - jnp.max(s, axis=0, keepdims=True))
    p = jnp.where(valid, e / jnp.sum(e, axis=0, keepdims=True), 0.0)
    ocT = jnp.dot(vcT_ref[...], p.astype(BF16), preferred_element_type=F32)

    qw = jnp.concatenate([qT, jnp.where(xrow == 0, NEG_INF, 0.0)], axis=0).astype(BF16)
    kw = jnp.concatenate([r[...] for r in kws], axis=0)
    sw = jnp.dot(kw, qw, preferred_element_type=F32) + bw_ref[...]
    ew = jnp.exp(sw - jnp.max(sw, axis=0, keepdims=True))
    vw = jnp.concatenate([r[...] for r in vws], axis=1)
    owT = jnp.dot(vw, ew.astype(BF16), preferred_element_type=F32) / jnp.sum(ew, axis=0, keepdims=True)

    nbt = tk // NSA_BLOCK
    jl = t0 // tk
    s1 = slope.astype(BF16).astype(F32)
    r1 = slope - s1
    s2 = r1.astype(BF16).astype(F32)
    s3 = (r1 - s2).astype(BF16).astype(F32)
    half = float(tk // 2)
    ext = jnp.zeros((NSA_CDIM - dh, ncol), F32)
    for r, v in enumerate((s1, s2, s3, s1 * half, s2 * half, s3 * half)):
        ext = jnp.where(xrow == nbt + r, v, ext)

    qd = jnp.concatenate([qT, ext], axis=0).astype(BF16)
    sd = jnp.dot(kd_ref[...], qd, preferred_element_type=F32)
    kposd = t0 + lax.broadcasted_iota(jnp.int32, (qb, 1), 0)
    sd = jnp.where(kposd > tpos, NEG_INF, sd)
    mx_d = jnp.max(sd, axis=0, keepdims=True)
    pd = jnp.exp(sd - mx_d)
    m_d = mx_d + slope * (jl * tk - tpos).astype(F32)
    l_d = jnp.sum(pd, axis=0, keepdims=True)
    acc_d = jnp.dot(vdT_ref[...], pd.astype(BF16), preferred_element_type=F32)

    imp = p[:, 0:qb]
    for g in range(1, NSA_GROUP):
        imp = imp + p[:, g * qb:(g + 1) * qb]
    blk = lax.broadcasted_iota(jnp.int32, (nblk, qb), 0)
    cur = tpos[:, 0:qb] >> (NSA_BLOCK.bit_length() - 1)
    forced = (blk == 0) | (blk == cur) | (blk == cur - 1)
    score = jnp.where(blk > cur, -1.0, jnp.where(forced, NSA_GROUP + 1.0, imp))
    if nblk >= NSA_TOPK:
        selb = _topk_select_bias(jnp.where(forced, -TAKEN, score), NSA_TOPK - 3)
    else:
        selb = _topk_select_bias(score, nblk)
    blkc = lax.broadcasted_iota(jnp.int32, (nblk, ncol), 0)
    selb_ref[0:nblk, :] = jnp.where(blkc >= t0 // NSA_BLOCK, NEG_INF, jnp.concatenate([selb] * NSA_GROUP, axis=1))
    selb_ref[nblk:nblk + nbt, :] = jnp.full((nbt, ncol), NEG_INF, F32)

    fl = jnp.max(selb_ref[0:nblk, 0:qb].reshape(nblk // nbt, nbt, qb), axis=1)
    tiles_ref[0] = 0
    n_act = jnp.int32(0)
    for j in range(nblk // nbt):
        tiles_ref[n_act] = j
        n_act = n_act + (jnp.max(fl[j:j + 1, :]) > 0.5 * NEG_INF).astype(jnp.int32)
    npairs = (n_act + 1) // 2

    qs_ref[0:dh, :] = qT
    qs_ref[dh:, :] = ext

    def tile_id(t):
        return tiles_ref[jnp.clip(t, 0, jnp.maximum(n_act - 1, 0))]

    def qk_into(t, s_ref):
        tid = tile_id(t)
        b0 = pl.multiple_of(jnp.where(t < n_act, tid * nbt, nblk), 8)
        qs_ref[dh:dh + nbt, :] = selb_ref[pl.ds(b0, nbt), :]
        k0 = pl.multiple_of(tid * tk, tk)
        s_ref[...] = jnp.dot(ksel_ref[pl.ds(k0, tk), :], qs_ref[...].astype(BF16), preferred_element_type=F32)

    def soft(t, s_ref, p_ref, m):
        cj = slope * (tile_id(t) * tk - tpos).astype(F32)
        sj = s_ref[...]
        m_new = jnp.maximum(m, jnp.max(sj, axis=0, keepdims=True) + cj)
        p_ref[...] = jnp.exp(sj - (m_new - cj)).astype(BF16)
        return m_new, jnp.exp(m - m_new)

    def pv(t, p_ref):
        return jnp.dot(vselT_ref[tile_id(t)], p_ref[...], preferred_element_type=F32)

    def pair(i, carry):
        m, accp = carry
        ta = 2 * i
        pvb = pv(ta - 1, pb_ref)
        qk_into(ta + 1, sb_ref)
        m, alpha = soft(ta, sa_ref, pa_ref, m)
        accp = alpha * (accp + pvb)
        pva = pv(ta, pa_ref)
        qk_into(ta + 2, sa_ref)
        m, alpha = soft(ta + 1, sb_ref, pb_ref, m)
        accp = alpha * (accp + pva)
        return m, accp

    pb_ref[...] = jnp.zeros(pb_ref.shape, BF16)
    qk_into(0, sa_ref)
    vrows = vselT_ref.shape[1]
    init = (jnp.full((1, ncol), NEG_INF, F32), jnp.zeros((vrows, ncol), F32))
    m_s, accp = lax.fori_loop(0, npairs, pair, init)
    accl = accp + pv(2 * npairs - 1, pb_ref)
    acc_s, l_s = accl[0:dh], accl[dh:dh + 1]

    m_f = jnp.maximum(m_s, m_d)
    a_s = jnp.exp(m_s - m_f)
    a_d = jnp.exp(m_d - m_f)
    osT = (a_s * acc_s + a_d * acc_d) / (a_s * l_s + a_d * l_d)

    sg = jax.nn.sigmoid(gT_ref[...])

    def gate(c):
        return jnp.concatenate([sg[c * NSA_GROUP + g:c * NSA_GROUP + g + 1, :] for g in range(NSA_GROUP)], axis=1)

    oT = gate(0) * ocT + gate(1) * osT + gate(2) * owT
    o4 = jnp.concatenate([oT[:, g * qb:(g + 1) * qb] for g in range(NSA_GROUP)], axis=0)
    out_ref[...] = o4.T


def _alibi_slopes():
    h = jnp.arange(NSA_HEADS, dtype=F32)
    return (2.0 ** (-8.0 * (h + 1.0) / NSA_HEADS)).reshape(NSA_KV_HEADS, NSA_GROUP)


def _nsa_prompt(qT, gT, kvc, kv_sel, kv_win):
    s_len = qT.shape[1]
    dh, qb, tk, kvh = NSA_HEAD_DIM, NSA_QBLOCK, NSA_KTILE, NSA_KV_HEADS
    assert s_len % tk == 0
    nblk = s_len // NSA_BLOCK
    slopes = _alibi_slopes()
    slope_cols = jnp.repeat(slopes, qb, axis=1).reshape(kvh, 1, NSA_COLS)

    def heads_major(a):
        return a.reshape(a.shape[0], kvh, dh).transpose(1, 0, 2)

    def pad_lanes(a):
        return jnp.pad(a, ((0, 0), (0, 0), (0, NSA_CDIM - a.shape[-1])))

    kc = pad_lanes(heads_major(kvc[:, :NSA_KV_WIDTH])).astype(BF16)
    vcT = heads_major(kvc[:, NSA_KV_WIDTH:]).transpose(0, 2, 1).astype(BF16)
    r = jnp.arange(tk)
    onehot = (r[:, None] // NSA_BLOCK == jnp.arange(tk // NSA_BLOCK)[None, :]).astype(F32)
    lo = (r % (tk // 2)).astype(F32)[:, None]
    hi = (r // (tk // 2)).astype(F32)[:, None]
    kext = jnp.concatenate([onehot, lo, lo, lo, hi, hi, hi], axis=1)
    kext = jnp.tile(kext, (s_len // tk, 1))
    ksel = pad_lanes(jnp.concatenate(
        [heads_major(kv_sel[:, :NSA_KV_WIDTH]), jnp.broadcast_to(kext[None], (kvh,) + kext.shape)], axis=-1)).astype(BF16)
    vsel = heads_major(kv_sel[:, NSA_KV_WIDTH:]).astype(BF16)
    vselT = vsel.reshape(kvh, s_len // tk, tk, dh).transpose(0, 1, 3, 2)
    ones_rows = jnp.zeros((kvh, s_len // tk, 8, tk), BF16).at[:, :, 0, :].set(1.0)
    vselT = jnp.concatenate([vselT, ones_rows], axis=2)
    vselT_flat = vsel.transpose(0, 2, 1)
    kwin = heads_major(kv_win[:, :NSA_KV_WIDTH])
    kwin = jnp.concatenate([kwin, jnp.zeros((kvh, s_len, 1), F32)], axis=-1)
    padk = jnp.zeros((kvh, NSA_WINDOW, dh + 1), F32).at[:, :, dh].set(1.0)
    kwin = pad_lanes(jnp.concatenate([padk, kwin], axis=1)).astype(BF16)
    vwinT = jnp.pad(heads_major(kv_win[:, NSA_KV_WIDTH:]), ((0, 0), (NSA_WINDOW, 0), (0, 0))).transpose(0, 2, 1).astype(BF16)
    rr = jnp.arange(NSA_WKEYS)[:, None]
    cc = jnp.arange(NSA_COLS)[None, :]
    dw = (cc % qb) + NSA_WINDOW - rr
    bw = jnp.where((dw >= 0) & (dw <= NSA_WINDOW), -slope_cols * dw.astype(F32)[None], NEG_INF)

    nq = s_len // qb
    nwb = NSA_WKEYS // qb
    kw_specs = [pl.BlockSpec((None, qb, NSA_CDIM), functools.partial(lambda k, i, j: (k, i + j, 0), j=j)) for j in range(nwb)]
    vw_specs = [pl.BlockSpec((None, dh, qb), functools.partial(lambda k, i, j: (k, 0, i + j), j=j)) for j in range(nwb)]
    assert NSA_WINDOW % qb == 0 and s_len % qb == 0 and tk % qb == 0
    return pl.pallas_call(
        functools.partial(_nsa_prompt_kernel, nblk=nblk, nwb=nwb),
        grid=(kvh, nq),
        in_specs=[pl.BlockSpec((None, 1, NSA_COLS), lambda k, i: (k, 0, 0)),
                  pl.BlockSpec((NSA_GROUP * dh, qb), lambda k, i: (k, i)),
                  pl.BlockSpec((None, 16, qb), lambda k, i: (k, 0, i)),
                  pl.BlockSpec((None, nblk, NSA_CDIM), lambda k, i: (k, 0, 0)),
                  pl.BlockSpec((None, dh, nblk), lambda k, i: (k, 0, 0)),
                  pl.BlockSpec((None, s_len, NSA_CDIM), lambda k, i: (k, 0, 0)),
                  pl.BlockSpec((None, s_len // tk, dh + 8, tk), lambda k, i: (k, 0, 0, 0)),
                  pl.BlockSpec((None, qb, NSA_CDIM), lambda k, i: (k, i, 0)),
                  pl.BlockSpec((None, dh, qb), lambda k, i: (k, 0, i))]
                 + kw_specs + vw_specs
                 + [pl.BlockSpec((None, NSA_WKEYS, NSA_COLS), lambda k, i: (k, 0, 0))],
        out_specs=pl.BlockSpec((qb, NSA_GROUP * dh), lambda k, i: (i, k)),
        out_shape=jax.ShapeDtypeStruct((s_len, NSA_Q_WIDTH), F32),
        scratch_shapes=[pltpu.VMEM((NSA_CDIM, NSA_COLS), F32),
                        pltpu.VMEM((nblk + tk // NSA_BLOCK, NSA_COLS), F32),
                        pltpu.VMEM((tk, NSA_COLS), F32), pltpu.VMEM((tk, NSA_COLS), F32),
                        pltpu.VMEM((tk, NSA_COLS), BF16), pltpu.VMEM((tk, NSA_COLS), BF16),
                        pltpu.SMEM((s_len // tk + 1,), jnp.int32)],
        compiler_params=_cparams(("arbitrary", "arbitrary")), name="nsa_prompt",
    )(slope_cols, qT, gT.reshape(kvh, 16, s_len), kc, vcT, ksel, vselT, ksel, vselT_flat,
      *([kwin] * nwb), *([vwinT] * nwb), bw)


def _hgrn_chunk(qc, zf, vc, lb, st):
    c, sub = HG_CHUNK, HG_SUB
    logf = jnp.log(lb + (1.0 - lb) * jax.nn.sigmoid(zf))
    kc = (1.0 - lb) * jax.nn.sigmoid(-zf)
    tri = (lax.broadcasted_iota(jnp.int32, (c, c), 0) >= lax.broadcasted_iota(jnp.int32, (c, c), 1)).astype(F32)
    cb = jnp.dot(tri, logf, preferred_element_type=F32, precision=lax.Precision.HIGHEST)
    o = lax.dot_general((qc * jnp.exp(cb)).astype(BF16), st.astype(BF16), (((1,), (1,)), ((), ())),
                        preferred_element_type=F32)
    t3 = lax.broadcasted_iota(jnp.int32, (sub, sub, 1), 0) >= lax.broadcasted_iota(jnp.int32, (sub, sub, 1), 1)
    outs = []
    for a in range(c // sub):
        ra = slice(a * sub, (a + 1) * sub)
        cba, qa, ka, va = cb[ra], qc[ra], kc[ra], vc[ra]
        d3 = cba[:, None, :] - cba[None, :, :]
        x3 = jnp.where(t3, jnp.exp(d3), 0.0) * qa[:, None, :] * ka[None, :, :]
        att3 = jnp.sum(x3, axis=2, keepdims=True)
        oa = o[ra] + jnp.sum(att3 * va[None, :, :], axis=1)
        if a > 0:
            ref = cb[a * sub - 1:a * sub, :]
            qd = (qa * jnp.exp(cba - ref)).astype(BF16)
            kd = (kc[:a * sub] * jnp.exp(ref - cb[:a * sub])).astype(BF16)
            att = lax.dot_general(qd, kd, (((1,), (1,)), ((), ())), preferred_element_type=F32)
            oa = oa + jnp.dot(att.astype(BF16), vc[:a * sub].astype(BF16), preferred_element_type=F32)
        outs.append(oa)
    o = jnp.concatenate(outs, axis=0)
    last = cb[c - 1:c, :]
    kdec = (kc * jnp.exp(last - cb)).astype(BF16)
    st = st * jnp.exp(last) + lax.dot_general(vc.astype(BF16), kdec, (((0,), (0,)), ((), ())),
                                              preferred_element_type=F32)
    return o, st


def _hgrn_kernel(lb_ref, wn_ref, q_ref, f_ref, v_ref, g_ref, s0_ref, o_ref, sfin_ref, st_ref, *, nsub):
    c = pl.program_id(1)
    hpb = HG_HPB

    @pl.when(c == 0)
    def _():
        for j in range(hpb):
            st_ref[j] = s0_ref[j].T

    wn = wn_ref[...]

    def body(u, sts):
        rows = pl.ds(pl.multiple_of(u * HG_CHUNK, HG_CHUNK), HG_CHUNK)
        out = []
        for j in range(hpb):
            cols = slice(j * HG_DK, (j + 1) * HG_DK)
            o, st = _hgrn_chunk(q_ref[rows, cols], f_ref[rows, cols], v_ref[rows, cols], lb_ref[:, cols], sts[j])
            g = g_ref[rows, cols]
            o_ref[rows, cols] = _rms(o, wn) * (g * jax.nn.sigmoid(g))
            out.append(st)
        return tuple(out)

    sts = lax.fori_loop(0, nsub, body, tuple(st_ref[j] for j in range(hpb)))
    for j in range(hpb):
        st_ref[j] = sts[j]

    @pl.when(c == pl.num_programs(1) - 1)
    def _():
        for j in range(hpb):
            sfin_ref[j] = sts[j].T


def _hgrn_prompt(hq, hf, hi, hg, lb, wn, s0, tb):
    t = hq.shape[0]
    hpb = HG_HPB
    assert t % tb == 0 and tb % HG_CHUNK == 0 and HG_HEADS % hpb == 0 and HG_DK == HG_DV
    tok = pl.BlockSpec((tb, hpb * HG_DK), lambda h, c: (c, h))
    stt = pl.BlockSpec((hpb, HG_DK, HG_DV), lambda h, c: (h, 0, 0))
    return pl.pallas_call(
        functools.partial(_hgrn_kernel, nsub=tb // HG_CHUNK),
        grid=(HG_HEADS // hpb, t // tb),
        in_specs=[pl.BlockSpec((1, hpb * HG_DK), lambda h, c: (0, h)),
                  pl.BlockSpec((1, HG_DV), lambda h, c: (0, 0)),
                  tok, tok, tok, tok, stt],
        out_specs=[tok, stt],
        out_shape=[jax.ShapeDtypeStruct((t, HG_HEADS * HG_DV), F32),
                   jax.ShapeDtypeStruct((HG_HEADS, HG_DK, HG_DV), F32)],
        scratch_shapes=[pltpu.VMEM((hpb, HG_DV, HG_DK), F32)],
        compiler_params=_cparams(("arbitrary", "arbitrary")), name="hgrn_prompt",
    )(lb, wn, hq, hf, hi, hg, s0)


def _hgrn_step_kernel(lb_ref, wn_ref, q_ref, f_ref, v_ref, g_ref, s0_ref, o_ref, s_ref):
    wn = wn_ref[...]
    for h in range(HG_HEADS):
        rk = slice(h * HG_DK, (h + 1) * HG_DK)
        lb = lb_ref[rk, :]
        zf = f_ref[rk, :]
        f = lb + (1.0 - lb) * jax.nn.sigmoid(zf)
        kk = (1.0 - lb) * jax.nn.sigmoid(-zf)
        vrow = v_ref[:, h * HG_DV:(h + 1) * HG_DV]
        s_new = f * s0_ref[h] + kk * vrow
        s_ref[h] = s_new
        o = jnp.sum(s_new * q_ref[rk, :], axis=0, keepdims=True)
        g = g_ref[:, h * HG_DV:(h + 1) * HG_DV]
        o_ref[:, h * HG_DV:(h + 1) * HG_DV] = _rms(o, wn) * (g * jax.nn.sigmoid(g))


def _hgrn_step(hq_col, hf_col, hi, hg, lb_col, wn, s0):
    b = hi.shape[0]
    col = pl.BlockSpec((None, HG_WIDTH, 1), lambda i: (i, 0, 0))
    row = pl.BlockSpec((None, 1, HG_WIDTH), lambda i: (i, 0, 0))
    st = pl.BlockSpec((None, HG_HEADS, HG_DK, HG_DV), lambda i: (i, 0, 0, 0))
    return pl.pallas_call(
        _hgrn_step_kernel, grid=(b,),
        in_specs=[pl.BlockSpec((HG_WIDTH, 1), lambda i: (0, 0)), pl.BlockSpec((1, HG_DV), lambda i: (0, 0)),
                  col, col, row, row, st],
        out_specs=[row, st],
        out_shape=[jax.ShapeDtypeStruct((b, 1, HG_WIDTH), F32), jax.ShapeDtypeStruct(s0.shape, F32)],
        compiler_params=_cparams(("parallel",)), name="hgrn_step",
    )(lb_col, wn, hq_col, hf_col, hi, hg, s0)


def _merge_kernel(x_ref, oa_ref, ob_ref, ma_ref, mb_ref, pa_ref, pb_ref, wo_ref, y_ref):
    ya = jnp.dot(oa_ref[...].astype(BF16), pa_ref[...], preferred_element_type=F32)
    yb = jnp.dot(ob_ref[...].astype(BF16), pb_ref[...], preferred_element_type=F32)
    mix = jax.nn.sigmoid(ma_ref[...]) * ya + jax.nn.sigmoid(mb_ref[...]) * yb
    y_ref[...] = x_ref[...] + jnp.dot(mix.astype(BF16), wo_ref[...], preferred_element_type=F32)


def _merge(x, oa, ob, ma, mb, pa, pb, wo, tm):
    t, d = x.shape
    tok = pl.BlockSpec((tm, d), lambda i: (i, 0))
    wsp = pl.BlockSpec((d, d), lambda i: (0, 0))
    return pl.pallas_call(
        _merge_kernel, grid=(t // tm,),
        in_specs=[tok, tok, tok, tok, tok, wsp, wsp, wsp],
        out_specs=tok, out_shape=jax.ShapeDtypeStruct((t, d), F32),
        compiler_params=_cparams(("parallel",)), name="branch_merge",
    )(x, oa, ob, ma, mb, pa, pb, wo)


def _topk_rows(s, k):
    n = s.shape[0]
    rows = lax.broadcasted_iota(jnp.int32, s.shape, 0).astype(F32)
    tops = []
    for r in range(k):
        mx = jnp.max(s, axis=0, keepdims=True)
        idx = jnp.min(jnp.where(s == mx, rows, float(n)), axis=0, keepdims=True)
        s = jnp.where(rows == idx, -TAKEN * (1.0 + r / 64.0), s)
        tops.append(mx)
    rank = jnp.where(s < -0.5 * TAKEN, (s * (-1.0 / TAKEN) - 1.0) * 64.0, float(n))
    return jnp.concatenate(tops, axis=0), rank


_STAIR_GROUPS = ((0, 16), (1, 8), (2, 8), (3, 8))
_STAIR_QUAD = (4, 5, 6, 7)
_STAIR_TAIL = 8


def _stair_rows(t1, t2, op):
    tt = t1.shape[1]
    parts = [op(jnp.broadcast_to(t1[a:a + 1, :], (nb, tt)), t2[0:nb, :]) for a, nb in _STAIR_GROUPS]
    r16 = lax.broadcasted_iota(jnp.int32, (16, tt), 0)
    v1 = jnp.broadcast_to(t1[_STAIR_QUAD[3]:_STAIR_QUAD[3] + 1, :], (16, tt))
    v2 = jnp.broadcast_to(t2[3:4, :], (16, tt))
    for q in (2, 1, 0):
        v1 = jnp.where(r16 < 4 * (q + 1), jnp.broadcast_to(t1[_STAIR_QUAD[q]:_STAIR_QUAD[q] + 1, :], (16, tt)), v1)
        v2 = jnp.where((r16 & 3) == q, jnp.broadcast_to(t2[q:q + 1, :], (16, tt)), v2)
    parts.append(op(v1, v2))
    parts.append(op(t1[_STAIR_TAIL:, :], jnp.broadcast_to(t2[0:1, :], (PK_TOPK - _STAIR_TAIL, tt))))
    return jnp.concatenate(parts, axis=0)


def _stair_row_counts(selc):
    out, r0 = [], 0
    for _, nb in _STAIR_GROUPS:
        out.append(jnp.sum(selc[r0:r0 + nb, :], axis=0, keepdims=True))
        r0 += nb
    quad = selc[r0:r0 + 16, :]
    r16 = lax.broadcasted_iota(jnp.int32, quad.shape, 0)
    for q in range(4):
        out.append(jnp.sum(jnp.where((r16 >> 2) == q, quad, 0.0), axis=0, keepdims=True))
    r0 += 16
    for a in range(PK_TOPK - _STAIR_TAIL):
        out.append(selc[r0 + a:r0 + a + 1, :])
    return out


def _peer_kernel(x_ref, wn_ref, wq_ref, sk_ref, u_ref, vT_ref, wf_ref, y_ref,
                 hnT_ref, acc_ref, wcat_ref, wodd_ref, s_ref, n_ref, a1_ref, r2_ref, e2_ref, *, ib, nkeys, final_norm):
    i = pl.program_id(1)
    kt = PK_TOPK
    hd = PK_DIM // 2

    @pl.when(i == 0)
    def _():
        hnT = _rms(x_ref[...], wn_ref[...]).T.astype(BF16)
        hnT_ref[...] = hnT
        acc_ref[...] = jnp.zeros(acc_ref.shape, F32)
        wcat_ref[...] = jnp.zeros(wcat_ref.shape, BF16)
        wodd_ref[...] = jnp.zeros(wodd_ref.shape, BF16)
        tt = hnT.shape[1]
        nlc = tt // LANES
        lcu = 2 if nlc % 2 == 0 else 1
        for h in range(PK_HEADS):
            for c in range(2):
                r0 = (h * 2 + c) * hd
                qhc = jnp.dot(wq_ref[r0:r0 + hd, :], hnT, preferred_element_type=F32)
                s = jnp.dot(sk_ref[h * 2 + c], qhc.astype(BF16), preferred_element_type=F32)
                for lc in range(nlc):
                    s_ref[c, lc] = s[:, lc * LANES:(lc + 1) * LANES]

            def chunk(lc, carry):
                for sub in range(lcu):
                    chunk_one(lc * lcu + sub)
                return carry

            def chunk_one(lc):
                s0, s1 = s_ref[0, lc], s_ref[1, lc]
                top0, rank0 = _topk_rows(s0, kt)
                top1, rank1 = _topk_rows(s1, kt)
                cand = _stair_rows(top0, top1, jnp.add)
                _, crank = _topk_rows(cand, kt)
                selc = (crank < float(kt)).astype(F32)
                n_a = _stair_row_counts(selc)
                e1t = jnp.exp(top0 - top0[0:1, :])
                e2t = jnp.exp(top1 - top1[0:1, :])
                z = jnp.sum(selc * _stair_rows(e1t, e2t, jnp.multiply), axis=0, keepdims=True)
                nfull = jnp.zeros(s0.shape, F32)
                for a in range(kt):
                    nfull = jnp.where(rank0 == float(a), n_a[a], nfull)
                n_ref[h, lc] = nfull
                a1_ref[h, lc] = jnp.exp(s0 - top0[0:1, :]) / z
                r2_ref[h, lc] = rank1.astype(BF16)
                e2_ref[h, lc] = jnp.exp(s1 - top1[0:1, :]).astype(BF16)

            lax.fori_loop(0, nlc // lcu, chunk, 0)

    nsteps = pl.num_programs(1) - 1
    nch = 2
    cw = hnT_ref.shape[1] // nch

    def step(w_read, w_write):
        hnT = hnT_ref[...]
        nlc = hnT.shape[1] // LANES

        def lanes_cat(ref, h, rows=slice(None)):
            return jnp.concatenate([ref[h, lc, rows, :] for lc in range(nlc)], axis=1)

        for ip in range(ib // 2):
            if ip % (ib // 2 // nch) == 0:
                ch = ip // (ib // 2 // nch)
                cols = slice(ch * cw, (ch + 1) * cw)
                acc_ref[:, cols] += jnp.dot(vT_ref[...], w_read[:, cols], preferred_element_type=F32)
            iis = (2 * ip, 2 * ip + 1)
            aTs = [jnp.dot(u_ref[ii * nkeys:(ii + 1) * nkeys, :], hnT, preferred_element_type=F32) for ii in iis]
            gsums = [jnp.zeros(aTs[0].shape, BF16) for _ in iis]
            for h in range(PK_HEADS):
                r2 = lanes_cat(r2_ref, h)
                e2 = lanes_cat(e2_ref, h)
                for n, ii in enumerate(iis):
                    row = pl.ds(i * ib + ii, 1)
                    nrow = lanes_cat(n_ref, h, row).astype(BF16)
                    arow = lanes_cat(a1_ref, h, row).astype(BF16)
                    gsums[n] = gsums[n] + jnp.where(r2 < nrow, arow * e2, jnp.zeros((), BF16))
            for n, ii in enumerate(iis):
                w_write[ii * nkeys:(ii + 1) * nkeys, :] = jax.nn.gelu(aTs[n].astype(BF16)) * gsums[n]

    @pl.when((i < nsteps) & (i % 2 == 0))
    def _():
        step(wodd_ref, wcat_ref)

    @pl.when((i < nsteps) & (i % 2 == 1))
    def _():
        step(wcat_ref, wodd_ref)

    @pl.when(i == nsteps)
    def _():
        w_last = wodd_ref if (nkeys // ib) % 2 == 0 else wcat_ref
        acc = acc_ref[...] + jnp.dot(vT_ref[...], w_last[...], preferred_element_type=F32)
        y = x_ref[...] + acc.T
        if final_norm:
            y = _rms(y, wf_ref[...])
        y_ref[...] = y


def _peer(x, wn, wqT, sk, u, vT, wf, tt, ib, final_norm):
    t, d = x.shape
    nkeys = sk.shape[1]
    assert t % tt == 0 and nkeys % ib == 0
    nsteps = nkeys // ib
    nlc = tt // LANES
    stat = pltpu.VMEM((PK_HEADS, nlc, nkeys, LANES), F32)
    stat16 = pltpu.VMEM((PK_HEADS, nlc, nkeys, LANES), BF16)
    wbuf = pltpu.VMEM((ib * nkeys, tt), BF16)
    return pl.pallas_call(
        functools.partial(_peer_kernel, ib=ib, nkeys=nkeys, final_norm=final_norm),
        grid=(t // tt, nsteps + 1),
        in_specs=[pl.BlockSpec((tt, d), lambda a, i: (a, 0)),
                  pl.BlockSpec((1, d), lambda a, i: (0, 0)),
                  pl.BlockSpec(wqT.shape, lambda a, i: (0, 0)),
                  pl.BlockSpec(sk.shape, lambda a, i: (0, 0, 0)),
                  pl.BlockSpec((ib * nkeys, d), lambda a, i: (jnp.minimum(i, nsteps - 1), 0)),
                  pl.BlockSpec((None, d, ib * nkeys), lambda a, i: (jnp.maximum(i - 1, 0), 0, 0)),
                  pl.BlockSpec((1, d), lambda a, i: (0, 0))],
        out_specs=pl.BlockSpec((tt, d), lambda a, i: (a, 0)),
        out_shape=jax.ShapeDtypeStruct((t, d), F32),
        scratch_shapes=[pltpu.VMEM((d, tt), BF16), pltpu.VMEM((d, tt), F32), wbuf, wbuf,
                        pltpu.VMEM((2, nlc, nkeys, LANES), F32), stat, stat, stat16, stat16],
        compiler_params=_cparams(("arbitrary", "arbitrary")), name="peer_dense",
    )(x, wn, wqT, sk, u, vT, wf)


def _compress_pages_kernel(pt_ref, *refs, pg, bpp):
    w_ref, o_ref = refs[pg], refs[pg + 1]
    nt = (((1,), (1,)), ((), ()))
    for j in range(pg):
        x = refs[j][...]
        halves = [lax.dot_general(w_ref[c], x[c].reshape(-1, x.shape[-1]).astype(BF16), nt, preferred_element_type=F32)
                  for c in range(x.shape[0])]
        o_ref[j * bpp:(j + 1) * bpp, :] = jnp.concatenate(halves, axis=1)


def _compress_pages(cache_t, page_table, w_sel, pg):
    tail = cache_t.shape[1:]
    db, npages = page_table.shape
    bpp = w_sel.shape[1]
    width = tail[0] * tail[1] * tail[2]
    assert npages % pg == 0
    zeros = (0,) * len(tail)
    page_specs = [pl.BlockSpec((None,) + tail, functools.partial(lambda b, g, pt, j: (pt[b, g * pg + j],) + zeros, j=j))
                  for j in range(pg)]
    return pl.pallas_call(
        functools.partial(_compress_pages_kernel, pg=pg, bpp=bpp),
        grid_spec=pltpu.PrefetchScalarGridSpec(
            num_scalar_prefetch=1, grid=(db, npages // pg),
            in_specs=page_specs + [pl.BlockSpec(w_sel.shape, lambda b, g, pt: (0, 0, 0))],
            out_specs=pl.BlockSpec((None, pg * bpp, width), lambda b, g, pt: (b, g, 0))),
        out_shape=jax.ShapeDtypeStruct((db, npages * bpp, width), F32),
        compiler_params=_cparams(("arbitrary", "arbitrary")), name="nsa_compress_pages",
    )(page_table, *([cache_t] * pg), w_sel)


def _nsa_decode_head_kernel(slope_ref, qbd_ref, g_ref, kvc_ref, wnew_ref, cwin_ref, part_ref, idx_ref,
                            *, past_len, k_past):
    kvw, grp, dh = NSA_KV_WIDTH, NSA_GROUP, NSA_HEAD_DIM
    qbd = qbd_ref[...]
    qb = qbd.astype(BF16)
    slope = slope_ref[...]
    nh = qbd.shape[0]
    npb = kvc_ref.shape[0]
    nt = (((1,), (1,)), ((), ()))
    kvc = kvc_ref[...]
    sc = lax.dot_general(qb, kvc[:, :kvw].astype(BF16), nt, preferred_element_type=F32)
    c_end = lax.broadcasted_iota(jnp.int32, (1, npb), 1) * NSA_BLOCK + (NSA_BLOCK - 1)
    s = sc - slope * (past_len - c_end).astype(F32)
    e = jnp.exp(s - jnp.max(s, axis=1, keepdims=True))
    p = e / jnp.sum(e, axis=1, keepdims=True)
    oc = jnp.dot(p.astype(BF16), kvc[:, kvw:].astype(BF16), preferred_element_type=F32)
    rows = []
    for k in range(NSA_KV_HEADS):
        r = p[k * grp:k * grp + 1, :]
        for g in range(1, grp):
            r = r + p[k * grp + g:k * grp + g + 1, :]
        rows.append(jnp.broadcast_to(r, (grp, npb)))
    imp = jnp.concatenate(rows, axis=0)
    blk = lax.broadcasted_iota(jnp.int32, (nh, npb), 1)
    score = jnp.where((blk == 0) | (blk == npb - 1), grp + 1.0, imp)
    lanes = blk.astype(F32)
    picks = []
    for _ in range(k_past):
        mx = jnp.max(score, axis=1, keepdims=True)
        idx = jnp.min(jnp.where(score == mx, lanes, float(npb)), axis=1, keepdims=True)
        score = jnp.where(lanes == idx, LOWEST, score)
        picks.append(idx)
    idx_ref[...] = jnp.concatenate(picks, axis=1).astype(jnp.int32)
    sg = jax.nn.sigmoid(g_ref[...])
    wnew = wnew_ref[...]
    nw = cwin_ref.shape[-1]
    dw = (nw - lax.broadcasted_iota(jnp.int32, (1, nw), 1)).astype(F32)
    for k in range(NSA_KV_HEADS):
        hs, ds = slice(k * grp, (k + 1) * grp), slice(k * dh, (k + 1) * dh)
        qk = qbd[hs, ds]
        sw = jnp.dot(qk.astype(BF16), cwin_ref[0, k].astype(BF16), preferred_element_type=F32) - slope[hs] * dw
        s_n = jnp.sum(qk * wnew[:, ds], axis=1, keepdims=True)
        m = jnp.maximum(jnp.max(sw, axis=1, keepdims=True), s_n)
        ew = jnp.exp(sw - m)
        en = jnp.exp(s_n - m)
        pv = lax.dot_general(ew.astype(BF16), cwin_ref[1, k].astype(BF16), nt, preferred_element_type=F32)
        ow = (pv + en * wnew[:, kvw + k * dh:kvw + (k + 1) * dh]) / (jnp.sum(ew, axis=1, keepdims=True) + en)
        part_ref[hs, :] = sg[hs, 0:1] * oc[hs, ds] + sg[hs, 2:3] * ow


def _nsa_decode_gather_kernel(pt_ref, ix_ref, slope_ref, q_ref, g_ref, part_ref, new_ref, *refs, nsel, bpp, past_len):
    blocks, o_ref = refs[:nsel], refs[nsel]
    b, k = pl.program_id(0), pl.program_id(1)
    page = blocks[0].shape[-1]
    q = q_ref[...]
    slope = slope_ref[...]
    kt = jnp.concatenate([blocks[s][0] for s in range(nsel)], axis=1).astype(BF16)
    vt = jnp.concatenate([blocks[s][1] for s in range(nsel)], axis=1).astype(BF16)
    sc = jnp.dot(q.astype(BF16), kt, preferred_element_type=F32)
    lane = lax.broadcasted_iota(jnp.int32, (1, page), 1)
    lblk = lane // NSA_BLOCK
    lkey = lane - lblk * NSA_BLOCK
    bias = []
    for s in range(nsel):
        ib = ix_ref[b, k, s]
        kpos = ib * NSA_BLOCK + lkey
        bias.append(jnp.where(lblk == ib % bpp, -slope * (past_len - kpos).astype(F32), NEG_INF))
    sc = sc + jnp.concatenate(bias, axis=1)
    k_own, v_own = new_ref[0], new_ref[1]
    s_n = jnp.sum(q * k_own, axis=1, keepdims=True)
    m = jnp.maximum(jnp.max(sc, axis=1, keepdims=True), s_n)
    p = jnp.exp(sc - m)
    p_n = jnp.exp(s_n - m)
    pv = lax.dot_general(p.astype(BF16), vt, (((1,), (1,)), ((), ())), preferred_element_type=F32)
    o_s = (pv + p_n * v_own) / (jnp.sum(p, axis=1, keepdims=True) + p_n)
    o_ref[...] = part_ref[...] + jax.nn.sigmoid(g_ref[...])[:, 1:2] * o_s


def _rows_minor(a):
    return a.transpose(0, 2, 3, 4, 1)


def _nsa_sample(q, kv_cmp, kv_sel, kv_win, glog, cache_cmp, cache_sel, cache_win, page_table, w_cmp):
    db = q.shape[0]
    kvh, grp, dh, kvw = NSA_KV_HEADS, NSA_GROUP, NSA_HEAD_DIM, NSA_KV_WIDTH
    page = cache_cmp.shape[1]
    npages = page_table.shape[1]
    past_len = npages * page
    nwin = cache_win.shape[1]
    assert past_len >= nwin and page % NSA_BLOCK == 0
    npb = past_len // NSA_BLOCK
    bpp = page // NSA_BLOCK
    pg = 32 if npages % 32 == 0 else 1
    width = 2 * kvw
    rowblk = jnp.arange(page) // NSA_BLOCK
    w_sel = jnp.where(rowblk[None, None, :] == jnp.arange(bpp)[None, :, None],
                      jnp.tile(w_cmp, (1, bpp))[:, None, :], 0.0).astype(BF16)
    kvc = _compress_pages(_rows_minor(cache_cmp), page_table, w_sel, pg)
    q4 = q.reshape(db, kvh, grp, dh) * (dh ** -0.5)
    qbd = (q4[:, :, :, None, :] * jnp.eye(kvh, dtype=F32)[None, :, None, :, None]).reshape(db, kvh * grp, kvw)
    slope = _alibi_slopes().reshape(kvh * grp, 1)
    nh = kvh * grp
    k_past = min(NSA_TOPK, npb + 1) - 1
    full2 = lambda shape: pl.BlockSpec(shape, lambda b: (0,) * len(shape))
    per_b = lambda shape: pl.BlockSpec((None,) + shape, lambda b: (b,) + (0,) * len(shape))
    part, idx = pl.pallas_call(
        functools.partial(_nsa_decode_head_kernel, past_len=past_len, k_past=k_past),
        grid=(db,),
        in_specs=[full2((nh, 1)), per_b((nh, kvw)), per_b((nh, 3)), per_b((npb, width)), per_b((1, width)),
                  per_b((2, kvh, dh, nwin))],
        out_specs=[per_b((nh, dh)), per_b((nh, k_past))],
        out_shape=[jax.ShapeDtypeStruct((db, nh, dh), F32), jax.ShapeDtypeStruct((db, nh, k_past), jnp.int32)],
        compiler_params=_cparams(("parallel",)), name="nsa_decode_head",
    )(slope, qbd, glog, kvc, kv_win.reshape(db, 1, width), _rows_minor(cache_win))
    idx4 = idx.reshape(db, kvh, grp, k_past)[:, :, 0, :]
    blk_specs = [pl.BlockSpec((None, 2, None, dh, page),
                              functools.partial(lambda b, k, pt, ix, s: (pt[b, ix[b, k, s] // bpp], 0, k, 0, 0), s=s))
                 for s in range(k_past)]
    hsp = lambda shape: pl.BlockSpec((None, None) + shape, lambda b, k, pt, ix: (b, k) + (0,) * len(shape))
    o = pl.pallas_call(
        functools.partial(_nsa_decode_gather_kernel, nsel=k_past, bpp=bpp, past_len=past_len),
        grid_spec=pltpu.PrefetchScalarGridSpec(
            num_scalar_prefetch=2, grid=(db, kvh),
            in_specs=[pl.BlockSpec((None, grp, 1), lambda b, k, pt, ix: (k, 0, 0)), hsp((grp, dh)), hsp((grp, 3)),
                      hsp((grp, dh)),
                      pl.BlockSpec((None, 2, None, 1, dh), lambda b, k, pt, ix: (b, 0, k, 0, 0))] + blk_specs,
            out_specs=hsp((grp, dh))),
        out_shape=jax.ShapeDtypeStruct((db, kvh, grp, dh), F32),
        compiler_params=_cparams(("arbitrary", "arbitrary")), name="nsa_decode_gather",
    )(page_table, idx4, slope.reshape(kvh, grp, 1), q4, glog.reshape(db, kvh, grp, 3), part.reshape(db, kvh, grp, dh),
      kv_sel.reshape(db, 2, kvh, 1, dh), *([_rows_minor(cache_sel)] * k_past))
    new_win = jnp.concatenate([cache_win, kv_win.reshape((db, 1) + cache_win.shape[2:])], axis=1)[:, 1:]
    return o.reshape(db, nh * dh), new_win


def _split_w_in(w_in):
    pts, acc = [], 0
    for w in (NSA_Q_WIDTH, 6 * NSA_KV_WIDTH, 3 * NSA_HEADS, HG_WIDTH, HG_WIDTH, HG_WIDTH, HG_WIDTH, w_in.shape[0]):
        acc += w
        pts.append(acc)
    return jnp.split(w_in, pts, axis=1)


def _prep_layer(w_in, w_proj_nsa, w_proj_hgrn, w_out, w_peer_q, peer_sub_keys, peer_u, peer_v):
    d = w_in.shape[0]
    wq, wkv, wg, wbq, wbf, wbi, wbg, wma, wmb = _split_w_in(w_in)
    wg = wg.reshape(d, NSA_KV_HEADS, NSA_GROUP, 3).transpose(0, 1, 3, 2).reshape(d, NSA_KV_HEADS, 3 * NSA_GROUP)
    wg = jnp.pad(wg, ((0, 0), (0, 0), (0, 16 - 3 * NSA_GROUP))).reshape(d, NSA_KV_HEADS * 16)
    wg = jnp.pad(wg, ((0, 0), (0, LANES - NSA_KV_HEADS * 16)))
    nkeys = peer_sub_keys.shape[2]
    return dict(
        w_qg=jnp.concatenate([wq, wg], axis=1).astype(BF16),
        w_kv=wkv.astype(BF16),
        w_hg=jnp.concatenate([wbq, wbf, wbi, wbg], axis=1).astype(BF16),
        w_m=jnp.concatenate([wma, wmb], axis=1).astype(BF16),
        pa=w_proj_nsa.astype(BF16), pb=w_proj_hgrn.astype(BF16), wo=w_out.astype(BF16),
        wqT=w_peer_q.T.astype(BF16),
        sk=peer_sub_keys.reshape(PK_HEADS * 2, nkeys, PK_DIM // 2).astype(BF16),
        u=peer_u.astype(BF16),
        vT=peer_v.astype(BF16).reshape(nkeys // PEER_IB, PEER_IB * nkeys, d).transpose(0, 2, 1),
    )


def _tile(t, pref):
    return pref if t % pref == 0 else t


def _peer_tokens(x, wn, wf, prm, final_norm):
    t = x.shape[0]
    tp = -(-t // LANES) * LANES
    xp = jnp.pad(x, ((0, tp - t), (0, 0)))
    tt = PEER_TT if tp % PEER_TT == 0 else LANES
    y = _peer(xp, wn, prm["wqT"], prm["sk"], prm["u"], prm["vT"], wf, tt, PEER_IB, final_norm)
    return y[:t]


def _layer_prompt(x, lb, prm, w_norm_mix, w_cmp, w_hgrn_norm, w_norm_ffn, w_norm_final, final_norm):
    s_len, d = x.shape
    tm = _tile(s_len, 512)
    kvw = NSA_KV_WIDTH
    kv_cmp, kv_sel, kv_win = _proj(x, w_norm_mix, prm["w_kv"], [2 * kvw] * 3, [False] * 3, tm)
    qT, gT = _proj(x, w_norm_mix, prm["w_qg"], [NSA_Q_WIDTH, LANES], [True, True], tm)
    hq, hf, hi, hg = _proj(x, w_norm_mix, prm["w_hg"], [HG_WIDTH] * 4, [False] * 4, tm)
    ma, mb = _proj(x, w_norm_mix, prm["w_m"], [d, d], [False, False], tm)
    wfull = jnp.concatenate([jnp.broadcast_to(w_cmp[0][:, None], (NSA_BLOCK, kvw)),
                             jnp.broadcast_to(w_cmp[1][:, None], (NSA_BLOCK, kvw))], axis=1)
    nblk = s_len // NSA_BLOCK
    kvc = _compress(kv_cmp, wfull, 8 if nblk % 8 == 0 else nblk)
    o_a = _nsa_prompt(qT, gT[:NSA_KV_HEADS * 16], kvc, kv_sel, kv_win)
    s0 = jnp.zeros((HG_HEADS, HG_DK, HG_DV), F32)
    o_b, s_fin = _hgrn_prompt(hq, hf, hi, hg, lb, w_hgrn_norm, s0, _tile(s_len, 512))
    x1 = _merge(x, o_a, o_b, ma, mb, prm["pa"], prm["pb"], prm["wo"], tm)
    x2 = _peer_tokens(x1, w_norm_ffn, w_norm_final, prm, final_norm)
    shp = (s_len, 2, NSA_KV_HEADS, NSA_HEAD_DIM)
    return x2, kv_cmp.reshape(shp), kv_sel.reshape(shp), kv_win.reshape(shp)[-NSA_WINDOW:], s_fin


def _layer_sample(x, lb, prm, w_norm_mix, w_cmp, w_hgrn_norm, w_norm_ffn, w_norm_final, final_norm,
                  cache_cmp, cache_sel, cache_win, s0, page_table):
    b, d = x.shape
    kvw = NSA_KV_WIDTH
    w_q = prm["w_qg"][:, :NSA_Q_WIDTH]
    (q,) = _proj(x, w_norm_mix, w_q, [NSA_Q_WIDTH], [False], b)
    kv_cmp, kv_sel, kv_win = _proj(x, w_norm_mix, prm["w_kv"], [2 * kvw] * 3, [False] * 3, b)
    (gTt,) = _proj(x, w_norm_mix, prm["w_qg"][:, NSA_Q_WIDTH:], [LANES], [False], b)
    hq, hf, hi, hg = _proj(x, w_norm_mix, prm["w_hg"], [HG_WIDTH] * 4, [False] * 4, b)
    ma, mb = _proj(x, w_norm_mix, prm["w_m"], [d, d], [False, False], b)
    shp = (b, 1, 2, NSA_KV_HEADS, NSA_HEAD_DIM)
    glog = (gTt[:, :NSA_KV_HEADS * 16].reshape(b, NSA_KV_HEADS, 16)[..., :3 * NSA_GROUP]
            .reshape(b, NSA_KV_HEADS, 3, NSA_GROUP).transpose(0, 1, 3, 2).reshape(b, NSA_HEADS, 3))
    o_a, new_win = _nsa_sample(q, kv_cmp, kv_sel, kv_win, glog, cache_cmp, cache_sel, cache_win, page_table, w_cmp)
    o_b, s_new = _hgrn_step(hq.reshape(b, HG_WIDTH, 1), hf.reshape(b, HG_WIDTH, 1), hi.reshape(b, 1, HG_WIDTH),
                            hg.reshape(b, 1, HG_WIDTH), lb.reshape(HG_WIDTH, 1), w_hgrn_norm, s0)
    x1 = _merge(x, o_a.reshape(b, NSA_Q_WIDTH), o_b.reshape(b, HG_WIDTH), ma, mb, prm["pa"], prm["pb"], prm["wo"], b)
    x2 = _peer_tokens(x1, w_norm_ffn, w_norm_final, prm, final_norm)
    return x2, kv_cmp.reshape(shp), kv_sel.reshape(shp), new_win, s_new


def kernel(x_prompt, x_sample, cache_cmp_kv, cache_sel_kv, cache_win_kv, state_hgrn, page_table,
           w_norm_mix, w_in, w_cmp, w_proj_nsa, w_proj_hgrn, w_hgrn_norm, hgrn_lb_logits, w_out,
           w_norm_ffn, w_peer_q, peer_sub_keys, peer_u, peer_v, w_norm_final):
    depth = w_in.shape[0]
    bsz, s_len, d = x_prompt.shape
    db, dt, _ = x_sample.shape
    assert dt == 1
    lbs = jnp.cumsum(jax.nn.softmax(hgrn_lb_logits.astype(F32), axis=0), axis=0)
    wfin = w_norm_final.reshape(1, d)
    xp = [x_prompt[b] for b in range(bsz)]
    xs = x_sample.reshape(db, d)
    st_p, st_s = [], []
    for l in range(depth):
        last = l == depth - 1
        prm = _prep_layer(w_in[l], w_proj_nsa[l], w_proj_hgrn[l], w_out[l], w_peer_q[l], peer_sub_keys[l],
                          peer_u[l], peer_v[l])
        shared = (lbs[l].reshape(1, HG_WIDTH), prm, w_norm_mix[l].reshape(1, d), w_cmp[l],
                  w_hgrn_norm[l].reshape(1, HG_DV), w_norm_ffn[l].reshape(1, d), wfin, last)
        outs = [_layer_prompt(xp[b], *shared) for b in range(bsz)]
        xp = [o[0] for o in outs]
        st_p.append(tuple(jnp.stack([o[k] for o in outs]) for k in range(1, 5)))
        xs, *ss = _layer_sample(xs, *shared, cache_cmp_kv[l], cache_sel_kv[l], cache_win_kv[l], state_hgrn[l],
                                page_table)
        st_s.append(tuple(ss))
    y_prompt = jnp.stack(xp)
    y_sample = xs.reshape(db, dt, d)
    return (y_prompt, y_sample,
            jnp.stack([s[0] for s in st_p]), jnp.stack([s[1] for s in st_p]),
            jnp.stack([s[2] for s in st_p]), jnp.stack([s[3] for s in st_p]),
            jnp.stack([s[0] for s in st_s]), jnp.stack([s[1] for s in st_s]),
            jnp.stack([s[2] for s in st_s]), jnp.stack([s[3] for s in st_s]))
```

```python
import functools

import jax
import jax.numpy as jnp
from jax import lax
from jax.experimental import pallas as pl
from jax.experimental.pallas import tpu as pltpu

F32 = jnp.float32
BF16 = jnp.bfloat16

NSA_HEADS = 16
NSA_KV_HEADS = 4
NSA_GROUP = NSA_HEADS // NSA_KV_HEADS
NSA_HEAD_DIM = 64
NSA_BLOCK = 64
NSA_TOPK = 16
NSA_WINDOW = 512
NSA_QBLOCK = 256
HG_HEADS = 8
HG_DK = 128
HG_DV = 128
HG_CHUNK = 64
HG_SUB = 16
HG_HPB = 8
PK_HEADS = 8
PK_DIM = 256
PK_TOPK = 16
RMS_EPS = 1e-6
NEG_INF = -1e30
LOWEST = -3e38
TAKEN = 2.0 ** 100

NSA_Q_WIDTH = NSA_HEADS * NSA_HEAD_DIM
NSA_KV_WIDTH = NSA_KV_HEADS * NSA_HEAD_DIM
HG_WIDTH = HG_HEADS * HG_DK

LANES = 128
VMEM_LIMIT_BYTES = 56 * 1024 * 1024

NSA_COLS = NSA_GROUP * NSA_QBLOCK
NSA_KTILE = 512
NSA_CDIM = 128
NSA_WKEYS = NSA_WINDOW + NSA_QBLOCK
PEER_TT = 512
PEER_IB = 16


def _cparams(sem):
    return pltpu.CompilerParams(dimension_semantics=sem, vmem_limit_bytes=VMEM_LIMIT_BYTES)


def _rms(x, w):
    return x * lax.rsqrt(jnp.mean(x * x, axis=-1, keepdims=True) + RMS_EPS) * w


def _proj_kernel(x_ref, wn_ref, w_ref, *out_refs, widths, transposed, chunk):
    hb = _rms(x_ref[...], wn_ref[...]).astype(BF16)
    off = 0
    for o_ref, wd, tr in zip(out_refs, widths, transposed):
        for c0 in range(0, wd, chunk):
            cw = min(chunk, wd - c0)
            r = jnp.dot(hb, w_ref[:, off + c0:off + c0 + cw], preferred_element_type=F32)
            if tr:
                o_ref[c0:c0 + cw, :] = r.T
            else:
                o_ref[:, c0:c0 + cw] = r
        off += wd


def _proj(x, wn, w, widths, transposed, tm):
    t, d = x.shape
    n = w.shape[1]
    assert sum(widths) == n and t % tm == 0
    out_shape, out_specs = [], []
    for wd, tr in zip(widths, transposed):
        if tr:
            out_shape.append(jax.ShapeDtypeStruct((wd, t), F32))
            out_specs.append(pl.BlockSpec((wd, tm), lambda i: (0, i)))
        else:
            out_shape.append(jax.ShapeDtypeStruct((t, wd), F32))
            out_specs.append(pl.BlockSpec((tm, wd), lambda i: (i, 0)))
    return pl.pallas_call(
        functools.partial(_proj_kernel, widths=tuple(widths), transposed=tuple(transposed), chunk=512),
        grid=(t // tm,),
        in_specs=[pl.BlockSpec((tm, d), lambda i: (i, 0)),
                  pl.BlockSpec((1, d), lambda i: (0, 0)),
                  pl.BlockSpec((d, n), lambda i: (0, 0))],
        out_specs=out_specs, out_shape=out_shape,
        compiler_params=_cparams(("parallel",)), name="rms_proj",
    )(x, wn, w)


def _compress_kernel(kv_ref, w_ref, o_ref, *, nb):
    x = kv_ref[...]
    width = x.shape[-1]
    x3 = x.reshape(nb, NSA_BLOCK, width) * w_ref[...][None]
    o_ref[...] = jnp.sum(x3, axis=1)


def _compress(kv, wfull, nb):
    t, width = kv.shape
    rows = nb * NSA_BLOCK
    assert t % rows == 0
    return pl.pallas_call(
        functools.partial(_compress_kernel, nb=nb),
        grid=(t // rows,),
        in_specs=[pl.BlockSpec((rows, width), lambda i: (i, 0)),
                  pl.BlockSpec((NSA_BLOCK, width), lambda i: (0, 0))],
        out_specs=pl.BlockSpec((nb, width), lambda i: (i, 0)),
        out_shape=jax.ShapeDtypeStruct((t // NSA_BLOCK, width), F32),
        compiler_params=_cparams(("parallel",)), name="nsa_compress",
    )(kv, wfull)


def _topk_select_bias(score, k):
    n = score.shape[0]
    rows = lax.broadcasted_iota(jnp.int32, score.shape, 0).astype(F32)
    for _ in range(k):
        mx = jnp.max(score, axis=0, keepdims=True)
        idx = jnp.min(jnp.where(score == mx, rows, float(n)), axis=0, keepdims=True)
        score = jnp.where(rows == idx, -TAKEN, score)
    return jnp.where(score < -0.5 * TAKEN, 0.0, NEG_INF)


def _nsa_prompt_kernel(slope_ref, qT_ref, gT_ref, kc_ref, vcT_ref, ksel_ref, vselT_ref,
                       kd_ref, vdT_ref, *rest, nblk, nwb):
    kws, vws = rest[:nwb], rest[nwb:2 * nwb]
    bw_ref, out_ref, qs_ref, selb_ref, sa_ref, sb_ref, pa_ref, pb_ref, tiles_ref = rest[2 * nwb:]
    i = pl.program_id(1)
    t0 = i * NSA_QBLOCK
    dh, qb, ncol, tk = NSA_HEAD_DIM, NSA_QBLOCK, NSA_COLS, NSA_KTILE
    slope = slope_ref[...]
    col = lax.broadcasted_iota(jnp.int32, (1, ncol), 1)
    tpos = t0 + (col & (qb - 1))
    tposf = tpos.astype(F32)
    q4 = qT_ref[...] * (dh ** -0.5)
    qT = jnp.concatenate([q4[g * dh:(g + 1) * dh, :] for g in range(NSA_GROUP)], axis=1)
    xrow = lax.broadcasted_iota(jnp.int32, (NSA_CDIM - dh, ncol), 0)

    qc = jnp.concatenate([qT, jnp.zeros((NSA_CDIM - dh, ncol), F32)], axis=0).astype(BF16)
    sc = jnp.dot(kc_ref[...], qc, preferred_element_type=F32)
    c_end = lax.broadcasted_iota(jnp.int32, (nblk, 1), 0) * NSA_BLOCK + (NSA_BLOCK - 1)
    valid = c_end <= tpos
    s = jnp.where(valid, sc - slope * (tposf - c_end.astype(F32)), NEG_INF)
    e = jnp.exp(s - jnp.max(s, axis=0, keepdims=True))
    p = jnp.where(valid, e / jnp.sum(e, axis=0, keepdims=True), 0.0)
    ocT = jnp.dot(vcT_ref[...], p.astype(BF16), preferred_element_type=F32)

    qw = jnp.concatenate([qT, jnp.where(xrow == 0, NEG_INF, 0.0)], axis=0).astype(BF16)
    kw = jnp.concatenate([r[...] for r in kws], axis=0)
    sw = jnp.dot(kw, qw, preferred_element_type=F32) + bw_ref[...]
    ew = jnp.exp(sw - jnp.max(sw, axis=0, keepdims=True))
    vw = jnp.concatenate([r[...] for r in vws], axis=1)
    owT = jnp.dot(vw, ew.astype(BF16), preferred_element_type=F32) / jnp.sum(ew, axis=0, keepdims=True)

    nbt = tk // NSA_BLOCK
    jl = t0 // tk
    s1 = slope.astype(BF16).astype(F32)
    r1 = slope - s1
    s2 = r1.astype(BF16).astype(F32)
    s3 = (r1 - s2).astype(BF16).astype(F32)
    half = float(tk // 2)
    ext = jnp.zeros((NSA_CDIM - dh, ncol), F32)
    for r, v in enumerate((s1, s2, s3, s1 * half, s2 * half, s3 * half)):
        ext = jnp.where(xrow == nbt + r, v, ext)

    qd = jnp.concatenate([qT, ext], axis=0).astype(BF16)
    sd = jnp.dot(kd_ref[...], qd, preferred_element_type=F32)
    kposd = t0 + lax.broadcasted_iota(jnp.int32, (qb, 1), 0)
    sd = jnp.where(kposd > tpos, NEG_INF, sd)
    mx_d = jnp.max(sd, axis=0, keepdims=True)
    pd = jnp.exp(sd - mx_d)
    m_d = mx_d + slope * (jl * tk - tpos).astype(F32)
    l_d = jnp.sum(pd, axis=0, keepdims=True)
    acc_d = jnp.dot(vdT_ref[...], pd.astype(BF16), preferred_element_type=F32)

    imp = p[:, 0:qb]
    for g in range(1, NSA_GROUP):
        imp = imp + p[:, g * qb:(g + 1) * qb]
    blk = lax.broadcasted_iota(jnp.int32, (nblk, qb), 0)
    cur = tpos[:, 0:qb] >> (NSA_BLOCK.bit_length() - 1)
    forced = (blk == 0) | (blk == cur) | (blk == cur - 1)
    score = jnp.where(blk > cur, -1.0, jnp.where(forced, NSA_GROUP + 1.0, imp))
    if nblk >= NSA_TOPK:
        selb = _topk_select_bias(jnp.where(forced, -TAKEN, score), NSA_TOPK - 3)
    else:
        selb = _topk_select_bias(score, nblk)
    blkc = lax.broadcasted_iota(jnp.int32, (nblk, ncol), 0)
    selb_ref[0:nblk, :] = jnp.where(blkc >= t0 // NSA_BLOCK, NEG_INF, jnp.concatenate([selb] * NSA_GROUP, axis=1))
    selb_ref[nblk:nblk + nbt, :] = jnp.full((nbt, ncol), NEG_INF, F32)

    fl = jnp.max(selb_ref[0:nblk, 0:qb].reshape(nblk // nbt, nbt, qb), axis=1)
    tiles_ref[0] = 0
    n_act = jnp.int32(0)
    for j in range(nblk // nbt):
        tiles_ref[n_act] = j
        n_act = n_act + (jnp.max(fl[j:j + 1, :]) > 0.5 * NEG_INF).astype(jnp.int32)
    npairs = (n_act + 1) // 2

    qs_ref[0:dh, :] = qT
    qs_ref[dh:, :] = ext

    def tile_id(t):
        return tiles_ref[jnp.clip(t, 0, jnp.maximum(n_act - 1, 0))]

    def qk_into(t, s_ref):
        tid = tile_id(t)
        b0 = pl.multiple_of(jnp.where(t < n_act, tid * nbt, nblk), 8)
        qs_ref[dh:dh + nbt, :] = selb_ref[pl.ds(b0, nbt), :]
        k0 = pl.multiple_of(tid * tk, tk)
        s_ref[...] = jnp.dot(ksel_ref[pl.ds(k0, tk), :], qs_ref[...].astype(BF16), preferred_element_type=F32)

    def soft(t, s_ref, p_ref, m):
        cj = slope * (tile_id(t) * tk - tpos).astype(F32)
        sj = s_ref[...]
        m_new = jnp.maximum(m, jnp.max(sj, axis=0, keepdims=True) + cj)
        p_ref[...] = jnp.exp(sj - (m_new - cj)).astype(BF16)
        return m_new, jnp.exp(m - m_new)

    def pv(t, p_ref):
        return jnp.dot(vselT_ref[tile_id(t)], p_ref[...], preferred_element_type=F32)

    def pair(i, carry):
        m, accp = carry
        ta = 2 * i
        pvb = pv(ta - 1, pb_ref)
        qk_into(ta + 1, sb_ref)
        m, alpha = soft(ta, sa_ref, pa_ref, m)
        accp = alpha * (accp + pvb)
        pva = pv(ta, pa_ref)
        qk_into(ta + 2, sa_ref)
        m, alpha = soft(ta + 1, sb_ref, pb_ref, m)
        accp = alpha * (accp + pva)
        return m, accp

    pb_ref[...] = jnp.zeros(pb_ref.shape, BF16)
    qk_into(0, sa_ref)
    vrows = vselT_ref.shape[1]
    init = (jnp.full((1, ncol), NEG_INF, F32), jnp.zeros((vrows, ncol), F32))
    m_s, accp = lax.fori_loop(0, npairs, pair, init)
    accl = accp + pv(2 * npairs - 1, pb_ref)
    acc_s, l_s = accl[0:dh], accl[dh:dh + 1]

    m_f = jnp.maximum(m_s, m_d)
    a_s = jnp.exp(m_s - m_f)
    a_d = jnp.exp(m_d - m_f)
    osT = (a_s * acc_s + a_d * acc_d) / (a_s * l_s + a_d * l_d)

    sg = jax.nn.sigmoid(gT_ref[...])

    def gate(c):
        return jnp.concatenate([sg[c * NSA_GROUP + g:c * NSA_GROUP + g + 1, :] for g in range(NSA_GROUP)], axis=1)

    oT = gate(0) * ocT + gate(1) * osT + gate(2) * owT
    o4 = jnp.concatenate([oT[:, g * qb:(g + 1) * qb] for g in range(NSA_GROUP)], axis=0)
    out_ref[...] = o4.T


def _alibi_slopes():
    h = jnp.arange(NSA_HEADS, dtype=F32)
    return (2.0 ** (-8.0 * (h + 1.0) / NSA_HEADS)).reshape(NSA_KV_HEADS, NSA_GROUP)


def _nsa_prompt(qT, gT, kvc, kv_sel, kv_win):
    s_len = qT.shape[1]
    dh, qb, tk, kvh = NSA_HEAD_DIM, NSA_QBLOCK, NSA_KTILE, NSA_KV_HEADS
    assert s_len % tk == 0
    nblk = s_len // NSA_BLOCK
    slopes = _alibi_slopes()
    slope_cols = jnp.repeat(slopes, qb, axis=1).reshape(kvh, 1, NSA_COLS)

    def heads_major(a):
        return a.reshape(a.shape[0], kvh, dh).transpose(1, 0, 2)

    def pad_lanes(a):
        return jnp.pad(a, ((0, 0), (0, 0), (0, NSA_CDIM - a.shape[-1])))

    kc = pad_lanes(heads_major(kvc[:, :NSA_KV_WIDTH])).astype(BF16)
    vcT = heads_major(kvc[:, NSA_KV_WIDTH:]).transpose(0, 2, 1).astype(BF16)
    r = jnp.arange(tk)
    onehot = (r[:, None] // NSA_BLOCK == jnp.arange(tk // NSA_BLOCK)[None, :]).astype(F32)
    lo = (r % (tk // 2)).astype(F32)[:, None]
    hi = (r // (tk // 2)).astype(F32)[:, None]
    kext = jnp.concatenate([onehot, lo, lo, lo, hi, hi, hi], axis=1)
    kext = jnp.tile(kext, (s_len // tk, 1))
    ksel = pad_lanes(jnp.concatenate(
        [heads_major(kv_sel[:, :NSA_KV_WIDTH]), jnp.broadcast_to(kext[None], (kvh,) + kext.shape)], axis=-1)).astype(BF16)
    vsel = heads_major(kv_sel[:, NSA_KV_WIDTH:]).astype(BF16)
    vselT = vsel.reshape(kvh, s_len // tk, tk, dh).transpose(0, 1, 3, 2)
    ones_rows = jnp.zeros((kvh, s_len // tk, 8, tk), BF16).at[:, :, 0, :].set(1.0)
    vselT = jnp.concatenate([vselT, ones_rows], axis=2)
    vselT_flat = vsel.transpose(0, 2, 1)
    kwin = heads_major(kv_win[:, :NSA_KV_WIDTH])
    kwin = jnp.concatenate([kwin, jnp.zeros((kvh, s_len, 1), F32)], axis=-1)
    padk = jnp.zeros((kvh, NSA_WINDOW, dh + 1), F32).at[:, :, dh].set(1.0)
    kwin = pad_lanes(jnp.concatenate([padk, kwin], axis=1)).astype(BF16)
    vwinT = jnp.pad(heads_major(kv_win[:, NSA_KV_WIDTH:]), ((0, 0), (NSA_WINDOW, 0), (0, 0))).transpose(0, 2, 1).astype(BF16)
    rr = jnp.arange(NSA_WKEYS)[:, None]
    cc = jnp.arange(NSA_COLS)[None, :]
    dw = (cc % qb) + NSA_WINDOW - rr
    bw = jnp.where((dw >= 0) & (dw <= NSA_WINDOW), -slope_cols * dw.astype(F32)[None], NEG_INF)

    nq = s_len // qb
    nwb = NSA_WKEYS // qb
    kw_specs = [pl.BlockSpec((None, qb, NSA_CDIM), functools.partial(lambda k, i, j: (k, i + j, 0), j=j)) for j in range(nwb)]
    vw_specs = [pl.BlockSpec((None, dh, qb), functools.partial(lambda k, i, j: (k, 0, i + j), j=j)) for j in range(nwb)]
    assert NSA_WINDOW % qb == 0 and s_len % qb == 0 and tk % qb == 0
    return pl.pallas_call(
        functools.partial(_nsa_prompt_kernel, nblk=nblk, nwb=nwb),
        grid=(kvh, nq),
        in_specs=[pl.BlockSpec((None, 1, NSA_COLS), lambda k, i: (k, 0, 0)),
                  pl.BlockSpec((NSA_GROUP * dh, qb), lambda k, i: (k, i)),
                  pl.BlockSpec((None, 16, qb), lambda k, i: (k, 0, i)),
                  pl.BlockSpec((None, nblk, NSA_CDIM), lambda k, i: (k, 0, 0)),
                  pl.BlockSpec((None, dh, nblk), lambda k, i: (k, 0, 0)),
                  pl.BlockSpec((None, s_len, NSA_CDIM), lambda k, i: (k, 0, 0)),
                  pl.BlockSpec((None, s_len // tk, dh + 8, tk), lambda k, i: (k, 0, 0, 0)),
                  pl.BlockSpec((None, qb, NSA_CDIM), lambda k, i: (k, i, 0)),
                  pl.BlockSpec((None, dh, qb), lambda k, i: (k, 0, i))]
                 + kw_specs + vw_specs
                 + [pl.BlockSpec((None, NSA_WKEYS, NSA_COLS), lambda k, i: (k, 0, 0))],
        out_specs=pl.BlockSpec((qb, NSA_GROUP * dh), lambda k, i: (i, k)),
        out_shape=jax.ShapeDtypeStruct((s_len, NSA_Q_WIDTH), F32),
        scratch_shapes=[pltpu.VMEM((NSA_CDIM, NSA_COLS), F32),
                        pltpu.VMEM((nblk + tk // NSA_BLOCK, NSA_COLS), F32),
                        pltpu.VMEM((tk, NSA_COLS), F32), pltpu.VMEM((tk, NSA_COLS), F32),
                        pltpu.VMEM((tk, NSA_COLS), BF16), pltpu.VMEM((tk, NSA_COLS), BF16),
                        pltpu.SMEM((s_len // tk + 1,), jnp.int32)],
        compiler_params=_cparams(("arbitrary", "arbitrary")), name="nsa_prompt",
    )(slope_cols, qT, gT.reshape(kvh, 16, s_len), kc, vcT, ksel, vselT, ksel, vselT_flat,
      *([kwin] * nwb), *([vwinT] * nwb), bw)


def _hgrn_chunk(qc, zf, vc, lb, st):
    c, sub = HG_CHUNK, HG_SUB
    logf = jnp.log(lb + (1.0 - lb) * jax.nn.sigmoid(zf))
    kc = (1.0 - lb) * jax.nn.sigmoid(-zf)
    tri = (lax.broadcasted_iota(jnp.int32, (c, c), 0) >= lax.broadcasted_iota(jnp.int32, (c, c), 1)).astype(F32)
    cb = jnp.dot(tri, logf, preferred_element_type=F32, precision=lax.Precision.HIGHEST)
    o = lax.dot_general((qc * jnp.exp(cb)).astype(BF16), st.astype(BF16), (((1,), (1,)), ((), ())),
                        preferred_element_type=F32)
    t3 = lax.broadcasted_iota(jnp.int32, (sub, sub, 1), 0) >= lax.broadcasted_iota(jnp.int32, (sub, sub, 1), 1)
    outs = []
    for a in range(c // sub):
        ra = slice(a * sub, (a + 1) * sub)
        cba, qa, ka, va = cb[ra], qc[ra], kc[ra], vc[ra]
        d3 = cba[:, None, :] - cba[None, :, :]
        x3 = jnp.where(t3, jnp.exp(d3), 0.0) * qa[:, None, :] * ka[None, :, :]
        att3 = jnp.sum(x3, axis=2, keepdims=True)
        oa = o[ra] + jnp.sum(att3 * va[None, :, :], axis=1)
        if a > 0:
            ref = cb[a * sub - 1:a * sub, :]
            qd = (qa * jnp.exp(cba - ref)).astype(BF16)
            kd = (kc[:a * sub] * jnp.exp(ref - cb[:a * sub])).astype(BF16)
            att = lax.dot_general(qd, kd, (((1,), (1,)), ((), ())), preferred_element_type=F32)
            oa = oa + jnp.dot(att.astype(BF16), vc[:a * sub].astype(BF16), preferred_element_type=F32)
        outs.append(oa)
    o = jnp.concatenate(outs, axis=0)
    last = cb[c - 1:c, :]
    kdec = (kc * jnp.exp(last - cb)).astype(BF16)
    st = st * jnp.exp(last) + lax.dot_general(vc.astype(BF16), kdec, (((0,), (0,)), ((), ())),
                                              preferred_element_type=F32)
    return o, st


def _hgrn_kernel(lb_ref, wn_ref, q_ref, f_ref, v_ref, g_ref, s0_ref, o_ref, sfin_ref, st_ref, *, nsub):
    c = pl.program_id(1)
    hpb = HG_HPB

    @pl.when(c == 0)
    def _():
        for j in range(hpb):
            st_ref[j] = s0_ref[j].T

    wn = wn_ref[...]

    def body(u, sts):
        rows = pl.ds(pl.multiple_of(u * HG_CHUNK, HG_CHUNK), HG_CHUNK)
        out = []
        for j in range(hpb):
            cols = slice(j * HG_DK, (j + 1) * HG_DK)
            o, st = _hgrn_chunk(q_ref[rows, cols], f_ref[rows, cols], v_ref[rows, cols], lb_ref[:, cols], sts[j])
            g = g_ref[rows, cols]
            o_ref[rows, cols] = _rms(o, wn) * (g * jax.nn.sigmoid(g))
            out.append(st)
        return tuple(out)

    sts = lax.fori_loop(0, nsub, body, tuple(st_ref[j] for j in range(hpb)))
    for j in range(hpb):
        st_ref[j] = sts[j]

    @pl.when(c == pl.num_programs(1) - 1)
    def _():
        for j in range(hpb):
            sfin_ref[j] = sts[j].T


def _hgrn_prompt(hq, hf, hi, hg, lb, wn, s0, tb):
    t = hq.shape[0]
    hpb = HG_HPB
    assert t % tb == 0 and tb % HG_CHUNK == 0 and HG_HEADS % hpb == 0 and HG_DK == HG_DV
    tok = pl.BlockSpec((tb, hpb * HG_DK), lambda h, c: (c, h))
    stt = pl.BlockSpec((hpb, HG_DK, HG_DV), lambda h, c: (h, 0, 0))
    return pl.pallas_call(
        functools.partial(_hgrn_kernel, nsub=tb // HG_CHUNK),
        grid=(HG_HEADS // hpb, t // tb),
        in_specs=[pl.BlockSpec((1, hpb * HG_DK), lambda h, c: (0, h)),
                  pl.BlockSpec((1, HG_DV), lambda h, c: (0, 0)),
                  tok, tok, tok, tok, stt],
        out_specs=[tok, stt],
        out_shape=[jax.ShapeDtypeStruct((t, HG_HEADS * HG_DV), F32),
                   jax.ShapeDtypeStruct((HG_HEADS, HG_DK, HG_DV), F32)],
        scratch_shapes=[pltpu.VMEM((hpb, HG_DV, HG_DK), F32)],
        compiler_params=_cparams(("arbitrary", "arbitrary")), name="hgrn_prompt",
    )(lb, wn, hq, hf, hi, hg, s0)


def _hgrn_step_kernel(lb_ref, wn_ref, q_ref, f_ref, v_ref, g_ref, s0_ref, o_ref, s_ref):
    wn = wn_ref[...]
    eye = lax.broadcasted_iota(jnp.int32, (HG_DK, HG_DK), 0) == lax.broadcasted_iota(jnp.int32, (HG_DK, HG_DK), 1)

    def col(row):
        return jnp.sum(jnp.where(eye, row, 0.0), axis=1, keepdims=True)

    for h in range(HG_HEADS):
        cs = slice(h * HG_DK, (h + 1) * HG_DK)
        lb = lb_ref[:, cs]
        zf = f_ref[:, cs]
        f = col(lb + (1.0 - lb) * jax.nn.sigmoid(zf))
        kk = col((1.0 - lb) * jax.nn.sigmoid(-zf))
        vrow = v_ref[:, cs]
        s_new = f * s0_ref[h] + kk * vrow
        s_ref[h] = s_new
        o = jnp.sum(s_new * col(q_ref[:, cs]), axis=0, keepdims=True)
        g = g_ref[:, cs]
        o_ref[:, cs] = _rms(o, wn) * (g * jax.nn.sigmoid(g))


def _hgrn_step(hq, hf, hi, hg, lb, wn, s0):
    b = hi.shape[0]
    row = pl.BlockSpec((None, 1, HG_WIDTH), lambda i: (i, 0, 0))
    st = pl.BlockSpec((None, HG_HEADS, HG_DK, HG_DV), lambda i: (i, 0, 0, 0))
    return pl.pallas_call(
        _hgrn_step_kernel, grid=(b,),
        in_specs=[pl.BlockSpec((1, HG_WIDTH), lambda i: (0, 0)), pl.BlockSpec((1, HG_DV), lambda i: (0, 0)),
                  row, row, row, row, st],
        out_specs=[row, st],
        out_shape=[jax.ShapeDtypeStruct((b, 1, HG_WIDTH), F32), jax.ShapeDtypeStruct(s0.shape, F32)],
        compiler_params=_cparams(("parallel",)), name="hgrn_step",
    )(lb, wn, hq, hf, hi, hg, s0)


def _merge_kernel(x_ref, oa_ref, ob_ref, ma_ref, mb_ref, pa_ref, pb_ref, wo_ref, y_ref):
    ya = jnp.dot(oa_ref[...].astype(BF16), pa_ref[...], preferred_element_type=F32)
    yb = jnp.dot(ob_ref[...].astype(BF16), pb_ref[...], preferred_element_type=F32)
    mix = jax.nn.sigmoid(ma_ref[...]) * ya + jax.nn.sigmoid(mb_ref[...]) * yb
    y_ref[...] = x_ref[...] + jnp.dot(mix.astype(BF16), wo_ref[...], preferred_element_type=F32)


def _merge(x, oa, ob, ma, mb, pa, pb, wo, tm):
    t, d = x.shape
    tok = pl.BlockSpec((tm, d), lambda i: (i, 0))
    wsp = pl.BlockSpec((d, d), lambda i: (0, 0))
    return pl.pallas_call(
        _merge_kernel, grid=(t // tm,),
        in_specs=[tok, tok, tok, tok, tok, wsp, wsp, wsp],
        out_specs=tok, out_shape=jax.ShapeDtypeStruct((t, d), F32),
        compiler_params=_cparams(("parallel",)), name="branch_merge",
    )(x, oa, ob, ma, mb, pa, pb, wo)


def _topk_rows(s, k):
    n = s.shape[0]
    rows = lax.broadcasted_iota(jnp.int32, s.shape, 0).astype(F32)
    tops = []
    for r in range(k):
        mx = jnp.max(s, axis=0, keepdims=True)
        idx = jnp.min(jnp.where(s == mx, rows, float(n)), axis=0, keepdims=True)
        s = jnp.where(rows == idx, -TAKEN * (1.0 + r / 64.0), s)
        tops.append(mx)
    rank = jnp.where(s < -0.5 * TAKEN, (s * (-1.0 / TAKEN) - 1.0) * 64.0, float(n))
    return jnp.concatenate(tops, axis=0), rank


_STAIR_GROUPS = ((0, 16), (1, 8), (2, 8), (3, 8))
_STAIR_QUAD = (4, 5, 6, 7)
_STAIR_TAIL = 8


def _stair_rows(t1, t2, op):
    tt = t1.shape[1]
    parts = [op(jnp.broadcast_to(t1[a:a + 1, :], (nb, tt)), t2[0:nb, :]) for a, nb in _STAIR_GROUPS]
    r16 = lax.broadcasted_iota(jnp.int32, (16, tt), 0)
    v1 = jnp.broadcast_to(t1[_STAIR_QUAD[3]:_STAIR_QUAD[3] + 1, :], (16, tt))
    v2 = jnp.broadcast_to(t2[3:4, :], (16, tt))
    for q in (2, 1, 0):
        v1 = jnp.where(r16 < 4 * (q + 1), jnp.broadcast_to(t1[_STAIR_QUAD[q]:_STAIR_QUAD[q] + 1, :], (16, tt)), v1)
        v2 = jnp.where((r16 & 3) == q, jnp.broadcast_to(t2[q:q + 1, :], (16, tt)), v2)
    parts.append(op(v1, v2))
    parts.append(op(t1[_STAIR_TAIL:, :], jnp.broadcast_to(t2[0:1, :], (PK_TOPK - _STAIR_TAIL, tt))))
    return jnp.concatenate(parts, axis=0)


def _stair_row_counts(selc):
    out, r0 = [], 0
    for _, nb in _STAIR_GROUPS:
        out.append(jnp.sum(selc[r0:r0 + nb, :], axis=0, keepdims=True))
        r0 += nb
    quad = selc[r0:r0 + 16, :]
    r16 = lax.broadcasted_iota(jnp.int32, quad.shape, 0)
    for q in range(4):
        out.append(jnp.sum(jnp.where((r16 >> 2) == q, quad, 0.0), axis=0, keepdims=True))
    r0 += 16
    for a in range(PK_TOPK - _STAIR_TAIL):
        out.append(selc[r0 + a:r0 + a + 1, :])
    return out


def _peer_kernel(x_ref, wn_ref, wq_ref, sk_ref, u_ref, vT_ref, wf_ref, y_ref,
                 hnT_ref, acc_ref, wcat_ref, wodd_ref, s_ref, n_ref, a1_ref, r2_ref, e2_ref, *, ib, nkeys, final_norm):
    i = pl.program_id(1)
    kt = PK_TOPK
    hd = PK_DIM // 2

    @pl.when(i == 0)
    def _():
        hnT = _rms(x_ref[...], wn_ref[...]).T.astype(BF16)
        hnT_ref[...] = hnT
        acc_ref[...] = jnp.zeros(acc_ref.shape, F32)
        wcat_ref[...] = jnp.zeros(wcat_ref.shape, BF16)
        wodd_ref[...] = jnp.zeros(wodd_ref.shape, BF16)
        tt = hnT.shape[1]
        nlc = tt // LANES
        lcu = 2 if nlc % 2 == 0 else 1
        for h in range(PK_HEADS):
            for c in range(2):
                r0 = (h * 2 + c) * hd
                qhc = jnp.dot(wq_ref[r0:r0 + hd, :], hnT, preferred_element_type=F32)
                s = jnp.dot(sk_ref[h * 2 + c], qhc.astype(BF16), preferred_element_type=F32)
                for lc in range(nlc):
                    s_ref[c, lc] = s[:, lc * LANES:(lc + 1) * LANES]

            def chunk(lc, carry):
                for sub in range(lcu):
                    chunk_one(lc * lcu + sub)
                return carry

            def chunk_one(lc):
                s0, s1 = s_ref[0, lc], s_ref[1, lc]
                top0, rank0 = _topk_rows(s0, kt)
                top1, rank1 = _topk_rows(s1, kt)
                cand = _stair_rows(top0, top1, jnp.add)
                _, crank = _topk_rows(cand, kt)
                selc = (crank < float(kt)).astype(F32)
                n_a = _stair_row_counts(selc)
                e1t = jnp.exp(top0 - top0[0:1, :])
                e2t = jnp.exp(top1 - top1[0:1, :])
                z = jnp.sum(selc * _stair_rows(e1t, e2t, jnp.multiply), axis=0, keepdims=True)
                nfull = jnp.zeros(s0.shape, F32)
                for a in range(kt):
                    nfull = jnp.where(rank0 == float(a), n_a[a], nfull)
                n_ref[h, lc] = nfull
                a1_ref[h, lc] = jnp.exp(s0 - top0[0:1, :]) / z
                r2_ref[h, lc] = rank1.astype(BF16)
                e2_ref[h, lc] = jnp.exp(s1 - top1[0:1, :]).astype(BF16)

            lax.fori_loop(0, nlc // lcu, chunk, 0)

    nsteps = pl.num_programs(1) - 1
    nch = 2
    cw = hnT_ref.shape[1] // nch

    def step(w_read, w_write):
        hnT = hnT_ref[...]
        nlc = hnT.shape[1] // LANES

        def lanes_cat(ref, h, rows=slice(None)):
            return jnp.concatenate([ref[h, lc, rows, :] for lc in range(nlc)], axis=1)

        for ip in range(ib // 2):
            if ip % (ib // 2 // nch) == 0:
                ch = ip // (ib // 2 // nch)
                cols = slice(ch * cw, (ch + 1) * cw)
                acc_ref[:, cols] += jnp.dot(vT_ref[...], w_read[:, cols], preferred_element_type=F32)
            iis = (2 * ip, 2 * ip + 1)
            aTs = [jnp.dot(u_ref[ii * nkeys:(ii + 1) * nkeys, :], hnT, preferred_element_type=F32) for ii in iis]
            gsums = [jnp.zeros(aTs[0].shape, BF16) for _ in iis]
            for h in range(PK_HEADS):
                r2 = lanes_cat(r2_ref, h)
                e2 = lanes_cat(e2_ref, h)
                for n, ii in enumerate(iis):
                    row = pl.ds(i * ib + ii, 1)
                    nrow = lanes_cat(n_ref, h, row).astype(BF16)
                    arow = lanes_cat(a1_ref, h, row).astype(BF16)
                    gsums[n] = gsums[n] + jnp.where(r2 < nrow, arow * e2, jnp.zeros((), BF16))
            for n, ii in enumerate(iis):
                w_write[ii * nkeys:(ii + 1) * nkeys, :] = jax.nn.gelu(aTs[n].astype(BF16)) * gsums[n]

    @pl.when((i < nsteps) & (i % 2 == 0))
    def _():
        step(wodd_ref, wcat_ref)

    @pl.when((i < nsteps) & (i % 2 == 1))
    def _():
        step(wcat_ref, wodd_ref)

    @pl.when(i == nsteps)
    def _():
        w_last = wodd_ref if (nkeys // ib) % 2 == 0 else wcat_ref
        acc = acc_ref[...] + jnp.dot(vT_ref[...], w_last[...], preferred_element_type=F32)
        y = x_ref[...] + acc.T
        if final_norm:
            y = _rms(y, wf_ref[...])
        y_ref[...] = y


def _peer(x, wn, wqT, sk, u, vT, wf, tt, ib, final_norm):
    t, d = x.shape
    nkeys = sk.shape[1]
    assert t % tt == 0 and nkeys % ib == 0
    nsteps = nkeys // ib
    nlc = tt // LANES
    stat = pltpu.VMEM((PK_HEADS, nlc, nkeys, LANES), F32)
    stat16 = pltpu.VMEM((PK_HEADS, nlc, nkeys, LANES), BF16)
    wbuf = pltpu.VMEM((ib * nkeys, tt), BF16)
    return pl.pallas_call(
        functools.partial(_peer_kernel, ib=ib, nkeys=nkeys, final_norm=final_norm),
        grid=(t // tt, nsteps + 1),
        in_specs=[pl.BlockSpec((tt, d), lambda a, i: (a, 0)),
                  pl.BlockSpec((1, d), lambda a, i: (0, 0)),
                  pl.BlockSpec(wqT.shape, lambda a, i: (0, 0)),
                  pl.BlockSpec(sk.shape, lambda a, i: (0, 0, 0)),
                  pl.BlockSpec((ib * nkeys, d), lambda a, i: (jnp.minimum(i, nsteps - 1), 0)),
                  pl.BlockSpec((None, d, ib * nkeys), lambda a, i: (jnp.maximum(i - 1, 0), 0, 0)),
                  pl.BlockSpec((1, d), lambda a, i: (0, 0))],
        out_specs=pl.BlockSpec((tt, d), lambda a, i: (a, 0)),
        out_shape=jax.ShapeDtypeStruct((t, d), F32),
        scratch_shapes=[pltpu.VMEM((d, tt), BF16), pltpu.VMEM((d, tt), F32), wbuf, wbuf,
                        pltpu.VMEM((2, nlc, nkeys, LANES), F32), stat, stat, stat16, stat16],
        compiler_params=_cparams(("arbitrary", "arbitrary")), name="peer_dense",
    )(x, wn, wqT, sk, u, vT, wf)


def _compress_pages_kernel(pt_ref, *refs, pg, bpp):
    w_ref, o_ref = refs[pg], refs[pg + 1]
    nt = (((1,), (1,)), ((), ()))
    for j in range(pg):
        x = refs[j][...]
        halves = [lax.dot_general(w_ref[c], x[c].reshape(-1, x.shape[-1]).astype(BF16), nt, preferred_element_type=F32)
                  for c in range(x.shape[0])]
        o_ref[j * bpp:(j + 1) * bpp, :] = jnp.concatenate(halves, axis=1)


def _compress_pages(cache_t, page_table, w_sel, pg):
    tail = cache_t.shape[1:]
    db, npages = page_table.shape
    bpp = w_sel.shape[1]
    width = tail[0] * tail[1] * tail[2]
    assert npages % pg == 0
    zeros = (0,) * len(tail)
    page_specs = [pl.BlockSpec((None,) + tail, functools.partial(lambda b, g, pt, j: (pt[b, g * pg + j],) + zeros, j=j))
                  for j in range(pg)]
    return pl.pallas_call(
        functools.partial(_compress_pages_kernel, pg=pg, bpp=bpp),
        grid_spec=pltpu.PrefetchScalarGridSpec(
            num_scalar_prefetch=1, grid=(db, npages // pg),
            in_specs=page_specs + [pl.BlockSpec(w_sel.shape, lambda b, g, pt: (0, 0, 0))],
            out_specs=pl.BlockSpec((None, pg * bpp, width), lambda b, g, pt: (b, g, 0))),
        out_shape=jax.ShapeDtypeStruct((db, npages * bpp, width), F32),
        compiler_params=_cparams(("arbitrary", "arbitrary")), name="nsa_compress_pages",
    )(page_table, *([cache_t] * pg), w_sel)


def _nsa_decode_head_kernel(slope_ref, qbd_ref, g_ref, kvc_ref, wnew_ref, cwin_ref, part_ref, idx_ref,
                            *, past_len, k_past):
    kvw, grp, dh = NSA_KV_WIDTH, NSA_GROUP, NSA_HEAD_DIM
    qbd = qbd_ref[...]
    qb = qbd.astype(BF16)
    slope = slope_ref[...]
    nh = qbd.shape[0]
    npb = kvc_ref.shape[0]
    nt = (((1,), (1,)), ((), ()))
    kvc = kvc_ref[...]
    sc = lax.dot_general(qb, kvc[:, :kvw].astype(BF16), nt, preferred_element_type=F32)
    c_end = lax.broadcasted_iota(jnp.int32, (1, npb), 1) * NSA_BLOCK + (NSA_BLOCK - 1)
    s = sc - slope * (past_len - c_end).astype(F32)
    e = jnp.exp(s - jnp.max(s, axis=1, keepdims=True))
    p = e / jnp.sum(e, axis=1, keepdims=True)
    oc = jnp.dot(p.astype(BF16), kvc[:, kvw:].astype(BF16), preferred_element_type=F32)
    rows = []
    for k in range(NSA_KV_HEADS):
        r = p[k * grp:k * grp + 1, :]
        for g in range(1, grp):
            r = r + p[k * grp + g:k * grp + g + 1, :]
        rows.append(jnp.broadcast_to(r, (grp, npb)))
    imp = jnp.concatenate(rows, axis=0)
    blk = lax.broadcasted_iota(jnp.int32, (nh, npb), 1)
    score = jnp.where((blk == 0) | (blk == npb - 1), grp + 1.0, imp)
    lanes = blk.astype(F32)
    picks = []
    for _ in range(k_past):
        mx = jnp.max(score, axis=1, keepdims=True)
        idx = jnp.min(jnp.where(score == mx, lanes, float(npb)), axis=1, keepdims=True)
        score = jnp.where(lanes == idx, LOWEST, score)
        picks.append(idx)
    idx_ref[...] = jnp.concatenate(picks, axis=1).astype(jnp.int32)
    sg = jax.nn.sigmoid(g_ref[...])
    wnew = wnew_ref[...]
    nw = cwin_ref.shape[-1]
    dw = (nw - lax.broadcasted_iota(jnp.int32, (1, nw), 1)).astype(F32)
    for k in range(NSA_KV_HEADS):
        hs, ds = slice(k * grp, (k + 1) * grp), slice(k * dh, (k + 1) * dh)
        qk = qbd[hs, ds]
        sw = jnp.dot(qk.astype(BF16), cwin_ref[0, k].astype(BF16), preferred_element_type=F32) - slope[hs] * dw
        s_n = jnp.sum(qk * wnew[:, ds], axis=1, keepdims=True)
        m = jnp.maximum(jnp.max(sw, axis=1, keepdims=True), s_n)
        ew = jnp.exp(sw - m)
        en = jnp.exp(s_n - m)
        pv = lax.dot_general(ew.astype(BF16), cwin_ref[1, k].astype(BF16), nt, preferred_element_type=F32)
        ow = (pv + en * wnew[:, kvw + k * dh:kvw + (k + 1) * dh]) / (jnp.sum(ew, axis=1, keepdims=True) + en)
        part_ref[hs, :] = sg[hs, 0:1] * oc[hs, ds] + sg[hs, 2:3] * ow


def _nsa_decode_gather_kernel(pt_ref, ix_ref, slope_ref, q_ref, g_ref, part_ref, new_ref, *refs, nsel, bpp, past_len):
    blocks, o_ref = refs[:nsel], refs[nsel]
    b, k = pl.program_id(0), pl.program_id(1)
    page = blocks[0].shape[-1]
    q = q_ref[...]
    slope = slope_ref[...]
    kt = jnp.concatenate([blocks[s][0] for s in range(nsel)], axis=1).astype(BF16)
    vt = jnp.concatenate([blocks[s][1] for s in range(nsel)], axis=1).astype(BF16)
    sc = jnp.dot(q.astype(BF16), kt, preferred_element_type=F32)
    lane = lax.broadcasted_iota(jnp.int32, (1, page), 1)
    lblk = lane // NSA_BLOCK
    lkey = lane - lblk * NSA_BLOCK
    bias = []
    for s in range(nsel):
        ib = ix_ref[b, k, s]
        kpos = ib * NSA_BLOCK + lkey
        bias.append(jnp.where(lblk == ib % bpp, -slope * (past_len - kpos).astype(F32), NEG_INF))
    sc = sc + jnp.concatenate(bias, axis=1)
    k_own, v_own = new_ref[0], new_ref[1]
    s_n = jnp.sum(q * k_own, axis=1, keepdims=True)
    m = jnp.maximum(jnp.max(sc, axis=1, keepdims=True), s_n)
    p = jnp.exp(sc - m)
    p_n = jnp.exp(s_n - m)
    pv = lax.dot_general(p.astype(BF16), vt, (((1,), (1,)), ((), ())), preferred_element_type=F32)
    o_s = (pv + p_n * v_own) / (jnp.sum(p, axis=1, keepdims=True) + p_n)
    o_ref[...] = part_ref[...] + jax.nn.sigmoid(g_ref[...])[:, 1:2] * o_s


def _rows_minor(a):
    return a.transpose(0, 2, 3, 4, 1)


def _nsa_sample(q, kv_cmp, kv_sel, kv_win, glog, cache_cmp, cache_sel, cache_win, page_table, w_cmp):
    db = q.shape[0]
    kvh, grp, dh, kvw = NSA_KV_HEADS, NSA_GROUP, NSA_HEAD_DIM, NSA_KV_WIDTH
    page = cache_cmp.shape[1]
    npages = page_table.shape[1]
    past_len = npages * page
    nwin = cache_win.shape[1]
    assert past_len >= nwin and page % NSA_BLOCK == 0
    npb = past_len // NSA_BLOCK
    bpp = page // NSA_BLOCK
    pg = 32 if npages % 32 == 0 else 1
    width = 2 * kvw
    rowblk = jnp.arange(page) // NSA_BLOCK
    w_sel = jnp.where(rowblk[None, None, :] == jnp.arange(bpp)[None, :, None],
                      jnp.tile(w_cmp, (1, bpp))[:, None, :], 0.0).astype(BF16)
    kvc = _compress_pages(_rows_minor(cache_cmp), page_table, w_sel, pg)
    q4 = q.reshape(db, kvh, grp, dh) * (dh ** -0.5)
    qbd = (q4[:, :, :, None, :] * jnp.eye(kvh, dtype=F32)[None, :, None, :, None]).reshape(db, kvh * grp, kvw)
    slope = _alibi_slopes().reshape(kvh * grp, 1)
    nh = kvh * grp
    k_past = min(NSA_TOPK, npb + 1) - 1
    full2 = lambda shape: pl.BlockSpec(shape, lambda b: (0,) * len(shape))
    per_b = lambda shape: pl.BlockSpec((None,) + shape, lambda b: (b,) + (0,) * len(shape))
    part, idx = pl.pallas_call(
        functools.partial(_nsa_decode_head_kernel, past_len=past_len, k_past=k_past),
        grid=(db,),
        in_specs=[full2((nh, 1)), per_b((nh, kvw)), per_b((nh, 3)), per_b((npb, width)), per_b((1, width)),
                  per_b((2, kvh, dh, nwin))],
        out_specs=[per_b((nh, dh)), per_b((nh, k_past))],
        out_shape=[jax.ShapeDtypeStruct((db, nh, dh), F32), jax.ShapeDtypeStruct((db, nh, k_past), jnp.int32)],
        compiler_params=_cparams(("parallel",)), name="nsa_decode_head",
    )(slope, qbd, glog, kvc, kv_win.reshape(db, 1, width), _rows_minor(cache_win))
    idx4 = idx.reshape(db, kvh, grp, k_past)[:, :, 0, :]
    blk_specs = [pl.BlockSpec((None, 2, None, dh, page),
                              functools.partial(lambda b, k, pt, ix, s: (pt[b, ix[b, k, s] // bpp], 0, k, 0, 0), s=s))
                 for s in range(k_past)]
    hsp = lambda shape: pl.BlockSpec((None, None) + shape, lambda b, k, pt, ix: (b, k) + (0,) * len(shape))
    o = pl.pallas_call(
        functools.partial(_nsa_decode_gather_kernel, nsel=k_past, bpp=bpp, past_len=past_len),
        grid_spec=pltpu.PrefetchScalarGridSpec(
            num_scalar_prefetch=2, grid=(db, kvh),
            in_specs=[pl.BlockSpec((None, grp, 1), lambda b, k, pt, ix: (k, 0, 0)), hsp((grp, dh)), hsp((grp, 3)),
                      hsp((grp, dh)),
                      pl.BlockSpec((None, 2, None, 1, dh), lambda b, k, pt, ix: (b, 0, k, 0, 0))] + blk_specs,
            out_specs=hsp((grp, dh))),
        out_shape=jax.ShapeDtypeStruct((db, kvh, grp, dh), F32),
        compiler_params=_cparams(("arbitrary", "arbitrary")), name="nsa_decode_gather",
    )(page_table, idx4, slope.reshape(kvh, grp, 1), q4, glog.reshape(db, kvh, grp, 3), part.reshape(db, kvh, grp, dh),
      kv_sel.reshape(db, 2, kvh, 1, dh), *([_rows_minor(cache_sel)] * k_past))
    new_win = jnp.concatenate([cache_win, kv_win.reshape((db, 1) + cache_win.shape[2:])], axis=1)[:, 1:]
    return o.reshape(db, nh * dh), new_win


def _split_w_in(w_in):
    pts, acc = [], 0
    for w in (NSA_Q_WIDTH, 6 * NSA_KV_WIDTH, 3 * NSA_HEADS, HG_WIDTH, HG_WIDTH, HG_WIDTH, HG_WIDTH, w_in.shape[0]):
        acc += w
        pts.append(acc)
    return jnp.split(w_in, pts, axis=1)


def _prep_layer(w_in, w_proj_nsa, w_proj_hgrn, w_out, w_peer_q, peer_sub_keys, peer_u, peer_v):
    d = w_in.shape[0]
    wq, wkv, wg, wbq, wbf, wbi, wbg, wma, wmb = _split_w_in(w_in)
    wg = wg.reshape(d, NSA_KV_HEADS, NSA_GROUP, 3).transpose(0, 1, 3, 2).reshape(d, NSA_KV_HEADS, 3 * NSA_GROUP)
    wg = jnp.pad(wg, ((0, 0), (0, 0), (0, 16 - 3 * NSA_GROUP))).reshape(d, NSA_KV_HEADS * 16)
    wg = jnp.pad(wg, ((0, 0), (0, LANES - NSA_KV_HEADS * 16)))
    nkeys = peer_sub_keys.shape[2]
    return dict(
        w_qg=jnp.concatenate([wq, wg], axis=1).astype(BF16),
        w_kv=wkv.astype(BF16),
        w_hg=jnp.concatenate([wbq, wbf, wbi, wbg], axis=1).astype(BF16),
        w_m=jnp.concatenate([wma, wmb], axis=1).astype(BF16),
        pa=w_proj_nsa.astype(BF16), pb=w_proj_hgrn.astype(BF16), wo=w_out.astype(BF16),
        wqT=w_peer_q.T.astype(BF16),
        sk=peer_sub_keys.reshape(PK_HEADS * 2, nkeys, PK_DIM // 2).astype(BF16),
        u=peer_u.astype(BF16),
        vT=peer_v.astype(BF16).reshape(nkeys // PEER_IB, PEER_IB * nkeys, d).transpose(0, 2, 1),
    )


def _tile(t, pref):
    return pref if t % pref == 0 else t


def _peer_tokens(x, wn, wf, prm, final_norm):
    t = x.shape[0]
    tp = -(-t // LANES) * LANES
    xp = jnp.pad(x, ((0, tp - t), (0, 0)))
    tt = PEER_TT if tp % PEER_TT == 0 else LANES
    y = _peer(xp, wn, prm["wqT"], prm["sk"], prm["u"], prm["vT"], wf, tt, PEER_IB, final_norm)
    return y[:t]


def _layer_prompt(x, lb, prm, w_norm_mix, w_cmp, w_hgrn_norm, w_norm_ffn, w_norm_final, final_norm):
    s_len, d = x.shape
    tm = _tile(s_len, 512)
    kvw = NSA_KV_WIDTH
    kv_cmp, kv_sel, kv_win = _proj(x, w_norm_mix, prm["w_kv"], [2 * kvw] * 3, [False] * 3, tm)
    qT, gT = _proj(x, w_norm_mix, prm["w_qg"], [NSA_Q_WIDTH, LANES], [True, True], tm)
    hq, hf, hi, hg = _proj(x, w_norm_mix, prm["w_hg"], [HG_WIDTH] * 4, [False] * 4, tm)
    ma, mb = _proj(x, w_norm_mix, prm["w_m"], [d, d], [False, False], tm)
    wfull = jnp.concatenate([jnp.broadcast_to(w_cmp[0][:, None], (NSA_BLOCK, kvw)),
                             jnp.broadcast_to(w_cmp[1][:, None], (NSA_BLOCK, kvw))], axis=1)
    nblk = s_len // NSA_BLOCK
    kvc = _compress(kv_cmp, wfull, 8 if nblk % 8 == 0 else nblk)
    o_a = _nsa_prompt(qT, gT[:NSA_KV_HEADS * 16], kvc, kv_sel, kv_win)
    s0 = jnp.zeros((HG_HEADS, HG_DK, HG_DV), F32)
    o_b, s_fin = _hgrn_prompt(hq, hf, hi, hg, lb, w_hgrn_norm, s0, _tile(s_len, 512))
    x1 = _merge(x, o_a, o_b, ma, mb, prm["pa"], prm["pb"], prm["wo"], tm)
    x2 = _peer_tokens(x1, w_norm_ffn, w_norm_final, prm, final_norm)
    shp = (s_len, 2, NSA_KV_HEADS, NSA_HEAD_DIM)
    return x2, kv_cmp.reshape(shp), kv_sel.reshape(shp), kv_win.reshape(shp)[-NSA_WINDOW:], s_fin


def _layer_sample(x, lb, prm, w_norm_mix, w_cmp, w_hgrn_norm, w_norm_ffn, w_norm_final, final_norm,
                  cache_cmp, cache_sel, cache_win, s0, page_table):
    b, d = x.shape
    kvw = NSA_KV_WIDTH
    w_q = prm["w_qg"][:, :NSA_Q_WIDTH]
    (q,) = _proj(x, w_norm_mix, w_q, [NSA_Q_WIDTH], [False], b)
    kv_cmp, kv_sel, kv_win = _proj(x, w_norm_mix, prm["w_kv"], [2 * kvw] * 3, [False] * 3, b)
    (gTt,) = _proj(x, w_norm_mix, prm["w_qg"][:, NSA_Q_WIDTH:], [LANES], [False], b)
    hq, hf, hi, hg = _proj(x, w_norm_mix, prm["w_hg"], [HG_WIDTH] * 4, [False] * 4, b)
    ma, mb = _proj(x, w_norm_mix, prm["w_m"], [d, d], [False, False], b)
    shp = (b, 1, 2, NSA_KV_HEADS, NSA_HEAD_DIM)
    glog = (gTt[:, :NSA_KV_HEADS * 16].reshape(b, NSA_KV_HEADS, 16)[..., :3 * NSA_GROUP]
            .reshape(b, NSA_KV_HEADS, 3, NSA_GROUP).transpose(0, 1, 3, 2).reshape(b, NSA_HEADS, 3))
    o_a, new_win = _nsa_sample(q, kv_cmp, kv_sel, kv_win, glog, cache_cmp, cache_sel, cache_win, page_table, w_cmp)
    o_b, s_new = _hgrn_step(hq.reshape(b, 1, HG_WIDTH), hf.reshape(b, 1, HG_WIDTH), hi.reshape(b, 1, HG_WIDTH),
                            hg.reshape(b, 1, HG_WIDTH), lb, w_hgrn_norm, s0)
    x1 = _merge(x, o_a.reshape(b, NSA_Q_WIDTH), o_b.reshape(b, HG_WIDTH), ma, mb, prm["pa"], prm["pb"], prm["wo"], b)
    x2 = _peer_tokens(x1, w_norm_ffn, w_norm_final, prm, final_norm)
    return x2, kv_cmp.reshape(shp), kv_sel.reshape(shp), new_win, s_new


def kernel(x_prompt, x_sample, cache_cmp_kv, cache_sel_kv, cache_win_kv, state_hgrn, page_table,
           w_norm_mix, w_in, w_cmp, w_proj_nsa, w_proj_hgrn, w_hgrn_norm, hgrn_lb_logits, w_out,
           w_norm_ffn, w_peer_q, peer_sub_keys, peer_u, peer_v, w_norm_final):
    depth = w_in.shape[0]
    bsz, s_len, d = x_prompt.shape
    db, dt, _ = x_sample.shape
    assert dt == 1
    lbs = jnp.cumsum(jax.nn.softmax(hgrn_lb_logits.astype(F32), axis=0), axis=0)
    wfin = w_norm_final.reshape(1, d)
    xp = [x_prompt[b] for b in range(bsz)]
    xs = x_sample.reshape(db, d)
    st_p, st_s = [], []
    for l in range(depth):
        last = l == depth - 1
        prm = _prep_layer(w_in[l], w_proj_nsa[l], w_proj_hgrn[l], w_out[l], w_peer_q[l], peer_sub_keys[l],
                          peer_u[l], peer_v[l])
        shared = (lbs[l].reshape(1, HG_WIDTH), prm, w_norm_mix[l].reshape(1, d), w_cmp[l],
                  w_hgrn_norm[l].reshape(1, HG_DV), w_norm_ffn[l].reshape(1, d), wfin, last)
        outs = [_layer_prompt(xp[b], *shared) for b in range(bsz)]
        xp = [o[0] for o in outs]
        st_p.append(tuple(jnp.stack([o[k] for o in outs]) for k in range(1, 5)))
        xs, *ss = _layer_sample(xs, *shared, cache_cmp_kv[l], cache_sel_kv[l], cache_win_kv[l], state_hgrn[l],
                                page_table)
        st_s.append(tuple(ss))
    y_prompt = jnp.stack(xp)
    y_sample = xs.reshape(db, dt, d)
    return (y_prompt, y_sample,
            jnp.stack([s[0] for s in st_p]), jnp.stack([s[1] for s in st_p]),
            jnp.stack([s[2] for s in st_p]), jnp.stack([s[3] for s in st_p]),
            jnp.stack([s[0] for s in st_s]), jnp.stack([s[1] for s in st_s]),
            jnp.stack([s[2] for s in st_s]), jnp.stack([s[3] for s in st_s]))
```
